```python
import math
import jax
import jax.numpy as jnp
from jax import lax
import numpy as np

D_MODEL = 1024
BATCH = 4
SEQ = 8192
DEPTH = 1
DEC_BATCH = 32
DEC_SEQ = 8
PAST_LEN = 16384
PAGE_SIZE = 128

F32 = jnp.float32

GDN_HEADS = 4
GDN_DK = 128
GDN_DV = 128
GDN_QK = GDN_HEADS * GDN_DK
GDN_V = GDN_HEADS * GDN_DV
GDN_CONV_DIM = 2 * GDN_QK + GDN_V
GDN_CONV = 4
GDN_CHUNK = 64

NSA_HEADS = 8
NSA_KV_HEADS = 2
NSA_GROUP = NSA_HEADS // NSA_KV_HEADS
NSA_HD = 64
NSA_Q = NSA_HEADS * NSA_HD
NSA_KV = NSA_KV_HEADS * NSA_HD
CMP_STRIDE = 16
CMP_LEN = 2 * CMP_STRIDE
SEL_BLOCK = 64
SEL_TOP = 16
WINDOW = 512
Q_BLOCK = 128
FORCE_SCORE = 1.0e4

N_GROUPS = 4
EXPERTS_PER_GROUP = 8
N_EXPERTS = N_GROUPS * EXPERTS_PER_GROUP
TOP_K_IN_GROUP = 2
D_EXPERT = 256

RMS_EPS = 1e-6

PROJ_SIZES = (GDN_CONV_DIM, GDN_V, GDN_HEADS, GDN_HEADS, NSA_Q, 6 * NSA_KV, 3 * NSA_HEADS, D_MODEL, D_MODEL)
PROJ_DIM = sum(PROJ_SIZES)
PROJ_SPLITS = tuple(sum(PROJ_SIZES[:i + 1]) for i in range(len(PROJ_SIZES) - 1))

kernel_name = 'hybrid_gdn_nsa_hmoe_step'


def rmsnorm(x, g):
    xf = x.astype(F32)
    y = xf * lax.rsqrt(jnp.mean(xf * xf, axis=-1, keepdims=True) + RMS_EPS)
    return (y * g.astype(F32)).astype(x.dtype)


def l2norm(x):
    return x * lax.rsqrt(jnp.sum(x * x, axis=-1, keepdims=True) + RMS_EPS)


def masked_softmax(s, mask):
    s = jnp.where(mask, s, -jnp.inf)
    m = jnp.max(s, axis=-1, keepdims=True)
    m = jnp.where(jnp.isfinite(m), m, 0.0)
    e = jnp.where(mask, jnp.exp(s - m), 0.0)
    return e / jnp.maximum(jnp.sum(e, axis=-1, keepdims=True), 1e-30)


def alibi_slopes():
    h = jnp.arange(1, NSA_HEADS + 1, dtype=F32)
    return jnp.exp2(-8.0 * h / NSA_HEADS).reshape(NSA_KV_HEADS, NSA_GROUP)


def project(x, norm_g, w_in):
    xn = rmsnorm(x, norm_g)
    return jnp.split(jnp.einsum('bld,de->ble', xn, w_in), PROJ_SPLITS, axis=-1)


def causal_conv(u, buf, w):
    L = u.shape[1]
    full = jnp.concatenate([buf.astype(u.dtype), u], axis=1)
    out = w[0] * full[:, 0:L]
    for i in range(1, GDN_CONV):
        out = out + w[i] * full[:, i:i + L]
    return jax.nn.silu(out), full[:, L:]


def gated_delta_chunked(q, k, v, g, beta, s0, chunk):
    B, L, H, DK = q.shape
    DV = v.shape[-1]
    n = L // chunk

    def to_c(t):
        return t.astype(F32).reshape(B, n, chunk, H, t.shape[-1]).transpose(0, 3, 1, 2, 4)

    qc, kc, vc = to_c(q), to_c(k), to_c(v)
    gc = g.astype(F32).reshape(B, n, chunk, H).transpose(0, 3, 1, 2)
    bc = beta.astype(F32).reshape(B, n, chunk, H).transpose(0, 3, 1, 2)
    gcum = jnp.cumsum(gc, axis=-1)
    tri_incl = jnp.tril(jnp.ones((chunk, chunk), bool))
    tri_strict = jnp.tril(jnp.ones((chunk, chunk), bool), -1)
    decay = jnp.exp(jnp.where(tri_incl, gcum[..., :, None] - gcum[..., None, :], -jnp.inf))
    kb = kc * bc[..., None]
    a_strict = jnp.where(tri_strict, jnp.einsum('bhncd,bhnsd->bhncs', kb, kc) * decay, 0.0)
    rhs = jnp.concatenate([vc * bc[..., None], kb * jnp.exp(gcum)[..., None]], axis=-1)
    sol = lax.linalg.triangular_solve(a_strict, rhs, left_side=True, lower=True, unit_diagonal=True)
    u_coef, w_coef = sol[..., :DV], sol[..., DV:]
    qk = jnp.einsum('bhncd,bhnsd->bhncs', qc, kc) * decay
    q_dec = qc * jnp.exp(gcum)[..., None]
    k_tail = kc * jnp.exp(gcum[..., -1:] - gcum)[..., None]
    g_last = jnp.exp(gcum[..., -1])
    xs = tuple(jnp.moveaxis(t, 2, 0) for t in (u_coef, w_coef, qk, q_dec, k_tail, g_last))

    def step(S, inp):
        u_c, w_c, qk_c, qd_c, kt_c, gl_c = inp
        u = u_c - jnp.einsum('bhck,bhkv->bhcv', w_c, S)
        o = jnp.einsum('bhck,bhkv->bhcv', qd_c, S) + jnp.einsum('bhcs,bhsv->bhcv', qk_c, u)
        S = S * gl_c[..., None, None] + jnp.einsum('bhck,bhcv->bhkv', kt_c, u)
        return S, o

    S, o = lax.scan(step, s0.astype(F32), xs)
    return o.transpose(1, 0, 3, 2, 4).reshape(B, L, H, DV), S


def gdn_branch(qkv_pre, z, b_raw, a_raw, conv_buf, s0, lp):
    B, L, _ = qkv_pre.shape
    qkv, conv_new = causal_conv(qkv_pre, conv_buf, lp['gdn_conv_w'])
    qkv = qkv.astype(F32)
    q = l2norm(qkv[..., :GDN_QK].reshape(B, L, GDN_HEADS, GDN_DK)) * (GDN_DK ** -0.5)
    k = l2norm(qkv[..., GDN_QK:2 * GDN_QK].reshape(B, L, GDN_HEADS, GDN_DK))
    v = qkv[..., 2 * GDN_QK:].reshape(B, L, GDN_HEADS, GDN_DV)
    beta = jax.nn.sigmoid(b_raw.astype(F32))
    g = -jnp.exp(lp['gdn_a_log'].astype(F32)) * jax.nn.softplus(a_raw.astype(F32) + lp['gdn_dt_bias'].astype(F32))
    o, s_new = gated_delta_chunked(q, k, v, g, beta, s0, math.gcd(L, GDN_CHUNK))
    o = rmsnorm(o, lp['gdn_norm_g']) * jax.nn.silu(z.astype(F32).reshape(B, L, GDN_HEADS, GDN_DV))
    return o.reshape(B, L, GDN_V), s_new, conv_new


def compress(k, pos_w, w_c):
    B, T = k.shape[:2]
    n_ch = T // CMP_STRIDE
    ch = k[:, :n_ch * CMP_STRIDE].reshape(B, n_ch, CMP_STRIDE, NSA_KV_HEADS, NSA_HD)
    pw = pos_w.astype(F32)
    head = jnp.einsum('bmpgd,p->bmgd', ch, pw[:CMP_STRIDE])
    tail = jnp.einsum('bmpgd,p->bmgd', ch, pw[CMP_STRIDE:])
    blocks = head[:, :-1] + tail[:, 1:]
    return jnp.einsum('bcgd,de->bcge', blocks, w_c.astype(F32))


def sel_overlap(n_cmp, n_sel):
    i = jnp.arange(n_cmp)[:, None]
    j = jnp.arange(n_sel)[None, :]
    lo = jnp.maximum(i * CMP_STRIDE, j * SEL_BLOCK)
    hi = jnp.minimum(i * CMP_STRIDE + CMP_LEN, (j + 1) * SEL_BLOCK)
    return jnp.maximum(hi - lo, 0).astype(F32) / CMP_LEN


def nsa_sources(full4, lp):
    f = full4.astype(F32)
    B, T = f.shape[:2]
    kc = compress(f[:, :, 0], lp['cmp_pos_wk'], lp['cmp_wk'])
    vc = compress(f[:, :, 1], lp['cmp_pos_wv'], lp['cmp_wv'])
    ns = -(-T // SEL_BLOCK)
    sel = jnp.pad(f[:, :, 2:4], ((0, 0), (0, ns * SEL_BLOCK - T), (0, 0), (0, 0), (0, 0)))
    sel = sel.reshape(B, ns, SEL_BLOCK, 2, NSA_KV_HEADS, NSA_HD).transpose(3, 0, 4, 1, 2, 5)
    return kc, vc, sel[0], sel[1]


def cmp_attend(q, qpos, kc, vc, slopes):
    n_cmp = kc.shape[1]
    blk_end = jnp.arange(n_cmp) * CMP_STRIDE + CMP_LEN - 1
    dist = qpos[:, None] - blk_end[None, :]
    s = jnp.einsum('bqghd,bcgd->bghqc', q, kc) - slopes[:, :, None, None] * dist.astype(F32)
    p = masked_softmax(s, dist >= 0)
    return jnp.einsum('bghqc,bcgd->bqghd', p, vc), p


def sel_attend(q, qpos, p_cmp, kb, vb, slopes):
    B, G, NS = kb.shape[:3]
    Q = q.shape[1]
    imp = jnp.einsum('bghqc,cs->bgqs', p_cmp, sel_overlap(p_cmp.shape[-1], NS))
    blk = jnp.arange(NS)[None, :]
    cur = (qpos // SEL_BLOCK)[:, None]
    forced = (blk == 0) | (blk == cur) | (blk == cur - 1)
    score = jnp.where(forced, FORCE_SCORE, imp)
    score = jnp.where(blk <= cur, score, -jnp.inf)
    top_v, top_i = lax.top_k(score, min(SEL_TOP, NS))
    ok = jnp.isfinite(top_v)
    take = jax.vmap(jax.vmap(lambda blocks, idx: blocks[idx]))
    kg = take(kb, top_i)
    vg = take(vb, top_i)
    n = top_i.shape[-1]
    kpos = top_i[..., None] * SEL_BLOCK + jnp.arange(SEL_BLOCK)
    dist = qpos[:, None, None] - kpos
    mask = (dist >= 0) & ok[..., None]
    s = jnp.einsum('bqghd,bgqnkd->bghqnk', q, kg) - slopes[None, :, :, None, None, None] * dist[:, :, None].astype(F32)
    p = masked_softmax(s.reshape(B, G, NSA_GROUP, Q, n * SEL_BLOCK), mask.reshape(B, G, 1, Q, n * SEL_BLOCK))
    return jnp.einsum('bghqm,bgqmd->bqghd', p, vg.reshape(B, G, Q, n * SEL_BLOCK, NSA_HD))


def win_attend(q, qpos, kw, vw, kwpos, slopes):
    dist = qpos[:, None] - kwpos[None, :]
    mask = (dist >= 0) & (dist <= WINDOW) & (kwpos >= 0)[None, :]
    s = jnp.einsum('bqghd,bkgd->bghqk', q, kw) - slopes[:, :, None, None] * dist.astype(F32)
    return jnp.einsum('bghqk,bkgd->bqghd', masked_softmax(s, mask), vw)


def nsa_core(q, gates, qpos, kc, vc, kb, vb, kw, vw, kwpos):
    slopes = alibi_slopes()
    o_c, p_c = cmp_attend(q, qpos, kc, vc, slopes)
    o_s = sel_attend(q, qpos, p_c, kb, vb, slopes)
    o_w = win_attend(q, qpos, kw, vw, kwpos, slopes)
    return gates[..., 0:1] * o_c + gates[..., 1:2] * o_s + gates[..., 2:3] * o_w


def nsa_prompt(q, gates, kv6, lp, win_buf):
    B, L = q.shape[:2]
    kc, vc, kb, vb = nsa_sources(kv6[:, :, :4], lp)
    kvw_raw = jnp.pad(kv6[:, :, 4:6], ((0, 0), (WINDOW, 0), (0, 0), (0, 0), (0, 0)))
    kvw = kvw_raw.astype(F32)
    n_blk = L // Q_BLOCK
    qb = jnp.swapaxes(q.reshape(B, n_blk, Q_BLOCK, NSA_KV_HEADS, NSA_GROUP, NSA_HD), 0, 1)
    gb = jnp.swapaxes(gates.reshape(B, n_blk, Q_BLOCK, NSA_KV_HEADS, NSA_GROUP, 3), 0, 1)

    def one_block(args):
        q_i, g_i, i = args
        start = i * Q_BLOCK
        qpos = start + jnp.arange(Q_BLOCK)
        kw = lax.dynamic_slice_in_dim(kvw, start, WINDOW + Q_BLOCK, axis=1)
        kwpos = start - WINDOW + jnp.arange(WINDOW + Q_BLOCK)
        return nsa_core(q_i, g_i, qpos, kc, vc, kb, vb, kw[:, :, 0], kw[:, :, 1], kwpos)

    o = lax.map(one_block, (qb, gb, jnp.arange(n_blk)))
    o = jnp.swapaxes(o, 0, 1).reshape(B, L, NSA_Q)
    return o, kvw_raw[:, kvw_raw.shape[1] - win_buf:]


def nsa_sample(q, gates, kv6, cache_kv_l, page_table, cache_win_l, lp):
    B, L = q.shape[:2]
    past_len = page_table.shape[1] * cache_kv_l.shape[1]
    past = cache_kv_l[page_table].reshape(B, past_len, 4, NSA_KV_HEADS, NSA_HD)
    full4 = jnp.concatenate([past, kv6[:, :, :4].astype(past.dtype)], axis=1)
    kc, vc, kb, vb = nsa_sources(full4, lp)
    win_buf = cache_win_l.shape[1]
    kvw = jnp.concatenate([cache_win_l, kv6[:, :, 4:6].astype(cache_win_l.dtype)], axis=1)
    kwpos = past_len - win_buf + jnp.arange(kvw.shape[1])
    qpos = past_len + jnp.arange(L)
    kvw_f = kvw.astype(F32)
    o = nsa_core(q, gates, qpos, kc, vc, kb, vb, kvw_f[:, :, 0], kvw_f[:, :, 1], kwpos)
    return o.reshape(B, L, NSA_Q), kvw[:, kvw.shape[1] - win_buf:]


def moe(xn, lp):
    N, D = xn.shape
    xf = xn.astype(F32)
    g_logit = xf @ lp['w_grp'].astype(F32) + lp['b_grp'].astype(F32)
    grp_oh = jax.nn.one_hot(jnp.argmax(g_logit, axis=-1), N_GROUPS, dtype=F32)
    p_grp = jnp.sum(jax.nn.softmax(g_logit, axis=-1) * grp_oh, axis=-1, keepdims=True)
    e_logit = (xf @ lp['w_rt'].astype(F32) + lp['b_rt'].astype(F32)).reshape(N, N_GROUPS, EXPERTS_PER_GROUP)
    e_in = jnp.einsum('nge,ng->ne', e_logit, grp_oh)
    top_v, top_i = lax.top_k(e_in, TOP_K_IN_GROUP)
    w = jax.nn.softmax(top_v, axis=-1) * p_grp
    combine = jnp.einsum('nk,nke->ne', w, jax.nn.one_hot(top_i, EXPERTS_PER_GROUP, dtype=F32))
    combine = combine[:, None, :] * grp_oh[:, :, None]
    wg = lp['w_e_gate'].reshape(N_GROUPS, EXPERTS_PER_GROUP, D, D_EXPERT)
    wu = lp['w_e_up'].reshape(N_GROUPS, EXPERTS_PER_GROUP, D, D_EXPERT)
    wd = lp['w_e_down'].reshape(N_GROUPS, EXPERTS_PER_GROUP, D_EXPERT, D)
    out = jnp.zeros((N, D), F32)
    for gi in range(N_GROUPS):
        h = jax.nn.silu(jnp.einsum('nd,edf->nef', xn, wg[gi])) * jnp.einsum('nd,edf->nef', xn, wu[gi])
        out = out + jnp.einsum('nef,efd->nd', h * combine[:, gi, :, None], wd[gi])
    return out


def layer_tail(x, y_a, y_b, gate_a, gate_b, lp):
    y_a = y_a.astype(x.dtype)
    y_b = y_b.astype(x.dtype)
    m = (jax.nn.sigmoid(gate_a) * jnp.einsum('blc,cd->bld', y_a, lp['w_branch_a'])
         + jax.nn.sigmoid(gate_b) * jnp.einsum('blc,cd->bld', y_b, lp['w_branch_b']))
    h = x + jnp.einsum('bld,de->ble', m, lp['w_out']).astype(x.dtype)
    B, L, D = h.shape
    hn = rmsnorm(h, lp['norm2_g']).reshape(B * L, D)
    return h + moe(hn, lp).reshape(B, L, D).astype(h.dtype)


def split_heads(x, lp):
    B, L, _ = x.shape
    qkv_pre, z, b_raw, a_raw, q_n, kv_n, g_n, gate_a, gate_b = project(x, lp['norm1_g'], lp['w_in'])
    q = q_n.astype(F32).reshape(B, L, NSA_KV_HEADS, NSA_GROUP, NSA_HD) * (NSA_HD ** -0.5)
    gates = jax.nn.sigmoid(g_n.astype(F32)).reshape(B, L, NSA_KV_HEADS, NSA_GROUP, 3)
    kv6 = kv_n.reshape(B, L, 6, NSA_KV_HEADS, NSA_HD)
    return (qkv_pre, z, b_raw, a_raw), q, gates, kv6, gate_a, gate_b


def prompt_layer(x, lp, win_buf):
    B = x.shape[0]
    gdn_in, q, gates, kv6, gate_a, gate_b = split_heads(x, lp)
    conv0 = jnp.zeros((B, GDN_CONV - 1, GDN_CONV_DIM), x.dtype)
    s0 = jnp.zeros((B, GDN_HEADS, GDN_DK, GDN_DV), F32)
    y_a, s_new, conv_new = gdn_branch(gdn_in[0], gdn_in[1], gdn_in[2], gdn_in[3], conv0, s0, lp)
    y_b, win_new = nsa_prompt(q, gates, kv6, lp, win_buf)
    y = layer_tail(x, y_a, y_b, gate_a, gate_b, lp)
    return y, kv6[:, :, :4], win_new, s_new, conv_new


def sample_layer(x, cache_kv_l, page_table, cache_win_l, s0, conv_buf, lp):
    gdn_in, q, gates, kv6, gate_a, gate_b = split_heads(x, lp)
    y_a, s_new, conv_new = gdn_branch(gdn_in[0], gdn_in[1], gdn_in[2], gdn_in[3], conv_buf, s0.astype(F32), lp)
    y_b, win_new = nsa_sample(q, gates, kv6, cache_kv_l, page_table, cache_win_l, lp)
    y = layer_tail(x, y_a, y_b, gate_a, gate_b, lp)
    return y, kv6[:, :, :4], win_new, s_new.astype(s0.dtype), conv_new.astype(conv_buf.dtype)


def setup_inputs(seed: int = 0) -> dict:
    key = jax.random.key(seed)
    ks = jax.random.split(key, 32)
    n_pages = PAST_LEN // PAGE_SIZE
    n_used = DEC_BATCH * n_pages
    n_phys = n_used + max(1, n_used // 4)
    win_buf = min(WINDOW, PAST_LEN)

    def nrm(k, shape, scale):
        return scale * jax.random.normal(k, shape, F32)

    page_table = jax.random.permutation(ks[3], n_phys)[:n_used].reshape(DEC_BATCH, n_pages).astype(jnp.int32)
    return {
        'x_prompt': nrm(ks[0], (BATCH, SEQ, D_MODEL), 1.0),
        'x_sample': nrm(ks[1], (DEC_BATCH, DEC_SEQ, D_MODEL), 1.0),
        'cache_kv': nrm(ks[2], (DEPTH, n_phys, PAGE_SIZE, 4, NSA_KV_HEADS, NSA_HD), 1.0),
        'page_table': page_table,
        'cache_win': nrm(ks[4], (DEPTH, DEC_BATCH, win_buf, 2, NSA_KV_HEADS, NSA_HD), 1.0),
        'state_gdn': nrm(ks[5], (DEPTH, DEC_BATCH, GDN_HEADS, GDN_DK, GDN_DV), 0.5),
        'state_conv': nrm(ks[6], (DEPTH, DEC_BATCH, GDN_CONV - 1, GDN_CONV_DIM), 1.0),
        'norm1_g': 1.0 + nrm(ks[7], (DEPTH, D_MODEL), 0.05),
        'w_in': nrm(ks[8], (DEPTH, D_MODEL, PROJ_DIM), D_MODEL ** -0.5),
        'gdn_conv_w': nrm(ks[9], (DEPTH, GDN_CONV, GDN_CONV_DIM), 0.5),
        'gdn_a_log': jnp.log(jax.random.uniform(ks[10], (DEPTH, GDN_HEADS), F32, 1.0, 16.0)),
        'gdn_dt_bias': -2.0 + nrm(ks[11], (DEPTH, GDN_HEADS), 0.5),
        'gdn_norm_g': 1.0 + nrm(ks[12], (DEPTH, GDN_DV), 0.05),
        'cmp_pos_wk': (1.0 + nrm(ks[13], (DEPTH, CMP_LEN), 0.1)) / CMP_LEN,
        'cmp_pos_wv': (1.0 + nrm(ks[14], (DEPTH, CMP_LEN), 0.1)) / CMP_LEN,
        'cmp_wk': nrm(ks[15], (DEPTH, NSA_HD, NSA_HD), 2.0 * NSA_HD ** -0.5),
        'cmp_wv': nrm(ks[16], (DEPTH, NSA_HD, NSA_HD), 2.0 * NSA_HD ** -0.5),
        'w_branch_a': nrm(ks[17], (DEPTH, GDN_V, D_MODEL), GDN_V ** -0.5),
        'w_branch_b': nrm(ks[18], (DEPTH, NSA_Q, D_MODEL), NSA_Q ** -0.5),
        'w_out': nrm(ks[19], (DEPTH, D_MODEL, D_MODEL), D_MODEL ** -0.5),
        'norm2_g': 1.0 + nrm(ks[20], (DEPTH, D_MODEL), 0.05),
        'w_grp': nrm(ks[21], (DEPTH, D_MODEL, N_GROUPS), D_MODEL ** -0.5),
        'b_grp': nrm(ks[22], (DEPTH, N_GROUPS), 0.01),
        'w_rt': nrm(ks[23], (DEPTH, D_MODEL, N_EXPERTS), D_MODEL ** -0.5),
        'b_rt': nrm(ks[24], (DEPTH, N_EXPERTS), 0.01),
        'w_e_gate': nrm(ks[25], (DEPTH, N_EXPERTS, D_MODEL, D_EXPERT), D_MODEL ** -0.5),
        'w_e_up': nrm(ks[26], (DEPTH, N_EXPERTS, D_MODEL, D_EXPERT), D_MODEL ** -0.5),
        'w_e_down': nrm(ks[27], (DEPTH, N_EXPERTS, D_EXPERT, D_MODEL), D_EXPERT ** -0.5),
        'norm_f_g': 1.0 + nrm(ks[28], (D_MODEL,), 0.05),
    }


def reference(x_prompt, x_sample, cache_kv, page_table, cache_win, state_gdn, state_conv,
              norm1_g, w_in, gdn_conv_w, gdn_a_log, gdn_dt_bias, gdn_norm_g,
              cmp_pos_wk, cmp_pos_wv, cmp_wk, cmp_wv,
              w_branch_a, w_branch_b, w_out, norm2_g,
              w_grp, b_grp, w_rt, b_rt, w_e_gate, w_e_up, w_e_down, norm_f_g):
    win_buf = cache_win.shape[2]
    hp, hs = x_prompt, x_sample
    outs = [[] for _ in range(8)]
    for l in range(DEPTH):
        lp = {
            'norm1_g': norm1_g[l], 'w_in': w_in[l],
            'gdn_conv_w': gdn_conv_w[l], 'gdn_a_log': gdn_a_log[l], 'gdn_dt_bias': gdn_dt_bias[l],
            'gdn_norm_g': gdn_norm_g[l],
            'cmp_pos_wk': cmp_pos_wk[l], 'cmp_pos_wv': cmp_pos_wv[l], 'cmp_wk': cmp_wk[l], 'cmp_wv': cmp_wv[l],
            'w_branch_a': w_branch_a[l], 'w_branch_b': w_branch_b[l], 'w_out': w_out[l], 'norm2_g': norm2_g[l],
            'w_grp': w_grp[l], 'b_grp': b_grp[l], 'w_rt': w_rt[l], 'b_rt': b_rt[l],
            'w_e_gate': w_e_gate[l], 'w_e_up': w_e_up[l], 'w_e_down': w_e_down[l],
        }
        hp, kvp, winp, sp, cp = prompt_layer(hp, lp, win_buf)
        hs, kvs, wins, ss, cs = sample_layer(hs, cache_kv[l], page_table, cache_win[l], state_gdn[l], state_conv[l], lp)
        for lst, val in zip(outs, (kvp, kvs, winp, wins, sp, ss, cp, cs)):
            lst.append(val)
    kv_prompt, kv_sample, win_prompt, win_sample, gdn_prompt, gdn_sample, conv_prompt, conv_sample = [jnp.stack(v) for v in outs]
    y_prompt = rmsnorm(hp, norm_f_g)
    y_sample = rmsnorm(hs, norm_f_g)
    return (y_prompt, y_sample, kv_prompt, kv_sample, win_prompt, win_sample, gdn_prompt, gdn_sample, conv_prompt, conv_sample)
```

```python
import functools
import math

import numpy as np
import jax
import jax.numpy as jnp
from jax import lax
from jax.experimental import pallas as pl
from jax.experimental.pallas import tpu as pltpu

F32 = jnp.float32
BF16 = jnp.bfloat16
HI = lax.Precision.HIGHEST

LANES = 128
SUBLANES = 8
VMEM_LIMIT = 56 * 1024 * 1024

D_MODEL = 1024
GDN_HEADS = 4
GDN_DK = 128
GDN_DV = 128
GDN_QK = GDN_HEADS * GDN_DK
GDN_V = GDN_HEADS * GDN_DV
GDN_CONV_DIM = 2 * GDN_QK + GDN_V
GDN_CONV = 4
GDN_CHUNK = 64
NSA_HEADS = 8
NSA_KV_HEADS = 2
NSA_GROUP = NSA_HEADS // NSA_KV_HEADS
NSA_HD = 64
NSA_Q = NSA_HEADS * NSA_HD
NSA_KV = NSA_KV_HEADS * NSA_HD
CMP_STRIDE = 16
CMP_LEN = 2 * CMP_STRIDE
SEL_BLOCK = 64
SEL_TOP = 16
WINDOW = 512
FORCE_SCORE = 1.0e4
N_GROUPS = 4
EXPERTS_PER_GROUP = 8
N_EXPERTS = N_GROUPS * EXPERTS_PER_GROUP
D_EXPERT = 256
RMS_EPS = 1e-6
NEG_INF = float("-inf")

_OFF_QKV = 0
_OFF_Z = _OFF_QKV + GDN_CONV_DIM
_OFF_B = _OFF_Z + GDN_V
_OFF_A = _OFF_B + GDN_HEADS
_OFF_Q = _OFF_A + GDN_HEADS
_OFF_KV = _OFF_Q + NSA_Q
_OFF_G = _OFF_KV + 6 * NSA_KV
_OFF_GA = _OFF_G + 3 * NSA_HEADS
_OFF_GB = _OFF_GA + D_MODEL

SM_B = 0
SM_A = 4
SM_G = 8

_PK = {}
_c = 0
for _n, _w in (("qkv", GDN_CONV_DIM), ("z", GDN_V), ("q8", NSA_HEADS * LANES), ("kv4", 4 * NSA_KV),
               ("kvw", 2 * NSA_KV), ("ga", D_MODEL), ("gb", D_MODEL), ("sm", LANES)):
    _PK[_n] = (_c, _w)
    _c += _w
PK_DIM = _c


def _cparams(sem):
    return pltpu.CompilerParams(dimension_semantics=sem, vmem_limit_bytes=VMEM_LIMIT)


def _nt(a, b, precision=None):
    return lax.dot_general(a, b, (((1,), (1,)), ((), ())), preferred_element_type=F32, precision=precision)


def _tn(a, b, precision=None):
    return lax.dot_general(a, b, (((0,), (0,)), ((), ())), preferred_element_type=F32, precision=precision)


def _mm(a, b, precision=None):
    return jnp.dot(a, b, preferred_element_type=F32, precision=precision)


def _sigmoid(x):
    return 1.0 / (1.0 + jnp.exp(-x))


def _silu(x):
    return x * _sigmoid(x)


def _iota(shape, dim):
    return lax.broadcasted_iota(jnp.int32, shape, dim)


def _pack_w_in(w_in):
    q = w_in[:, _OFF_Q:_OFF_Q + NSA_Q]
    zeros64 = jnp.zeros((D_MODEL, NSA_HD), w_in.dtype)
    q8 = []
    for h in range(NSA_HEADS):
        qh = q[:, h * NSA_HD:(h + 1) * NSA_HD]
        q8.append(jnp.concatenate([qh, zeros64] if h < NSA_GROUP else [zeros64, qh], axis=1))
    sm = jnp.concatenate([w_in[:, _OFF_B:_OFF_B + GDN_HEADS], w_in[:, _OFF_A:_OFF_A + GDN_HEADS],
                          w_in[:, _OFF_G:_OFF_G + 3 * NSA_HEADS],
                          jnp.zeros((D_MODEL, LANES - 2 * GDN_HEADS - 3 * NSA_HEADS), w_in.dtype)], axis=1)
    cols = [w_in[:, _OFF_QKV:_OFF_QKV + GDN_CONV_DIM], w_in[:, _OFF_Z:_OFF_Z + GDN_V]] + q8 + [
        w_in[:, _OFF_KV:_OFF_KV + 4 * NSA_KV], w_in[:, _OFF_KV + 4 * NSA_KV:_OFF_KV + 6 * NSA_KV],
        w_in[:, _OFF_GA:_OFF_GA + D_MODEL], w_in[:, _OFF_GB:_OFF_GB + D_MODEL], sm]
    return jnp.concatenate(cols, axis=1).astype(BF16)


def _proj_kernel(x_ref, g_ref, w_ref, qkv_ref, z_ref, q8_ref, kv4_ref, kvw_ref, ga_ref, gb_ref, sm_ref,
                 selbf_ref, winbf_ref):
    x = x_ref[...]
    xn = (x * lax.rsqrt(jnp.mean(x * x, axis=-1, keepdims=True) + RMS_EPS) * g_ref[...]).astype(BF16)

    def seg(name):
        a, w = _PK[name]
        return _mm(xn, w_ref[:, a:a + w])

    qkv_ref[...] = seg("qkv")
    z_ref[...] = seg("z")
    q8_ref[...] = (seg("q8") * (NSA_HD ** -0.5)).astype(BF16)
    kv4 = seg("kv4")
    kv4_ref[...] = kv4
    selbf_ref[...] = kv4[:, 2 * NSA_KV:].astype(BF16)
    kvw = seg("kvw")
    kvw_ref[...] = kvw
    winbf_ref[...] = kvw.astype(BF16)
    ga_ref[...] = seg("ga")
    gb_ref[...] = seg("gb")
    sm_ref[...] = seg("sm")


def _project(x2d, norm_g, w_pk):
    n = x2d.shape[0]
    tm = min(256, n)
    widths = [("qkv", F32), ("z", F32), ("q8", BF16), ("kv4", F32), ("kvw", F32), ("ga", F32), ("gb", F32),
              ("sm", F32)]
    out_shape = [jax.ShapeDtypeStruct((n, _PK[k][1]), dt) for k, dt in widths]
    out_shape += [jax.ShapeDtypeStruct((n, 2 * NSA_KV), BF16), jax.ShapeDtypeStruct((n, 2 * NSA_KV), BF16)]
    out_specs = [pl.BlockSpec((tm, s.shape[1]), lambda i: (i, 0)) for s in out_shape]
    return pl.pallas_call(
        _proj_kernel,
        grid=(n // tm,),
        in_specs=[pl.BlockSpec((tm, D_MODEL), lambda i: (i, 0)),
                  pl.BlockSpec((1, D_MODEL), lambda i: (0, 0)),
                  pl.BlockSpec((D_MODEL, PK_DIM), lambda i: (0, 0), pipeline_mode=pl.Buffered(1))],
        out_specs=out_specs,
        out_shape=out_shape,
        compiler_params=_cparams(("parallel",)),
        name="proj",
    )(x2d, norm_g.reshape(1, D_MODEL), w_pk)


def _unit_lower_inverse(a, c):
    r = _iota((c, c), 0)
    col = _iota((c, c), 1)
    eye = (r == col).astype(F32)
    n1 = jnp.where((r >> 3) == (col >> 3), -a, 0.0)
    n2 = _mm(n1, n1, HI)
    n4 = _mm(n2, n2, HI)
    t = _mm(_mm(eye + n1, eye + n2, HI), eye + n4, HI)
    s = SUBLANES
    while s < c:
        sh = s.bit_length() - 1
        off = ((r >> (sh + 1)) == (col >> (sh + 1))) & ((r >> sh) != (col >> sh))
        t = t - _mm(_mm(t, jnp.where(off, a, 0.0), HI), t, HI)
        s *= 2
    return t


def _gdn_kernel(qkv_ref, z_ref, sm_ref, cbuf_ref, s0_ref, cw_ref, alog_ref, dtb_ref, ng_ref,
                y_ref, snew_ref, cnew_ref, ext_ref, st_ref, *, chunk):
    c = chunk
    ci = pl.program_id(1)

    @pl.when(ci == 0)
    def _():
        ext_ref[0:SUBLANES, :] = cbuf_ref[0]
        st_ref[...] = s0_ref[0]

    u = qkv_ref[...]
    ext_ref[SUBLANES:SUBLANES + c, :] = u
    base = SUBLANES - (GDN_CONV - 1)
    conv = cw_ref[0:1, :] * ext_ref[base:base + c, :]
    for i in range(1, GDN_CONV - 1):
        conv = conv + cw_ref[i:i + 1, :] * ext_ref[base + i:base + i + c, :]
    conv = conv + cw_ref[GDN_CONV - 1:GDN_CONV, :] * u
    halo = ext_ref[c:c + SUBLANES, :]
    ext_ref[0:SUBLANES, :] = halo
    cnew_ref[0] = halo
    qkv = _silu(conv)

    sm = sm_ref[...]
    beta_all = _sigmoid(sm)
    xa = sm + dtb_ref[...]
    softplus = jnp.maximum(xa, 0.0) + jnp.log(1.0 + jnp.exp(-jnp.abs(xa)))
    g_all = -jnp.exp(alog_ref[...]) * softplus

    r = _iota((c, c), 0)
    col = _iota((c, c), 1)
    tri_incl = r >= col
    tri_strict = r > col
    gcum_all = _mm(tri_incl.astype(F32), g_all, HI)
    pick = (_iota((SUBLANES, LANES), 1) == _iota((SUBLANES, LANES), 0) + SM_A).astype(F32)
    gcum_rows = _nt(pick, gcum_all, HI)

    ng = ng_ref[...]
    z = z_ref[...]
    for h in range(GDN_HEADS):
        q = qkv[:, h * GDN_DK:(h + 1) * GDN_DK]
        k = qkv[:, GDN_QK + h * GDN_DK:GDN_QK + (h + 1) * GDN_DK]
        v = qkv[:, 2 * GDN_QK + h * GDN_DV:2 * GDN_QK + (h + 1) * GDN_DV]
        q = q * lax.rsqrt(jnp.sum(q * q, axis=-1, keepdims=True) + RMS_EPS) * (GDN_DK ** -0.5)
        k = k * lax.rsqrt(jnp.sum(k * k, axis=-1, keepdims=True) + RMS_EPS)
        beta = beta_all[:, SM_B + h:SM_B + h + 1]
        gc_col = gcum_all[:, SM_A + h:SM_A + h + 1]
        gc_row = gcum_rows[h:h + 1, :]
        gc_last = gcum_all[c - 1:c, SM_A + h:SM_A + h + 1]
        decay = jnp.exp(jnp.where(tri_incl, gc_col - gc_row, NEG_INF))
        kb = k * beta
        k_bf = k.astype(BF16)
        a_strict = jnp.where(tri_strict, _nt(kb.astype(BF16), k_bf) * decay, 0.0)
        t_inv = _unit_lower_inverse(a_strict, c)
        eg = jnp.exp(gc_col)
        u_coef = _mm(t_inv, v * beta, HI)
        w_coef = _mm(t_inv, kb * eg, HI)
        qk = _nt(q.astype(BF16), k_bf) * decay
        q_dec = q * eg
        k_tail = k * jnp.exp(gc_last - gc_col)
        g_last = jnp.exp(gc_last)
        s_old = st_ref[h]
        s_bf = s_old.astype(BF16)
        uu = u_coef - _mm(w_coef.astype(BF16), s_bf)
        uu_bf = uu.astype(BF16)
        o = _mm(q_dec.astype(BF16), s_bf) + _mm(qk.astype(BF16), uu_bf)
        st_ref[h] = s_old * g_last + _tn(k_tail.astype(BF16), uu_bf)
        on = o * lax.rsqrt(jnp.mean(o * o, axis=-1, keepdims=True) + RMS_EPS) * ng
        y_ref[:, h * GDN_DV:(h + 1) * GDN_DV] = on * _silu(z[:, h * GDN_DV:(h + 1) * GDN_DV])

    @pl.when(ci == pl.num_programs(1) - 1)
    def _():
        snew_ref[0] = st_ref[...]


def _gdn(qkv, z, sm, conv_buf, s0, conv_w, a_log, dt_bias, norm_g, b, l):
    c = math.gcd(l, GDN_CHUNK)
    nc = l // c
    cbuf8 = jnp.pad(conv_buf, ((0, 0), (SUBLANES - (GDN_CONV - 1), 0), (0, 0)))
    pad_a = (SM_A, LANES - SM_A - GDN_HEADS)
    alog_row = jnp.pad(a_log, pad_a).reshape(1, LANES)
    dtb_row = jnp.pad(dt_bias, pad_a).reshape(1, LANES)
    row = lambda bi, ci: (bi * nc + ci, 0)
    fixed = lambda bi, ci: (0, 0)
    y, s_new, c_new = pl.pallas_call(
        functools.partial(_gdn_kernel, chunk=c),
        grid=(b, nc),
        in_specs=[pl.BlockSpec((c, GDN_CONV_DIM), row),
                  pl.BlockSpec((c, GDN_V), row),
                  pl.BlockSpec((c, LANES), row),
                  pl.BlockSpec((1, SUBLANES, GDN_CONV_DIM), lambda bi, ci: (bi, 0, 0)),
                  pl.BlockSpec((1, GDN_HEADS, GDN_DK, GDN_DV), lambda bi, ci: (bi, 0, 0, 0)),
                  pl.BlockSpec((GDN_CONV, GDN_CONV_DIM), fixed),
                  pl.BlockSpec((1, LANES), fixed),
                  pl.BlockSpec((1, LANES), fixed),
                  pl.BlockSpec((1, GDN_DV), fixed)],
        out_specs=[pl.BlockSpec((c, GDN_V), row),
                   pl.BlockSpec((1, GDN_HEADS, GDN_DK, GDN_DV), lambda bi, ci: (bi, 0, 0, 0)),
                   pl.BlockSpec((1, SUBLANES, GDN_CONV_DIM), lambda bi, ci: (bi, 0, 0))],
        out_shape=[jax.ShapeDtypeStruct((b * l, GDN_V), F32),
                   jax.ShapeDtypeStruct((b, GDN_HEADS, GDN_DK, GDN_DV), F32),
                   jax.ShapeDtypeStruct((b, SUBLANES, GDN_CONV_DIM), F32)],
        scratch_shapes=[pltpu.VMEM((c + SUBLANES, GDN_CONV_DIM), F32),
                        pltpu.VMEM((GDN_HEADS, GDN_DK, GDN_DV), F32)],
        compiler_params=_cparams(("parallel", "arbitrary")),
        name="gdn",
    )(qkv, z, sm, cbuf8, s0, conv_w, alog_row, dtb_row, norm_g.reshape(1, GDN_DV))
    return y, s_new, c_new[:, SUBLANES - (GDN_CONV - 1):]


def _pool_kernel(*refs, n_in):
    refs = refs[len(refs) - 2 * n_in - 3:]
    x_refs, pw_ref, head_ref, tail_ref = refs[:2 * n_in], refs[2 * n_in], refs[2 * n_in + 1], refs[2 * n_in + 2]
    rows = x_refs[0].shape[0]
    n = rows // CMP_STRIDE
    for j, x_ref in enumerate(x_refs):
        half = slice((j % 2) * NSA_KV, (j % 2 + 1) * NSA_KV)
        head = None
        tail = None
        for p in range(CMP_STRIDE):
            xr = x_ref[pl.ds(p, n, stride=CMP_STRIDE), :]
            hp = xr * pw_ref[p:p + 1, half]
            tp = xr * pw_ref[CMP_STRIDE + p:CMP_STRIDE + p + 1, half]
            head = hp if head is None else head + hp
            tail = tp if tail is None else tail + tp
        head_ref[(j // 2) * n:(j // 2 + 1) * n, half] = head
        tail_ref[(j // 2) * n:(j // 2 + 1) * n, half] = tail


def _pos_weights(pos_wk, pos_wv):
    return jnp.concatenate([jnp.broadcast_to(pos_wk[:, None], (CMP_LEN, NSA_KV)),
                            jnp.broadcast_to(pos_wv[:, None], (CMP_LEN, NSA_KV))], axis=1).astype(F32)


def _pool_rows(kv4, pw):
    n = kv4.shape[0]
    r = min(2048, n)
    w = 2 * NSA_KV
    return pl.pallas_call(
        functools.partial(_pool_kernel, n_in=1),
        grid=(n // r,),
        in_specs=[pl.BlockSpec((r, NSA_KV), lambda i: (i, 0)),
                  pl.BlockSpec((r, NSA_KV), lambda i: (i, 1)),
                  pl.BlockSpec((CMP_LEN, w), lambda i: (0, 0))],
        out_specs=[pl.BlockSpec((r // CMP_STRIDE, w), lambda i: (i, 0))] * 2,
        out_shape=[jax.ShapeDtypeStruct((n // CMP_STRIDE, w), F32)] * 2,
        compiler_params=_cparams(("parallel",)),
        name="pool_rows",
    )(kv4, kv4, pw)


def _pool_pages(cache3, page_table, pw, pages_per_step):
    b, n_pages = page_table.shape
    page = cache3.shape[1]
    pg = pages_per_step
    w = 2 * NSA_KV
    n_out = pg * page // CMP_STRIDE
    steps = n_pages // pg

    def page_map(k, half):
        return lambda bi, j, pt: (pt[bi, j * pg + k], 0, half)

    grid_spec = pltpu.PrefetchScalarGridSpec(
        num_scalar_prefetch=1,
        grid=(b, steps),
        in_specs=[pl.BlockSpec((None, page, NSA_KV), page_map(k, half)) for k in range(pg) for half in range(2)]
        + [pl.BlockSpec((CMP_LEN, w), lambda bi, j, pt: (0, 0))],
        out_specs=[pl.BlockSpec((n_out, w), lambda bi, j, pt: (bi * steps + j, 0))] * 2,
    )
    return pl.pallas_call(
        functools.partial(_pool_kernel, n_in=pg),
        grid_spec=grid_spec,
        out_shape=[jax.ShapeDtypeStruct((b * steps * n_out, w), F32)] * 2,
        compiler_params=_cparams(("parallel", "arbitrary")),
        name="pool_pages",
    )(page_table, *([cache3] * (2 * pg)), pw)


def _cmp_kernel(head_ref, tail_ref, wk_ref, wv_ref, kc_ref, vc_ref):
    n = head_ref.shape[0]
    blocks = head_ref[...] + pltpu.roll(tail_ref[...], n - 1, 0)
    kc_ref[...] = _mm(blocks[:, :NSA_KV].astype(BF16), wk_ref[...]).astype(BF16)
    vc_ref[...] = _mm(blocks[:, NSA_KV:].astype(BF16), wv_ref[...]).astype(BF16)


def _block_diag2(w):
    z = jnp.zeros_like(w)
    return jnp.concatenate([jnp.concatenate([w, z], axis=1), jnp.concatenate([z, w], axis=1)], axis=0).astype(BF16)


def _compress(head, tail, cmp_wk, cmp_wv, b):
    n_ch = head.shape[0] // b
    w = 2 * NSA_KV
    return pl.pallas_call(
        _cmp_kernel,
        grid=(b,),
        in_specs=[pl.BlockSpec((n_ch, w), lambda i: (i, 0))] * 2
        + [pl.BlockSpec((NSA_KV, NSA_KV), lambda i: (0, 0))] * 2,
        out_specs=[pl.BlockSpec((n_ch, NSA_KV), lambda i: (i, 0))] * 2,
        out_shape=[jax.ShapeDtypeStruct((b * n_ch, NSA_KV), BF16)] * 2,
        compiler_params=_cparams(("parallel",)),
        name="compress",
    )(head, tail, _block_diag2(cmp_wk), _block_diag2(cmp_wv))


def _overlap(n_ch, nsp):
    i = jnp.arange(n_ch)[:, None]
    j = jnp.arange(nsp)[None, :]
    lo = jnp.maximum(i * CMP_STRIDE, j * SEL_BLOCK)
    hi = jnp.minimum(i * CMP_STRIDE + CMP_LEN, (j + 1) * SEL_BLOCK)
    ov = jnp.maximum(hi - lo, 0).astype(F32) / CMP_LEN
    return jnp.where(i < n_ch - 1, ov, 0.0)


def _slope(h):
    return 2.0 ** (-(h + 1))


def _softmax_rows(s, mask):
    s = jnp.where(mask, s, NEG_INF)
    m = jnp.max(s, axis=-1, keepdims=True)
    m = jnp.where(m == NEG_INF, 0.0, m)
    e = jnp.where(mask, jnp.exp(s - m), 0.0)
    return e, jnp.maximum(jnp.sum(e, axis=-1, keepdims=True), 1e-30)


def _top_blocks(imp, qpos, ns_lanes):
    blk = _iota(imp.shape, 1)
    cur = qpos >> 6
    forced = (blk == 0) | (blk == cur) | (blk == cur - 1)
    score = jnp.where(forced, FORCE_SCORE, imp)
    work = jnp.where(blk <= cur, score, NEG_INF)
    sel = jnp.zeros(imp.shape, F32)
    for _ in range(SEL_TOP):
        m = jnp.max(work, axis=-1, keepdims=True)
        cand = jnp.where((work == m) & (m > NEG_INF), blk, ns_lanes)
        first = jnp.min(cand, axis=-1, keepdims=True)
        hit = blk == first
        sel = jnp.where(hit, 1.0, sel)
        work = jnp.where(hit, NEG_INF, work)
    return sel


def _cmp_attend(q8, kc, vc, ov, qpos, tq, ns_lanes):
    n_ch = kc.shape[0]
    s_all = _nt(q8, kc)
    blk_end = _iota((1, n_ch), 1) * CMP_STRIDE + (CMP_LEN - 1)
    dist = qpos - blk_end
    distf = dist.astype(F32)
    valid = dist >= 0
    outs = []
    sels = []
    for g in range(NSA_KV_HEADS):
        psum = None
        for hh in range(NSA_GROUP):
            h = g * NSA_GROUP + hh
            s = s_all[h * tq:(h + 1) * tq] - _slope(h) * distf
            e, den = _softmax_rows(s, valid)
            p = e / den
            outs.append(_mm(p.astype(BF16), vc))
            psum = p if psum is None else psum + p
        imp = _mm(psum, ov, HI)
        sels.append(_top_blocks(imp, qpos, ns_lanes))
    return outs, sels


def _win_attend(q8, kw, vw, qpos, kwpos, tq):
    s_all = _nt(q8, kw)
    dist = qpos - kwpos
    distf = dist.astype(F32)
    mask = (dist >= 0) & (dist <= WINDOW) & (kwpos >= 0)
    outs = []
    for h in range(NSA_HEADS):
        s = s_all[h * tq:(h + 1) * tq] - _slope(h) * distf
        e, den = _softmax_rows(s, mask)
        outs.append(_mm((e / den).astype(BF16), vw))
    return outs


def _sel_tile(q8, k, v, sels_bf, qpos, kpos, m_ref, l_ref, acc_ref, tq):
    tk = k.shape[0]
    nsp = sels_bf[0].shape[1]
    s_all = _nt(q8, k)
    dist = qpos - kpos
    distf = dist.astype(F32)
    expand = jnp.where(_iota((nsp, tk), 0) == (kpos >> 6), 1.0, 0.0).astype(BF16)
    for g in range(NSA_KV_HEADS):
        allow = (_mm(sels_bf[g], expand) > 0.5) & (dist >= 0)
        for hh in range(NSA_GROUP):
            h = g * NSA_GROUP + hh
            rows = slice(h * tq, (h + 1) * tq)
            s = jnp.where(allow, s_all[rows] - _slope(h) * distf, NEG_INF)
            m_old = m_ref[rows]
            m_new = jnp.maximum(m_old, jnp.max(s, axis=-1, keepdims=True))
            m_use = jnp.where(m_new == NEG_INF, 0.0, m_new)
            alpha = jnp.exp(m_old - m_use)
            e = jnp.exp(s - m_use)
            l_ref[rows] = alpha * l_ref[rows] + jnp.sum(e, axis=-1, keepdims=True)
            acc_ref[rows] = alpha * acc_ref[rows] + _mm(e.astype(BF16), v)
            m_ref[rows] = m_new


def _merge_heads(o_c, o_s, o_w, gates, tq):
    lane = _iota((tq, LANES), 1)
    mixed = []
    for h in range(NSA_HEADS):
        g0 = gates[:, SM_G + 3 * h:SM_G + 3 * h + 1]
        g1 = gates[:, SM_G + 3 * h + 1:SM_G + 3 * h + 2]
        g2 = gates[:, SM_G + 3 * h + 2:SM_G + 3 * h + 3]
        mixed.append(g0 * o_c[h] + g1 * o_s[h] + g2 * o_w[h])
    cols = []
    for p in range(NSA_HEADS // 2):
        a, b = mixed[2 * p], mixed[2 * p + 1]
        if 2 * p < NSA_GROUP:
            cols.append(jnp.where(lane < NSA_HD, a, pltpu.roll(b, NSA_HD, 1)))
        else:
            cols.append(jnp.where(lane < NSA_HD, pltpu.roll(a, NSA_HD, 1), b))
    return jnp.concatenate(cols, axis=1)


def _nsa_prompt_kernel(q8_ref, sm_ref, kc_ref, vc_ref, ks_ref, vs_ref, kw_ref, vw_ref, ov_ref, y_ref,
                       m_ref, l_ref, acc_ref, *, tq, tk):
    i = pl.program_id(1)
    start = i * tq
    nsp = ov_ref.shape[1]
    q8 = jnp.concatenate([q8_ref[:, h * LANES:(h + 1) * LANES] for h in range(NSA_HEADS)], axis=0)
    qpos = start + _iota((tq, 1), 0)

    o_c, sels = _cmp_attend(q8, kc_ref[...], vc_ref[...], ov_ref[...], qpos, tq, nsp)
    sels_bf = [s.astype(BF16) for s in sels]

    m_ref[...] = jnp.full(m_ref.shape, NEG_INF, F32)
    l_ref[...] = jnp.zeros(l_ref.shape, F32)
    acc_ref[...] = jnp.zeros(acc_ref.shape, F32)
    n_tiles = (start + tq + tk - 1) // tk

    def body(t, carry):
        off = pl.multiple_of(t * tk, tk)
        kpos = off + _iota((1, tk), 1)
        _sel_tile(q8, ks_ref[pl.ds(off, tk), :], vs_ref[pl.ds(off, tk), :], sels_bf, qpos, kpos,
                  m_ref, l_ref, acc_ref, tq)
        return carry

    lax.fori_loop(0, n_tiles, body, 0)
    o_s = [acc_ref[h * tq:(h + 1) * tq] / jnp.maximum(l_ref[h * tq:(h + 1) * tq], 1e-30) for h in range(NSA_HEADS)]

    wstart = pl.multiple_of(start, tq)
    kwpos = start - WINDOW + _iota((1, tq + WINDOW), 1)
    o_w = _win_attend(q8, kw_ref[pl.ds(wstart, tq + WINDOW), :], vw_ref[pl.ds(wstart, tq + WINDOW), :],
                      qpos, kwpos, tq)
    y_ref[...] = _merge_heads(o_c, o_s, o_w, _sigmoid(sm_ref[...]), tq)


def _nsa_prompt(q8, sm, kc, vc, selbf, winbf, b, l):
    tq = 128
    tk = 256
    n_ch = l // CMP_STRIDE
    ns = -(-l // SEL_BLOCK)
    nsp = -(-ns // LANES) * LANES
    nq = l // tq
    lp = l + WINDOW
    winp = jnp.pad(winbf.reshape(b, l, 2 * NSA_KV), ((0, 0), (WINDOW, 0), (0, 0))).reshape(b * lp, 2 * NSA_KV)
    ov = _overlap(n_ch, nsp)
    row = lambda bi, i: (bi * nq + i, 0)
    return pl.pallas_call(
        functools.partial(_nsa_prompt_kernel, tq=tq, tk=tk),
        grid=(b, nq),
        in_specs=[pl.BlockSpec((tq, NSA_HEADS * LANES), row),
                  pl.BlockSpec((tq, LANES), row),
                  pl.BlockSpec((n_ch, NSA_KV), lambda bi, i: (bi, 0)),
                  pl.BlockSpec((n_ch, NSA_KV), lambda bi, i: (bi, 0)),
                  pl.BlockSpec((l, NSA_KV), lambda bi, i: (bi, 0)),
                  pl.BlockSpec((l, NSA_KV), lambda bi, i: (bi, 1)),
                  pl.BlockSpec((lp, NSA_KV), lambda bi, i: (bi, 0)),
                  pl.BlockSpec((lp, NSA_KV), lambda bi, i: (bi, 1)),
                  pl.BlockSpec((n_ch, nsp), lambda bi, i: (0, 0))],
        out_specs=pl.BlockSpec((tq, NSA_Q), row),
        out_shape=jax.ShapeDtypeStruct((b * l, NSA_Q), F32),
        scratch_shapes=[pltpu.VMEM((NSA_HEADS * tq, 1), F32),
                        pltpu.VMEM((NSA_HEADS * tq, 1), F32),
                        pltpu.VMEM((NSA_HEADS * tq, NSA_KV), F32)],
        compiler_params=_cparams(("parallel", "arbitrary")),
        name="nsa_prompt",
    )(q8, sm, kc, vc, selbf, selbf, winp, winp, ov)


def _nsa_sample_kernel(*refs, pg, page, tq, past_len):
    pt_ref = refs[0]
    del pt_ref
    page_refs = refs[1:1 + pg]
    (q8_ref, sm_ref, kc_ref, vc_ref, knew_ref, kw_ref, ov_ref, y_ref,
     m_ref, l_ref, acc_ref, sel_ref, oc_ref) = refs[1 + pg:]
    j = pl.program_id(1)
    nsp = ov_ref.shape[1]
    q8 = jnp.concatenate([q8_ref[:, h * LANES:(h + 1) * LANES] for h in range(NSA_HEADS)], axis=0)
    qpos = past_len + _iota((tq, 1), 0)

    @pl.when(j == 0)
    def _():
        o_c, sels = _cmp_attend(q8, kc_ref[...], vc_ref[...], ov_ref[...], qpos, tq, nsp)
        for h in range(NSA_HEADS):
            oc_ref[h * tq:(h + 1) * tq] = o_c[h]
        for g in range(NSA_KV_HEADS):
            sel_ref[g] = sels[g]
        m_ref[...] = jnp.full(m_ref.shape, NEG_INF, F32)
        l_ref[...] = jnp.zeros(l_ref.shape, F32)
        acc_ref[...] = jnp.zeros(acc_ref.shape, F32)

    sels_bf = [sel_ref[g].astype(BF16) for g in range(NSA_KV_HEADS)]
    for k in range(pg):
        kv = page_refs[k][...]
        kpos = (j * pg + k) * page + _iota((1, page), 1)
        _sel_tile(q8, kv[:, :NSA_KV].astype(BF16), kv[:, NSA_KV:].astype(BF16), sels_bf, qpos, kpos,
                  m_ref, l_ref, acc_ref, tq)

    @pl.when(j == pl.num_programs(1) - 1)
    def _():
        kn = knew_ref[...]
        kpos = past_len + _iota((1, SEL_BLOCK), 1)
        _sel_tile(q8, kn[:, :NSA_KV], kn[:, NSA_KV:], sels_bf, qpos, kpos, m_ref, l_ref, acc_ref, tq)
        o_s = [acc_ref[h * tq:(h + 1) * tq] / jnp.maximum(l_ref[h * tq:(h + 1) * tq], 1e-30)
               for h in range(NSA_HEADS)]
        kw = kw_ref[...]
        wrows = kw.shape[0]
        kwpos = past_len - WINDOW + _iota((1, wrows), 1)
        kwpos = jnp.where(kwpos < past_len + tq, kwpos, -1)
        o_w = _win_attend(q8, kw[:, :NSA_KV], kw[:, NSA_KV:], qpos, kwpos, tq)
        o_c = [oc_ref[h * tq:(h + 1) * tq] for h in range(NSA_HEADS)]
        y_ref[...] = _merge_heads(o_c, o_s, o_w, _sigmoid(sm_ref[...]), tq)


def _nsa_sample(q8, sm, kc, vc, selbf_new, win_full_bf, cache3, page_table, b, l, win_buf):
    n_pages = page_table.shape[1]
    page = cache3.shape[1]
    past_len = n_pages * page
    assert win_buf == WINDOW and l <= SEL_BLOCK and past_len % SEL_BLOCK == 0
    pg = 8
    steps = n_pages // pg
    n_ch = past_len // CMP_STRIDE
    ns = -(-(past_len + l) // SEL_BLOCK)
    nsp = -(-ns // LANES) * LANES
    w = 2 * NSA_KV
    ov = _overlap(n_ch, nsp)
    knew = jnp.pad(selbf_new.reshape(b, l, w), ((0, 0), (0, SEL_BLOCK - l), (0, 0))).reshape(b * SEL_BLOCK, w)
    wrows = -(-(win_buf + l) // LANES) * LANES
    kwp = jnp.pad(win_full_bf, ((0, 0), (0, wrows - win_buf - l), (0, 0))).reshape(b * wrows, w)

    def page_map(k):
        return lambda bi, j, pt: (pt[bi, j * pg + k], 0, 1)

    per_b = lambda bi, j, pt: (bi, 0)
    grid_spec = pltpu.PrefetchScalarGridSpec(
        num_scalar_prefetch=1,
        grid=(b, steps),
        in_specs=[pl.BlockSpec((None, page, w), page_map(k)) for k in range(pg)]
        + [pl.BlockSpec((None, l, NSA_HEADS * LANES), lambda bi, j, pt: (bi, 0, 0)),
           pl.BlockSpec((l, LANES), per_b),
           pl.BlockSpec((n_ch, NSA_KV), per_b),
           pl.BlockSpec((n_ch, NSA_KV), per_b),
           pl.BlockSpec((SEL_BLOCK, w), per_b),
           pl.BlockSpec((wrows, w), per_b),
           pl.BlockSpec((n_ch, nsp), lambda bi, j, pt: (0, 0))],
        out_specs=pl.BlockSpec((l, NSA_Q), per_b),
        scratch_shapes=[pltpu.VMEM((NSA_HEADS * l, 1), F32),
                        pltpu.VMEM((NSA_HEADS * l, 1), F32),
                        pltpu.VMEM((NSA_HEADS * l, NSA_KV), F32),
                        pltpu.VMEM((NSA_KV_HEADS, l, nsp), F32),
                        pltpu.VMEM((NSA_HEADS * l, NSA_KV), F32)],
    )
    return pl.pallas_call(
        functools.partial(_nsa_sample_kernel, pg=pg, page=page, tq=l, past_len=past_len),
        grid_spec=grid_spec,
        out_shape=jax.ShapeDtypeStruct((b * l, NSA_Q), F32),
        compiler_params=_cparams(("parallel", "arbitrary")),
        name="nsa_sample",
    )(page_table, *([cache3] * pg), q8.reshape(b, l, NSA_HEADS * LANES), sm, kc, vc, knew, kwp, ov)


def _tail_kernel(x_ref, ya_ref, yb_ref, ga_ref, gb_ref, wa_ref, wb_ref, wo_ref, n2_ref, wr_ref, br_ref,
                 h_ref, hn_ref, cmb_ref):
    ma = _mm(ya_ref[...].astype(BF16), wa_ref[...])
    mb = _mm(yb_ref[...].astype(BF16), wb_ref[...])
    m = _sigmoid(ga_ref[...]) * ma + _sigmoid(gb_ref[...]) * mb
    h = x_ref[...] + _mm(m.astype(BF16), wo_ref[...])
    h_ref[...] = h
    hn = h * lax.rsqrt(jnp.mean(h * h, axis=-1, keepdims=True) + RMS_EPS) * n2_ref[...]
    hn_ref[...] = hn.astype(BF16)

    logit = _mm(hn, wr_ref[...], HI) + br_ref[...]
    lane = _iota(logit.shape, 1)
    is_grp = (lane >= N_EXPERTS) & (lane < N_EXPERTS + N_GROUPS)
    gl = jnp.where(is_grp, logit, NEG_INF)
    gmax = jnp.max(gl, axis=-1, keepdims=True)
    gidx = jnp.min(jnp.where(gl == gmax, lane, LANES), axis=-1, keepdims=True) - N_EXPERTS
    p_grp = 1.0 / jnp.sum(jnp.exp(gl - gmax), axis=-1, keepdims=True)
    el = jnp.where((lane >> 3) == gidx, logit, NEG_INF)
    t1 = jnp.max(el, axis=-1, keepdims=True)
    i1 = jnp.min(jnp.where(el == t1, lane, LANES), axis=-1, keepdims=True)
    el2 = jnp.where(lane == i1, NEG_INF, el)
    t2 = jnp.max(el2, axis=-1, keepdims=True)
    i2 = jnp.min(jnp.where(el2 == t2, lane, LANES), axis=-1, keepdims=True)
    e2 = jnp.exp(t2 - t1)
    w1 = p_grp / (1.0 + e2)
    w2 = p_grp * e2 / (1.0 + e2)
    cmb_ref[...] = jnp.where(lane == i1, w1, jnp.where(lane == i2, w2, 0.0))


def _tail(x2d, ya, yb, ga, gb, wa, wb, wo, norm2_g, w_route, b_route):
    n = x2d.shape[0]
    tm = min(256, n)
    row = lambda i: (i, 0)
    fixed = lambda i: (0, 0)
    return pl.pallas_call(
        _tail_kernel,
        grid=(n // tm,),
        in_specs=[pl.BlockSpec((tm, D_MODEL), row),
                  pl.BlockSpec((tm, GDN_V), row),
                  pl.BlockSpec((tm, NSA_Q), row),
                  pl.BlockSpec((tm, D_MODEL), row),
                  pl.BlockSpec((tm, D_MODEL), row),
                  pl.BlockSpec((GDN_V, D_MODEL), fixed),
                  pl.BlockSpec((NSA_Q, D_MODEL), fixed),
                  pl.BlockSpec((D_MODEL, D_MODEL), fixed),
                  pl.BlockSpec((1, D_MODEL), fixed),
                  pl.BlockSpec((D_MODEL, LANES), fixed),
                  pl.BlockSpec((1, LANES), fixed)],
        out_specs=[pl.BlockSpec((tm, D_MODEL), row),
                   pl.BlockSpec((tm, D_MODEL), row),
                   pl.BlockSpec((tm, LANES), row)],
        out_shape=[jax.ShapeDtypeStruct((n, D_MODEL), F32),
                   jax.ShapeDtypeStruct((n, D_MODEL), BF16),
                   jax.ShapeDtypeStruct((n, LANES), F32)],
        compiler_params=_cparams(("parallel",)),
        name="tail",
    )(x2d, ya, yb, ga, gb, wa, wb, wo, norm2_g.reshape(1, D_MODEL), w_route, b_route)


def _moe_kernel(h_ref, hn_ref, cmb_ref, wg_ref, wu_ref, wd_ref, nf_ref, y_ref, acc_ref):
    e = pl.program_id(1)

    @pl.when(e == 0)
    def _():
        acc_ref[...] = jnp.zeros(acc_ref.shape, F32)

    hn = hn_ref[...]
    cmb = cmb_ref[...]
    wgt = jnp.sum(jnp.where(_iota(cmb.shape, 1) == e, cmb, 0.0), axis=-1, keepdims=True)
    act = _silu(_mm(hn, wg_ref[0])) * _mm(hn, wu_ref[0]) * wgt
    acc_ref[...] += _mm(act.astype(BF16), wd_ref[0])

    @pl.when(e == pl.num_programs(1) - 1)
    def _():
        t = h_ref[...] + acc_ref[...]
        y_ref[...] = t * lax.rsqrt(jnp.mean(t * t, axis=-1, keepdims=True) + RMS_EPS) * nf_ref[...]


def _moe(h, hn, cmb, wg, wu, wd, norm_f_g):
    n = h.shape[0]
    tm = min(1024, n)
    row = lambda i, e: (i, 0)
    return pl.pallas_call(
        _moe_kernel,
        grid=(n // tm, N_EXPERTS),
        in_specs=[pl.BlockSpec((tm, D_MODEL), row),
                  pl.BlockSpec((tm, D_MODEL), row),
                  pl.BlockSpec((tm, LANES), row),
                  pl.BlockSpec((1, D_MODEL, D_EXPERT), lambda i, e: (e, 0, 0)),
                  pl.BlockSpec((1, D_MODEL, D_EXPERT), lambda i, e: (e, 0, 0)),
                  pl.BlockSpec((1, D_EXPERT, D_MODEL), lambda i, e: (e, 0, 0)),
                  pl.BlockSpec((1, D_MODEL), lambda i, e: (0, 0))],
        out_specs=pl.BlockSpec((tm, D_MODEL), row),
        out_shape=jax.ShapeDtypeStruct((n, D_MODEL), F32),
        scratch_shapes=[pltpu.VMEM((tm, D_MODEL), F32)],
        compiler_params=_cparams(("parallel", "arbitrary")),
        name="moe",
    )(h, hn, cmb, wg, wu, wd, norm_f_g.reshape(1, D_MODEL))


def _route_weights(w_grp, b_grp, w_rt, b_rt):
    pad = LANES - N_EXPERTS - N_GROUPS
    w = jnp.concatenate([w_rt, w_grp, jnp.zeros((D_MODEL, pad), F32)], axis=1)
    bias = jnp.concatenate([b_rt, b_grp, jnp.zeros((pad,), F32)]).reshape(1, LANES)
    return w, bias


def _finish(x2d, ya, yb, ga, gb, wts):
    h, hn, cmb = _tail(x2d, ya, yb, ga, gb, wts["wa"], wts["wb"], wts["wo"], wts["norm2_g"],
                       wts["w_route"], wts["b_route"])
    return _moe(h, hn, cmb, wts["wg"], wts["wu"], wts["wd"], wts["norm_f_g"])


def _prompt_layer(x, wts):
    b, l, _ = x.shape
    x2d = x.reshape(b * l, D_MODEL)
    qkv, z, q8, kv4, kvw, ga, gb, sm, selbf, winbf = _project(x2d, wts["norm1_g"], wts["w_pk"])
    conv0 = jnp.zeros((b, GDN_CONV - 1, GDN_CONV_DIM), F32)
    s0 = jnp.zeros((b, GDN_HEADS, GDN_DK, GDN_DV), F32)
    ya, s_new, conv_new = _gdn(qkv, z, sm, conv0, s0, wts["conv_w"], wts["a_log"], wts["dt_bias"],
                               wts["gdn_norm_g"], b, l)
    head, tail = _pool_rows(kv4, wts["pw"])
    kc, vc = _compress(head, tail, wts["cmp_wk"], wts["cmp_wv"], b)
    yb = _nsa_prompt(q8, sm, kc, vc, selbf, winbf, b, l)
    y = _finish(x2d, ya, yb, ga, gb, wts)
    win_buf = min(WINDOW, l)
    win_new = kvw.reshape(b, l, 2, NSA_KV_HEADS, NSA_HD)[:, l - win_buf:]
    return (y.reshape(b, l, D_MODEL), kv4.reshape(b, l, 4, NSA_KV_HEADS, NSA_HD), win_new, s_new, conv_new)


def _sample_layer(x, cache_kv_l, page_table, cache_win_l, s0, conv_buf, wts):
    b, l, _ = x.shape
    x2d = x.reshape(b * l, D_MODEL)
    qkv, z, q8, kv4, kvw, ga, gb, sm, selbf, winbf = _project(x2d, wts["norm1_g"], wts["w_pk"])
    ya, s_new, conv_new = _gdn(qkv, z, sm, conv_buf, s0, wts["conv_w"], wts["a_log"], wts["dt_bias"],
                               wts["gdn_norm_g"], b, l)
    n_phys, page = cache_kv_l.shape[:2]
    cache3 = cache_kv_l.reshape(n_phys, page, 4 * NSA_KV)
    head, tail = _pool_pages(cache3, page_table, wts["pw"], 8)
    kc, vc = _compress(head, tail, wts["cmp_wk"], wts["cmp_wv"], b)
    win_buf = cache_win_l.shape[1]
    win_full = jnp.concatenate([cache_win_l.reshape(b, win_buf, 2 * NSA_KV), kvw.reshape(b, l, 2 * NSA_KV)], axis=1)
    yb = _nsa_sample(q8, sm, kc, vc, selbf, win_full.astype(BF16), cache3, page_table, b, l, win_buf)
    y = _finish(x2d, ya, yb, ga, gb, wts)
    win_new = win_full[:, l:].reshape(b, win_buf, 2, NSA_KV_HEADS, NSA_HD)
    return (y.reshape(b, l, D_MODEL), kv4.reshape(b, l, 4, NSA_KV_HEADS, NSA_HD), win_new, s_new, conv_new)


def kernel(x_prompt, x_sample, cache_kv, page_table, cache_win, state_gdn, state_conv, norm1_g, w_in, gdn_conv_w, gdn_a_log, gdn_dt_bias, gdn_norm_g, cmp_pos_wk, cmp_pos_wv, cmp_wk, cmp_wv, w_branch_a, w_branch_b, w_out, norm2_g, w_grp, b_grp, w_rt, b_rt, w_e_gate, w_e_up, w_e_down, norm_f_g):
    assert w_in.shape[0] == 1, "single layer"
    w_route, b_route = _route_weights(w_grp[0], b_grp[0], w_rt[0], b_rt[0])
    wts = dict(
        norm1_g=norm1_g[0], w_pk=_pack_w_in(w_in[0]),
        conv_w=gdn_conv_w[0], a_log=gdn_a_log[0], dt_bias=gdn_dt_bias[0], gdn_norm_g=gdn_norm_g[0],
        pw=_pos_weights(cmp_pos_wk[0], cmp_pos_wv[0]), cmp_wk=cmp_wk[0], cmp_wv=cmp_wv[0],
        wa=w_branch_a[0].astype(BF16), wb=w_branch_b[0].astype(BF16), wo=w_out[0].astype(BF16),
        norm2_g=norm2_g[0], w_route=w_route, b_route=b_route,
        wg=w_e_gate[0].astype(BF16), wu=w_e_up[0].astype(BF16), wd=w_e_down[0].astype(BF16),
        norm_f_g=norm_f_g,
    )
    yp, kvp, winp, sp, cp = _prompt_layer(x_prompt, wts)
    ys, kvs, wins, ss, cs = _sample_layer(x_sample, cache_kv[0], page_table, cache_win[0], state_gdn[0],
                                          state_conv[0], wts)
    return (yp, ys, kvp[None], kvs[None], winp[None], wins[None], sp[None], ss[None], cp[None], cs[None])
```

```python
import functools
import math

import numpy as np
import jax
import jax.numpy as jnp
from jax import lax
from jax.experimental import pallas as pl
from jax.experimental.pallas import tpu as pltpu

F32 = jnp.float32
BF16 = jnp.bfloat16
HI = lax.Precision.HIGHEST

LANES = 128
SUBLANES = 8
VMEM_LIMIT = 56 * 1024 * 1024

D_MODEL = 1024
GDN_HEADS = 4
GDN_DK = 128
GDN_DV = 128
GDN_QK = GDN_HEADS * GDN_DK
GDN_V = GDN_HEADS * GDN_DV
GDN_CONV_DIM = 2 * GDN_QK + GDN_V
GDN_CONV = 4
GDN_CHUNK = 64
GDN_SEQS_PER_STEP = 4
NSA_HEADS = 8
NSA_KV_HEADS = 2
NSA_GROUP = NSA_HEADS // NSA_KV_HEADS
NSA_HD = 64
NSA_Q = NSA_HEADS * NSA_HD
NSA_KV = NSA_KV_HEADS * NSA_HD
CMP_STRIDE = 16
CMP_LEN = 2 * CMP_STRIDE
SEL_BLOCK = 64
SEL_TOP = 16
WINDOW = 512
FORCE_SCORE = 1.0e4
N_GROUPS = 4
EXPERTS_PER_GROUP = 8
N_EXPERTS = N_GROUPS * EXPERTS_PER_GROUP
D_EXPERT = 256
RMS_EPS = 1e-6
NEG_INF = float("-inf")

_OFF_QKV = 0
_OFF_Z = _OFF_QKV + GDN_CONV_DIM
_OFF_B = _OFF_Z + GDN_V
_OFF_A = _OFF_B + GDN_HEADS
_OFF_Q = _OFF_A + GDN_HEADS
_OFF_KV = _OFF_Q + NSA_Q
_OFF_G = _OFF_KV + 6 * NSA_KV
_OFF_GA = _OFF_G + 3 * NSA_HEADS
_OFF_GB = _OFF_GA + D_MODEL

SM_B = 0
SM_A = 4
SM_G = 8

_PK = {}
_c = 0
for _n, _w in (("qkv", GDN_CONV_DIM), ("z", GDN_V), ("q8", NSA_HEADS * LANES), ("kv4", 4 * NSA_KV),
               ("kvw", 2 * NSA_KV), ("ga", D_MODEL), ("gb", D_MODEL), ("sm", LANES)):
    _PK[_n] = (_c, _w)
    _c += _w
PK_DIM = _c


def _cparams(sem):
    return pltpu.CompilerParams(dimension_semantics=sem, vmem_limit_bytes=VMEM_LIMIT)


def _nt(a, b, precision=None):
    return lax.dot_general(a, b, (((1,), (1,)), ((), ())), preferred_element_type=F32, precision=precision)


def _tn(a, b, precision=None):
    return lax.dot_general(a, b, (((0,), (0,)), ((), ())), preferred_element_type=F32, precision=precision)


def _mm(a, b, precision=None):
    return jnp.dot(a, b, preferred_element_type=F32, precision=precision)


def _sigmoid(x):
    return 1.0 / (1.0 + jnp.exp(-x))


def _silu(x):
    return x * _sigmoid(x)


def _iota(shape, dim):
    return lax.broadcasted_iota(jnp.int32, shape, dim)


def _pack_w_in(w_in):
    q = w_in[:, _OFF_Q:_OFF_Q + NSA_Q]
    zeros64 = jnp.zeros((D_MODEL, NSA_HD), w_in.dtype)
    q8 = []
    for h in range(NSA_HEADS):
        qh = q[:, h * NSA_HD:(h + 1) * NSA_HD]
        q8.append(jnp.concatenate([qh, zeros64] if h < NSA_GROUP else [zeros64, qh], axis=1))
    sm = jnp.concatenate([w_in[:, _OFF_B:_OFF_B + GDN_HEADS], w_in[:, _OFF_A:_OFF_A + GDN_HEADS],
                          w_in[:, _OFF_G:_OFF_G + 3 * NSA_HEADS],
                          jnp.zeros((D_MODEL, LANES - 2 * GDN_HEADS - 3 * NSA_HEADS), w_in.dtype)], axis=1)
    cols = [w_in[:, _OFF_QKV:_OFF_QKV + GDN_CONV_DIM], w_in[:, _OFF_Z:_OFF_Z + GDN_V]] + q8 + [
        w_in[:, _OFF_KV:_OFF_KV + 4 * NSA_KV], w_in[:, _OFF_KV + 4 * NSA_KV:_OFF_KV + 6 * NSA_KV],
        w_in[:, _OFF_GA:_OFF_GA + D_MODEL], w_in[:, _OFF_GB:_OFF_GB + D_MODEL], sm]
    return jnp.concatenate(cols, axis=1).astype(BF16)


def _pos_features(pos, shape):
    lane = _iota(shape, 1)
    feat = jnp.where(lane == 0, (pos >> 7) * LANES, jnp.where(lane == 1, pos & (LANES - 1),
                                                               jnp.where(lane < 4, 1, 0)))
    return feat.astype(F32).astype(BF16)


def _proj_kernel(x_ref, g_ref, w_ref, qkv_ref, z_ref, q8_ref, kv4_ref, kvw_ref, ga_ref, gb_ref, sm_ref,
                 selbf_ref, winbf_ref, ksa_ref, kwa_ref, vst_ref, vwt_ref, *, tiles_per_seq):
    tm = x_ref.shape[0]
    x = x_ref[...]
    xn = (x * lax.rsqrt(jnp.mean(x * x, axis=-1, keepdims=True) + RMS_EPS) * g_ref[...]).astype(BF16)

    def seg(name):
        a, w = _PK[name]
        return _mm(xn, w_ref[:, a:a + w])

    qkv_ref[...] = seg("qkv")
    z_ref[...] = seg("z")
    q8_ref[...] = (seg("q8") * (NSA_HD ** -0.5)).astype(BF16)
    kv4 = seg("kv4")
    kv4_ref[...] = kv4
    selbf_ref[...] = kv4[:, 2 * NSA_KV:].astype(BF16)
    kvw = seg("kvw")
    kvw_ref[...] = kvw
    winbf_ref[...] = kvw.astype(BF16)
    ga_ref[...] = seg("ga")
    gb_ref[...] = seg("gb")
    sm_ref[...] = seg("sm")
    pos = (pl.program_id(0) % tiles_per_seq) * tm + _iota((tm, 1), 0)
    feat = _pos_features(pos, (tm, LANES))
    ksa_ref[:, :NSA_KV] = kv4[:, 2 * NSA_KV:3 * NSA_KV].astype(BF16)
    ksa_ref[:, NSA_KV:] = feat
    kwa_ref[:, :NSA_KV] = kvw[:, :NSA_KV].astype(BF16)
    kwa_ref[:, NSA_KV:] = feat
    vst_ref[...] = jnp.transpose(kv4[:, 3 * NSA_KV:]).astype(BF16)
    vwt_ref[...] = jnp.transpose(kvw[:, NSA_KV:]).astype(BF16)


def _project(x2d, norm_g, w_pk, seq_len):
    n = x2d.shape[0]
    tm = min(256, n)
    widths = [("qkv", F32), ("z", F32), ("q8", BF16), ("kv4", F32), ("kvw", F32), ("ga", F32), ("gb", F32),
              ("sm", F32)]
    out_shape = [jax.ShapeDtypeStruct((n, _PK[k][1]), dt) for k, dt in widths]
    out_shape += [jax.ShapeDtypeStruct((n, 2 * NSA_KV), BF16)] * 4
    out_specs = [pl.BlockSpec((tm, s.shape[1]), lambda i: (i, 0)) for s in out_shape]
    out_shape += [jax.ShapeDtypeStruct((NSA_KV, n), BF16)] * 2
    out_specs += [pl.BlockSpec((NSA_KV, tm), lambda i: (0, i))] * 2
    return pl.pallas_call(
        functools.partial(_proj_kernel, tiles_per_seq=max(seq_len // tm, 1)),
        grid=(n // tm,),
        in_specs=[pl.BlockSpec((tm, D_MODEL), lambda i: (i, 0)),
                  pl.BlockSpec((1, D_MODEL), lambda i: (0, 0)),
                  pl.BlockSpec((D_MODEL, PK_DIM), lambda i: (0, 0), pipeline_mode=pl.Buffered(1))],
        out_specs=out_specs,
        out_shape=out_shape,
        compiler_params=_cparams(("parallel",)),
        name="proj",
    )(x2d, norm_g.reshape(1, D_MODEL), w_pk)


def _unit_lower_inverses(a_list, c):
    r = _iota((c, c), 0)
    col = _iota((c, c), 1)
    eye = (r == col).astype(F32)
    n1 = [jnp.where((r >> 3) == (col >> 3), -a, 0.0) for a in a_list]
    n2 = [_mm(x, x, HI) for x in n1]
    n4 = [_mm(x, x, HI) for x in n2]
    t = [_mm(eye + x, eye + y, HI) for x, y in zip(n1, n2)]
    t = [_mm(x, eye + y, HI) for x, y in zip(t, n4)]
    s = SUBLANES
    while s < c:
        sh = s.bit_length() - 1
        off = ((r >> (sh + 1)) == (col >> (sh + 1))) & ((r >> sh) != (col >> sh))
        ta = [_mm(x, jnp.where(off, a, 0.0), HI) for x, a in zip(t, a_list)]
        t = [x - _mm(y, x, HI) for x, y in zip(t, ta)]
        s *= 2
    return t


def _gdn_kernel(qkv_ref, z_ref, sm_ref, cbuf_ref, s0_ref, cw_ref, alog_ref, dtb_ref, ng_ref,
                y_ref, snew_ref, cnew_ref, ext_ref, st_ref, *, chunk, nb):
    c = chunk
    ci = pl.program_id(1)

    @pl.when(ci == 0)
    def _():
        ext_ref[:, 0:SUBLANES, :] = cbuf_ref[...]
        st_ref[...] = s0_ref[...]

    r = _iota((c, c), 0)
    col = _iota((c, c), 1)
    tri_incl = r >= col
    tri_strict = r > col
    tri_f = tri_incl.astype(F32)
    pick = (_iota((SUBLANES, LANES), 1) == _iota((SUBLANES, LANES), 0) + SM_A).astype(F32)
    base = SUBLANES - (GDN_CONV - 1)
    ng = ng_ref[...]

    units = []
    for bb in range(nb):
        u = qkv_ref[bb]
        ext_ref[bb, SUBLANES:SUBLANES + c, :] = u
        conv = cw_ref[0:1, :] * ext_ref[bb, base:base + c, :]
        for i in range(1, GDN_CONV - 1):
            conv = conv + cw_ref[i:i + 1, :] * ext_ref[bb, base + i:base + i + c, :]
        conv = conv + cw_ref[GDN_CONV - 1:GDN_CONV, :] * u
        halo = ext_ref[bb, c:c + SUBLANES, :]
        ext_ref[bb, 0:SUBLANES, :] = halo
        cnew_ref[bb] = halo
        qkv = _silu(conv)

        sm = sm_ref[bb]
        beta_all = _sigmoid(sm)
        xa = sm + dtb_ref[...]
        softplus = jnp.maximum(xa, 0.0) + jnp.log(1.0 + jnp.exp(-jnp.abs(xa)))
        g_all = -jnp.exp(alog_ref[...]) * softplus
        gcum_all = _mm(tri_f, g_all, HI)
        gcum_rows = _nt(pick, gcum_all, HI)
        for h in range(GDN_HEADS):
            q = qkv[:, h * GDN_DK:(h + 1) * GDN_DK]
            k = qkv[:, GDN_QK + h * GDN_DK:GDN_QK + (h + 1) * GDN_DK]
            v = qkv[:, 2 * GDN_QK + h * GDN_DV:2 * GDN_QK + (h + 1) * GDN_DV]
            q = q * lax.rsqrt(jnp.sum(q * q, axis=-1, keepdims=True) + RMS_EPS) * (GDN_DK ** -0.5)
            k = k * lax.rsqrt(jnp.sum(k * k, axis=-1, keepdims=True) + RMS_EPS)
            beta = beta_all[:, SM_B + h:SM_B + h + 1]
            gc_col = gcum_all[:, SM_A + h:SM_A + h + 1]
            gc_row = gcum_rows[h:h + 1, :]
            gc_last = gcum_all[c - 1:c, SM_A + h:SM_A + h + 1]
            units.append(dict(bb=bb, h=h, q=q, k=k, v=v, beta=beta, gc_col=gc_col, gc_last=gc_last,
                              decay=jnp.exp(jnp.where(tri_incl, gc_col - gc_row, NEG_INF)),
                              eg=jnp.exp(gc_col), kb=k * beta, k_bf=k.astype(BF16)))

    a_list = [jnp.where(tri_strict, _nt(un["kb"].astype(BF16), un["k_bf"]) * un["decay"], 0.0) for un in units]
    t_list = _unit_lower_inverses(a_list, c)
    u_coef = [_mm(t, un["v"] * un["beta"], HI) for t, un in zip(t_list, units)]
    w_coef = [_mm(t, un["kb"] * un["eg"], HI) for t, un in zip(t_list, units)]
    qk = [(_nt(un["q"].astype(BF16), un["k_bf"]) * un["decay"]).astype(BF16) for un in units]
    s_old = [st_ref[un["bb"], un["h"]] for un in units]
    s_bf = [s.astype(BF16) for s in s_old]
    uu = [uc - _mm(wc.astype(BF16), s) for uc, wc, s in zip(u_coef, w_coef, s_bf)]
    uu_bf = [x.astype(BF16) for x in uu]
    o_list = [_mm((un["q"] * un["eg"]).astype(BF16), s) + _mm(a, x)
              for un, s, a, x in zip(units, s_bf, qk, uu_bf)]
    for un, s, x, o in zip(units, s_old, uu_bf, o_list):
        bb, h = un["bb"], un["h"]
        k_tail = un["k"] * jnp.exp(un["gc_last"] - un["gc_col"])
        st_ref[bb, h] = s * jnp.exp(un["gc_last"]) + _tn(k_tail.astype(BF16), x)
        on = o * lax.rsqrt(jnp.mean(o * o, axis=-1, keepdims=True) + RMS_EPS) * ng
        y_ref[bb, :, h * GDN_DV:(h + 1) * GDN_DV] = on * _silu(z_ref[bb, :, h * GDN_DV:(h + 1) * GDN_DV])

    @pl.when(ci == pl.num_programs(1) - 1)
    def _():
        snew_ref[...] = st_ref[...]


def _gdn(qkv, z, sm, conv_buf, s0, conv_w, a_log, dt_bias, norm_g, b, l):
    c = math.gcd(l, GDN_CHUNK)
    nc = l // c
    nb = math.gcd(b, GDN_SEQS_PER_STEP)
    cbuf8 = jnp.pad(conv_buf, ((0, 0), (SUBLANES - (GDN_CONV - 1), 0), (0, 0)))
    pad_a = (SM_A, LANES - SM_A - GDN_HEADS)
    alog_row = jnp.pad(a_log, pad_a).reshape(1, LANES)
    dtb_row = jnp.pad(dt_bias, pad_a).reshape(1, LANES)
    row = lambda bi, ci: (bi, ci, 0)
    per_seq3 = lambda bi, ci: (bi, 0, 0)
    per_seq4 = lambda bi, ci: (bi, 0, 0, 0)
    fixed = lambda bi, ci: (0, 0)
    y, s_new, c_new = pl.pallas_call(
        functools.partial(_gdn_kernel, chunk=c, nb=nb),
        grid=(b // nb, nc),
        in_specs=[pl.BlockSpec((nb, c, GDN_CONV_DIM), row),
                  pl.BlockSpec((nb, c, GDN_V), row),
                  pl.BlockSpec((nb, c, LANES), row),
                  pl.BlockSpec((nb, SUBLANES, GDN_CONV_DIM), per_seq3),
                  pl.BlockSpec((nb, GDN_HEADS, GDN_DK, GDN_DV), per_seq4),
                  pl.BlockSpec((GDN_CONV, GDN_CONV_DIM), fixed),
                  pl.BlockSpec((1, LANES), fixed),
                  pl.BlockSpec((1, LANES), fixed),
                  pl.BlockSpec((1, GDN_DV), fixed)],
        out_specs=[pl.BlockSpec((nb, c, GDN_V), row),
                   pl.BlockSpec((nb, GDN_HEADS, GDN_DK, GDN_DV), per_seq4),
                   pl.BlockSpec((nb, SUBLANES, GDN_CONV_DIM), per_seq3)],
        out_shape=[jax.ShapeDtypeStruct((b, l, GDN_V), F32),
                   jax.ShapeDtypeStruct((b, GDN_HEADS, GDN_DK, GDN_DV), F32),
                   jax.ShapeDtypeStruct((b, SUBLANES, GDN_CONV_DIM), F32)],
        scratch_shapes=[pltpu.VMEM((nb, c + SUBLANES, GDN_CONV_DIM), F32),
                        pltpu.VMEM((nb, GDN_HEADS, GDN_DK, GDN_DV), F32)],
        compiler_params=_cparams(("parallel", "arbitrary")),
        name="gdn",
    )(qkv.reshape(b, l, GDN_CONV_DIM), z.reshape(b, l, GDN_V), sm.reshape(b, l, LANES), cbuf8, s0, conv_w,
      alog_row, dtb_row, norm_g.reshape(1, GDN_DV))
    return y.reshape(b * l, GDN_V), s_new, c_new[:, SUBLANES - (GDN_CONV - 1):]


def _pool_kernel(*refs, n_in):
    refs = refs[len(refs) - 2 * n_in - 3:]
    x_refs, pw_ref, head_ref, tail_ref = refs[:2 * n_in], refs[2 * n_in], refs[2 * n_in + 1], refs[2 * n_in + 2]
    rows = x_refs[0].shape[0]
    n = rows // CMP_STRIDE
    for j, x_ref in enumerate(x_refs):
        half = slice((j % 2) * NSA_KV, (j % 2 + 1) * NSA_KV)
        head = None
        tail = None
        for p in range(CMP_STRIDE):
            xr = x_ref[pl.ds(p, n, stride=CMP_STRIDE), :]
            hp = xr * pw_ref[p:p + 1, half]
            tp = xr * pw_ref[CMP_STRIDE + p:CMP_STRIDE + p + 1, half]
            head = hp if head is None else head + hp
            tail = tp if tail is None else tail + tp
        head_ref[(j // 2) * n:(j // 2 + 1) * n, half] = head
        tail_ref[(j // 2) * n:(j // 2 + 1) * n, half] = tail


def _pos_weights(pos_wk, pos_wv):
    return jnp.concatenate([jnp.broadcast_to(pos_wk[:, None], (CMP_LEN, NSA_KV)),
                            jnp.broadcast_to(pos_wv[:, None], (CMP_LEN, NSA_KV))], axis=1).astype(F32)


def _pool_rows(kv4, pw):
    n = kv4.shape[0]
    r = min(2048, n)
    w = 2 * NSA_KV
    return pl.pallas_call(
        functools.partial(_pool_kernel, n_in=1),
        grid=(n // r,),
        in_specs=[pl.BlockSpec((r, NSA_KV), lambda i: (i, 0)),
                  pl.BlockSpec((r, NSA_KV), lambda i: (i, 1)),
                  pl.BlockSpec((CMP_LEN, w), lambda i: (0, 0))],
        out_specs=[pl.BlockSpec((r // CMP_STRIDE, w), lambda i: (i, 0))] * 2,
        out_shape=[jax.ShapeDtypeStruct((n // CMP_STRIDE, w), F32)] * 2,
        compiler_params=_cparams(("parallel",)),
        name="pool_rows",
    )(kv4, kv4, pw)


def _pool_pages(cache3, page_table, pw, pages_per_step):
    b, n_pages = page_table.shape
    page = cache3.shape[1]
    pg = pages_per_step
    w = 2 * NSA_KV
    n_out = pg * page // CMP_STRIDE
    steps = n_pages // pg

    def page_map(k, half):
        return lambda bi, j, pt: (pt[bi, j * pg + k], 0, half)

    grid_spec = pltpu.PrefetchScalarGridSpec(
        num_scalar_prefetch=1,
        grid=(b, steps),
        in_specs=[pl.BlockSpec((None, page, NSA_KV), page_map(k, half)) for k in range(pg) for half in range(2)]
        + [pl.BlockSpec((CMP_LEN, w), lambda bi, j, pt: (0, 0))],
        out_specs=[pl.BlockSpec((n_out, w), lambda bi, j, pt: (bi * steps + j, 0))] * 2,
    )
    return pl.pallas_call(
        functools.partial(_pool_kernel, n_in=pg),
        grid_spec=grid_spec,
        out_shape=[jax.ShapeDtypeStruct((b * steps * n_out, w), F32)] * 2,
        compiler_params=_cparams(("parallel", "arbitrary")),
        name="pool_pages",
    )(page_table, *([cache3] * (2 * pg)), pw)


def _cmp_kernel(head_ref, tail_ref, wk_ref, wv_ref, kca_ref, vc_ref, vct_ref):
    n = head_ref.shape[0]
    blocks = head_ref[...] + pltpu.roll(tail_ref[...], n - 1, 0)
    kca_ref[:, :NSA_KV] = _mm(blocks[:, :NSA_KV].astype(BF16), wk_ref[...]).astype(BF16)
    blk_end = _iota((n, 1), 0) * CMP_STRIDE + (CMP_LEN - 1)
    kca_ref[:, NSA_KV:] = _pos_features(blk_end, (n, LANES))
    vc = _mm(blocks[:, NSA_KV:].astype(BF16), wv_ref[...])
    vc_ref[...] = vc.astype(BF16)
    vct_ref[...] = jnp.transpose(vc).astype(BF16)


def _block_diag2(w):
    z = jnp.zeros_like(w)
    return jnp.concatenate([jnp.concatenate([w, z], axis=1), jnp.concatenate([z, w], axis=1)], axis=0).astype(BF16)


def _compress(head, tail, cmp_wk, cmp_wv, b):
    n_ch = head.shape[0] // b
    w = 2 * NSA_KV
    return pl.pallas_call(
        _cmp_kernel,
        grid=(b,),
        in_specs=[pl.BlockSpec((n_ch, w), lambda i: (i, 0))] * 2
        + [pl.BlockSpec((NSA_KV, NSA_KV), lambda i: (0, 0))] * 2,
        out_specs=[pl.BlockSpec((n_ch, w), lambda i: (i, 0)),
                   pl.BlockSpec((n_ch, NSA_KV), lambda i: (i, 0)),
                   pl.BlockSpec((NSA_KV, n_ch), lambda i: (0, i))],
        out_shape=[jax.ShapeDtypeStruct((b * n_ch, w), BF16),
                   jax.ShapeDtypeStruct((b * n_ch, NSA_KV), BF16),
                   jax.ShapeDtypeStruct((NSA_KV, b * n_ch), BF16)],
        compiler_params=_cparams(("parallel",)),
        name="compress",
    )(head, tail, _block_diag2(cmp_wk), _block_diag2(cmp_wv))


def _overlap(n_ch, nsp):
    i = jnp.arange(n_ch)[:, None]
    j = jnp.arange(nsp)[None, :]
    lo = jnp.maximum(i * CMP_STRIDE, j * SEL_BLOCK)
    hi = jnp.minimum(i * CMP_STRIDE + CMP_LEN, (j + 1) * SEL_BLOCK)
    ov = jnp.maximum(hi - lo, 0).astype(F32) / CMP_LEN
    return jnp.where(i < n_ch - 1, ov, 0.0)


def _slope(h):
    return 2.0 ** (-(h + 1))


def _softmax_rows(s, mask):
    s = jnp.where(mask, s, NEG_INF)
    m = jnp.max(s, axis=-1, keepdims=True)
    m = jnp.where(m == NEG_INF, 0.0, m)
    e = jnp.where(mask, jnp.exp(s - m), 0.0)
    return e, jnp.maximum(jnp.sum(e, axis=-1, keepdims=True), 1e-30)


def _top_blocks(imp, qpos, ns_lanes):
    blk = _iota(imp.shape, 1)
    cur = qpos >> 6
    forced = (blk == 0) | (blk == cur) | (blk == cur - 1)
    score = jnp.where(forced, FORCE_SCORE, imp)
    work = jnp.where(blk <= cur, score, NEG_INF)
    sel = jnp.zeros(imp.shape, F32)
    for _ in range(SEL_TOP):
        m = jnp.max(work, axis=-1, keepdims=True)
        cand = jnp.where((work == m) & (m > NEG_INF), blk, ns_lanes)
        first = jnp.min(cand, axis=-1, keepdims=True)
        hit = blk == first
        sel = jnp.where(hit, 1.0, sel)
        work = jnp.where(hit, NEG_INF, work)
    return sel


def _row_slopes(tq):
    head = _iota((NSA_HEADS * tq, 1), 0) >> (tq.bit_length() - 1)
    slope = jnp.zeros((NSA_HEADS * tq, 1), F32)
    for h in range(NSA_HEADS):
        slope = jnp.where(head == h, _slope(h), slope)
    return slope


def _rows_attend(q8, k, v, slope, dist, mask):
    s = _nt(q8, k) - slope * dist.astype(F32)
    e, den = _softmax_rows(s, mask)
    p = e / den
    return _mm(p.astype(BF16), v), p


def _rows_online(q8, k, v, slope, dist, mask, m_ref, l_ref, acc_ref):
    s = jnp.where(mask, _nt(q8, k) - slope * dist.astype(F32), NEG_INF)
    m_old = m_ref[...]
    m_new = jnp.maximum(m_old, jnp.max(s, axis=-1, keepdims=True))
    m_use = jnp.where(m_new == NEG_INF, 0.0, m_new)
    alpha = jnp.exp(m_old - m_use)
    e = jnp.exp(s - m_use)
    l_ref[...] = alpha * l_ref[...] + jnp.sum(e, axis=-1, keepdims=True)
    acc_ref[...] = alpha * acc_ref[...] + _mm(e.astype(BF16), v)
    m_ref[...] = m_new


def _merge_heads(o_c, o_s, o_w, gates, tq):
    lane = _iota((tq, LANES), 1)
    mixed = []
    for h in range(NSA_HEADS):
        rows = slice(h * tq, (h + 1) * tq)
        g0 = gates[:, SM_G + 3 * h:SM_G + 3 * h + 1]
        g1 = gates[:, SM_G + 3 * h + 1:SM_G + 3 * h + 2]
        g2 = gates[:, SM_G + 3 * h + 2:SM_G + 3 * h + 3]
        mixed.append(g0 * o_c[rows] + g1 * o_s[rows] + g2 * o_w[rows])
    cols = []
    for p in range(NSA_HEADS // 2):
        a, b = mixed[2 * p], mixed[2 * p + 1]
        if 2 * p < NSA_GROUP:
            cols.append(jnp.where(lane < NSA_HD, a, pltpu.roll(b, NSA_HD, 1)))
        else:
            cols.append(jnp.where(lane < NSA_HD, pltpu.roll(a, NSA_HD, 1), b))
    return jnp.concatenate(cols, axis=1)


def _softmax_cols(s, mask):
    s = jnp.where(mask, s, NEG_INF)
    m = jnp.max(s, axis=0, keepdims=True)
    m = jnp.where(m == NEG_INF, 0.0, m)
    e = jnp.exp(s - m)
    return e * (1.0 / jnp.maximum(jnp.sum(e, axis=0, keepdims=True), 1e-30))


def _split3(x):
    hi = x.astype(BF16)
    r1 = x - hi.astype(F32)
    mid = r1.astype(BF16)
    lo = (r1 - mid.astype(F32)).astype(BF16)
    return hi, mid, lo


def _top_blocks_cols(imp, qpos, nsp):
    blk = _iota(imp.shape, 0)
    cur = qpos >> 6
    forced = (blk == 0) | (blk == cur) | (blk == cur - 1)
    work = jnp.where(blk <= cur, jnp.where(forced, FORCE_SCORE, imp), NEG_INF)
    neg = jnp.full(imp.shape, NEG_INF, F32)
    for _ in range(SEL_TOP):
        m = jnp.max(work, axis=0, keepdims=True)
        cand = jnp.where((work == m) & (m > NEG_INF), blk, nsp)
        hit = blk == jnp.min(cand, axis=0, keepdims=True)
        neg = jnp.where(hit, 0.0, neg)
        work = jnp.where(hit, NEG_INF, work)
    return neg


def _nsa_prompt_kernel(q8_ref, sm_ref, kc_ref, vct_ref, ks_ref, vst_ref, kw_ref, vwt_ref, ovt_ref, y_ref,
                       neg_ref, m_ref, l_ref, acc_ref, *, tq, tk):
    i = pl.program_id(1)
    start = i * tq
    r = NSA_HEADS * tq
    nsp, n_ch = ovt_ref.shape
    col = _iota((1, r), 1)
    qpos = start + (col & (tq - 1))

    q8 = jnp.concatenate([q8_ref[:, h * LANES:(h + 1) * LANES] for h in range(NSA_HEADS)], axis=0)
    eye = (_iota((LANES, LANES), 0) == _iota((LANES, LANES), 1)).astype(BF16)
    qt = _nt(eye, q8).astype(BF16)
    head = col >> (tq.bit_length() - 1)
    slope = jnp.zeros((1, r), F32)
    for h in range(NSA_HEADS):
        slope = jnp.where(head == h, _slope(h), slope)
    frow = _iota((2 * SUBLANES, r), 0)
    feat = jnp.where(frow < 2, slope, jnp.where(frow == 2, -slope * ((qpos >> 7) * LANES).astype(F32),
                                                jnp.where(frow == 3, -slope * (qpos & (LANES - 1)).astype(F32), 0.0)))
    qta = jnp.concatenate([qt, feat.astype(BF16), jnp.zeros((LANES - 2 * SUBLANES, r), BF16)], axis=0)

    blk_end = _iota((n_ch, 1), 0) * CMP_STRIDE + (CMP_LEN - 1)
    p = _softmax_cols(_mm(kc_ref[...], qta), blk_end <= qpos)
    o_c = _mm(vct_ref[...], p.astype(BF16))
    psum = []
    for g in range(NSA_KV_HEADS):
        acc = p[:, g * NSA_GROUP * tq:(g * NSA_GROUP + 1) * tq]
        for hh in range(1, NSA_GROUP):
            acc = acc + p[:, (g * NSA_GROUP + hh) * tq:(g * NSA_GROUP + hh + 1) * tq]
        psum.append(acc)
    ovt = ovt_ref[...]
    imp = sum(_mm(ovt, piece) for piece in _split3(jnp.concatenate(psum, axis=1)))
    qpos2 = start + (_iota((1, NSA_KV_HEADS * tq), 1) & (tq - 1))
    neg = _top_blocks_cols(imp, qpos2, nsp)
    for h in range(NSA_HEADS):
        g = h // NSA_GROUP
        neg_ref[:, h * tq:(h + 1) * tq] = neg[:, g * tq:(g + 1) * tq]

    m_ref[...] = jnp.full(m_ref.shape, NEG_INF, F32)
    l_ref[...] = jnp.zeros(l_ref.shape, F32)
    acc_ref[...] = jnp.zeros(acc_ref.shape, F32)
    bpt = tk // SEL_BLOCK

    def tile(t, causal):
        off = pl.multiple_of(t * tk, tk)
        s = _mm(ks_ref[pl.ds(off, tk), :], qta)
        s = jnp.concatenate([s[j * SEL_BLOCK:(j + 1) * SEL_BLOCK] + neg_ref[pl.ds(t * bpt + j, 1), :]
                             for j in range(bpt)], axis=0)
        if causal:
            s = jnp.where(off + _iota((tk, 1), 0) <= qpos, s, NEG_INF)
        m_old = m_ref[...]
        m_new = jnp.maximum(m_old, jnp.max(s, axis=0, keepdims=True))
        m_use = jnp.where(m_new == NEG_INF, 0.0, m_new)
        alpha = jnp.exp(m_old - m_use)
        e = jnp.exp(s - m_use)
        l_ref[...] = alpha * l_ref[...] + jnp.sum(e, axis=0, keepdims=True)
        acc_ref[...] = alpha * acc_ref[...] + _mm(vst_ref[:, pl.ds(off, tk)], e.astype(BF16))
        m_ref[...] = m_new

    n_full = start // tk

    def body(t, carry):
        tile(t, False)
        return carry

    lax.fori_loop(0, n_full, body, 0)
    tile(n_full, True)
    o_s = acc_ref[...] * (1.0 / jnp.maximum(l_ref[...], 1e-30))

    wl = tq + WINDOW
    ws = pl.multiple_of(jnp.maximum(start - WINDOW, 0), tq)
    dist = qpos - (ws + _iota((wl, 1), 0))
    pw = _softmax_cols(_mm(kw_ref[pl.ds(ws, wl), :], qta), (dist >= 0) & (dist <= WINDOW))
    o_w = _mm(vwt_ref[:, pl.ds(ws, wl)], pw.astype(BF16))

    gt = jnp.transpose(_sigmoid(sm_ref[...]))
    gate = [jnp.concatenate([gt[SM_G + 3 * h + c:SM_G + 3 * h + c + 1, :] for h in range(NSA_HEADS)], axis=1)
            for c in range(3)]
    o = gate[0] * o_c + gate[1] * o_s + gate[2] * o_w
    yt = jnp.concatenate([o[(h // NSA_GROUP) * NSA_HD:(h // NSA_GROUP + 1) * NSA_HD, h * tq:(h + 1) * tq]
                          for h in range(NSA_HEADS)], axis=0)
    y_ref[...] = jnp.transpose(yt)


def _nsa_prompt(q8, sm, kca, vct, ksa, vst, kwa, vwt, b, l):
    tq = 128
    tk = 256
    assert l % tk == 0 and l >= tq + WINDOW
    n_ch = l // CMP_STRIDE
    ns = -(-l // SEL_BLOCK)
    nsp = -(-ns // LANES) * LANES
    nq = l // tq
    w = 2 * NSA_KV
    ovt = jnp.transpose(_overlap(n_ch, nsp)).astype(BF16)
    row = lambda bi, i: (bi * nq + i, 0)
    seq_rows = lambda bi, i: (bi, 0)
    seq_cols = lambda bi, i: (0, bi)
    return pl.pallas_call(
        functools.partial(_nsa_prompt_kernel, tq=tq, tk=tk),
        grid=(b, nq),
        in_specs=[pl.BlockSpec((tq, NSA_HEADS * LANES), row),
                  pl.BlockSpec((tq, LANES), row),
                  pl.BlockSpec((n_ch, w), seq_rows),
                  pl.BlockSpec((NSA_KV, n_ch), seq_cols),
                  pl.BlockSpec((l, w), seq_rows),
                  pl.BlockSpec((NSA_KV, l), seq_cols),
                  pl.BlockSpec((l, w), seq_rows),
                  pl.BlockSpec((NSA_KV, l), seq_cols),
                  pl.BlockSpec((nsp, n_ch), lambda bi, i: (0, 0))],
        out_specs=pl.BlockSpec((tq, NSA_Q), row),
        out_shape=jax.ShapeDtypeStruct((b * l, NSA_Q), F32),
        scratch_shapes=[pltpu.VMEM((nsp, NSA_HEADS * tq), F32),
                        pltpu.VMEM((1, NSA_HEADS * tq), F32),
                        pltpu.VMEM((1, NSA_HEADS * tq), F32),
                        pltpu.VMEM((NSA_KV, NSA_HEADS * tq), F32)],
        compiler_params=_cparams(("parallel", "arbitrary")),
        name="nsa_prompt",
    )(q8, sm, kca, vct, ksa, vst, kwa, vwt, ovt)


def _nsa_sample_kernel(*refs, pg, page, tq, past_len):
    pt_ref = refs[0]
    del pt_ref
    page_refs = refs[1:1 + pg]
    (q8_ref, sm_ref, kc_ref, vc_ref, knew_ref, kw_ref, ov_ref, y_ref,
     m_ref, l_ref, acc_ref, sel_ref, oc_ref) = refs[1 + pg:]
    j = pl.program_id(1)
    nsp = ov_ref.shape[1]
    r = NSA_HEADS * tq
    q8 = jnp.concatenate([q8_ref[:, h * LANES:(h + 1) * LANES].astype(F32) for h in range(NSA_HEADS)],
                         axis=0).astype(BF16)
    qpos = past_len + (_iota((r, 1), 0) & (tq - 1))
    slope = _row_slopes(tq)

    @pl.when(j == 0)
    def _():
        n_ch = kc_ref.shape[0]
        dist = qpos - (_iota((1, n_ch), 1) * CMP_STRIDE + (CMP_LEN - 1))
        o_c, p = _rows_attend(q8, kc_ref[...], vc_ref[...], slope, dist, dist >= 0)
        oc_ref[...] = o_c
        for g in range(NSA_KV_HEADS):
            psum = p[g * NSA_GROUP * tq:(g * NSA_GROUP + 1) * tq]
            for hh in range(1, NSA_GROUP):
                psum = psum + p[(g * NSA_GROUP + hh) * tq:(g * NSA_GROUP + hh + 1) * tq]
            sel = _top_blocks(_mm(psum, ov_ref[...], HI), qpos[:tq], nsp)
            for hh in range(NSA_GROUP):
                sel_ref[(g * NSA_GROUP + hh) * tq:(g * NSA_GROUP + hh + 1) * tq, :] = sel
        m_ref[...] = jnp.full(m_ref.shape, NEG_INF, F32)
        l_ref[...] = jnp.zeros(l_ref.shape, F32)
        acc_ref[...] = jnp.zeros(acc_ref.shape, F32)

    sel_bf = sel_ref[...].astype(BF16)

    def sel_step(k, v, kpos):
        n = k.shape[0]
        expand = jnp.where(_iota((nsp, n), 0) == (kpos >> 6), 1.0, 0.0).astype(BF16)
        dist = qpos - kpos
        mask = (_mm(sel_bf, expand) > 0.5) & (dist >= 0)
        _rows_online(q8, k, v, slope, dist, mask, m_ref, l_ref, acc_ref)

    kv = jnp.concatenate([page_refs[k][...] for k in range(pg)], axis=0)
    sel_step(kv[:, :NSA_KV].astype(BF16), kv[:, NSA_KV:].astype(BF16),
             j * (pg * page) + _iota((1, pg * page), 1))

    @pl.when(j == pl.num_programs(1) - 1)
    def _():
        kn = knew_ref[...]
        sel_step(kn[:, :NSA_KV], kn[:, NSA_KV:], past_len + _iota((1, SEL_BLOCK), 1))
        o_s = acc_ref[...] / jnp.maximum(l_ref[...], 1e-30)
        kw = kw_ref[...]
        wrows = kw.shape[0]
        kwpos = past_len - WINDOW + _iota((1, wrows), 1)
        dist = qpos - kwpos
        mask = (dist >= 0) & (dist <= WINDOW) & (kwpos >= 0) & (kwpos < past_len + tq)
        o_w, _ = _rows_attend(q8, kw[:, :NSA_KV], kw[:, NSA_KV:], slope, dist, mask)
        y_ref[...] = _merge_heads(oc_ref[...], o_s, o_w, _sigmoid(sm_ref[...]), tq)


def _nsa_sample(q8, sm, kc, vc, selbf_new, win_full_bf, cache3, page_table, b, l, win_buf):
    n_pages = page_table.shape[1]
    page = cache3.shape[1]
    past_len = n_pages * page
    assert win_buf == WINDOW and l <= SEL_BLOCK and past_len % SEL_BLOCK == 0
    pg = 8
    steps = n_pages // pg
    n_ch = past_len // CMP_STRIDE
    ns = -(-(past_len + l) // SEL_BLOCK)
    nsp = -(-ns // LANES) * LANES
    w = 2 * NSA_KV
    ov = _overlap(n_ch, nsp)
    knew = jnp.pad(selbf_new.reshape(b, l, w), ((0, 0), (0, SEL_BLOCK - l), (0, 0))).reshape(b * SEL_BLOCK, w)
    wrows = -(-(win_buf + l) // LANES) * LANES
    kwp = jnp.pad(win_full_bf, ((0, 0), (0, wrows - win_buf - l), (0, 0))).reshape(b * wrows, w)

    def page_map(k):
        return lambda bi, j, pt: (pt[bi, j * pg + k], 0, 1)

    per_b = lambda bi, j, pt: (bi, 0)
    grid_spec = pltpu.PrefetchScalarGridSpec(
        num_scalar_prefetch=1,
        grid=(b, steps),
        in_specs=[pl.BlockSpec((None, page, w), page_map(k)) for k in range(pg)]
        + [pl.BlockSpec((None, l, NSA_HEADS * LANES), lambda bi, j, pt: (bi, 0, 0)),
           pl.BlockSpec((l, LANES), per_b),
           pl.BlockSpec((n_ch, NSA_KV), per_b),
           pl.BlockSpec((n_ch, NSA_KV), per_b),
           pl.BlockSpec((SEL_BLOCK, w), per_b),
           pl.BlockSpec((wrows, w), per_b),
           pl.BlockSpec((n_ch, nsp), lambda bi, j, pt: (0, 0))],
        out_specs=pl.BlockSpec((l, NSA_Q), per_b),
        scratch_shapes=[pltpu.VMEM((NSA_HEADS * l, 1), F32),
                        pltpu.VMEM((NSA_HEADS * l, 1), F32),
                        pltpu.VMEM((NSA_HEADS * l, NSA_KV), F32),
                        pltpu.VMEM((NSA_HEADS * l, nsp), F32),
                        pltpu.VMEM((NSA_HEADS * l, NSA_KV), F32)],
    )
    return pl.pallas_call(
        functools.partial(_nsa_sample_kernel, pg=pg, page=page, tq=l, past_len=past_len),
        grid_spec=grid_spec,
        out_shape=jax.ShapeDtypeStruct((b * l, NSA_Q), F32),
        compiler_params=_cparams(("parallel", "arbitrary")),
        name="nsa_sample",
    )(page_table, *([cache3] * pg), q8.reshape(b, l, NSA_HEADS * LANES), sm, kc, vc, knew, kwp, ov)


def _tail_kernel(x_ref, ya_ref, yb_ref, ga_ref, gb_ref, wa_ref, wb_ref, wo_ref, n2_ref, wr_ref, br_ref,
                 h_ref, hn_ref, cmb_ref):
    ma = _mm(ya_ref[...].astype(BF16), wa_ref[...])
    mb = _mm(yb_ref[...].astype(BF16), wb_ref[...])
    m = _sigmoid(ga_ref[...]) * ma + _sigmoid(gb_ref[...]) * mb
    h = x_ref[...] + _mm(m.astype(BF16), wo_ref[...])
    h_ref[...] = h
    hn = h * lax.rsqrt(jnp.mean(h * h, axis=-1, keepdims=True) + RMS_EPS) * n2_ref[...]
    hn_ref[...] = hn.astype(BF16)

    logit = _mm(hn, wr_ref[...], HI) + br_ref[...]
    lane = _iota(logit.shape, 1)
    is_grp = (lane >= N_EXPERTS) & (lane < N_EXPERTS + N_GROUPS)
    gl = jnp.where(is_grp, logit, NEG_INF)
    gmax = jnp.max(gl, axis=-1, keepdims=True)
    gidx = jnp.min(jnp.where(gl == gmax, lane, LANES), axis=-1, keepdims=True) - N_EXPERTS
    p_grp = 1.0 / jnp.sum(jnp.exp(gl - gmax), axis=-1, keepdims=True)
    el = jnp.where((lane >> 3) == gidx, logit, NEG_INF)
    t1 = jnp.max(el, axis=-1, keepdims=True)
    i1 = jnp.min(jnp.where(el == t1, lane, LANES), axis=-1, keepdims=True)
    el2 = jnp.where(lane == i1, NEG_INF, el)
    t2 = jnp.max(el2, axis=-1, keepdims=True)
    i2 = jnp.min(jnp.where(el2 == t2, lane, LANES), axis=-1, keepdims=True)
    e2 = jnp.exp(t2 - t1)
    w1 = p_grp / (1.0 + e2)
    w2 = p_grp * e2 / (1.0 + e2)
    cmb_ref[...] = jnp.where(lane == i1, w1, jnp.where(lane == i2, w2, 0.0))


def _tail(x2d, ya, yb, ga, gb, wa, wb, wo, norm2_g, w_route, b_route):
    n = x2d.shape[0]
    tm = min(256, n)
    row = lambda i: (i, 0)
    fixed = lambda i: (0, 0)
    return pl.pallas_call(
        _tail_kernel,
        grid=(n // tm,),
        in_specs=[pl.BlockSpec((tm, D_MODEL), row),
                  pl.BlockSpec((tm, GDN_V), row),
                  pl.BlockSpec((tm, NSA_Q), row),
                  pl.BlockSpec((tm, D_MODEL), row),
                  pl.BlockSpec((tm, D_MODEL), row),
                  pl.BlockSpec((GDN_V, D_MODEL), fixed),
                  pl.BlockSpec((NSA_Q, D_MODEL), fixed),
                  pl.BlockSpec((D_MODEL, D_MODEL), fixed),
                  pl.BlockSpec((1, D_MODEL), fixed),
                  pl.BlockSpec((D_MODEL, LANES), fixed),
                  pl.BlockSpec((1, LANES), fixed)],
        out_specs=[pl.BlockSpec((tm, D_MODEL), row),
                   pl.BlockSpec((tm, D_MODEL), row),
                   pl.BlockSpec((tm, LANES), row)],
        out_shape=[jax.ShapeDtypeStruct((n, D_MODEL), F32),
                   jax.ShapeDtypeStruct((n, D_MODEL), BF16),
                   jax.ShapeDtypeStruct((n, LANES), F32)],
        compiler_params=_cparams(("parallel",)),
        name="tail",
    )(x2d, ya, yb, ga, gb, wa, wb, wo, norm2_g.reshape(1, D_MODEL), w_route, b_route)


def _moe_kernel(h_ref, hn_ref, cmb_ref, wg_ref, wu_ref, wd_ref, nf_ref, y_ref, acc_ref):
    e = pl.program_id(1)

    @pl.when(e == 0)
    def _():
        acc_ref[...] = jnp.zeros(acc_ref.shape, F32)

    hn = hn_ref[...]
    cmb = cmb_ref[...]
    wgt = jnp.sum(jnp.where(_iota(cmb.shape, 1) == e, cmb, 0.0), axis=-1, keepdims=True)
    act = _silu(_mm(hn, wg_ref[0])) * _mm(hn, wu_ref[0]) * wgt
    acc_ref[...] += _mm(act.astype(BF16), wd_ref[0])

    @pl.when(e == pl.num_programs(1) - 1)
    def _():
        t = h_ref[...] + acc_ref[...]
        y_ref[...] = t * lax.rsqrt(jnp.mean(t * t, axis=-1, keepdims=True) + RMS_EPS) * nf_ref[...]


def _moe(h, hn, cmb, wg, wu, wd, norm_f_g):
    n = h.shape[0]
    tm = min(1024, n)
    row = lambda i, e: (i, 0)
    return pl.pallas_call(
        _moe_kernel,
        grid=(n // tm, N_EXPERTS),
        in_specs=[pl.BlockSpec((tm, D_MODEL), row),
                  pl.BlockSpec((tm, D_MODEL), row),
                  pl.BlockSpec((tm, LANES), row),
                  pl.BlockSpec((1, D_MODEL, D_EXPERT), lambda i, e: (e, 0, 0)),
                  pl.BlockSpec((1, D_MODEL, D_EXPERT), lambda i, e: (e, 0, 0)),
                  pl.BlockSpec((1, D_EXPERT, D_MODEL), lambda i, e: (e, 0, 0)),
                  pl.BlockSpec((1, D_MODEL), lambda i, e: (0, 0))],
        out_specs=pl.BlockSpec((tm, D_MODEL), row),
        out_shape=jax.ShapeDtypeStruct((n, D_MODEL), F32),
        scratch_shapes=[pltpu.VMEM((tm, D_MODEL), F32)],
        compiler_params=_cparams(("parallel", "arbitrary")),
        name="moe",
    )(h, hn, cmb, wg, wu, wd, norm_f_g.reshape(1, D_MODEL))


def _route_weights(w_grp, b_grp, w_rt, b_rt):
    pad = LANES - N_EXPERTS - N_GROUPS
    w = jnp.concatenate([w_rt, w_grp, jnp.zeros((D_MODEL, pad), F32)], axis=1)
    bias = jnp.concatenate([b_rt, b_grp, jnp.zeros((pad,), F32)]).reshape(1, LANES)
    return w, bias


def _finish(x2d, ya, yb, ga, gb, wts):
    h, hn, cmb = _tail(x2d, ya, yb, ga, gb, wts["wa"], wts["wb"], wts["wo"], wts["norm2_g"],
                       wts["w_route"], wts["b_route"])
    return _moe(h, hn, cmb, wts["wg"], wts["wu"], wts["wd"], wts["norm_f_g"])


def _prompt_layer(x, wts):
    b, l, _ = x.shape
    x2d = x.reshape(b * l, D_MODEL)
    qkv, z, q8, kv4, kvw, ga, gb, sm, _, _, ksa, kwa, vst, vwt = _project(x2d, wts["norm1_g"], wts["w_pk"], l)
    conv0 = jnp.zeros((b, GDN_CONV - 1, GDN_CONV_DIM), F32)
    s0 = jnp.zeros((b, GDN_HEADS, GDN_DK, GDN_DV), F32)
    ya, s_new, conv_new = _gdn(qkv, z, sm, conv0, s0, wts["conv_w"], wts["a_log"], wts["dt_bias"],
                               wts["gdn_norm_g"], b, l)
    head, tail = _pool_rows(kv4, wts["pw"])
    kca, _, vct = _compress(head, tail, wts["cmp_wk"], wts["cmp_wv"], b)
    yb = _nsa_prompt(q8, sm, kca, vct, ksa, vst, kwa, vwt, b, l)
    y = _finish(x2d, ya, yb, ga, gb, wts)
    win_buf = min(WINDOW, l)
    win_new = kvw.reshape(b, l, 2, NSA_KV_HEADS, NSA_HD)[:, l - win_buf:]
    return (y.reshape(b, l, D_MODEL), kv4.reshape(b, l, 4, NSA_KV_HEADS, NSA_HD), win_new, s_new, conv_new)


def _sample_layer(x, cache_kv_l, page_table, cache_win_l, s0, conv_buf, wts):
    b, l, _ = x.shape
    x2d = x.reshape(b * l, D_MODEL)
    qkv, z, q8, kv4, kvw, ga, gb, sm, selbf = _project(x2d, wts["norm1_g"], wts["w_pk"], l)[:9]
    ya, s_new, conv_new = _gdn(qkv, z, sm, conv_buf, s0, wts["conv_w"], wts["a_log"], wts["dt_bias"],
                               wts["gdn_norm_g"], b, l)
    n_phys, page = cache_kv_l.shape[:2]
    cache3 = cache_kv_l.reshape(n_phys, page, 4 * NSA_KV)
    head, tail = _pool_pages(cache3, page_table, wts["pw"], 8)
    kc, vc, _ = _compress(head, tail, wts["cmp_wk"], wts["cmp_wv"], b)
    win_buf = cache_win_l.shape[1]
    win_full = jnp.concatenate([cache_win_l.reshape(b, win_buf, 2 * NSA_KV), kvw.reshape(b, l, 2 * NSA_KV)], axis=1)
    yb = _nsa_sample(q8, sm, kc, vc, selbf, win_full.astype(BF16), cache3, page_table, b, l, win_buf)
    y = _finish(x2d, ya, yb, ga, gb, wts)
    win_new = win_full[:, l:].reshape(b, win_buf, 2, NSA_KV_HEADS, NSA_HD)
    return (y.reshape(b, l, D_MODEL), kv4.reshape(b, l, 4, NSA_KV_HEADS, NSA_HD), win_new, s_new, conv_new)


def kernel(x_prompt, x_sample, cache_kv, page_table, cache_win, state_gdn, state_conv, norm1_g, w_in, gdn_conv_w, gdn_a_log, gdn_dt_bias, gdn_norm_g, cmp_pos_wk, cmp_pos_wv, cmp_wk, cmp_wv, w_branch_a, w_branch_b, w_out, norm2_g, w_grp, b_grp, w_rt, b_rt, w_e_gate, w_e_up, w_e_down, norm_f_g):
    assert w_in.shape[0] == 1, "single layer"
    w_route, b_route = _route_weights(w_grp[0], b_grp[0], w_rt[0], b_rt[0])
    wts = dict(
        norm1_g=norm1_g[0], w_pk=_pack_w_in(w_in[0]),
        conv_w=gdn_conv_w[0], a_log=gdn_a_log[0], dt_bias=gdn_dt_bias[0], gdn_norm_g=gdn_norm_g[0],
        pw=_pos_weights(cmp_pos_wk[0], cmp_pos_wv[0]), cmp_wk=cmp_wk[0], cmp_wv=cmp_wv[0],
        wa=w_branch_a[0].astype(BF16), wb=w_branch_b[0].astype(BF16), wo=w_out[0].astype(BF16),
        norm2_g=norm2_g[0], w_route=w_route, b_route=b_route,
        wg=w_e_gate[0].astype(BF16), wu=w_e_up[0].astype(BF16), wd=w_e_down[0].astype(BF16),
        norm_f_g=norm_f_g,
    )
    yp, kvp, winp, sp, cp = _prompt_layer(x_prompt, wts)
    ys, kvs, wins, ss, cs = _sample_layer(x_sample, cache_kv[0], page_table, cache_win[0], state_gdn[0],
                                          state_conv[0], wts)
    return (yp, ys, kvp[None], kvs[None], winp[None], wins[None], sp[None], ss[None], cp[None], cs[None])
```

```python
import functools
import math

import numpy as np
import jax
import jax.numpy as jnp
from jax import lax
from jax.experimental import pallas as pl
from jax.experimental.pallas import tpu as pltpu

F32 = jnp.float32
BF16 = jnp.bfloat16
HI = lax.Precision.HIGHEST

LANES = 128
SUBLANES = 8
VMEM_LIMIT = 56 * 1024 * 1024

D_MODEL = 1024
GDN_HEADS = 4
GDN_DK = 128
GDN_DV = 128
GDN_QK = GDN_HEADS * GDN_DK
GDN_V = GDN_HEADS * GDN_DV
GDN_CONV_DIM = 2 * GDN_QK + GDN_V
GDN_CONV = 4
GDN_CHUNK = 64
GDN_SEQS_PER_STEP = 4
NSA_HEADS = 8
NSA_KV_HEADS = 2
NSA_GROUP = NSA_HEADS // NSA_KV_HEADS
NSA_HD = 64
NSA_Q = NSA_HEADS * NSA_HD
NSA_KV = NSA_KV_HEADS * NSA_HD
CMP_STRIDE = 16
CMP_LEN = 2 * CMP_STRIDE
SEL_BLOCK = 64
SEL_TOP = 16
WINDOW = 512
FORCE_SCORE = 1.0e4
N_GROUPS = 4
EXPERTS_PER_GROUP = 8
N_EXPERTS = N_GROUPS * EXPERTS_PER_GROUP
D_EXPERT = 256
RMS_EPS = 1e-6
NEG_INF = float("-inf")

_OFF_QKV = 0
_OFF_Z = _OFF_QKV + GDN_CONV_DIM
_OFF_B = _OFF_Z + GDN_V
_OFF_A = _OFF_B + GDN_HEADS
_OFF_Q = _OFF_A + GDN_HEADS
_OFF_KV = _OFF_Q + NSA_Q
_OFF_G = _OFF_KV + 6 * NSA_KV
_OFF_GA = _OFF_G + 3 * NSA_HEADS
_OFF_GB = _OFF_GA + D_MODEL

SM_B = 0
SM_A = 4
SM_G = 8

_PK = {}
_c = 0
for _n, _w in (("qkv", GDN_CONV_DIM), ("z", GDN_V), ("q8", NSA_HEADS * LANES), ("kv4", 4 * NSA_KV),
               ("kvw", 2 * NSA_KV), ("ga", D_MODEL), ("gb", D_MODEL), ("sm", LANES)):
    _PK[_n] = (_c, _w)
    _c += _w
PK_DIM = _c


def _cparams(sem):
    return pltpu.CompilerParams(dimension_semantics=sem, vmem_limit_bytes=VMEM_LIMIT)


def _nt(a, b, precision=None):
    return lax.dot_general(a, b, (((1,), (1,)), ((), ())), preferred_element_type=F32, precision=precision)


def _tn(a, b, precision=None):
    return lax.dot_general(a, b, (((0,), (0,)), ((), ())), preferred_element_type=F32, precision=precision)


def _mm(a, b, precision=None):
    return jnp.dot(a, b, preferred_element_type=F32, precision=precision)


def _sigmoid(x):
    return 1.0 / (1.0 + jnp.exp(-x))


def _silu(x):
    return x * _sigmoid(x)


def _iota(shape, dim):
    return lax.broadcasted_iota(jnp.int32, shape, dim)


def _pack_w_in(w_in):
    q = w_in[:, _OFF_Q:_OFF_Q + NSA_Q]
    zeros64 = jnp.zeros((D_MODEL, NSA_HD), w_in.dtype)
    q8 = []
    for h in range(NSA_HEADS):
        qh = q[:, h * NSA_HD:(h + 1) * NSA_HD]
        q8.append(jnp.concatenate([qh, zeros64] if h < NSA_GROUP else [zeros64, qh], axis=1))
    sm = jnp.concatenate([w_in[:, _OFF_B:_OFF_B + GDN_HEADS], w_in[:, _OFF_A:_OFF_A + GDN_HEADS],
                          w_in[:, _OFF_G:_OFF_G + 3 * NSA_HEADS],
                          jnp.zeros((D_MODEL, LANES - 2 * GDN_HEADS - 3 * NSA_HEADS), w_in.dtype)], axis=1)
    cols = [w_in[:, _OFF_QKV:_OFF_QKV + GDN_CONV_DIM], w_in[:, _OFF_Z:_OFF_Z + GDN_V]] + q8 + [
        w_in[:, _OFF_KV:_OFF_KV + 4 * NSA_KV], w_in[:, _OFF_KV + 4 * NSA_KV:_OFF_KV + 6 * NSA_KV],
        w_in[:, _OFF_GA:_OFF_GA + D_MODEL], w_in[:, _OFF_GB:_OFF_GB + D_MODEL], sm]
    return jnp.concatenate(cols, axis=1).astype(BF16)


def _pos_features(pos, shape):
    lane = _iota(shape, 1)
    feat = jnp.where(lane == 0, (pos >> 7) * LANES, jnp.where(lane == 1, pos & (LANES - 1),
                                                               jnp.where(lane < 4, 1, 0)))
    return feat.astype(F32).astype(BF16)


def _proj_kernel(x_ref, g_ref, w_ref, qkv_ref, z_ref, q8_ref, ga_ref, gb_ref, sm_ref, *kv_refs, tiles_per_seq):
    tm = x_ref.shape[0]
    x = x_ref[...]
    xn = (x * lax.rsqrt(jnp.mean(x * x, axis=-1, keepdims=True) + RMS_EPS) * g_ref[...]).astype(BF16)

    def seg(name):
        a, w = _PK[name]
        return _mm(xn, w_ref[:, a:a + w])

    qkv_ref[...] = seg("qkv")
    z_ref[...] = seg("z")
    q8_ref[...] = (seg("q8") * (NSA_HD ** -0.5)).astype(BF16)
    ga_ref[...] = seg("ga")
    gb_ref[...] = seg("gb")
    sm_ref[...] = seg("sm")
    kv4 = seg("kv4")
    kvw = seg("kvw")
    if tiles_per_seq == 0:
        kv4_ref, kvw_ref = kv_refs
        kv4_ref[...] = kv4
        kvw_ref[...] = kvw
        return
    kv4t_ref, kvwt_ref, cmp_ref, ksa_ref, kwa_ref, vst_ref, vwt_ref = kv_refs
    kv4t = jnp.transpose(kv4)
    kvwt = jnp.transpose(kvw)
    kv4t_ref[...] = kv4t
    kvwt_ref[...] = kvwt
    cmp_ref[...] = kv4[:, :2 * NSA_KV]
    pos = (pl.program_id(0) % tiles_per_seq) * tm + _iota((tm, 1), 0)
    feat = _pos_features(pos, (tm, LANES))
    ksa_ref[:, :NSA_KV] = kv4[:, 2 * NSA_KV:3 * NSA_KV].astype(BF16)
    ksa_ref[:, NSA_KV:] = feat
    kwa_ref[:, :NSA_KV] = kvw[:, :NSA_KV].astype(BF16)
    kwa_ref[:, NSA_KV:] = feat
    vst_ref[...] = kv4t[3 * NSA_KV:].astype(BF16)
    vwt_ref[...] = kvwt[NSA_KV:].astype(BF16)


def _project(x2d, norm_g, w_pk, b, l):
    n = x2d.shape[0]
    tm = min(256, n)
    long_seq = l % tm == 0
    tps = l // tm if long_seq else 0
    w = 2 * NSA_KV
    row = lambda i: (i, 0)
    out_shape = [jax.ShapeDtypeStruct((n, _PK[k][1]), dt) for k, dt in
                 (("qkv", F32), ("z", F32), ("q8", BF16), ("ga", F32), ("gb", F32), ("sm", F32))]
    out_specs = [pl.BlockSpec((tm, s.shape[1]), row) for s in out_shape]
    if long_seq:
        seq_t = lambda i: (i // tps, 0, i % tps)
        out_shape += [jax.ShapeDtypeStruct((b, 2 * w, l), F32), jax.ShapeDtypeStruct((b, w, l), F32),
                      jax.ShapeDtypeStruct((n, w), F32), jax.ShapeDtypeStruct((n, w), BF16),
                      jax.ShapeDtypeStruct((n, w), BF16), jax.ShapeDtypeStruct((NSA_KV, n), BF16),
                      jax.ShapeDtypeStruct((NSA_KV, n), BF16)]
        out_specs += [pl.BlockSpec((None, 2 * w, tm), seq_t), pl.BlockSpec((None, w, tm), seq_t),
                      pl.BlockSpec((tm, w), row), pl.BlockSpec((tm, w), row), pl.BlockSpec((tm, w), row),
                      pl.BlockSpec((NSA_KV, tm), lambda i: (0, i)), pl.BlockSpec((NSA_KV, tm), lambda i: (0, i))]
    else:
        out_shape += [jax.ShapeDtypeStruct((n, 2 * w), F32), jax.ShapeDtypeStruct((n, w), F32)]
        out_specs += [pl.BlockSpec((tm, 2 * w), row), pl.BlockSpec((tm, w), row)]
    return pl.pallas_call(
        functools.partial(_proj_kernel, tiles_per_seq=tps),
        grid=(n // tm,),
        in_specs=[pl.BlockSpec((tm, D_MODEL), row),
                  pl.BlockSpec((1, D_MODEL), lambda i: (0, 0)),
                  pl.BlockSpec((D_MODEL, PK_DIM), lambda i: (0, 0), pipeline_mode=pl.Buffered(1))],
        out_specs=out_specs,
        out_shape=out_shape,
        compiler_params=_cparams(("parallel",)),
        name="proj",
    )(x2d, norm_g.reshape(1, D_MODEL), w_pk)


def _unit_lower_inverses(a_list, c):
    r = _iota((c, c), 0)
    col = _iota((c, c), 1)
    eye = (r == col).astype(F32)
    n1 = [jnp.where((r >> 3) == (col >> 3), -a, 0.0) for a in a_list]
    n2 = [_mm(x, x, HI) for x in n1]
    n4 = [_mm(x, x, HI) for x in n2]
    t = [_mm(eye + x, eye + y, HI) for x, y in zip(n1, n2)]
    t = [_mm(x, eye + y, HI) for x, y in zip(t, n4)]
    s = SUBLANES
    while s < c:
        sh = s.bit_length() - 1
        off = ((r >> (sh + 1)) == (col >> (sh + 1))) & ((r >> sh) != (col >> sh))
        ta = [_mm(x, jnp.where(off, a, 0.0), HI) for x, a in zip(t, a_list)]
        t = [x - _mm(y, x, HI) for x, y in zip(t, ta)]
        s *= 2
    return t


def _gdn_kernel(qkv_ref, z_ref, sm_ref, cbuf_ref, s0_ref, cw_ref, alog_ref, dtb_ref, ng_ref,
                y_ref, snew_ref, cnew_ref, ext_ref, st_ref, *, chunk, nb):
    c = chunk
    ci = pl.program_id(1)

    @pl.when(ci == 0)
    def _():
        ext_ref[:, 0:SUBLANES, :] = cbuf_ref[...]
        st_ref[...] = s0_ref[...]

    r = _iota((c, c), 0)
    col = _iota((c, c), 1)
    tri_incl = r >= col
    tri_strict = r > col
    tri_f = tri_incl.astype(F32)
    pick = (_iota((SUBLANES, LANES), 1) == _iota((SUBLANES, LANES), 0) + SM_A).astype(F32)
    base = SUBLANES - (GDN_CONV - 1)
    ng = ng_ref[...]

    units = []
    for bb in range(nb):
        u = qkv_ref[bb]
        ext_ref[bb, SUBLANES:SUBLANES + c, :] = u
        conv = cw_ref[0:1, :] * ext_ref[bb, base:base + c, :]
        for i in range(1, GDN_CONV - 1):
            conv = conv + cw_ref[i:i + 1, :] * ext_ref[bb, base + i:base + i + c, :]
        conv = conv + cw_ref[GDN_CONV - 1:GDN_CONV, :] * u
        halo = ext_ref[bb, c:c + SUBLANES, :]
        ext_ref[bb, 0:SUBLANES, :] = halo
        cnew_ref[bb] = halo
        qkv = _silu(conv)

        sm = sm_ref[bb]
        beta_all = _sigmoid(sm)
        xa = sm + dtb_ref[...]
        softplus = jnp.maximum(xa, 0.0) + jnp.log(1.0 + jnp.exp(-jnp.abs(xa)))
        g_all = -jnp.exp(alog_ref[...]) * softplus
        gcum_all = _mm(tri_f, g_all, HI)
        gcum_rows = _nt(pick, gcum_all, HI)
        for h in range(GDN_HEADS):
            q = qkv[:, h * GDN_DK:(h + 1) * GDN_DK]
            k = qkv[:, GDN_QK + h * GDN_DK:GDN_QK + (h + 1) * GDN_DK]
            v = qkv[:, 2 * GDN_QK + h * GDN_DV:2 * GDN_QK + (h + 1) * GDN_DV]
            q = q * lax.rsqrt(jnp.sum(q * q, axis=-1, keepdims=True) + RMS_EPS) * (GDN_DK ** -0.5)
            k = k * lax.rsqrt(jnp.sum(k * k, axis=-1, keepdims=True) + RMS_EPS)
            beta = beta_all[:, SM_B + h:SM_B + h + 1]
            gc_col = gcum_all[:, SM_A + h:SM_A + h + 1]
            gc_row = gcum_rows[h:h + 1, :]
            gc_last = gcum_all[c - 1:c, SM_A + h:SM_A + h + 1]
            units.append(dict(bb=bb, h=h, q=q, k=k, v=v, beta=beta, gc_col=gc_col, gc_last=gc_last,
                              decay=jnp.exp(jnp.where(tri_incl, gc_col - gc_row, NEG_INF)),
                              eg=jnp.exp(gc_col), kb=k * beta, k_bf=k.astype(BF16)))

    a_list = [jnp.where(tri_strict, _nt(un["kb"].astype(BF16), un["k_bf"]) * un["decay"], 0.0) for un in units]
    t_list = _unit_lower_inverses(a_list, c)
    u_coef = [_mm(t, un["v"] * un["beta"], HI) for t, un in zip(t_list, units)]
    w_coef = [_mm(t, un["kb"] * un["eg"], HI) for t, un in zip(t_list, units)]
    qk = [(_nt(un["q"].astype(BF16), un["k_bf"]) * un["decay"]).astype(BF16) for un in units]
    s_old = [st_ref[un["bb"], un["h"]] for un in units]
    s_bf = [s.astype(BF16) for s in s_old]
    uu = [uc - _mm(wc.astype(BF16), s) for uc, wc, s in zip(u_coef, w_coef, s_bf)]
    uu_bf = [x.astype(BF16) for x in uu]
    o_list = [_mm((un["q"] * un["eg"]).astype(BF16), s) + _mm(a, x)
              for un, s, a, x in zip(units, s_bf, qk, uu_bf)]
    for un, s, x, o in zip(units, s_old, uu_bf, o_list):
        bb, h = un["bb"], un["h"]
        k_tail = un["k"] * jnp.exp(un["gc_last"] - un["gc_col"])
        st_ref[bb, h] = s * jnp.exp(un["gc_last"]) + _tn(k_tail.astype(BF16), x)
        on = o * lax.rsqrt(jnp.mean(o * o, axis=-1, keepdims=True) + RMS_EPS) * ng
        y_ref[bb, :, h * GDN_DV:(h + 1) * GDN_DV] = on * _silu(z_ref[bb, :, h * GDN_DV:(h + 1) * GDN_DV])

    @pl.when(ci == pl.num_programs(1) - 1)
    def _():
        snew_ref[...] = st_ref[...]


def _gdn(qkv, z, sm, conv_buf, s0, conv_w, a_log, dt_bias, norm_g, b, l):
    c = math.gcd(l, GDN_CHUNK)
    nc = l // c
    nb = math.gcd(b, GDN_SEQS_PER_STEP)
    cbuf8 = jnp.pad(conv_buf, ((0, 0), (SUBLANES - (GDN_CONV - 1), 0), (0, 0)))
    pad_a = (SM_A, LANES - SM_A - GDN_HEADS)
    alog_row = jnp.pad(a_log, pad_a).reshape(1, LANES)
    dtb_row = jnp.pad(dt_bias, pad_a).reshape(1, LANES)
    row = lambda bi, ci: (bi, ci, 0)
    per_seq3 = lambda bi, ci: (bi, 0, 0)
    per_seq4 = lambda bi, ci: (bi, 0, 0, 0)
    fixed = lambda bi, ci: (0, 0)
    y, s_new, c_new = pl.pallas_call(
        functools.partial(_gdn_kernel, chunk=c, nb=nb),
        grid=(b // nb, nc),
        in_specs=[pl.BlockSpec((nb, c, GDN_CONV_DIM), row),
                  pl.BlockSpec((nb, c, GDN_V), row),
                  pl.BlockSpec((nb, c, LANES), row),
                  pl.BlockSpec((nb, SUBLANES, GDN_CONV_DIM), per_seq3),
                  pl.BlockSpec((nb, GDN_HEADS, GDN_DK, GDN_DV), per_seq4),
                  pl.BlockSpec((GDN_CONV, GDN_CONV_DIM), fixed),
                  pl.BlockSpec((1, LANES), fixed),
                  pl.BlockSpec((1, LANES), fixed),
                  pl.BlockSpec((1, GDN_DV), fixed)],
        out_specs=[pl.BlockSpec((nb, c, GDN_V), row),
                   pl.BlockSpec((nb, GDN_HEADS, GDN_DK, GDN_DV), per_seq4),
                   pl.BlockSpec((nb, SUBLANES, GDN_CONV_DIM), per_seq3)],
        out_shape=[jax.ShapeDtypeStruct((b, l, GDN_V), F32),
                   jax.ShapeDtypeStruct((b, GDN_HEADS, GDN_DK, GDN_DV), F32),
                   jax.ShapeDtypeStruct((b, SUBLANES, GDN_CONV_DIM), F32)],
        scratch_shapes=[pltpu.VMEM((nb, c + SUBLANES, GDN_CONV_DIM), F32),
                        pltpu.VMEM((nb, GDN_HEADS, GDN_DK, GDN_DV), F32)],
        compiler_params=_cparams(("parallel", "arbitrary")),
        name="gdn",
    )(qkv.reshape(b, l, GDN_CONV_DIM), z.reshape(b, l, GDN_V), sm.reshape(b, l, LANES), cbuf8, s0, conv_w,
      alog_row, dtb_row, norm_g.reshape(1, GDN_DV))
    return y.reshape(b * l, GDN_V), s_new, c_new[:, SUBLANES - (GDN_CONV - 1):]


def _pool_kernel(*refs, n_in):
    refs = refs[len(refs) - 2 * n_in - 3:]
    x_refs, pw_ref, head_ref, tail_ref = refs[:2 * n_in], refs[2 * n_in], refs[2 * n_in + 1], refs[2 * n_in + 2]
    rows = x_refs[0].shape[0]
    n = rows // CMP_STRIDE
    for j, x_ref in enumerate(x_refs):
        half = slice((j % 2) * NSA_KV, (j % 2 + 1) * NSA_KV)
        head = None
        tail = None
        for p in range(CMP_STRIDE):
            xr = x_ref[pl.ds(p, n, stride=CMP_STRIDE), :]
            hp = xr * pw_ref[p:p + 1, half]
            tp = xr * pw_ref[CMP_STRIDE + p:CMP_STRIDE + p + 1, half]
            head = hp if head is None else head + hp
            tail = tp if tail is None else tail + tp
        head_ref[(j // 2) * n:(j // 2 + 1) * n, half] = head
        tail_ref[(j // 2) * n:(j // 2 + 1) * n, half] = tail


def _pos_weights(pos_wk, pos_wv):
    return jnp.concatenate([jnp.broadcast_to(pos_wk[:, None], (CMP_LEN, NSA_KV)),
                            jnp.broadcast_to(pos_wv[:, None], (CMP_LEN, NSA_KV))], axis=1).astype(F32)


def _pool_rows(kv4, pw):
    n = kv4.shape[0]
    r = min(2048, n)
    w = 2 * NSA_KV
    return pl.pallas_call(
        functools.partial(_pool_kernel, n_in=1),
        grid=(n // r,),
        in_specs=[pl.BlockSpec((r, NSA_KV), lambda i: (i, 0)),
                  pl.BlockSpec((r, NSA_KV), lambda i: (i, 1)),
                  pl.BlockSpec((CMP_LEN, w), lambda i: (0, 0))],
        out_specs=[pl.BlockSpec((r // CMP_STRIDE, w), lambda i: (i, 0))] * 2,
        out_shape=[jax.ShapeDtypeStruct((n // CMP_STRIDE, w), F32)] * 2,
        compiler_params=_cparams(("parallel",)),
        name="pool_rows",
    )(kv4, kv4, pw)


def _pool_pages_kernel(*refs, pg):
    page_refs = refs[1:1 + pg]
    pk_hi, pk_lo, pv_hi, pv_lo, head_ref, tail_ref = refs[1 + pg:]
    x = jnp.concatenate([r[...] for r in page_refs], axis=1)
    n = head_ref.shape[1]
    for half, (p_hi, p_lo) in enumerate(((pk_hi, pk_lo), (pv_hi, pv_lo))):
        xs = x[half * NSA_KV:(half + 1) * NSA_KV]
        hi = xs.astype(BF16)
        lo = (xs - hi.astype(F32)).astype(BF16)
        out = _mm(hi, p_hi[...]) + _mm(lo, p_hi[...]) + _mm(hi, p_lo[...])
        head_ref[half * NSA_KV:(half + 1) * NSA_KV, :] = out[:, :n]
        tail_ref[half * NSA_KV:(half + 1) * NSA_KV, :] = out[:, n:]


def _pool_matrix(pos_w, positions):
    chunks = positions // CMP_STRIDE
    p = jnp.arange(positions)[:, None]
    c = jnp.arange(chunks)[None, :]
    inside = (p // CMP_STRIDE) == c
    head = jnp.where(inside, pos_w[:CMP_STRIDE][p % CMP_STRIDE], 0.0)
    tail = jnp.where(inside, pos_w[CMP_STRIDE:][p % CMP_STRIDE], 0.0)
    m = jnp.concatenate([head, tail], axis=1).astype(F32)
    hi = m.astype(BF16)
    return hi, (m - hi.astype(F32)).astype(BF16)


def _pool_pages(cache_t, page_table, pos_wk, pos_wv):
    b, n_pages = page_table.shape
    page = cache_t.shape[2]
    pg = LANES * CMP_STRIDE // page
    w = 2 * NSA_KV
    steps = n_pages // pg
    n_ch = n_pages * page // CMP_STRIDE
    mats = _pool_matrix(pos_wk, pg * page) + _pool_matrix(pos_wv, pg * page)

    def page_map(k):
        return lambda bi, j, pt: (pt[bi, j * pg + k], 0, 0)

    fixed = lambda bi, j, pt: (0, 0)
    grid_spec = pltpu.PrefetchScalarGridSpec(
        num_scalar_prefetch=1,
        grid=(b, steps),
        in_specs=[pl.BlockSpec((None, w, page), page_map(k)) for k in range(pg)]
        + [pl.BlockSpec((pg * page, 2 * LANES), fixed)] * 4,
        out_specs=[pl.BlockSpec((None, w, LANES), lambda bi, j, pt: (bi, 0, j))] * 2,
    )
    return pl.pallas_call(
        functools.partial(_pool_pages_kernel, pg=pg),
        grid_spec=grid_spec,
        out_shape=[jax.ShapeDtypeStruct((b, w, n_ch), F32)] * 2,
        compiler_params=_cparams(("parallel", "arbitrary")),
        name="pool_pages",
    )(page_table, *([cache_t] * pg), *mats)


def _cmp_t_kernel(head_ref, tail_ref, wk_ref, wv_ref, kct_ref, vct_ref):
    n = head_ref.shape[1]
    blocks = head_ref[...] + pltpu.roll(tail_ref[...], n - 1, 1)
    kct_ref[...] = _mm(wk_ref[...], blocks[:NSA_KV].astype(BF16)).astype(BF16)
    vct_ref[...] = _mm(wv_ref[...], blocks[NSA_KV:].astype(BF16)).astype(BF16)


def _compress_t(head_t, tail_t, cmp_wk, cmp_wv):
    b, w, n_ch = head_t.shape
    seq = lambda i: (i, 0, 0)
    return pl.pallas_call(
        _cmp_t_kernel,
        grid=(b,),
        in_specs=[pl.BlockSpec((None, w, n_ch), seq)] * 2 + [pl.BlockSpec((NSA_KV, NSA_KV), lambda i: (0, 0))] * 2,
        out_specs=[pl.BlockSpec((None, NSA_KV, n_ch), seq)] * 2,
        out_shape=[jax.ShapeDtypeStruct((b, NSA_KV, n_ch), BF16)] * 2,
        compiler_params=_cparams(("parallel",)),
        name="compress_t",
    )(head_t, tail_t, jnp.transpose(_block_diag2(cmp_wk)), jnp.transpose(_block_diag2(cmp_wv)))


def _cmp_kernel(head_ref, tail_ref, wk_ref, wv_ref, kca_ref, vct_ref):
    n = head_ref.shape[0]
    blocks = head_ref[...] + pltpu.roll(tail_ref[...], n - 1, 0)
    kca_ref[:, :NSA_KV] = _mm(blocks[:, :NSA_KV].astype(BF16), wk_ref[...]).astype(BF16)
    blk_end = _iota((n, 1), 0) * CMP_STRIDE + (CMP_LEN - 1)
    kca_ref[:, NSA_KV:] = _pos_features(blk_end, (n, LANES))
    vct_ref[...] = _nt(wv_ref[...], blocks[:, NSA_KV:].astype(BF16)).astype(BF16)


def _block_diag2(w):
    z = jnp.zeros_like(w)
    return jnp.concatenate([jnp.concatenate([w, z], axis=1), jnp.concatenate([z, w], axis=1)], axis=0).astype(BF16)


def _compress(head, tail, cmp_wk, cmp_wv, b):
    n_ch = head.shape[0] // b
    w = 2 * NSA_KV
    return pl.pallas_call(
        _cmp_kernel,
        grid=(b,),
        in_specs=[pl.BlockSpec((n_ch, w), lambda i: (i, 0))] * 2
        + [pl.BlockSpec((NSA_KV, NSA_KV), lambda i: (0, 0))] * 2,
        out_specs=[pl.BlockSpec((n_ch, w), lambda i: (i, 0)),
                   pl.BlockSpec((NSA_KV, n_ch), lambda i: (0, i))],
        out_shape=[jax.ShapeDtypeStruct((b * n_ch, w), BF16),
                   jax.ShapeDtypeStruct((NSA_KV, b * n_ch), BF16)],
        compiler_params=_cparams(("parallel",)),
        name="compress",
    )(head, tail, _block_diag2(cmp_wk), jnp.transpose(_block_diag2(cmp_wv)))


def _overlap(n_ch, nsp):
    i = jnp.arange(n_ch)[:, None]
    j = jnp.arange(nsp)[None, :]
    lo = jnp.maximum(i * CMP_STRIDE, j * SEL_BLOCK)
    hi = jnp.minimum(i * CMP_STRIDE + CMP_LEN, (j + 1) * SEL_BLOCK)
    ov = jnp.maximum(hi - lo, 0).astype(F32) / CMP_LEN
    return jnp.where(i < n_ch - 1, ov, 0.0)


def _slope(h):
    return 2.0 ** (-(h + 1))


def _softmax_rows(s, mask):
    s = jnp.where(mask, s, NEG_INF)
    m = jnp.max(s, axis=-1, keepdims=True)
    m = jnp.where(m == NEG_INF, 0.0, m)
    e = jnp.where(mask, jnp.exp(s - m), 0.0)
    return e, jnp.maximum(jnp.sum(e, axis=-1, keepdims=True), 1e-30)


def _top_blocks(imp, qpos, ns_lanes):
    blk = _iota(imp.shape, 1)
    cur = qpos >> 6
    forced = (blk == 0) | (blk == cur) | (blk == cur - 1)
    score = jnp.where(forced, FORCE_SCORE, imp)
    work = jnp.where(blk <= cur, score, NEG_INF)
    sel = jnp.zeros(imp.shape, F32)
    for _ in range(SEL_TOP):
        m = jnp.max(work, axis=-1, keepdims=True)
        cand = jnp.where((work == m) & (m > NEG_INF), blk, ns_lanes)
        first = jnp.min(cand, axis=-1, keepdims=True)
        hit = blk == first
        sel = jnp.where(hit, 1.0, sel)
        work = jnp.where(hit, NEG_INF, work)
    return sel


def _row_slopes(tq):
    head = _iota((NSA_HEADS * tq, 1), 0) >> (tq.bit_length() - 1)
    slope = jnp.zeros((NSA_HEADS * tq, 1), F32)
    for h in range(NSA_HEADS):
        slope = jnp.where(head == h, _slope(h), slope)
    return slope


def _rows_attend(q8, kt, vt, slope, dist, mask):
    s = _mm(q8, kt) - slope * dist.astype(F32)
    e, den = _softmax_rows(s, mask)
    p = e / den
    return _nt(p.astype(BF16), vt), p


def _rows_online(q8, kt, vt, slope, dist, mask, m_ref, l_ref, acc_ref):
    s = jnp.where(mask, _mm(q8, kt) - slope * dist.astype(F32), NEG_INF)
    m_old = m_ref[...]
    m_new = jnp.maximum(m_old, jnp.max(s, axis=-1, keepdims=True))
    m_use = jnp.where(m_new == NEG_INF, 0.0, m_new)
    alpha = jnp.exp(m_old - m_use)
    e = jnp.exp(s - m_use)
    l_ref[...] = alpha * l_ref[...] + jnp.sum(e, axis=-1, keepdims=True)
    acc_ref[...] = alpha * acc_ref[...] + _nt(e.astype(BF16), vt)
    m_ref[...] = m_new


def _merge_heads(o_c, o_s, o_w, gates, tq):
    lane = _iota((tq, LANES), 1)
    mixed = []
    for h in range(NSA_HEADS):
        rows = slice(h * tq, (h + 1) * tq)
        g0 = gates[:, SM_G + 3 * h:SM_G + 3 * h + 1]
        g1 = gates[:, SM_G + 3 * h + 1:SM_G + 3 * h + 2]
        g2 = gates[:, SM_G + 3 * h + 2:SM_G + 3 * h + 3]
        mixed.append(g0 * o_c[rows] + g1 * o_s[rows] + g2 * o_w[rows])
    cols = []
    for p in range(NSA_HEADS // 2):
        a, b = mixed[2 * p], mixed[2 * p + 1]
        if 2 * p < NSA_GROUP:
            cols.append(jnp.where(lane < NSA_HD, a, pltpu.roll(b, NSA_HD, 1)))
        else:
            cols.append(jnp.where(lane < NSA_HD, pltpu.roll(a, NSA_HD, 1), b))
    return jnp.concatenate(cols, axis=1)


def _softmax_cols(s, mask):
    s = jnp.where(mask, s, NEG_INF)
    m = jnp.max(s, axis=0, keepdims=True)
    m = jnp.where(m == NEG_INF, 0.0, m)
    e = jnp.exp(s - m)
    return e * (1.0 / jnp.maximum(jnp.sum(e, axis=0, keepdims=True), 1e-30))


def _split3(x):
    hi = x.astype(BF16)
    r1 = x - hi.astype(F32)
    mid = r1.astype(BF16)
    lo = (r1 - mid.astype(F32)).astype(BF16)
    return hi, mid, lo


def _top_blocks_cols(imp, qpos, nsp):
    blk = _iota(imp.shape, 0)
    cur = qpos >> 6
    forced = (blk == 0) | (blk == cur) | (blk == cur - 1)
    work = jnp.where(blk <= cur, jnp.where(forced, FORCE_SCORE, imp), NEG_INF)
    neg = jnp.full(imp.shape, NEG_INF, F32)
    for _ in range(SEL_TOP):
        m = jnp.max(work, axis=0, keepdims=True)
        cand = jnp.where((work == m) & (m > NEG_INF), blk, nsp)
        hit = blk == jnp.min(cand, axis=0, keepdims=True)
        neg = jnp.where(hit, 0.0, neg)
        work = jnp.where(hit, NEG_INF, work)
    return neg


def _nsa_prompt_kernel(q8_ref, sm_ref, kc_ref, vct_ref, ks_ref, vst_ref, kw_ref, vwt_ref, ovt_ref, y_ref,
                       neg_ref, m_ref, l_ref, acc_ref, qta_ref, sa_ref, sb_ref, *, tq, tk):
    i = pl.program_id(1)
    start = i * tq
    r = NSA_HEADS * tq
    nsp, n_ch = ovt_ref.shape
    col = _iota((1, r), 1)
    qpos = start + (col & (tq - 1))

    q8 = jnp.concatenate([q8_ref[:, h * LANES:(h + 1) * LANES] for h in range(NSA_HEADS)], axis=0)
    eye = (_iota((LANES, LANES), 0) == _iota((LANES, LANES), 1)).astype(BF16)
    qt = _nt(eye, q8).astype(BF16)
    head = col >> (tq.bit_length() - 1)
    slope = jnp.zeros((1, r), F32)
    for h in range(NSA_HEADS):
        slope = jnp.where(head == h, _slope(h), slope)
    frow = _iota((2 * SUBLANES, r), 0)
    feat = jnp.where(frow < 2, slope, jnp.where(frow == 2, -slope * ((qpos >> 7) * LANES).astype(F32),
                                                jnp.where(frow == 3, -slope * (qpos & (LANES - 1)).astype(F32), 0.0)))
    qta = jnp.concatenate([qt, feat.astype(BF16), jnp.zeros((LANES - 2 * SUBLANES, r), BF16)], axis=0)

    blk_end = _iota((n_ch, 1), 0) * CMP_STRIDE + (CMP_LEN - 1)
    p = _softmax_cols(_mm(kc_ref[...], qta), blk_end <= qpos)
    o_c = _mm(vct_ref[...], p.astype(BF16))
    psum = []
    for g in range(NSA_KV_HEADS):
        acc = p[:, g * NSA_GROUP * tq:(g * NSA_GROUP + 1) * tq]
        for hh in range(1, NSA_GROUP):
            acc = acc + p[:, (g * NSA_GROUP + hh) * tq:(g * NSA_GROUP + hh + 1) * tq]
        psum.append(acc)
    ovt = ovt_ref[...]
    imp = sum(_mm(ovt, piece) for piece in _split3(jnp.concatenate(psum, axis=1)))
    qpos2 = start + (_iota((1, NSA_KV_HEADS * tq), 1) & (tq - 1))
    neg = _top_blocks_cols(imp, qpos2, nsp)
    for h in range(NSA_HEADS):
        g = h // NSA_GROUP
        neg_ref[:, h * tq:(h + 1) * tq] = neg[:, g * tq:(g + 1) * tq]

    m_ref[...] = jnp.full(m_ref.shape, NEG_INF, F32)
    l_ref[...] = jnp.zeros(l_ref.shape, F32)
    acc_ref[...] = jnp.zeros(acc_ref.shape, F32)
    bpt = tk // SEL_BLOCK

    qta_ref[...] = qta

    def scores(t, buf):
        off = pl.multiple_of(t * tk, tk)
        buf[...] = _mm(ks_ref[pl.ds(off, tk), :], qta_ref[...])

    def tile(t, buf, causal):
        off = pl.multiple_of(t * tk, tk)
        s = jnp.concatenate([buf[j * SEL_BLOCK:(j + 1) * SEL_BLOCK, :] + neg_ref[pl.ds(t * bpt + j, 1), :]
                             for j in range(bpt)], axis=0)
        if causal:
            s = jnp.where(off + _iota((tk, 1), 0) <= qpos, s, NEG_INF)
        m_old = m_ref[...]
        m_new = jnp.maximum(m_old, jnp.max(s, axis=0, keepdims=True))
        m_use = jnp.where(m_new == NEG_INF, 0.0, m_new)
        alpha = jnp.exp(m_old - m_use)
        e = jnp.exp(s - m_use)
        l_ref[...] = alpha * l_ref[...] + jnp.sum(e, axis=0, keepdims=True)
        acc_ref[...] = alpha * acc_ref[...] + _mm(vst_ref[:, pl.ds(off, tk)], e.astype(BF16))
        m_ref[...] = m_new

    n_full = start // tk
    pairs = n_full // 2
    scores(0, sa_ref)

    def body(u, carry):
        scores(2 * u + 1, sb_ref)
        tile(2 * u, sa_ref, False)
        scores(2 * u + 2, sa_ref)
        tile(2 * u + 1, sb_ref, False)
        return carry

    lax.fori_loop(0, pairs, body, 0)

    @pl.when(n_full % 2 == 1)
    def _():
        scores(n_full, sb_ref)
        tile(n_full - 1, sa_ref, False)
        tile(n_full, sb_ref, True)

    @pl.when(n_full % 2 == 0)
    def _():
        tile(n_full, sa_ref, True)
    o_s = acc_ref[...] * (1.0 / jnp.maximum(l_ref[...], 1e-30))

    wl = tq + WINDOW
    ws = pl.multiple_of(jnp.maximum(start - WINDOW, 0), tq)
    dist = qpos - (ws + _iota((wl, 1), 0))
    pw = _softmax_cols(_mm(kw_ref[pl.ds(ws, wl), :], qta), (dist >= 0) & (dist <= WINDOW))
    o_w = _mm(vwt_ref[:, pl.ds(ws, wl)], pw.astype(BF16))

    gt = jnp.transpose(_sigmoid(sm_ref[...]))
    gate = [jnp.concatenate([gt[SM_G + 3 * h + c:SM_G + 3 * h + c + 1, :] for h in range(NSA_HEADS)], axis=1)
            for c in range(3)]
    o = gate[0] * o_c + gate[1] * o_s + gate[2] * o_w
    yt = jnp.concatenate([o[(h // NSA_GROUP) * NSA_HD:(h // NSA_GROUP + 1) * NSA_HD, h * tq:(h + 1) * tq]
                          for h in range(NSA_HEADS)], axis=0)
    y_ref[...] = jnp.transpose(yt)


def _nsa_prompt(q8, sm, kca, vct, ksa, vst, kwa, vwt, b, l):
    tq = 128
    tk = 256
    assert l % tk == 0 and l >= tq + WINDOW
    n_ch = l // CMP_STRIDE
    ns = -(-l // SEL_BLOCK)
    nsp = -(-ns // LANES) * LANES
    nq = l // tq
    w = 2 * NSA_KV
    ovt = jnp.transpose(_overlap(n_ch, nsp)).astype(BF16)
    row = lambda bi, i: (bi * nq + i, 0)
    seq_rows = lambda bi, i: (bi, 0)
    seq_cols = lambda bi, i: (0, bi)
    return pl.pallas_call(
        functools.partial(_nsa_prompt_kernel, tq=tq, tk=tk),
        grid=(b, nq),
        in_specs=[pl.BlockSpec((tq, NSA_HEADS * LANES), row),
                  pl.BlockSpec((tq, LANES), row),
                  pl.BlockSpec((n_ch, w), seq_rows),
                  pl.BlockSpec((NSA_KV, n_ch), seq_cols),
                  pl.BlockSpec((l, w), seq_rows),
                  pl.BlockSpec((NSA_KV, l), seq_cols),
                  pl.BlockSpec((l, w), seq_rows),
                  pl.BlockSpec((NSA_KV, l), seq_cols),
                  pl.BlockSpec((nsp, n_ch), lambda bi, i: (0, 0))],
        out_specs=pl.BlockSpec((tq, NSA_Q), row),
        out_shape=jax.ShapeDtypeStruct((b * l, NSA_Q), F32),
        scratch_shapes=[pltpu.VMEM((nsp, NSA_HEADS * tq), F32),
                        pltpu.VMEM((1, NSA_HEADS * tq), F32),
                        pltpu.VMEM((1, NSA_HEADS * tq), F32),
                        pltpu.VMEM((NSA_KV, NSA_HEADS * tq), F32),
                        pltpu.VMEM((2 * NSA_KV, NSA_HEADS * tq), BF16),
                        pltpu.VMEM((tk, NSA_HEADS * tq), F32),
                        pltpu.VMEM((tk, NSA_HEADS * tq), F32)],
        compiler_params=_cparams(("parallel", "arbitrary")),
        name="nsa_prompt",
    )(q8, sm, kca, vct, ksa, vst, kwa, vwt, ovt)


def _nsa_sample_kernel(*refs, pg, page, tq, past_len):
    pt_ref = refs[0]
    del pt_ref
    page_refs = refs[1:1 + pg]
    (q8_ref, sm_ref, kc_ref, vc_ref, knew_ref, kw_ref, ov_ref, y_ref,
     m_ref, l_ref, acc_ref, sel_ref, oc_ref) = refs[1 + pg:]
    j = pl.program_id(1)
    nsp = ov_ref.shape[1]
    r = NSA_HEADS * tq
    q8 = jnp.concatenate([q8_ref[:, h * LANES:(h + 1) * LANES].astype(F32) for h in range(NSA_HEADS)],
                         axis=0).astype(BF16)
    qpos = past_len + (_iota((r, 1), 0) & (tq - 1))
    slope = _row_slopes(tq)

    @pl.when(j == 0)
    def _():
        n_ch = kc_ref.shape[1]
        dist = qpos - (_iota((1, n_ch), 1) * CMP_STRIDE + (CMP_LEN - 1))
        o_c, p = _rows_attend(q8, kc_ref[...], vc_ref[...], slope, dist, dist >= 0)
        oc_ref[...] = o_c
        for g in range(NSA_KV_HEADS):
            psum = p[g * NSA_GROUP * tq:(g * NSA_GROUP + 1) * tq]
            for hh in range(1, NSA_GROUP):
                psum = psum + p[(g * NSA_GROUP + hh) * tq:(g * NSA_GROUP + hh + 1) * tq]
            sel = _top_blocks(_mm(psum, ov_ref[...], HI), qpos[:tq], nsp)
            for hh in range(NSA_GROUP):
                sel_ref[(g * NSA_GROUP + hh) * tq:(g * NSA_GROUP + hh + 1) * tq, :] = sel
        m_ref[...] = jnp.full(m_ref.shape, NEG_INF, F32)
        l_ref[...] = jnp.zeros(l_ref.shape, F32)
        acc_ref[...] = jnp.zeros(acc_ref.shape, F32)

    sel_bf = sel_ref[...].astype(BF16)

    def sel_step(kv, kpos):
        n = kv.shape[1]
        expand = jnp.where(_iota((nsp, n), 0) == (kpos >> 6), 1.0, 0.0).astype(BF16)
        dist = qpos - kpos
        mask = (_mm(sel_bf, expand) > 0.5) & (dist >= 0)
        _rows_online(q8, kv[:NSA_KV].astype(BF16), kv[NSA_KV:].astype(BF16), slope, dist, mask,
                     m_ref, l_ref, acc_ref)

    sel_step(jnp.concatenate([page_refs[k][...] for k in range(pg)], axis=1),
             j * (pg * page) + _iota((1, pg * page), 1))

    @pl.when(j == pl.num_programs(1) - 1)
    def _():
        sel_step(knew_ref[...], past_len + _iota((1, knew_ref.shape[1]), 1))
        o_s = acc_ref[...] / jnp.maximum(l_ref[...], 1e-30)
        kw = kw_ref[...]
        kwpos = past_len - WINDOW + _iota((1, kw.shape[1]), 1)
        dist = qpos - kwpos
        mask = (dist >= 0) & (dist <= WINDOW) & (kwpos >= 0) & (kwpos < past_len + tq)
        o_w, _ = _rows_attend(q8, kw[:NSA_KV].astype(BF16), kw[NSA_KV:].astype(BF16), slope, dist, mask)
        y_ref[...] = _merge_heads(oc_ref[...], o_s, o_w, _sigmoid(sm_ref[...]), tq)


def _nsa_sample(q8, sm, kct, vct, knew_t, win_t, cache_t, page_table, b, l):
    n_pages = page_table.shape[1]
    page = cache_t.shape[2]
    past_len = n_pages * page
    assert l <= SEL_BLOCK and past_len % SEL_BLOCK == 0 and past_len >= WINDOW
    pg = 16
    steps = n_pages // pg
    n_ch = past_len // CMP_STRIDE
    ns = -(-(past_len + l) // SEL_BLOCK)
    nsp = -(-ns // LANES) * LANES
    w = 2 * NSA_KV
    ov = _overlap(n_ch, nsp)
    wcols = win_t.shape[2]

    def page_map(k):
        return lambda bi, j, pt: (pt[bi, j * pg + k], 1, 0)

    per_b = lambda bi, j, pt: (bi, 0)
    per_b3 = lambda bi, j, pt: (bi, 0, 0)
    grid_spec = pltpu.PrefetchScalarGridSpec(
        num_scalar_prefetch=1,
        grid=(b, steps),
        in_specs=[pl.BlockSpec((None, w, page), page_map(k)) for k in range(pg)]
        + [pl.BlockSpec((None, l, NSA_HEADS * LANES), per_b3),
           pl.BlockSpec((l, LANES), per_b),
           pl.BlockSpec((None, NSA_KV, n_ch), per_b3),
           pl.BlockSpec((None, NSA_KV, n_ch), per_b3),
           pl.BlockSpec((None, w, LANES), per_b3),
           pl.BlockSpec((None, w, wcols), per_b3),
           pl.BlockSpec((n_ch, nsp), lambda bi, j, pt: (0, 0))],
        out_specs=pl.BlockSpec((l, NSA_Q), per_b),
        scratch_shapes=[pltpu.VMEM((NSA_HEADS * l, 1), F32),
                        pltpu.VMEM((NSA_HEADS * l, 1), F32),
                        pltpu.VMEM((NSA_HEADS * l, NSA_KV), F32),
                        pltpu.VMEM((NSA_HEADS * l, nsp), F32),
                        pltpu.VMEM((NSA_HEADS * l, NSA_KV), F32)],
    )
    return pl.pallas_call(
        functools.partial(_nsa_sample_kernel, pg=pg, page=page, tq=l, past_len=past_len),
        grid_spec=grid_spec,
        out_shape=jax.ShapeDtypeStruct((b * l, NSA_Q), F32),
        compiler_params=_cparams(("parallel", "arbitrary")),
        name="nsa_sample",
    )(page_table, *([cache_t] * pg), q8.reshape(b, l, NSA_HEADS * LANES), sm, kct, vct, knew_t, win_t, ov)


def _tail_kernel(x_ref, ya_ref, yb_ref, ga_ref, gb_ref, wa_ref, wb_ref, wo_ref, n2_ref, wr_ref, br_ref,
                 h_ref, hn_ref, cmb_ref):
    ma = _mm(ya_ref[...].astype(BF16), wa_ref[...])
    mb = _mm(yb_ref[...].astype(BF16), wb_ref[...])
    m = _sigmoid(ga_ref[...]) * ma + _sigmoid(gb_ref[...]) * mb
    h = x_ref[...] + _mm(m.astype(BF16), wo_ref[...])
    h_ref[...] = h
    hn = h * lax.rsqrt(jnp.mean(h * h, axis=-1, keepdims=True) + RMS_EPS) * n2_ref[...]
    hn_ref[...] = hn.astype(BF16)

    logit = _mm(hn, wr_ref[...], HI) + br_ref[...]
    lane = _iota(logit.shape, 1)
    is_grp = (lane >= N_EXPERTS) & (lane < N_EXPERTS + N_GROUPS)
    gl = jnp.where(is_grp, logit, NEG_INF)
    gmax = jnp.max(gl, axis=-1, keepdims=True)
    gidx = jnp.min(jnp.where(gl == gmax, lane, LANES), axis=-1, keepdims=True) - N_EXPERTS
    p_grp = 1.0 / jnp.sum(jnp.exp(gl - gmax), axis=-1, keepdims=True)
    el = jnp.where((lane >> 3) == gidx, logit, NEG_INF)
    t1 = jnp.max(el, axis=-1, keepdims=True)
    i1 = jnp.min(jnp.where(el == t1, lane, LANES), axis=-1, keepdims=True)
    el2 = jnp.where(lane == i1, NEG_INF, el)
    t2 = jnp.max(el2, axis=-1, keepdims=True)
    i2 = jnp.min(jnp.where(el2 == t2, lane, LANES), axis=-1, keepdims=True)
    e2 = jnp.exp(t2 - t1)
    w1 = p_grp / (1.0 + e2)
    w2 = p_grp * e2 / (1.0 + e2)
    cmb_ref[...] = jnp.where(lane == i1, w1, jnp.where(lane == i2, w2, 0.0))


def _tail(x2d, ya, yb, ga, gb, wa, wb, wo, norm2_g, w_route, b_route):
    n = x2d.shape[0]
    tm = min(256, n)
    row = lambda i: (i, 0)
    fixed = lambda i: (0, 0)
    return pl.pallas_call(
        _tail_kernel,
        grid=(n // tm,),
        in_specs=[pl.BlockSpec((tm, D_MODEL), row),
                  pl.BlockSpec((tm, GDN_V), row),
                  pl.BlockSpec((tm, NSA_Q), row),
                  pl.BlockSpec((tm, D_MODEL), row),
                  pl.BlockSpec((tm, D_MODEL), row),
                  pl.BlockSpec((GDN_V, D_MODEL), fixed),
                  pl.BlockSpec((NSA_Q, D_MODEL), fixed),
                  pl.BlockSpec((D_MODEL, D_MODEL), fixed),
                  pl.BlockSpec((1, D_MODEL), fixed),
                  pl.BlockSpec((D_MODEL, LANES), fixed),
                  pl.BlockSpec((1, LANES), fixed)],
        out_specs=[pl.BlockSpec((tm, D_MODEL), row),
                   pl.BlockSpec((tm, D_MODEL), row),
                   pl.BlockSpec((tm, LANES), row)],
        out_shape=[jax.ShapeDtypeStruct((n, D_MODEL), F32),
                   jax.ShapeDtypeStruct((n, D_MODEL), BF16),
                   jax.ShapeDtypeStruct((n, LANES), F32)],
        compiler_params=_cparams(("parallel",)),
        name="tail",
    )(x2d, ya, yb, ga, gb, wa, wb, wo, norm2_g.reshape(1, D_MODEL), w_route, b_route)


def _moe_kernel(h_ref, hn_ref, cmb_ref, wg_ref, wu_ref, wd_ref, nf_ref, y_ref, acc_ref):
    e = pl.program_id(1)

    @pl.when(e == 0)
    def _():
        acc_ref[...] = jnp.zeros(acc_ref.shape, F32)

    hn = hn_ref[...]
    cmb = cmb_ref[...]
    wgt = jnp.sum(jnp.where(_iota(cmb.shape, 1) == e, cmb, 0.0), axis=-1, keepdims=True)
    act = _silu(_mm(hn, wg_ref[0])) * _mm(hn, wu_ref[0]) * wgt
    acc_ref[...] += _mm(act.astype(BF16), wd_ref[0])

    @pl.when(e == pl.num_programs(1) - 1)
    def _():
        t = h_ref[...] + acc_ref[...]
        y_ref[...] = t * lax.rsqrt(jnp.mean(t * t, axis=-1, keepdims=True) + RMS_EPS) * nf_ref[...]


def _moe(h, hn, cmb, wg, wu, wd, norm_f_g):
    n = h.shape[0]
    tm = min(1024, n)
    row = lambda i, e: (i, 0)
    return pl.pallas_call(
        _moe_kernel,
        grid=(n // tm, N_EXPERTS),
        in_specs=[pl.BlockSpec((tm, D_MODEL), row),
                  pl.BlockSpec((tm, D_MODEL), row),
                  pl.BlockSpec((tm, LANES), row),
                  pl.BlockSpec((1, D_MODEL, D_EXPERT), lambda i, e: (e, 0, 0)),
                  pl.BlockSpec((1, D_MODEL, D_EXPERT), lambda i, e: (e, 0, 0)),
                  pl.BlockSpec((1, D_EXPERT, D_MODEL), lambda i, e: (e, 0, 0)),
                  pl.BlockSpec((1, D_MODEL), lambda i, e: (0, 0))],
        out_specs=pl.BlockSpec((tm, D_MODEL), row),
        out_shape=jax.ShapeDtypeStruct((n, D_MODEL), F32),
        scratch_shapes=[pltpu.VMEM((tm, D_MODEL), F32)],
        compiler_params=_cparams(("parallel", "arbitrary")),
        name="moe",
    )(h, hn, cmb, wg, wu, wd, norm_f_g.reshape(1, D_MODEL))


def _route_weights(w_grp, b_grp, w_rt, b_rt):
    pad = LANES - N_EXPERTS - N_GROUPS
    w = jnp.concatenate([w_rt, w_grp, jnp.zeros((D_MODEL, pad), F32)], axis=1)
    bias = jnp.concatenate([b_rt, b_grp, jnp.zeros((pad,), F32)]).reshape(1, LANES)
    return w, bias


def _finish(x2d, ya, yb, ga, gb, wts):
    h, hn, cmb = _tail(x2d, ya, yb, ga, gb, wts["wa"], wts["wb"], wts["wo"], wts["norm2_g"],
                       wts["w_route"], wts["b_route"])
    return _moe(h, hn, cmb, wts["wg"], wts["wu"], wts["wd"], wts["norm_f_g"])


def _prompt_layer(x, wts):
    b, l, _ = x.shape
    x2d = x.reshape(b * l, D_MODEL)
    qkv, z, q8, ga, gb, sm, kv4t, kvwt, cmp_rows, ksa, kwa, vst, vwt = _project(
        x2d, wts["norm1_g"], wts["w_pk"], b, l)
    conv0 = jnp.zeros((b, GDN_CONV - 1, GDN_CONV_DIM), F32)
    s0 = jnp.zeros((b, GDN_HEADS, GDN_DK, GDN_DV), F32)
    ya, s_new, conv_new = _gdn(qkv, z, sm, conv0, s0, wts["conv_w"], wts["a_log"], wts["dt_bias"],
                               wts["gdn_norm_g"], b, l)
    head, tail = _pool_rows(cmp_rows, wts["pw"])
    kca, vct = _compress(head, tail, wts["cmp_wk"], wts["cmp_wv"], b)
    yb = _nsa_prompt(q8, sm, kca, vct, ksa, vst, kwa, vwt, b, l)
    y = _finish(x2d, ya, yb, ga, gb, wts)
    win_buf = min(WINDOW, l)
    kv_new = jnp.transpose(kv4t.reshape(b, 4, NSA_KV_HEADS, NSA_HD, l), (0, 4, 1, 2, 3))
    win_new = jnp.transpose(kvwt[:, :, l - win_buf:].reshape(b, 2, NSA_KV_HEADS, NSA_HD, win_buf), (0, 4, 1, 2, 3))
    return (y.reshape(b, l, D_MODEL), kv_new, win_new, s_new, conv_new)


def _sample_layer(x, cache_kv_l, page_table, cache_win_l, s0, conv_buf, wts):
    b, l, _ = x.shape
    x2d = x.reshape(b * l, D_MODEL)
    qkv, z, q8, ga, gb, sm, kv4, kvw = _project(x2d, wts["norm1_g"], wts["w_pk"], b, l)
    ya, s_new, conv_new = _gdn(qkv, z, sm, conv_buf, s0, wts["conv_w"], wts["a_log"], wts["dt_bias"],
                               wts["gdn_norm_g"], b, l)
    n_phys, page = cache_kv_l.shape[:2]
    w = 2 * NSA_KV
    cache_t = jnp.transpose(cache_kv_l, (0, 2, 3, 4, 1)).reshape(n_phys, 2 * w, page)
    win_buf = cache_win_l.shape[1]
    win_old_t = jnp.transpose(cache_win_l, (0, 2, 3, 4, 1)).reshape(b, w, win_buf)
    head_t, tail_t = _pool_pages(cache_t, page_table, wts["pos_wk"], wts["pos_wv"])
    kct, vct = _compress_t(head_t, tail_t, wts["cmp_wk"], wts["cmp_wv"])
    new_t = jnp.transpose(kv4.reshape(b, l, 2 * w), (0, 2, 1))
    knew_t = jnp.pad(new_t[:, w:], ((0, 0), (0, 0), (0, LANES - l)))
    win_new_t = jnp.transpose(kvw.reshape(b, l, w), (0, 2, 1))
    wcols = -(-(win_buf + l) // LANES) * LANES
    win_t = jnp.concatenate([win_old_t, win_new_t, jnp.zeros((b, w, wcols - win_buf - l), F32)], axis=2)
    yb = _nsa_sample(q8, sm, kct, vct, knew_t, win_t, cache_t, page_table, b, l)
    y = _finish(x2d, ya, yb, ga, gb, wts)
    win_new = jnp.transpose(win_t[:, :, l:l + win_buf].reshape(b, 2, NSA_KV_HEADS, NSA_HD, win_buf), (0, 4, 1, 2, 3))
    return (y.reshape(b, l, D_MODEL), kv4.reshape(b, l, 4, NSA_KV_HEADS, NSA_HD), win_new, s_new, conv_new)


def kernel(x_prompt, x_sample, cache_kv, page_table, cache_win, state_gdn, state_conv, norm1_g, w_in, gdn_conv_w, gdn_a_log, gdn_dt_bias, gdn_norm_g, cmp_pos_wk, cmp_pos_wv, cmp_wk, cmp_wv, w_branch_a, w_branch_b, w_out, norm2_g, w_grp, b_grp, w_rt, b_rt, w_e_gate, w_e_up, w_e_down, norm_f_g):
    assert w_in.shape[0] == 1, "single layer"
    w_route, b_route = _route_weights(w_grp[0], b_grp[0], w_rt[0], b_rt[0])
    wts = dict(
        norm1_g=norm1_g[0], w_pk=_pack_w_in(w_in[0]),
        conv_w=gdn_conv_w[0], a_log=gdn_a_log[0], dt_bias=gdn_dt_bias[0], gdn_norm_g=gdn_norm_g[0],
        pw=_pos_weights(cmp_pos_wk[0], cmp_pos_wv[0]), pos_wk=cmp_pos_wk[0], pos_wv=cmp_pos_wv[0],
        cmp_wk=cmp_wk[0], cmp_wv=cmp_wv[0],
        wa=w_branch_a[0].astype(BF16), wb=w_branch_b[0].astype(BF16), wo=w_out[0].astype(BF16),
        norm2_g=norm2_g[0], w_route=w_route, b_route=b_route,
        wg=w_e_gate[0].astype(BF16), wu=w_e_up[0].astype(BF16), wd=w_e_down[0].astype(BF16),
        norm_f_g=norm_f_g,
    )
    yp, kvp, winp, sp, cp = _prompt_layer(x_prompt, wts)
    ys, kvs, wins, ss, cs = _sample_layer(x_sample, cache_kv[0], page_table, cache_win[0], state_gdn[0],
                                          state_conv[0], wts)
    return (yp, ys, kvp[None], kvs[None], winp[None], wins[None], sp[None], ss[None], cp[None], cs[None])
```

```python
import functools
import math

import numpy as np
import jax
import jax.numpy as jnp
from jax import lax
from jax.experimental import pallas as pl
from jax.experimental.pallas import tpu as pltpu

F32 = jnp.float32
BF16 = jnp.bfloat16
HI = lax.Precision.HIGHEST

LANES = 128
SUBLANES = 8
VMEM_LIMIT = 56 * 1024 * 1024

D_MODEL = 1024
GDN_HEADS = 4
GDN_DK = 128
GDN_DV = 128
GDN_QK = GDN_HEADS * GDN_DK
GDN_V = GDN_HEADS * GDN_DV
GDN_CONV_DIM = 2 * GDN_QK + GDN_V
GDN_CONV = 4
GDN_CHUNK = 64
GDN_SEQS_PER_STEP = 4
NSA_HEADS = 8
NSA_KV_HEADS = 2
NSA_GROUP = NSA_HEADS // NSA_KV_HEADS
NSA_HD = 64
NSA_Q = NSA_HEADS * NSA_HD
NSA_KV = NSA_KV_HEADS * NSA_HD
CMP_STRIDE = 16
CMP_LEN = 2 * CMP_STRIDE
SEL_BLOCK = 64
SEL_TOP = 16
WINDOW = 512
FORCE_SCORE = 1.0e4
N_GROUPS = 4
EXPERTS_PER_GROUP = 8
N_EXPERTS = N_GROUPS * EXPERTS_PER_GROUP
D_EXPERT = 256
RMS_EPS = 1e-6
NEG_INF = float("-inf")

_OFF_QKV = 0
_OFF_Z = _OFF_QKV + GDN_CONV_DIM
_OFF_B = _OFF_Z + GDN_V
_OFF_A = _OFF_B + GDN_HEADS
_OFF_Q = _OFF_A + GDN_HEADS
_OFF_KV = _OFF_Q + NSA_Q
_OFF_G = _OFF_KV + 6 * NSA_KV
_OFF_GA = _OFF_G + 3 * NSA_HEADS
_OFF_GB = _OFF_GA + D_MODEL

SM_B = 0
SM_A = 4
SM_G = 8

_PK = {}
_c = 0
for _n, _w in (("qkv", GDN_CONV_DIM), ("z", GDN_V), ("q8", NSA_HEADS * LANES), ("kv4", 4 * NSA_KV),
               ("kvw", 2 * NSA_KV), ("ga", D_MODEL), ("gb", D_MODEL), ("sm", LANES)):
    _PK[_n] = (_c, _w)
    _c += _w
PK_DIM = _c


def _cparams(sem):
    return pltpu.CompilerParams(dimension_semantics=sem, vmem_limit_bytes=VMEM_LIMIT)


def _nt(a, b, precision=None):
    return lax.dot_general(a, b, (((1,), (1,)), ((), ())), preferred_element_type=F32, precision=precision)


def _tn(a, b, precision=None):
    return lax.dot_general(a, b, (((0,), (0,)), ((), ())), preferred_element_type=F32, precision=precision)


def _mm(a, b, precision=None):
    return jnp.dot(a, b, preferred_element_type=F32, precision=precision)


def _sigmoid(x):
    return 1.0 / (1.0 + jnp.exp(-x))


def _silu(x):
    return x * _sigmoid(x)


def _iota(shape, dim):
    return lax.broadcasted_iota(jnp.int32, shape, dim)


def _ones_where(cond):
    return jnp.where(cond, 1.0, 0.0).astype(BF16)


def _pack_w_in(w_in):
    q = w_in[:, _OFF_Q:_OFF_Q + NSA_Q]
    zeros64 = jnp.zeros((D_MODEL, NSA_HD), w_in.dtype)
    q8 = []
    for h in range(NSA_HEADS):
        qh = q[:, h * NSA_HD:(h + 1) * NSA_HD]
        q8.append(jnp.concatenate([qh, zeros64] if h < NSA_GROUP else [zeros64, qh], axis=1))
    sm = jnp.concatenate([w_in[:, _OFF_B:_OFF_B + GDN_HEADS], w_in[:, _OFF_A:_OFF_A + GDN_HEADS],
                          w_in[:, _OFF_G:_OFF_G + 3 * NSA_HEADS],
                          jnp.zeros((D_MODEL, LANES - 2 * GDN_HEADS - 3 * NSA_HEADS), w_in.dtype)], axis=1)
    cols = [w_in[:, _OFF_QKV:_OFF_QKV + GDN_CONV_DIM], w_in[:, _OFF_Z:_OFF_Z + GDN_V]] + q8 + [
        w_in[:, _OFF_KV:_OFF_KV + 4 * NSA_KV], w_in[:, _OFF_KV + 4 * NSA_KV:_OFF_KV + 6 * NSA_KV],
        w_in[:, _OFF_GA:_OFF_GA + D_MODEL], w_in[:, _OFF_GB:_OFF_GB + D_MODEL], sm]
    return jnp.concatenate(cols, axis=1).astype(BF16)


def _pos_features(pos, shape):
    lane = _iota(shape, 1)
    feat = jnp.where(lane == 0, (pos >> 7) * LANES, jnp.where(lane == 1, pos & (LANES - 1),
                                                               jnp.where(lane < 4, 1, 0)))
    return feat.astype(F32).astype(BF16)


def _proj_kernel(x_ref, g_ref, w_ref, qkv_ref, z_ref, q8_ref, ga_ref, gb_ref, sm_ref, *kv_refs, tiles_per_seq):
    tm = x_ref.shape[0]
    x = x_ref[...]
    xn = (x * lax.rsqrt(jnp.mean(x * x, axis=-1, keepdims=True) + RMS_EPS) * g_ref[...]).astype(BF16)

    def seg(name):
        a, w = _PK[name]
        return _mm(xn, w_ref[:, a:a + w])

    qkv_ref[...] = seg("qkv")
    z_ref[...] = seg("z")
    q8_ref[...] = (seg("q8") * (NSA_HD ** -0.5)).astype(BF16)
    ga_ref[...] = seg("ga")
    gb_ref[...] = seg("gb")
    sm_ref[...] = seg("sm")
    kv4 = seg("kv4")
    kvw = seg("kvw")
    if tiles_per_seq == 0:
        kv4_ref, kvw_ref = kv_refs
        kv4_ref[...] = kv4
        kvw_ref[...] = kvw
        return
    kv4t_ref, kvwt_ref, cmp_ref, ksa_ref, kwa_ref, vst_ref, vwt_ref = kv_refs
    kv4t = jnp.transpose(kv4)
    kvwt = jnp.transpose(kvw)
    kv4t_ref[...] = kv4t
    kvwt_ref[...] = kvwt
    cmp_ref[...] = kv4[:, :2 * NSA_KV]
    pos = (pl.program_id(0) % tiles_per_seq) * tm + _iota((tm, 1), 0)
    feat = _pos_features(pos, (tm, LANES))
    ksa_ref[:, :NSA_KV] = kv4[:, 2 * NSA_KV:3 * NSA_KV].astype(BF16)
    ksa_ref[:, NSA_KV:] = feat
    kwa_ref[:, :NSA_KV] = kvw[:, :NSA_KV].astype(BF16)
    kwa_ref[:, NSA_KV:] = feat
    vst_ref[...] = kv4t[3 * NSA_KV:].astype(BF16)
    vwt_ref[...] = kvwt[NSA_KV:].astype(BF16)


def _project(x2d, norm_g, w_pk, b, l):
    n = x2d.shape[0]
    tm = min(256, n)
    long_seq = l % tm == 0
    tps = l // tm if long_seq else 0
    w = 2 * NSA_KV
    row = lambda i: (i, 0)
    out_shape = [jax.ShapeDtypeStruct((n, _PK[k][1]), dt) for k, dt in
                 (("qkv", F32), ("z", F32), ("q8", BF16), ("ga", F32), ("gb", F32), ("sm", F32))]
    out_specs = [pl.BlockSpec((tm, s.shape[1]), row) for s in out_shape]
    if long_seq:
        seq_t = lambda i: (i // tps, 0, i % tps)
        out_shape += [jax.ShapeDtypeStruct((b, 2 * w, l), F32), jax.ShapeDtypeStruct((b, w, l), F32),
                      jax.ShapeDtypeStruct((n, w), F32), jax.ShapeDtypeStruct((n, w), BF16),
                      jax.ShapeDtypeStruct((n, w), BF16), jax.ShapeDtypeStruct((NSA_KV, n), BF16),
                      jax.ShapeDtypeStruct((NSA_KV, n), BF16)]
        out_specs += [pl.BlockSpec((None, 2 * w, tm), seq_t), pl.BlockSpec((None, w, tm), seq_t),
                      pl.BlockSpec((tm, w), row), pl.BlockSpec((tm, w), row), pl.BlockSpec((tm, w), row),
                      pl.BlockSpec((NSA_KV, tm), lambda i: (0, i)), pl.BlockSpec((NSA_KV, tm), lambda i: (0, i))]
    else:
        out_shape += [jax.ShapeDtypeStruct((n, 2 * w), F32), jax.ShapeDtypeStruct((n, w), F32)]
        out_specs += [pl.BlockSpec((tm, 2 * w), row), pl.BlockSpec((tm, w), row)]
    return pl.pallas_call(
        functools.partial(_proj_kernel, tiles_per_seq=tps),
        grid=(n // tm,),
        in_specs=[pl.BlockSpec((tm, D_MODEL), row),
                  pl.BlockSpec((1, D_MODEL), lambda i: (0, 0)),
                  pl.BlockSpec((D_MODEL, PK_DIM), lambda i: (0, 0), pipeline_mode=pl.Buffered(1))],
        out_specs=out_specs,
        out_shape=out_shape,
        compiler_params=_cparams(("parallel",)),
        name="proj",
    )(x2d, norm_g.reshape(1, D_MODEL), w_pk)


def _unit_lower_inverses(a_list, c):
    r = _iota((c, c), 0)
    col = _iota((c, c), 1)
    eye = (r == col).astype(F32)
    n1 = [jnp.where((r >> 3) == (col >> 3), -a, 0.0) for a in a_list]
    n2 = [_mm(x, x, HI) for x in n1]
    n4 = [_mm(x, x, HI) for x in n2]
    t = [_mm(eye + x, eye + y, HI) for x, y in zip(n1, n2)]
    t = [_mm(x, eye + y, HI) for x, y in zip(t, n4)]
    s = SUBLANES
    while s < c:
        sh = s.bit_length() - 1
        off = ((r >> (sh + 1)) == (col >> (sh + 1))) & ((r >> sh) != (col >> sh))
        ta = [_mm(x, jnp.where(off, a, 0.0), HI) for x, a in zip(t, a_list)]
        t = [x - _mm(y, x, HI) for x, y in zip(t, ta)]
        s *= 2
    return t


def _gdn_kernel(qkv_ref, z_ref, sm_ref, cbuf_ref, s0_ref, cw_ref, alog_ref, dtb_ref, ng_ref,
                y_ref, snew_ref, cnew_ref, ext_ref, st_ref, *, chunk, nb):
    c = chunk
    ci = pl.program_id(1)

    @pl.when(ci == 0)
    def _():
        ext_ref[:, 0:SUBLANES, :] = cbuf_ref[...]
        st_ref[...] = s0_ref[...]

    r = _iota((c, c), 0)
    col = _iota((c, c), 1)
    tri_incl = r >= col
    tri_strict = r > col
    tri_f = tri_incl.astype(F32)
    pick = (_iota((SUBLANES, LANES), 1) == _iota((SUBLANES, LANES), 0) + SM_A).astype(F32)
    base = SUBLANES - (GDN_CONV - 1)
    ng = ng_ref[...]

    units = []
    for bb in range(nb):
        u = qkv_ref[bb]
        ext_ref[bb, SUBLANES:SUBLANES + c, :] = u
        conv = cw_ref[0:1, :] * ext_ref[bb, base:base + c, :]
        for i in range(1, GDN_CONV - 1):
            conv = conv + cw_ref[i:i + 1, :] * ext_ref[bb, base + i:base + i + c, :]
        conv = conv + cw_ref[GDN_CONV - 1:GDN_CONV, :] * u
        halo = ext_ref[bb, c:c + SUBLANES, :]
        ext_ref[bb, 0:SUBLANES, :] = halo
        cnew_ref[bb] = halo
        qkv = _silu(conv)

        sm = sm_ref[bb]
        beta_all = _sigmoid(sm)
        xa = sm + dtb_ref[...]
        softplus = jnp.maximum(xa, 0.0) + jnp.log(1.0 + jnp.exp(-jnp.abs(xa)))
        g_all = -jnp.exp(alog_ref[...]) * softplus
        gcum_all = _mm(tri_f, g_all, HI)
        gcum_rows = _nt(pick, gcum_all, HI)
        for h in range(GDN_HEADS):
            q = qkv[:, h * GDN_DK:(h + 1) * GDN_DK]
            k = qkv[:, GDN_QK + h * GDN_DK:GDN_QK + (h + 1) * GDN_DK]
            v = qkv[:, 2 * GDN_QK + h * GDN_DV:2 * GDN_QK + (h + 1) * GDN_DV]
            q = q * lax.rsqrt(jnp.sum(q * q, axis=-1, keepdims=True) + RMS_EPS) * (GDN_DK ** -0.5)
            k = k * lax.rsqrt(jnp.sum(k * k, axis=-1, keepdims=True) + RMS_EPS)
            beta = beta_all[:, SM_B + h:SM_B + h + 1]
            gc_col = gcum_all[:, SM_A + h:SM_A + h + 1]
            gc_row = gcum_rows[h:h + 1, :]
            gc_last = gcum_all[c - 1:c, SM_A + h:SM_A + h + 1]
            units.append(dict(bb=bb, h=h, q=q, k=k, v=v, beta=beta, gc_col=gc_col, gc_last=gc_last,
                              decay=jnp.exp(jnp.where(tri_incl, gc_col - gc_row, NEG_INF)),
                              eg=jnp.exp(gc_col), kb=k * beta, k_bf=k.astype(BF16)))

    a_list = [jnp.where(tri_strict, _nt(un["kb"].astype(BF16), un["k_bf"]) * un["decay"], 0.0) for un in units]
    t_list = _unit_lower_inverses(a_list, c)
    u_coef = [_mm(t, un["v"] * un["beta"], HI) for t, un in zip(t_list, units)]
    w_coef = [_mm(t, un["kb"] * un["eg"], HI) for t, un in zip(t_list, units)]
    qk = [(_nt(un["q"].astype(BF16), un["k_bf"]) * un["decay"]).astype(BF16) for un in units]
    s_old = [st_ref[un["bb"], un["h"]] for un in units]
    s_bf = [s.astype(BF16) for s in s_old]
    uu = [uc - _mm(wc.astype(BF16), s) for uc, wc, s in zip(u_coef, w_coef, s_bf)]
    uu_bf = [x.astype(BF16) for x in uu]
    o_list = [_mm((un["q"] * un["eg"]).astype(BF16), s) + _mm(a, x)
              for un, s, a, x in zip(units, s_bf, qk, uu_bf)]
    for un, s, x, o in zip(units, s_old, uu_bf, o_list):
        bb, h = un["bb"], un["h"]
        k_tail = un["k"] * jnp.exp(un["gc_last"] - un["gc_col"])
        st_ref[bb, h] = s * jnp.exp(un["gc_last"]) + _tn(k_tail.astype(BF16), x)
        on = o * lax.rsqrt(jnp.mean(o * o, axis=-1, keepdims=True) + RMS_EPS) * ng
        y_ref[bb, :, h * GDN_DV:(h + 1) * GDN_DV] = on * _silu(z_ref[bb, :, h * GDN_DV:(h + 1) * GDN_DV])

    @pl.when(ci == pl.num_programs(1) - 1)
    def _():
        snew_ref[...] = st_ref[...]


def _gdn(qkv, z, sm, conv_buf, s0, conv_w, a_log, dt_bias, norm_g, b, l):
    c = math.gcd(l, GDN_CHUNK)
    nc = l // c
    nb = math.gcd(b, GDN_SEQS_PER_STEP)
    cbuf8 = jnp.pad(conv_buf, ((0, 0), (SUBLANES - (GDN_CONV - 1), 0), (0, 0)))
    pad_a = (SM_A, LANES - SM_A - GDN_HEADS)
    alog_row = jnp.pad(a_log, pad_a).reshape(1, LANES)
    dtb_row = jnp.pad(dt_bias, pad_a).reshape(1, LANES)
    row = lambda bi, ci: (bi, ci, 0)
    per_seq3 = lambda bi, ci: (bi, 0, 0)
    per_seq4 = lambda bi, ci: (bi, 0, 0, 0)
    fixed = lambda bi, ci: (0, 0)
    y, s_new, c_new = pl.pallas_call(
        functools.partial(_gdn_kernel, chunk=c, nb=nb),
        grid=(b // nb, nc),
        in_specs=[pl.BlockSpec((nb, c, GDN_CONV_DIM), row),
                  pl.BlockSpec((nb, c, GDN_V), row),
                  pl.BlockSpec((nb, c, LANES), row),
                  pl.BlockSpec((nb, SUBLANES, GDN_CONV_DIM), per_seq3),
                  pl.BlockSpec((nb, GDN_HEADS, GDN_DK, GDN_DV), per_seq4),
                  pl.BlockSpec((GDN_CONV, GDN_CONV_DIM), fixed),
                  pl.BlockSpec((1, LANES), fixed),
                  pl.BlockSpec((1, LANES), fixed),
                  pl.BlockSpec((1, GDN_DV), fixed)],
        out_specs=[pl.BlockSpec((nb, c, GDN_V), row),
                   pl.BlockSpec((nb, GDN_HEADS, GDN_DK, GDN_DV), per_seq4),
                   pl.BlockSpec((nb, SUBLANES, GDN_CONV_DIM), per_seq3)],
        out_shape=[jax.ShapeDtypeStruct((b, l, GDN_V), F32),
                   jax.ShapeDtypeStruct((b, GDN_HEADS, GDN_DK, GDN_DV), F32),
                   jax.ShapeDtypeStruct((b, SUBLANES, GDN_CONV_DIM), F32)],
        scratch_shapes=[pltpu.VMEM((nb, c + SUBLANES, GDN_CONV_DIM), F32),
                        pltpu.VMEM((nb, GDN_HEADS, GDN_DK, GDN_DV), F32)],
        compiler_params=_cparams(("parallel", "arbitrary")),
        name="gdn",
    )(qkv.reshape(b, l, GDN_CONV_DIM), z.reshape(b, l, GDN_V), sm.reshape(b, l, LANES), cbuf8, s0, conv_w,
      alog_row, dtb_row, norm_g.reshape(1, GDN_DV))
    return y.reshape(b * l, GDN_V), s_new, c_new[:, SUBLANES - (GDN_CONV - 1):]


def _pool_kernel(*refs, n_in):
    refs = refs[len(refs) - 2 * n_in - 3:]
    x_refs, pw_ref, head_ref, tail_ref = refs[:2 * n_in], refs[2 * n_in], refs[2 * n_in + 1], refs[2 * n_in + 2]
    rows = x_refs[0].shape[0]
    n = rows // CMP_STRIDE
    for j, x_ref in enumerate(x_refs):
        half = slice((j % 2) * NSA_KV, (j % 2 + 1) * NSA_KV)
        head = None
        tail = None
        for p in range(CMP_STRIDE):
            xr = x_ref[pl.ds(p, n, stride=CMP_STRIDE), :]
            hp = xr * pw_ref[p:p + 1, half]
            tp = xr * pw_ref[CMP_STRIDE + p:CMP_STRIDE + p + 1, half]
            head = hp if head is None else head + hp
            tail = tp if tail is None else tail + tp
        head_ref[(j // 2) * n:(j // 2 + 1) * n, half] = head
        tail_ref[(j // 2) * n:(j // 2 + 1) * n, half] = tail


def _pos_weights(pos_wk, pos_wv):
    return jnp.concatenate([jnp.broadcast_to(pos_wk[:, None], (CMP_LEN, NSA_KV)),
                            jnp.broadcast_to(pos_wv[:, None], (CMP_LEN, NSA_KV))], axis=1).astype(F32)


def _pool_rows(kv4, pw):
    n = kv4.shape[0]
    r = min(2048, n)
    w = 2 * NSA_KV
    return pl.pallas_call(
        functools.partial(_pool_kernel, n_in=1),
        grid=(n // r,),
        in_specs=[pl.BlockSpec((r, NSA_KV), lambda i: (i, 0)),
                  pl.BlockSpec((r, NSA_KV), lambda i: (i, 1)),
                  pl.BlockSpec((CMP_LEN, w), lambda i: (0, 0))],
        out_specs=[pl.BlockSpec((r // CMP_STRIDE, w), lambda i: (i, 0))] * 2,
        out_shape=[jax.ShapeDtypeStruct((n // CMP_STRIDE, w), F32)] * 2,
        compiler_params=_cparams(("parallel",)),
        name="pool_rows",
    )(kv4, kv4, pw)


def _pool_pages_kernel(*refs, pg):
    page_refs = refs[1:1 + pg]
    pk_hi, pk_lo, pv_hi, pv_lo, head_ref, tail_ref = refs[1 + pg:]
    x = jnp.concatenate([r[...] for r in page_refs], axis=1)
    n = head_ref.shape[1]
    for half, (p_hi, p_lo) in enumerate(((pk_hi, pk_lo), (pv_hi, pv_lo))):
        xs = x[half * NSA_KV:(half + 1) * NSA_KV]
        hi = xs.astype(BF16)
        lo = (xs - hi.astype(F32)).astype(BF16)
        out = _mm(hi, p_hi[...]) + _mm(lo, p_hi[...]) + _mm(hi, p_lo[...])
        head_ref[half * NSA_KV:(half + 1) * NSA_KV, :] = out[:, :n]
        tail_ref[half * NSA_KV:(half + 1) * NSA_KV, :] = out[:, n:]


def _pool_matrix(pos_w, positions):
    chunks = positions // CMP_STRIDE
    p = jnp.arange(positions)[:, None]
    c = jnp.arange(chunks)[None, :]
    inside = (p // CMP_STRIDE) == c
    head = jnp.where(inside, pos_w[:CMP_STRIDE][p % CMP_STRIDE], 0.0)
    tail = jnp.where(inside, pos_w[CMP_STRIDE:][p % CMP_STRIDE], 0.0)
    m = jnp.concatenate([head, tail], axis=1).astype(F32)
    hi = m.astype(BF16)
    return hi, (m - hi.astype(F32)).astype(BF16)


def _pool_pages(cache_t, page_table, pos_wk, pos_wv):
    b, n_pages = page_table.shape
    page = cache_t.shape[2]
    pg = LANES * CMP_STRIDE // page
    w = 2 * NSA_KV
    steps = n_pages // pg
    n_ch = n_pages * page // CMP_STRIDE
    mats = _pool_matrix(pos_wk, pg * page) + _pool_matrix(pos_wv, pg * page)

    def page_map(k):
        return lambda bi, j, pt: (pt[bi, j * pg + k], 0, 0)

    fixed = lambda bi, j, pt: (0, 0)
    grid_spec = pltpu.PrefetchScalarGridSpec(
        num_scalar_prefetch=1,
        grid=(b, steps),
        in_specs=[pl.BlockSpec((None, w, page), page_map(k)) for k in range(pg)]
        + [pl.BlockSpec((pg * page, 2 * LANES), fixed)] * 4,
        out_specs=[pl.BlockSpec((None, w, LANES), lambda bi, j, pt: (bi, 0, j))] * 2,
    )
    return pl.pallas_call(
        functools.partial(_pool_pages_kernel, pg=pg),
        grid_spec=grid_spec,
        out_shape=[jax.ShapeDtypeStruct((b, w, n_ch), F32)] * 2,
        compiler_params=_cparams(("parallel", "arbitrary")),
        name="pool_pages",
    )(page_table, *([cache_t] * pg), *mats)


def _cmp_t_kernel(head_ref, tail_ref, wk_ref, wv_ref, kct_ref, vct_ref):
    n = head_ref.shape[1]
    blocks = head_ref[...] + pltpu.roll(tail_ref[...], n - 1, 1)
    kct_ref[...] = _mm(wk_ref[...], blocks[:NSA_KV].astype(BF16)).astype(BF16)
    vct_ref[...] = _mm(wv_ref[...], blocks[NSA_KV:].astype(BF16)).astype(BF16)


def _compress_t(head_t, tail_t, cmp_wk, cmp_wv):
    b, w, n_ch = head_t.shape
    seq = lambda i: (i, 0, 0)
    return pl.pallas_call(
        _cmp_t_kernel,
        grid=(b,),
        in_specs=[pl.BlockSpec((None, w, n_ch), seq)] * 2 + [pl.BlockSpec((NSA_KV, NSA_KV), lambda i: (0, 0))] * 2,
        out_specs=[pl.BlockSpec((None, NSA_KV, n_ch), seq)] * 2,
        out_shape=[jax.ShapeDtypeStruct((b, NSA_KV, n_ch), BF16)] * 2,
        compiler_params=_cparams(("parallel",)),
        name="compress_t",
    )(head_t, tail_t, jnp.transpose(_block_diag2(cmp_wk)), jnp.transpose(_block_diag2(cmp_wv)))


def _cmp_kernel(head_ref, tail_ref, wk_ref, wv_ref, kca_ref, vct_ref):
    n = head_ref.shape[0]
    blocks = head_ref[...] + pltpu.roll(tail_ref[...], n - 1, 0)
    kca_ref[:, :NSA_KV] = _mm(blocks[:, :NSA_KV].astype(BF16), wk_ref[...]).astype(BF16)
    blk_end = _iota((n, 1), 0) * CMP_STRIDE + (CMP_LEN - 1)
    kca_ref[:, NSA_KV:] = _pos_features(blk_end, (n, LANES))
    vct_ref[...] = _nt(wv_ref[...], blocks[:, NSA_KV:].astype(BF16)).astype(BF16)


def _block_diag2(w):
    z = jnp.zeros_like(w)
    return jnp.concatenate([jnp.concatenate([w, z], axis=1), jnp.concatenate([z, w], axis=1)], axis=0).astype(BF16)


def _compress(head, tail, cmp_wk, cmp_wv, b):
    n_ch = head.shape[0] // b
    w = 2 * NSA_KV
    return pl.pallas_call(
        _cmp_kernel,
        grid=(b,),
        in_specs=[pl.BlockSpec((n_ch, w), lambda i: (i, 0))] * 2
        + [pl.BlockSpec((NSA_KV, NSA_KV), lambda i: (0, 0))] * 2,
        out_specs=[pl.BlockSpec((n_ch, w), lambda i: (i, 0)),
                   pl.BlockSpec((NSA_KV, n_ch), lambda i: (0, i))],
        out_shape=[jax.ShapeDtypeStruct((b * n_ch, w), BF16),
                   jax.ShapeDtypeStruct((NSA_KV, b * n_ch), BF16)],
        compiler_params=_cparams(("parallel",)),
        name="compress",
    )(head, tail, _block_diag2(cmp_wk), jnp.transpose(_block_diag2(cmp_wv)))


def _overlap(n_ch, nsp):
    i = jnp.arange(n_ch)[:, None]
    j = jnp.arange(nsp)[None, :]
    lo = jnp.maximum(i * CMP_STRIDE, j * SEL_BLOCK)
    hi = jnp.minimum(i * CMP_STRIDE + CMP_LEN, (j + 1) * SEL_BLOCK)
    ov = jnp.maximum(hi - lo, 0).astype(F32) / CMP_LEN
    return jnp.where(i < n_ch - 1, ov, 0.0)


def _slope(h):
    return 2.0 ** (-(h + 1))


def _softmax_rows(s, mask):
    s = jnp.where(mask, s, NEG_INF)
    m = jnp.max(s, axis=-1, keepdims=True)
    m = jnp.where(m == NEG_INF, 0.0, m)
    e = jnp.where(mask, jnp.exp(s - m), 0.0)
    return e, jnp.maximum(jnp.sum(e, axis=-1, keepdims=True), 1e-30)


def _top_blocks(imp, qpos, ns_lanes):
    blk = _iota(imp.shape, 1)
    cur = qpos >> 6
    forced = (blk == 0) | (blk == cur) | (blk == cur - 1)
    score = jnp.where(forced, FORCE_SCORE, imp)
    work = jnp.where(blk <= cur, score, NEG_INF)
    sel = jnp.zeros(imp.shape, F32)
    for _ in range(SEL_TOP):
        m = jnp.max(work, axis=-1, keepdims=True)
        cand = jnp.where((work == m) & (m > NEG_INF), blk, ns_lanes)
        first = jnp.min(cand, axis=-1, keepdims=True)
        hit = blk == first
        sel = jnp.where(hit, 1.0, sel)
        work = jnp.where(hit, NEG_INF, work)
    return sel


def _row_slopes(tq):
    head = _iota((NSA_HEADS * tq, 1), 0) >> (tq.bit_length() - 1)
    slope = jnp.zeros((NSA_HEADS * tq, 1), F32)
    for h in range(NSA_HEADS):
        slope = jnp.where(head == h, _slope(h), slope)
    return slope


def _rows_attend(q8, kt, vt, slope, dist, mask):
    s = _mm(q8, kt) - slope * dist.astype(F32)
    e, den = _softmax_rows(s, mask)
    p = e / den
    return _nt(p.astype(BF16), vt), p


def _rows_online(q8, kt, vt, slope, dist, mask, m_ref, l_ref, acc_ref):
    s = jnp.where(mask, _mm(q8, kt) - slope * dist.astype(F32), NEG_INF)
    m_old = m_ref[...]
    m_new = jnp.maximum(m_old, jnp.max(s, axis=-1, keepdims=True))
    m_use = jnp.where(m_new == NEG_INF, 0.0, m_new)
    alpha = jnp.exp(m_old - m_use)
    e = jnp.exp(s - m_use)
    l_ref[...] = alpha * l_ref[...] + jnp.sum(e, axis=-1, keepdims=True)
    acc_ref[...] = alpha * acc_ref[...] + _nt(e.astype(BF16), vt)
    m_ref[...] = m_new


def _merge_heads(o_c, o_s, o_w, gates, tq):
    lane = _iota((tq, LANES), 1)
    mixed = []
    for h in range(NSA_HEADS):
        rows = slice(h * tq, (h + 1) * tq)
        g0 = gates[:, SM_G + 3 * h:SM_G + 3 * h + 1]
        g1 = gates[:, SM_G + 3 * h + 1:SM_G + 3 * h + 2]
        g2 = gates[:, SM_G + 3 * h + 2:SM_G + 3 * h + 3]
        mixed.append(g0 * o_c[rows] + g1 * o_s[rows] + g2 * o_w[rows])
    cols = []
    for p in range(NSA_HEADS // 2):
        a, b = mixed[2 * p], mixed[2 * p + 1]
        if 2 * p < NSA_GROUP:
            cols.append(jnp.where(lane < NSA_HD, a, pltpu.roll(b, NSA_HD, 1)))
        else:
            cols.append(jnp.where(lane < NSA_HD, pltpu.roll(a, NSA_HD, 1), b))
    return jnp.concatenate(cols, axis=1)


def _softmax_cols(s, mask):
    s = jnp.where(mask, s, NEG_INF)
    m = jnp.max(s, axis=0, keepdims=True)
    m = jnp.where(m == NEG_INF, 0.0, m)
    e = jnp.exp(s - m)
    return e * (1.0 / jnp.maximum(jnp.sum(e, axis=0, keepdims=True), 1e-30))


def _split3(x):
    hi = x.astype(BF16)
    r1 = x - hi.astype(F32)
    mid = r1.astype(BF16)
    lo = (r1 - mid.astype(F32)).astype(BF16)
    return hi, mid, lo


def _top_blocks_cols(imp, qpos, nsp):
    blk = _iota(imp.shape, 0)
    cur = qpos >> 6
    forced = (blk == 0) | (blk == cur) | (blk == cur - 1)
    work = jnp.where(blk <= cur, jnp.where(forced, FORCE_SCORE, imp), NEG_INF)
    neg = jnp.full(imp.shape, NEG_INF, F32)
    for _ in range(SEL_TOP):
        m = jnp.max(work, axis=0, keepdims=True)
        cand = jnp.where((work == m) & (m > NEG_INF), blk, nsp)
        hit = blk == jnp.min(cand, axis=0, keepdims=True)
        neg = jnp.where(hit, 0.0, neg)
        work = jnp.where(hit, NEG_INF, work)
    return neg


def _nsa_prompt_kernel(q8_ref, sm_ref, kc_ref, vct_ref, ks_ref, vst_ref, kw_ref, vwt_ref, ovt_ref, y_ref,
                       neg_ref, m_ref, l_ref, acc_ref, qta_ref, sa_ref, sb_ref, *, tq, tk):
    i = pl.program_id(1)
    start = i * tq
    r = NSA_HEADS * tq
    nsp, n_ch = ovt_ref.shape
    col = _iota((1, r), 1)
    qpos = start + (col & (tq - 1))

    q8 = jnp.concatenate([q8_ref[:, h * LANES:(h + 1) * LANES] for h in range(NSA_HEADS)], axis=0)
    eye = (_iota((LANES, LANES), 0) == _iota((LANES, LANES), 1)).astype(BF16)
    qt = _nt(eye, q8).astype(BF16)
    head = col >> (tq.bit_length() - 1)
    slope = jnp.zeros((1, r), F32)
    for h in range(NSA_HEADS):
        slope = jnp.where(head == h, _slope(h), slope)
    frow = _iota((2 * SUBLANES, r), 0)
    feat = jnp.where(frow < 2, slope, jnp.where(frow == 2, -slope * ((qpos >> 7) * LANES).astype(F32),
                                                jnp.where(frow == 3, -slope * (qpos & (LANES - 1)).astype(F32), 0.0)))
    qta = jnp.concatenate([qt, feat.astype(BF16), jnp.zeros((LANES - 2 * SUBLANES, r), BF16)], axis=0)

    blk_end = _iota((n_ch, 1), 0) * CMP_STRIDE + (CMP_LEN - 1)
    p = _softmax_cols(_mm(kc_ref[...], qta), blk_end <= qpos)
    o_c = _mm(vct_ref[...], p.astype(BF16))
    psum = []
    for g in range(NSA_KV_HEADS):
        acc = p[:, g * NSA_GROUP * tq:(g * NSA_GROUP + 1) * tq]
        for hh in range(1, NSA_GROUP):
            acc = acc + p[:, (g * NSA_GROUP + hh) * tq:(g * NSA_GROUP + hh + 1) * tq]
        psum.append(acc)
    ovt = ovt_ref[...]
    imp = sum(_mm(ovt, piece) for piece in _split3(jnp.concatenate(psum, axis=1)))
    qpos2 = start + (_iota((1, NSA_KV_HEADS * tq), 1) & (tq - 1))
    neg = _top_blocks_cols(imp, qpos2, nsp)
    for h in range(NSA_HEADS):
        g = h // NSA_GROUP
        neg_ref[:, h * tq:(h + 1) * tq] = neg[:, g * tq:(g + 1) * tq]

    m_ref[...] = jnp.full(m_ref.shape, NEG_INF, F32)
    l_ref[...] = jnp.zeros(l_ref.shape, F32)
    acc_ref[...] = jnp.zeros(acc_ref.shape, F32)
    bpt = tk // SEL_BLOCK

    qta_ref[...] = qta

    def scores(t, buf):
        off = pl.multiple_of(t * tk, tk)
        buf[...] = _mm(ks_ref[pl.ds(off, tk), :], qta_ref[...])

    def tile(t, buf, causal):
        off = pl.multiple_of(t * tk, tk)
        s = jnp.concatenate([buf[j * SEL_BLOCK:(j + 1) * SEL_BLOCK, :] + neg_ref[pl.ds(t * bpt + j, 1), :]
                             for j in range(bpt)], axis=0)
        if causal:
            s = jnp.where(off + _iota((tk, 1), 0) <= qpos, s, NEG_INF)
        m_old = m_ref[...]
        m_new = jnp.maximum(m_old, jnp.max(s, axis=0, keepdims=True))
        m_use = jnp.where(m_new == NEG_INF, 0.0, m_new)
        alpha = jnp.exp(m_old - m_use)
        e = jnp.exp(s - m_use)
        l_ref[...] = alpha * l_ref[...] + jnp.sum(e, axis=0, keepdims=True)
        acc_ref[...] = alpha * acc_ref[...] + _mm(vst_ref[:, pl.ds(off, tk)], e.astype(BF16))
        m_ref[...] = m_new

    n_full = start // tk
    pairs = n_full // 2
    scores(0, sa_ref)

    def body(u, carry):
        scores(2 * u + 1, sb_ref)
        tile(2 * u, sa_ref, False)
        scores(2 * u + 2, sa_ref)
        tile(2 * u + 1, sb_ref, False)
        return carry

    lax.fori_loop(0, pairs, body, 0)

    @pl.when(n_full % 2 == 1)
    def _():
        scores(n_full, sb_ref)
        tile(n_full - 1, sa_ref, False)
        tile(n_full, sb_ref, True)

    @pl.when(n_full % 2 == 0)
    def _():
        tile(n_full, sa_ref, True)
    o_s = acc_ref[...] * (1.0 / jnp.maximum(l_ref[...], 1e-30))

    wl = tq + WINDOW
    ws = pl.multiple_of(jnp.maximum(start - WINDOW, 0), tq)
    dist = qpos - (ws + _iota((wl, 1), 0))
    pw = _softmax_cols(_mm(kw_ref[pl.ds(ws, wl), :], qta), (dist >= 0) & (dist <= WINDOW))
    o_w = _mm(vwt_ref[:, pl.ds(ws, wl)], pw.astype(BF16))

    gt = jnp.transpose(_sigmoid(sm_ref[...]))
    gate = [jnp.concatenate([gt[SM_G + 3 * h + c:SM_G + 3 * h + c + 1, :] for h in range(NSA_HEADS)], axis=1)
            for c in range(3)]
    o = gate[0] * o_c + gate[1] * o_s + gate[2] * o_w
    yt = jnp.concatenate([o[(h // NSA_GROUP) * NSA_HD:(h // NSA_GROUP + 1) * NSA_HD, h * tq:(h + 1) * tq]
                          for h in range(NSA_HEADS)], axis=0)
    y_ref[...] = jnp.transpose(yt)


def _nsa_prompt(q8, sm, kca, vct, ksa, vst, kwa, vwt, b, l):
    tq = 128
    tk = 256
    assert l % tk == 0 and l >= tq + WINDOW
    n_ch = l // CMP_STRIDE
    ns = -(-l // SEL_BLOCK)
    nsp = -(-ns // LANES) * LANES
    nq = l // tq
    w = 2 * NSA_KV
    ovt = jnp.transpose(_overlap(n_ch, nsp)).astype(BF16)
    row = lambda bi, i: (bi * nq + i, 0)
    seq_rows = lambda bi, i: (bi, 0)
    seq_cols = lambda bi, i: (0, bi)
    return pl.pallas_call(
        functools.partial(_nsa_prompt_kernel, tq=tq, tk=tk),
        grid=(b, nq),
        in_specs=[pl.BlockSpec((tq, NSA_HEADS * LANES), row),
                  pl.BlockSpec((tq, LANES), row),
                  pl.BlockSpec((n_ch, w), seq_rows),
                  pl.BlockSpec((NSA_KV, n_ch), seq_cols),
                  pl.BlockSpec((l, w), seq_rows),
                  pl.BlockSpec((NSA_KV, l), seq_cols),
                  pl.BlockSpec((l, w), seq_rows),
                  pl.BlockSpec((NSA_KV, l), seq_cols),
                  pl.BlockSpec((nsp, n_ch), lambda bi, i: (0, 0))],
        out_specs=pl.BlockSpec((tq, NSA_Q), row),
        out_shape=jax.ShapeDtypeStruct((b * l, NSA_Q), F32),
        scratch_shapes=[pltpu.VMEM((nsp, NSA_HEADS * tq), F32),
                        pltpu.VMEM((1, NSA_HEADS * tq), F32),
                        pltpu.VMEM((1, NSA_HEADS * tq), F32),
                        pltpu.VMEM((NSA_KV, NSA_HEADS * tq), F32),
                        pltpu.VMEM((2 * NSA_KV, NSA_HEADS * tq), BF16),
                        pltpu.VMEM((tk, NSA_HEADS * tq), F32),
                        pltpu.VMEM((tk, NSA_HEADS * tq), F32)],
        compiler_params=_cparams(("parallel", "arbitrary")),
        name="nsa_prompt",
    )(q8, sm, kca, vct, ksa, vst, kwa, vwt, ovt)


def _nsa_sample_kernel(*refs, pg, page, tq, past_len):
    pt_ref = refs[0]
    del pt_ref
    page_refs = refs[1:1 + pg]
    (q8_ref, sm_ref, kc_ref, vc_ref, knew_ref, kw_ref, ov_ref, y_ref,
     m_ref, l_ref, acc_ref, sel_ref, oc_ref) = refs[1 + pg:]
    j = pl.program_id(1)
    nsp = ov_ref.shape[1]
    r = NSA_HEADS * tq
    q8 = jnp.concatenate([q8_ref[:, h * LANES:(h + 1) * LANES].astype(F32) for h in range(NSA_HEADS)],
                         axis=0).astype(BF16)
    qpos = past_len + (_iota((r, 1), 0) & (tq - 1))
    slope = _row_slopes(tq)

    @pl.when(j == 0)
    def _():
        n_ch = kc_ref.shape[1]
        dist = qpos - (_iota((1, n_ch), 1) * CMP_STRIDE + (CMP_LEN - 1))
        o_c, p = _rows_attend(q8, kc_ref[...], vc_ref[...], slope, dist, dist >= 0)
        oc_ref[...] = o_c
        for g in range(NSA_KV_HEADS):
            psum = p[g * NSA_GROUP * tq:(g * NSA_GROUP + 1) * tq]
            for hh in range(1, NSA_GROUP):
                psum = psum + p[(g * NSA_GROUP + hh) * tq:(g * NSA_GROUP + hh + 1) * tq]
            sel = _top_blocks(_mm(psum, ov_ref[...], HI), qpos[:tq], nsp)
            for hh in range(NSA_GROUP):
                sel_ref[(g * NSA_GROUP + hh) * tq:(g * NSA_GROUP + hh + 1) * tq, :] = sel
        m_ref[...] = jnp.full(m_ref.shape, NEG_INF, F32)
        l_ref[...] = jnp.zeros(l_ref.shape, F32)
        acc_ref[...] = jnp.zeros(acc_ref.shape, F32)

    sel_bf = sel_ref[...].astype(BF16)

    def sel_step(kv, kpos):
        n = kv.shape[1]
        expand = jnp.where(_iota((nsp, n), 0) == (kpos >> 6), 1.0, 0.0).astype(BF16)
        dist = qpos - kpos
        mask = (_mm(sel_bf, expand) > 0.5) & (dist >= 0)
        _rows_online(q8, kv[:NSA_KV].astype(BF16), kv[NSA_KV:].astype(BF16), slope, dist, mask,
                     m_ref, l_ref, acc_ref)

    sel_step(jnp.concatenate([page_refs[k][...] for k in range(pg)], axis=1),
             j * (pg * page) + _iota((1, pg * page), 1))

    @pl.when(j == pl.num_programs(1) - 1)
    def _():
        sel_step(knew_ref[...], past_len + _iota((1, knew_ref.shape[1]), 1))
        o_s = acc_ref[...] / jnp.maximum(l_ref[...], 1e-30)
        kw = kw_ref[...]
        kwpos = past_len - WINDOW + _iota((1, kw.shape[1]), 1)
        dist = qpos - kwpos
        mask = (dist >= 0) & (dist <= WINDOW) & (kwpos >= 0) & (kwpos < past_len + tq)
        o_w, _ = _rows_attend(q8, kw[:NSA_KV].astype(BF16), kw[NSA_KV:].astype(BF16), slope, dist, mask)
        y_ref[...] = _merge_heads(oc_ref[...], o_s, o_w, _sigmoid(sm_ref[...]), tq)


def _nsa_sample(q8, sm, kct, vct, knew_t, win_t, cache_t, page_table, b, l):
    n_pages = page_table.shape[1]
    page = cache_t.shape[2]
    past_len = n_pages * page
    assert l <= SEL_BLOCK and past_len % SEL_BLOCK == 0 and past_len >= WINDOW
    pg = 16
    steps = n_pages // pg
    n_ch = past_len // CMP_STRIDE
    ns = -(-(past_len + l) // SEL_BLOCK)
    nsp = -(-ns // LANES) * LANES
    w = 2 * NSA_KV
    ov = _overlap(n_ch, nsp)
    wcols = win_t.shape[2]

    def page_map(k):
        return lambda bi, j, pt: (pt[bi, j * pg + k], 1, 0)

    per_b = lambda bi, j, pt: (bi, 0)
    per_b3 = lambda bi, j, pt: (bi, 0, 0)
    grid_spec = pltpu.PrefetchScalarGridSpec(
        num_scalar_prefetch=1,
        grid=(b, steps),
        in_specs=[pl.BlockSpec((None, w, page), page_map(k)) for k in range(pg)]
        + [pl.BlockSpec((None, l, NSA_HEADS * LANES), per_b3),
           pl.BlockSpec((l, LANES), per_b),
           pl.BlockSpec((None, NSA_KV, n_ch), per_b3),
           pl.BlockSpec((None, NSA_KV, n_ch), per_b3),
           pl.BlockSpec((None, w, LANES), per_b3),
           pl.BlockSpec((None, w, wcols), per_b3),
           pl.BlockSpec((n_ch, nsp), lambda bi, j, pt: (0, 0))],
        out_specs=pl.BlockSpec((l, NSA_Q), per_b),
        scratch_shapes=[pltpu.VMEM((NSA_HEADS * l, 1), F32),
                        pltpu.VMEM((NSA_HEADS * l, 1), F32),
                        pltpu.VMEM((NSA_HEADS * l, NSA_KV), F32),
                        pltpu.VMEM((NSA_HEADS * l, nsp), F32),
                        pltpu.VMEM((NSA_HEADS * l, NSA_KV), F32)],
    )
    return pl.pallas_call(
        functools.partial(_nsa_sample_kernel, pg=pg, page=page, tq=l, past_len=past_len),
        grid_spec=grid_spec,
        out_shape=jax.ShapeDtypeStruct((b * l, NSA_Q), F32),
        compiler_params=_cparams(("parallel", "arbitrary")),
        name="nsa_sample",
    )(page_table, *([cache_t] * pg), q8.reshape(b, l, NSA_HEADS * LANES), sm, kct, vct, knew_t, win_t, ov)


def _tail_kernel(x_ref, ya_ref, yb_ref, ga_ref, gb_ref, wa_ref, wb_ref, wo_ref, n2_ref, wr_ref, br_ref,
                 h_ref, hn_ref, cmb_ref):
    ma = _mm(ya_ref[...].astype(BF16), wa_ref[...])
    mb = _mm(yb_ref[...].astype(BF16), wb_ref[...])
    m = _sigmoid(ga_ref[...]) * ma + _sigmoid(gb_ref[...]) * mb
    h = x_ref[...] + _mm(m.astype(BF16), wo_ref[...])
    h_ref[...] = h
    hn = h * lax.rsqrt(jnp.mean(h * h, axis=-1, keepdims=True) + RMS_EPS) * n2_ref[...]
    hn_ref[...] = hn.astype(BF16)

    logit = _mm(hn, wr_ref[...], HI) + br_ref[...]
    lane = _iota(logit.shape, 1)
    is_grp = (lane >= N_EXPERTS) & (lane < N_EXPERTS + N_GROUPS)
    gl = jnp.where(is_grp, logit, NEG_INF)
    gmax = jnp.max(gl, axis=-1, keepdims=True)
    gidx = jnp.min(jnp.where(gl == gmax, lane, LANES), axis=-1, keepdims=True) - N_EXPERTS
    p_grp = 1.0 / jnp.sum(jnp.exp(gl - gmax), axis=-1, keepdims=True)
    el = jnp.where((lane >> 3) == gidx, logit, NEG_INF)
    t1 = jnp.max(el, axis=-1, keepdims=True)
    i1 = jnp.min(jnp.where(el == t1, lane, LANES), axis=-1, keepdims=True)
    el2 = jnp.where(lane == i1, NEG_INF, el)
    t2 = jnp.max(el2, axis=-1, keepdims=True)
    i2 = jnp.min(jnp.where(el2 == t2, lane, LANES), axis=-1, keepdims=True)
    e2 = jnp.exp(t2 - t1)
    w1 = p_grp / (1.0 + e2)
    w2 = p_grp * e2 / (1.0 + e2)
    cmb_ref[...] = jnp.where(lane == i1, w1, jnp.where(lane == i2, w2,
                                                        jnp.where(lane == gidx + N_EXPERTS, 1.0, 0.0)))


def _tail(x2d, ya, yb, ga, gb, wa, wb, wo, norm2_g, w_route, b_route):
    n = x2d.shape[0]
    tm = min(256, n)
    row = lambda i: (i, 0)
    fixed = lambda i: (0, 0)
    return pl.pallas_call(
        _tail_kernel,
        grid=(n // tm,),
        in_specs=[pl.BlockSpec((tm, D_MODEL), row),
                  pl.BlockSpec((tm, GDN_V), row),
                  pl.BlockSpec((tm, NSA_Q), row),
                  pl.BlockSpec((tm, D_MODEL), row),
                  pl.BlockSpec((tm, D_MODEL), row),
                  pl.BlockSpec((GDN_V, D_MODEL), fixed),
                  pl.BlockSpec((NSA_Q, D_MODEL), fixed),
                  pl.BlockSpec((D_MODEL, D_MODEL), fixed),
                  pl.BlockSpec((1, D_MODEL), fixed),
                  pl.BlockSpec((D_MODEL, LANES), fixed),
                  pl.BlockSpec((1, LANES), fixed)],
        out_specs=[pl.BlockSpec((tm, D_MODEL), row),
                   pl.BlockSpec((tm, D_MODEL), row),
                   pl.BlockSpec((tm, LANES), row)],
        out_shape=[jax.ShapeDtypeStruct((n, D_MODEL), F32),
                   jax.ShapeDtypeStruct((n, D_MODEL), BF16),
                   jax.ShapeDtypeStruct((n, LANES), F32)],
        compiler_params=_cparams(("parallel",)),
        name="tail",
    )(x2d, ya, yb, ga, gb, wa, wb, wo, norm2_g.reshape(1, D_MODEL), w_route, b_route)


MOE_SLABS = 2 * N_GROUPS
MOE_SLAB_EXPERTS = N_EXPERTS // MOE_SLABS


def _moe_kernel(offs_ref, cnts_ref, h_ref, hn_ref, cmb_ref, wg_ref, wu_ref, wd_ref, nf_ref, y_ref,
                xs_ref, cs_ref, pt_ref, acc_ref, *, rb):
    i = pl.program_id(0)
    sl = pl.program_id(1)
    t = hn_ref.shape[0]
    lane = _iota((t, LANES), 1)

    @pl.when(sl == 0)
    def _():
        cmb = cmb_ref[...]
        oh = jnp.where((lane >= N_EXPERTS) & (lane < N_EXPERTS + N_GROUPS), cmb, 0.0)
        oh_bf = oh.astype(BF16)
        r_i = _iota((t, t), 0)
        c_i = _iota((t, t), 1)
        lt = _iota((LANES, LANES), 0) < _iota((LANES, LANES), 1)
        before = _mm(_ones_where(c_i < r_i), oh_bf)
        totals = jnp.broadcast_to(jnp.sum(oh, axis=0, keepdims=True), (SUBLANES, LANES))
        smaller = _mm(totals, lt.astype(F32), HI)[0:1]
        rank_col = jnp.sum((before + smaller) * oh, axis=1, keepdims=True)
        eye = (_iota((LANES, LANES), 0) == _iota((LANES, LANES), 1)).astype(BF16)
        oht = _nt(eye, oh_bf)
        before_t = _mm(oht.astype(BF16), _ones_where(r_i < c_i))
        totals_t = jnp.broadcast_to(jnp.sum(oht, axis=1, keepdims=True), (LANES, LANES))
        gt = _iota((LANES, LANES), 0) > _iota((LANES, LANES), 1)
        smaller_t = _mm(gt.astype(F32), totals_t, HI)[:, 0:1]
        rank_row = jnp.sum((before_t + smaller_t) * oht, axis=0, keepdims=True)
        perm = _ones_where(r_i == rank_row.astype(jnp.int32))
        pt_ref[...] = _ones_where(c_i == rank_col.astype(jnp.int32))
        xs_ref[...] = _mm(perm, hn_ref[...]).astype(BF16)
        c_hi = cmb.astype(BF16)
        c_lo = (cmb - c_hi.astype(F32)).astype(BF16)
        cs_ref[...] = _mm(perm, c_hi) + _mm(perm, c_lo)
        acc_ref[...] = jnp.zeros(acc_ref.shape, F32)

    g = sl // (MOE_SLABS // N_GROUPS)
    off = offs_ref[i, g]
    row0 = (off // (2 * SUBLANES)) * (2 * SUBLANES)
    nblk = (off + cnts_ref[i, g] - row0 + rb - 1) // rb
    lane_b = _iota((rb, LANES), 1)

    def block(j, carry):
        lo = row0 + j * rb
        r0 = pl.multiple_of(jnp.minimum(lo, t - rb), 2 * SUBLANES)
        x = xs_ref[pl.ds(r0, rb), :]
        cw = jnp.where(r0 + _iota((rb, 1), 0) >= lo, cs_ref[pl.ds(r0, rb), :], 0.0)
        act = _silu(_mm(x, wg_ref[0])) * _mm(x, wu_ref[0])
        parts = []
        for e in range(MOE_SLAB_EXPERTS):
            wgt = jnp.sum(jnp.where(lane_b == sl * MOE_SLAB_EXPERTS + e, cw, 0.0), axis=1, keepdims=True)
            parts.append((act[:, e * D_EXPERT:(e + 1) * D_EXPERT] * wgt).astype(BF16))
        acc_ref[pl.ds(r0, rb), :] += _mm(jnp.concatenate(parts, axis=1), wd_ref[0])
        return carry

    lax.fori_loop(0, nblk, block, 0)

    @pl.when(sl == pl.num_programs(1) - 1)
    def _():
        acc = acc_ref[...]
        a_hi = acc.astype(BF16)
        a_lo = (acc - a_hi.astype(F32)).astype(BF16)
        pt = pt_ref[...]
        v = h_ref[...] + _mm(pt, a_hi) + _mm(pt, a_lo)
        y_ref[...] = v * lax.rsqrt(jnp.mean(v * v, axis=-1, keepdims=True) + RMS_EPS) * nf_ref[...]


def _moe_weights(w_gate, w_up, w_down):
    def cols(w):
        w = w.astype(BF16).reshape(MOE_SLABS, MOE_SLAB_EXPERTS, D_MODEL, D_EXPERT)
        return jnp.transpose(w, (0, 2, 1, 3)).reshape(MOE_SLABS, D_MODEL, MOE_SLAB_EXPERTS * D_EXPERT)
    return cols(w_gate), cols(w_up), w_down.astype(BF16).reshape(MOE_SLABS, MOE_SLAB_EXPERTS * D_EXPERT, D_MODEL)


def _moe(h, hn, cmb, wg, wu, wd, norm_f_g):
    n = h.shape[0]
    t = min(1024, n)
    rb = min(256, t)
    tiles = n // t
    cnts = jnp.sum(cmb[:, N_EXPERTS:N_EXPERTS + N_GROUPS].reshape(tiles, t, N_GROUPS), axis=1).astype(jnp.int32)
    offs = jnp.cumsum(cnts, axis=1) - cnts
    ws = MOE_SLAB_EXPERTS * D_EXPERT
    row = lambda i, s, o, c: (i, 0)
    slab = lambda i, s, o, c: (s, 0, 0)
    grid_spec = pltpu.PrefetchScalarGridSpec(
        num_scalar_prefetch=2,
        grid=(tiles, MOE_SLABS),
        in_specs=[pl.BlockSpec((t, D_MODEL), row),
                  pl.BlockSpec((t, D_MODEL), row),
                  pl.BlockSpec((t, LANES), row),
                  pl.BlockSpec((1, D_MODEL, ws), slab),
                  pl.BlockSpec((1, D_MODEL, ws), slab),
                  pl.BlockSpec((1, ws, D_MODEL), slab),
                  pl.BlockSpec((1, D_MODEL), lambda i, s, o, c: (0, 0))],
        out_specs=pl.BlockSpec((t, D_MODEL), row),
        scratch_shapes=[pltpu.VMEM((t, D_MODEL), BF16),
                        pltpu.VMEM((t, LANES), F32),
                        pltpu.VMEM((t, t), BF16),
                        pltpu.VMEM((t, D_MODEL), F32)],
    )
    return pl.pallas_call(
        functools.partial(_moe_kernel, rb=rb),
        grid_spec=grid_spec,
        out_shape=jax.ShapeDtypeStruct((n, D_MODEL), F32),
        compiler_params=_cparams(("parallel", "arbitrary")),
        name="moe",
    )(offs, cnts, h, hn, cmb, wg, wu, wd, norm_f_g.reshape(1, D_MODEL))


def _route_weights(w_grp, b_grp, w_rt, b_rt):
    pad = LANES - N_EXPERTS - N_GROUPS
    w = jnp.concatenate([w_rt, w_grp, jnp.zeros((D_MODEL, pad), F32)], axis=1)
    bias = jnp.concatenate([b_rt, b_grp, jnp.zeros((pad,), F32)]).reshape(1, LANES)
    return w, bias


def _finish(x2d, ya, yb, ga, gb, wts):
    h, hn, cmb = _tail(x2d, ya, yb, ga, gb, wts["wa"], wts["wb"], wts["wo"], wts["norm2_g"],
                       wts["w_route"], wts["b_route"])
    return _moe(h, hn, cmb, wts["wg"], wts["wu"], wts["wd"], wts["norm_f_g"])


def _prompt_layer(x, wts):
    b, l, _ = x.shape
    x2d = x.reshape(b * l, D_MODEL)
    qkv, z, q8, ga, gb, sm, kv4t, kvwt, cmp_rows, ksa, kwa, vst, vwt = _project(
        x2d, wts["norm1_g"], wts["w_pk"], b, l)
    conv0 = jnp.zeros((b, GDN_CONV - 1, GDN_CONV_DIM), F32)
    s0 = jnp.zeros((b, GDN_HEADS, GDN_DK, GDN_DV), F32)
    ya, s_new, conv_new = _gdn(qkv, z, sm, conv0, s0, wts["conv_w"], wts["a_log"], wts["dt_bias"],
                               wts["gdn_norm_g"], b, l)
    head, tail = _pool_rows(cmp_rows, wts["pw"])
    kca, vct = _compress(head, tail, wts["cmp_wk"], wts["cmp_wv"], b)
    yb = _nsa_prompt(q8, sm, kca, vct, ksa, vst, kwa, vwt, b, l)
    y = _finish(x2d, ya, yb, ga, gb, wts)
    win_buf = min(WINDOW, l)
    kv_new = jnp.transpose(kv4t.reshape(b, 4, NSA_KV_HEADS, NSA_HD, l), (0, 4, 1, 2, 3))
    win_new = jnp.transpose(kvwt[:, :, l - win_buf:].reshape(b, 2, NSA_KV_HEADS, NSA_HD, win_buf), (0, 4, 1, 2, 3))
    return (y.reshape(b, l, D_MODEL), kv_new, win_new, s_new, conv_new)


def _sample_layer(x, cache_kv_l, page_table, cache_win_l, s0, conv_buf, wts):
    b, l, _ = x.shape
    x2d = x.reshape(b * l, D_MODEL)
    qkv, z, q8, ga, gb, sm, kv4, kvw = _project(x2d, wts["norm1_g"], wts["w_pk"], b, l)
    ya, s_new, conv_new = _gdn(qkv, z, sm, conv_buf, s0, wts["conv_w"], wts["a_log"], wts["dt_bias"],
                               wts["gdn_norm_g"], b, l)
    n_phys, page = cache_kv_l.shape[:2]
    w = 2 * NSA_KV
    cache_t = jnp.transpose(cache_kv_l, (0, 2, 3, 4, 1)).reshape(n_phys, 2 * w, page)
    win_buf = cache_win_l.shape[1]
    win_old_t = jnp.transpose(cache_win_l, (0, 2, 3, 4, 1)).reshape(b, w, win_buf)
    head_t, tail_t = _pool_pages(cache_t, page_table, wts["pos_wk"], wts["pos_wv"])
    kct, vct = _compress_t(head_t, tail_t, wts["cmp_wk"], wts["cmp_wv"])
    new_t = jnp.transpose(kv4.reshape(b, l, 2 * w), (0, 2, 1))
    knew_t = jnp.pad(new_t[:, w:], ((0, 0), (0, 0), (0, LANES - l)))
    win_new_t = jnp.transpose(kvw.reshape(b, l, w), (0, 2, 1))
    wcols = -(-(win_buf + l) // LANES) * LANES
    win_t = jnp.concatenate([win_old_t, win_new_t, jnp.zeros((b, w, wcols - win_buf - l), F32)], axis=2)
    yb = _nsa_sample(q8, sm, kct, vct, knew_t, win_t, cache_t, page_table, b, l)
    y = _finish(x2d, ya, yb, ga, gb, wts)
    win_new = jnp.transpose(win_t[:, :, l:l + win_buf].reshape(b, 2, NSA_KV_HEADS, NSA_HD, win_buf), (0, 4, 1, 2, 3))
    return (y.reshape(b, l, D_MODEL), kv4.reshape(b, l, 4, NSA_KV_HEADS, NSA_HD), win_new, s_new, conv_new)


def kernel(x_prompt, x_sample, cache_kv, page_table, cache_win, state_gdn, state_conv, norm1_g, w_in, gdn_conv_w, gdn_a_log, gdn_dt_bias, gdn_norm_g, cmp_pos_wk, cmp_pos_wv, cmp_wk, cmp_wv, w_branch_a, w_branch_b, w_out, norm2_g, w_grp, b_grp, w_rt, b_rt, w_e_gate, w_e_up, w_e_down, norm_f_g):
    assert w_in.shape[0] == 1, "single layer"
    w_route, b_route = _route_weights(w_grp[0], b_grp[0], w_rt[0], b_rt[0])
    wg, wu, wd = _moe_weights(w_e_gate[0], w_e_up[0], w_e_down[0])
    wts = dict(
        norm1_g=norm1_g[0], w_pk=_pack_w_in(w_in[0]),
        conv_w=gdn_conv_w[0], a_log=gdn_a_log[0], dt_bias=gdn_dt_bias[0], gdn_norm_g=gdn_norm_g[0],
        pw=_pos_weights(cmp_pos_wk[0], cmp_pos_wv[0]), pos_wk=cmp_pos_wk[0], pos_wv=cmp_pos_wv[0],
        cmp_wk=cmp_wk[0], cmp_wv=cmp_wv[0],
        wa=w_branch_a[0].astype(BF16), wb=w_branch_b[0].astype(BF16), wo=w_out[0].astype(BF16),
        norm2_g=norm2_g[0], w_route=w_route, b_route=b_route,
        wg=wg, wu=wu, wd=wd, norm_f_g=norm_f_g,
    )
    yp, kvp, winp, sp, cp = _prompt_layer(x_prompt, wts)
    ys, kvs, wins, ss, cs = _sample_layer(x_sample, cache_kv[0], page_table, cache_win[0], state_gdn[0],
                                          state_conv[0], wts)
    return (yp, ys, kvp[None], kvs[None], winp[None], wins[None], sp[None], ss[None], cp[None], cs[None])
```

```python
import functools
import math

import numpy as np
import jax
import jax.numpy as jnp
from jax import lax
from jax.experimental import pallas as pl
from jax.experimental.pallas import tpu as pltpu

F32 = jnp.float32
BF16 = jnp.bfloat16
HI = lax.Precision.HIGHEST

LANES = 128
SUBLANES = 8
VMEM_LIMIT = 56 * 1024 * 1024
ROW_TILE = 512

D_MODEL = 1024
GDN_HEADS = 4
GDN_DK = 128
GDN_DV = 128
GDN_QK = GDN_HEADS * GDN_DK
GDN_V = GDN_HEADS * GDN_DV
GDN_CONV_DIM = 2 * GDN_QK + GDN_V
GDN_CONV = 4
GDN_CHUNK = 64
GDN_SEQS_PER_STEP = 4
NSA_HEADS = 8
NSA_KV_HEADS = 2
NSA_GROUP = NSA_HEADS // NSA_KV_HEADS
NSA_HD = 64
NSA_Q = NSA_HEADS * NSA_HD
NSA_KV = NSA_KV_HEADS * NSA_HD
CMP_STRIDE = 16
CMP_LEN = 2 * CMP_STRIDE
SEL_BLOCK = 64
SEL_TOP = 16
WINDOW = 512
FORCE_SCORE = 1.0e4
N_GROUPS = 4
EXPERTS_PER_GROUP = 8
N_EXPERTS = N_GROUPS * EXPERTS_PER_GROUP
D_EXPERT = 256
RMS_EPS = 1e-6
VS_ROWS = NSA_KV + 2 * SUBLANES
NEG_INF = float("-inf")

_OFF_QKV = 0
_OFF_Z = _OFF_QKV + GDN_CONV_DIM
_OFF_B = _OFF_Z + GDN_V
_OFF_A = _OFF_B + GDN_HEADS
_OFF_Q = _OFF_A + GDN_HEADS
_OFF_KV = _OFF_Q + NSA_Q
_OFF_G = _OFF_KV + 6 * NSA_KV
_OFF_GA = _OFF_G + 3 * NSA_HEADS
_OFF_GB = _OFF_GA + D_MODEL

SM_B = 0
SM_A = 4
SM_G = 8

_PK = {}
_c = 0
for _n, _w in (("qkv", GDN_CONV_DIM), ("z", GDN_V), ("q8", NSA_HEADS * LANES), ("kv4", 4 * NSA_KV),
               ("kvw", 2 * NSA_KV), ("ga", D_MODEL), ("gb", D_MODEL), ("sm", LANES)):
    _PK[_n] = (_c, _w)
    _c += _w
PK_DIM = _c


def _cparams(sem):
    return pltpu.CompilerParams(dimension_semantics=sem, vmem_limit_bytes=VMEM_LIMIT)


def _nt(a, b, precision=None):
    return lax.dot_general(a, b, (((1,), (1,)), ((), ())), preferred_element_type=F32, precision=precision)


def _tn(a, b, precision=None):
    return lax.dot_general(a, b, (((0,), (0,)), ((), ())), preferred_element_type=F32, precision=precision)


def _mm(a, b, precision=None):
    return jnp.dot(a, b, preferred_element_type=F32, precision=precision)


def _mm3(a, b):
    a_hi = a.astype(BF16)
    b_hi = b.astype(BF16)
    a_lo = (a - a_hi.astype(F32)).astype(BF16)
    b_lo = (b - b_hi.astype(F32)).astype(BF16)
    return _mm(a_hi, b_hi) + _mm(a_lo, b_hi) + _mm(a_hi, b_lo)


def _sigmoid(x):
    return 1.0 / (1.0 + jnp.exp(-x))


def _silu(x):
    return x * _sigmoid(x)


def _iota(shape, dim):
    return lax.broadcasted_iota(jnp.int32, shape, dim)


def _ones_where(cond):
    return jnp.where(cond, 1.0, 0.0).astype(BF16)


def _pack_w_in(w_in):
    q = w_in[:, _OFF_Q:_OFF_Q + NSA_Q]
    zeros64 = jnp.zeros((D_MODEL, NSA_HD), w_in.dtype)
    q8 = []
    for h in range(NSA_HEADS):
        qh = q[:, h * NSA_HD:(h + 1) * NSA_HD]
        q8.append(jnp.concatenate([qh, zeros64] if h < NSA_GROUP else [zeros64, qh], axis=1))
    sm = jnp.concatenate([w_in[:, _OFF_B:_OFF_B + GDN_HEADS], w_in[:, _OFF_A:_OFF_A + GDN_HEADS],
                          w_in[:, _OFF_G:_OFF_G + 3 * NSA_HEADS],
                          jnp.zeros((D_MODEL, LANES - 2 * GDN_HEADS - 3 * NSA_HEADS), w_in.dtype)], axis=1)
    cols = [w_in[:, _OFF_QKV:_OFF_QKV + GDN_CONV_DIM], w_in[:, _OFF_Z:_OFF_Z + GDN_V]] + q8 + [
        w_in[:, _OFF_KV:_OFF_KV + 4 * NSA_KV], w_in[:, _OFF_KV + 4 * NSA_KV:_OFF_KV + 6 * NSA_KV],
        w_in[:, _OFF_GA:_OFF_GA + D_MODEL], w_in[:, _OFF_GB:_OFF_GB + D_MODEL], sm]
    return jnp.concatenate(cols, axis=1).astype(BF16)


def _pos_features(pos, shape):
    lane = _iota(shape, 1)
    feat = jnp.where(lane == 0, (pos >> 7) * LANES, jnp.where(lane == 1, pos & (LANES - 1),
                                                               jnp.where(lane < 4, 1, 0)))
    return feat.astype(F32).astype(BF16)


def _proj_kernel(x_ref, g_ref, w_ref, qkv_ref, z_ref, q8_ref, ga_ref, gb_ref, sm_ref, *kv_refs, tiles_per_seq):
    tm = x_ref.shape[0]
    x = x_ref[...]
    xn = (x * lax.rsqrt(jnp.mean(x * x, axis=-1, keepdims=True) + RMS_EPS) * g_ref[...]).astype(BF16)

    def seg(name):
        a, w = _PK[name]
        return _mm(xn, w_ref[:, a:a + w])

    qkv_ref[...] = seg("qkv")
    z_ref[...] = seg("z")
    q8_ref[...] = (seg("q8") * (NSA_HD ** -0.5)).astype(BF16)
    ga_ref[...] = seg("ga")
    gb_ref[...] = seg("gb")
    sm_ref[...] = seg("sm")
    kv4 = seg("kv4")
    kvw = seg("kvw")
    if tiles_per_seq == 0:
        kv4_ref, kvw_ref = kv_refs
        kv4_ref[...] = kv4
        kvw_ref[...] = kvw
        return
    kv4t_ref, kvwt_ref, cmp_ref, ksa_ref, kwa_ref, vst_ref, vwt_ref = kv_refs
    kv4t = jnp.transpose(kv4)
    kvwt = jnp.transpose(kvw)
    kv4t_ref[...] = kv4t
    kvwt_ref[...] = kvwt
    cmp_ref[...] = kv4[:, :2 * NSA_KV]
    pos = (pl.program_id(0) % tiles_per_seq) * tm + _iota((tm, 1), 0)
    feat = _pos_features(pos, (tm, LANES))
    ksa_ref[:, :NSA_KV] = kv4[:, 2 * NSA_KV:3 * NSA_KV].astype(BF16)
    ksa_ref[:, NSA_KV:] = feat
    kwa_ref[:, :NSA_KV] = kvw[:, :NSA_KV].astype(BF16)
    kwa_ref[:, NSA_KV:] = feat
    vst_ref[:NSA_KV, :] = kv4t[3 * NSA_KV:].astype(BF16)
    vst_ref[NSA_KV:, :] = _ones_where(_iota((VS_ROWS - NSA_KV, tm), 0) == 0)
    vwt_ref[...] = kvwt[NSA_KV:].astype(BF16)


def _project(x2d, norm_g, w_pk, b, l):
    n = x2d.shape[0]
    tm = min(ROW_TILE, n)
    long_seq = l % tm == 0
    tps = l // tm if long_seq else 0
    w = 2 * NSA_KV
    row = lambda i: (i, 0)
    out_shape = [jax.ShapeDtypeStruct((n, _PK[k][1]), dt) for k, dt in
                 (("qkv", F32), ("z", F32), ("q8", BF16), ("ga", F32), ("gb", F32), ("sm", F32))]
    out_specs = [pl.BlockSpec((tm, s.shape[1]), row) for s in out_shape]
    if long_seq:
        seq_t = lambda i: (i // tps, 0, i % tps)
        out_shape += [jax.ShapeDtypeStruct((b, 2 * w, l), F32), jax.ShapeDtypeStruct((b, w, l), F32),
                      jax.ShapeDtypeStruct((n, w), F32), jax.ShapeDtypeStruct((n, w), BF16),
                      jax.ShapeDtypeStruct((n, w), BF16), jax.ShapeDtypeStruct((VS_ROWS, n), BF16),
                      jax.ShapeDtypeStruct((NSA_KV, n), BF16)]
        out_specs += [pl.BlockSpec((None, 2 * w, tm), seq_t), pl.BlockSpec((None, w, tm), seq_t),
                      pl.BlockSpec((tm, w), row), pl.BlockSpec((tm, w), row), pl.BlockSpec((tm, w), row),
                      pl.BlockSpec((VS_ROWS, tm), lambda i: (0, i)), pl.BlockSpec((NSA_KV, tm), lambda i: (0, i))]
    else:
        out_shape += [jax.ShapeDtypeStruct((n, 2 * w), F32), jax.ShapeDtypeStruct((n, w), F32)]
        out_specs += [pl.BlockSpec((tm, 2 * w), row), pl.BlockSpec((tm, w), row)]
    return pl.pallas_call(
        functools.partial(_proj_kernel, tiles_per_seq=tps),
        grid=(n // tm,),
        in_specs=[pl.BlockSpec((tm, D_MODEL), row),
                  pl.BlockSpec((1, D_MODEL), lambda i: (0, 0)),
                  pl.BlockSpec((D_MODEL, PK_DIM), lambda i: (0, 0), pipeline_mode=pl.Buffered(1))],
        out_specs=out_specs,
        out_shape=out_shape,
        compiler_params=_cparams(("parallel",)),
        name="proj",
    )(x2d, norm_g.reshape(1, D_MODEL), w_pk)


def _unit_lower_inverses(a_list, c):
    r = _iota((c, c), 0)
    col = _iota((c, c), 1)
    eye = (r == col).astype(F32)
    n1 = [jnp.where((r >> 3) == (col >> 3), -a, 0.0) for a in a_list]
    n2 = [_mm3(x, x) for x in n1]
    n4 = [_mm3(x, x) for x in n2]
    t = [_mm3(eye + x, eye + y) for x, y in zip(n1, n2)]
    t = [_mm3(x, eye + y) for x, y in zip(t, n4)]
    s = SUBLANES
    while s < c:
        sh = s.bit_length() - 1
        off = ((r >> (sh + 1)) == (col >> (sh + 1))) & ((r >> sh) != (col >> sh))
        ta = [_mm3(x, jnp.where(off, a, 0.0)) for x, a in zip(t, a_list)]
        t = [x - _mm3(y, x) for x, y in zip(t, ta)]
        s *= 2
    return t


def _gdn_kernel(qkv_ref, z_ref, sm_ref, cbuf_ref, s0_ref, cw_ref, alog_ref, dtb_ref, ng_ref,
                y_ref, snew_ref, cnew_ref, ext_ref, st_ref, *, chunk, nb):
    c = chunk
    ci = pl.program_id(1)

    @pl.when(ci == 0)
    def _():
        ext_ref[:, 0:SUBLANES, :] = cbuf_ref[...]
        st_ref[...] = s0_ref[...]

    r = _iota((c, c), 0)
    col = _iota((c, c), 1)
    tri_incl = r >= col
    tri_strict = r > col
    tri_f = tri_incl.astype(F32)
    pick = (_iota((SUBLANES, LANES), 1) == _iota((SUBLANES, LANES), 0) + SM_A).astype(F32)
    base = SUBLANES - (GDN_CONV - 1)
    ng = ng_ref[...]

    units = []
    for bb in range(nb):
        u = qkv_ref[bb]
        ext_ref[bb, SUBLANES:SUBLANES + c, :] = u
        conv = cw_ref[0:1, :] * ext_ref[bb, base:base + c, :]
        for i in range(1, GDN_CONV - 1):
            conv = conv + cw_ref[i:i + 1, :] * ext_ref[bb, base + i:base + i + c, :]
        conv = conv + cw_ref[GDN_CONV - 1:GDN_CONV, :] * u
        halo = ext_ref[bb, c:c + SUBLANES, :]
        ext_ref[bb, 0:SUBLANES, :] = halo
        cnew_ref[bb] = halo
        qkv = _silu(conv)

        sm = sm_ref[bb]
        beta_all = _sigmoid(sm)
        xa = sm + dtb_ref[...]
        softplus = jnp.maximum(xa, 0.0) + jnp.log(1.0 + jnp.exp(-jnp.abs(xa)))
        g_all = -jnp.exp(alog_ref[...]) * softplus
        gcum_all = _mm(tri_f, g_all, HI)
        gcum_rows = _nt(pick, gcum_all, HI)
        for h in range(GDN_HEADS):
            q = qkv[:, h * GDN_DK:(h + 1) * GDN_DK]
            k = qkv[:, GDN_QK + h * GDN_DK:GDN_QK + (h + 1) * GDN_DK]
            v = qkv[:, 2 * GDN_QK + h * GDN_DV:2 * GDN_QK + (h + 1) * GDN_DV]
            q = q * lax.rsqrt(jnp.sum(q * q, axis=-1, keepdims=True) + RMS_EPS) * (GDN_DK ** -0.5)
            k = k * lax.rsqrt(jnp.sum(k * k, axis=-1, keepdims=True) + RMS_EPS)
            beta = beta_all[:, SM_B + h:SM_B + h + 1]
            gc_col = gcum_all[:, SM_A + h:SM_A + h + 1]
            gc_row = gcum_rows[h:h + 1, :]
            gc_last = gcum_all[c - 1:c, SM_A + h:SM_A + h + 1]
            units.append(dict(bb=bb, h=h, q=q, k=k, v=v, beta=beta, gc_col=gc_col, gc_last=gc_last,
                              decay=jnp.exp(jnp.where(tri_incl, gc_col - gc_row, NEG_INF)),
                              eg=jnp.exp(gc_col), kb=k * beta, k_bf=k.astype(BF16)))

    a_list = [jnp.where(tri_strict, _nt(un["kb"].astype(BF16), un["k_bf"]) * un["decay"], 0.0) for un in units]
    t_list = _unit_lower_inverses(a_list, c)
    u_coef = [_mm3(t, un["v"] * un["beta"]) for t, un in zip(t_list, units)]
    w_coef = [_mm3(t, un["kb"] * un["eg"]) for t, un in zip(t_list, units)]
    qk = [(_nt(un["q"].astype(BF16), un["k_bf"]) * un["decay"]).astype(BF16) for un in units]
    s_old = [st_ref[un["bb"], un["h"]] for un in units]
    s_bf = [s.astype(BF16) for s in s_old]
    uu = [uc - _mm(wc.astype(BF16), s) for uc, wc, s in zip(u_coef, w_coef, s_bf)]
    uu_bf = [x.astype(BF16) for x in uu]
    o_list = [_mm((un["q"] * un["eg"]).astype(BF16), s) + _mm(a, x)
              for un, s, a, x in zip(units, s_bf, qk, uu_bf)]
    for un, s, x, o in zip(units, s_old, uu_bf, o_list):
        bb, h = un["bb"], un["h"]
        k_tail = un["k"] * jnp.exp(un["gc_last"] - un["gc_col"])
        st_ref[bb, h] = s * jnp.exp(un["gc_last"]) + _tn(k_tail.astype(BF16), x)
        on = o * lax.rsqrt(jnp.mean(o * o, axis=-1, keepdims=True) + RMS_EPS) * ng
        y_ref[bb, :, h * GDN_DV:(h + 1) * GDN_DV] = on * _silu(z_ref[bb, :, h * GDN_DV:(h + 1) * GDN_DV])

    @pl.when(ci == pl.num_programs(1) - 1)
    def _():
        snew_ref[...] = st_ref[...]


def _gdn(qkv, z, sm, conv_buf, s0, conv_w, a_log, dt_bias, norm_g, b, l):
    c = math.gcd(l, GDN_CHUNK)
    nc = l // c
    nb = math.gcd(b, GDN_SEQS_PER_STEP)
    cbuf8 = jnp.pad(conv_buf, ((0, 0), (SUBLANES - (GDN_CONV - 1), 0), (0, 0)))
    pad_a = (SM_A, LANES - SM_A - GDN_HEADS)
    alog_row = jnp.pad(a_log, pad_a).reshape(1, LANES)
    dtb_row = jnp.pad(dt_bias, pad_a).reshape(1, LANES)
    row = lambda bi, ci: (bi, ci, 0)
    per_seq3 = lambda bi, ci: (bi, 0, 0)
    per_seq4 = lambda bi, ci: (bi, 0, 0, 0)
    fixed = lambda bi, ci: (0, 0)
    y, s_new, c_new = pl.pallas_call(
        functools.partial(_gdn_kernel, chunk=c, nb=nb),
        grid=(b // nb, nc),
        in_specs=[pl.BlockSpec((nb, c, GDN_CONV_DIM), row),
                  pl.BlockSpec((nb, c, GDN_V), row),
                  pl.BlockSpec((nb, c, LANES), row),
                  pl.BlockSpec((nb, SUBLANES, GDN_CONV_DIM), per_seq3),
                  pl.BlockSpec((nb, GDN_HEADS, GDN_DK, GDN_DV), per_seq4),
                  pl.BlockSpec((GDN_CONV, GDN_CONV_DIM), fixed),
                  pl.BlockSpec((1, LANES), fixed),
                  pl.BlockSpec((1, LANES), fixed),
                  pl.BlockSpec((1, GDN_DV), fixed)],
        out_specs=[pl.BlockSpec((nb, c, GDN_V), row),
                   pl.BlockSpec((nb, GDN_HEADS, GDN_DK, GDN_DV), per_seq4),
                   pl.BlockSpec((nb, SUBLANES, GDN_CONV_DIM), per_seq3)],
        out_shape=[jax.ShapeDtypeStruct((b, l, GDN_V), F32),
                   jax.ShapeDtypeStruct((b, GDN_HEADS, GDN_DK, GDN_DV), F32),
                   jax.ShapeDtypeStruct((b, SUBLANES, GDN_CONV_DIM), F32)],
        scratch_shapes=[pltpu.VMEM((nb, c + SUBLANES, GDN_CONV_DIM), F32),
                        pltpu.VMEM((nb, GDN_HEADS, GDN_DK, GDN_DV), F32)],
        compiler_params=_cparams(("parallel", "arbitrary")),
        name="gdn",
    )(qkv.reshape(b, l, GDN_CONV_DIM), z.reshape(b, l, GDN_V), sm.reshape(b, l, LANES), cbuf8, s0, conv_w,
      alog_row, dtb_row, norm_g.reshape(1, GDN_DV))
    return y.reshape(b * l, GDN_V), s_new, c_new[:, SUBLANES - (GDN_CONV - 1):]


def _pool_kernel(*refs, n_in):
    refs = refs[len(refs) - 2 * n_in - 3:]
    x_refs, pw_ref, head_ref, tail_ref = refs[:2 * n_in], refs[2 * n_in], refs[2 * n_in + 1], refs[2 * n_in + 2]
    rows = x_refs[0].shape[0]
    n = rows // CMP_STRIDE
    for j, x_ref in enumerate(x_refs):
        half = slice((j % 2) * NSA_KV, (j % 2 + 1) * NSA_KV)
        head = None
        tail = None
        for p in range(CMP_STRIDE):
            xr = x_ref[pl.ds(p, n, stride=CMP_STRIDE), :]
            hp = xr * pw_ref[p:p + 1, half]
            tp = xr * pw_ref[CMP_STRIDE + p:CMP_STRIDE + p + 1, half]
            head = hp if head is None else head + hp
            tail = tp if tail is None else tail + tp
        head_ref[(j // 2) * n:(j // 2 + 1) * n, half] = head
        tail_ref[(j // 2) * n:(j // 2 + 1) * n, half] = tail


def _pos_weights(pos_wk, pos_wv):
    return jnp.concatenate([jnp.broadcast_to(pos_wk[:, None], (CMP_LEN, NSA_KV)),
                            jnp.broadcast_to(pos_wv[:, None], (CMP_LEN, NSA_KV))], axis=1).astype(F32)


def _pool_rows(kv4, pw):
    n = kv4.shape[0]
    r = min(2048, n)
    w = 2 * NSA_KV
    return pl.pallas_call(
        functools.partial(_pool_kernel, n_in=1),
        grid=(n // r,),
        in_specs=[pl.BlockSpec((r, NSA_KV), lambda i: (i, 0)),
                  pl.BlockSpec((r, NSA_KV), lambda i: (i, 1)),
                  pl.BlockSpec((CMP_LEN, w), lambda i: (0, 0))],
        out_specs=[pl.BlockSpec((r // CMP_STRIDE, w), lambda i: (i, 0))] * 2,
        out_shape=[jax.ShapeDtypeStruct((n // CMP_STRIDE, w), F32)] * 2,
        compiler_params=_cparams(("parallel",)),
        name="pool_rows",
    )(kv4, kv4, pw)


def _pool_pages_kernel(*refs, pg):
    page_refs = refs[1:1 + pg]
    pk_hi, pk_lo, pv_hi, pv_lo, head_ref, tail_ref = refs[1 + pg:]
    x = jnp.concatenate([r[...] for r in page_refs], axis=1)
    n = head_ref.shape[1]
    for half, (p_hi, p_lo) in enumerate(((pk_hi, pk_lo), (pv_hi, pv_lo))):
        xs = x[half * NSA_KV:(half + 1) * NSA_KV]
        hi = xs.astype(BF16)
        lo = (xs - hi.astype(F32)).astype(BF16)
        out = _mm(hi, p_hi[...]) + _mm(lo, p_hi[...]) + _mm(hi, p_lo[...])
        head_ref[half * NSA_KV:(half + 1) * NSA_KV, :] = out[:, :n]
        tail_ref[half * NSA_KV:(half + 1) * NSA_KV, :] = out[:, n:]


def _pool_matrix(pos_w, positions):
    chunks = positions // CMP_STRIDE
    p = jnp.arange(positions)[:, None]
    c = jnp.arange(chunks)[None, :]
    inside = (p // CMP_STRIDE) == c
    head = jnp.where(inside, pos_w[:CMP_STRIDE][p % CMP_STRIDE], 0.0)
    tail = jnp.where(inside, pos_w[CMP_STRIDE:][p % CMP_STRIDE], 0.0)
    m = jnp.concatenate([head, tail], axis=1).astype(F32)
    hi = m.astype(BF16)
    return hi, (m - hi.astype(F32)).astype(BF16)


def _pool_pages(cache_t, page_table, pos_wk, pos_wv):
    b, n_pages = page_table.shape
    page = cache_t.shape[2]
    pg = LANES * CMP_STRIDE // page
    w = 2 * NSA_KV
    steps = n_pages // pg
    n_ch = n_pages * page // CMP_STRIDE
    mats = _pool_matrix(pos_wk, pg * page) + _pool_matrix(pos_wv, pg * page)

    def page_map(k):
        return lambda bi, j, pt: (pt[bi, j * pg + k], 0, 0)

    fixed = lambda bi, j, pt: (0, 0)
    grid_spec = pltpu.PrefetchScalarGridSpec(
        num_scalar_prefetch=1,
        grid=(b, steps),
        in_specs=[pl.BlockSpec((None, w, page), page_map(k)) for k in range(pg)]
        + [pl.BlockSpec((pg * page, 2 * LANES), fixed)] * 4,
        out_specs=[pl.BlockSpec((None, w, LANES), lambda bi, j, pt: (bi, 0, j))] * 2,
    )
    return pl.pallas_call(
        functools.partial(_pool_pages_kernel, pg=pg),
        grid_spec=grid_spec,
        out_shape=[jax.ShapeDtypeStruct((b, w, n_ch), F32)] * 2,
        compiler_params=_cparams(("parallel", "arbitrary")),
        name="pool_pages",
    )(page_table, *([cache_t] * pg), *mats)


def _cmp_t_kernel(head_ref, tail_ref, wk_ref, wv_ref, kct_ref, vct_ref):
    n = head_ref.shape[1]
    blocks = head_ref[...] + pltpu.roll(tail_ref[...], n - 1, 1)
    kct_ref[...] = _mm(wk_ref[...], blocks[:NSA_KV].astype(BF16)).astype(BF16)
    vct_ref[...] = _mm(wv_ref[...], blocks[NSA_KV:].astype(BF16)).astype(BF16)


def _compress_t(head_t, tail_t, cmp_wk, cmp_wv):
    b, w, n_ch = head_t.shape
    seq = lambda i: (i, 0, 0)
    return pl.pallas_call(
        _cmp_t_kernel,
        grid=(b,),
        in_specs=[pl.BlockSpec((None, w, n_ch), seq)] * 2 + [pl.BlockSpec((NSA_KV, NSA_KV), lambda i: (0, 0))] * 2,
        out_specs=[pl.BlockSpec((None, NSA_KV, n_ch), seq)] * 2,
        out_shape=[jax.ShapeDtypeStruct((b, NSA_KV, n_ch), BF16)] * 2,
        compiler_params=_cparams(("parallel",)),
        name="compress_t",
    )(head_t, tail_t, jnp.transpose(_block_diag2(cmp_wk)), jnp.transpose(_block_diag2(cmp_wv)))


def _cmp_kernel(head_ref, tail_ref, wk_ref, wv_ref, kca_ref, vct_ref):
    n = head_ref.shape[0]
    blocks = head_ref[...] + pltpu.roll(tail_ref[...], n - 1, 0)
    kca_ref[:, :NSA_KV] = _mm(blocks[:, :NSA_KV].astype(BF16), wk_ref[...]).astype(BF16)
    blk_end = _iota((n, 1), 0) * CMP_STRIDE + (CMP_LEN - 1)
    kca_ref[:, NSA_KV:] = _pos_features(blk_end, (n, LANES))
    vct_ref[...] = _nt(wv_ref[...], blocks[:, NSA_KV:].astype(BF16)).astype(BF16)


def _block_diag2(w):
    z = jnp.zeros_like(w)
    return jnp.concatenate([jnp.concatenate([w, z], axis=1), jnp.concatenate([z, w], axis=1)], axis=0).astype(BF16)


def _compress(head, tail, cmp_wk, cmp_wv, b):
    n_ch = head.shape[0] // b
    w = 2 * NSA_KV
    return pl.pallas_call(
        _cmp_kernel,
        grid=(b,),
        in_specs=[pl.BlockSpec((n_ch, w), lambda i: (i, 0))] * 2
        + [pl.BlockSpec((NSA_KV, NSA_KV), lambda i: (0, 0))] * 2,
        out_specs=[pl.BlockSpec((n_ch, w), lambda i: (i, 0)),
                   pl.BlockSpec((NSA_KV, n_ch), lambda i: (0, i))],
        out_shape=[jax.ShapeDtypeStruct((b * n_ch, w), BF16),
                   jax.ShapeDtypeStruct((NSA_KV, b * n_ch), BF16)],
        compiler_params=_cparams(("parallel",)),
        name="compress",
    )(head, tail, _block_diag2(cmp_wk), jnp.transpose(_block_diag2(cmp_wv)))


def _overlap(n_ch, nsp):
    i = jnp.arange(n_ch)[:, None]
    j = jnp.arange(nsp)[None, :]
    lo = jnp.maximum(i * CMP_STRIDE, j * SEL_BLOCK)
    hi = jnp.minimum(i * CMP_STRIDE + CMP_LEN, (j + 1) * SEL_BLOCK)
    ov = jnp.maximum(hi - lo, 0).astype(F32) / CMP_LEN
    return jnp.where(i < n_ch - 1, ov, 0.0)


def _slope(h):
    return 2.0 ** (-(h + 1))


def _softmax_rows(s, mask):
    s = jnp.where(mask, s, NEG_INF)
    m = jnp.max(s, axis=-1, keepdims=True)
    m = jnp.where(m == NEG_INF, 0.0, m)
    e = jnp.where(mask, jnp.exp(s - m), 0.0)
    return e, jnp.maximum(jnp.sum(e, axis=-1, keepdims=True), 1e-30)


def _top_blocks(imp, qpos, ns_lanes):
    blk = _iota(imp.shape, 1)
    cur = qpos >> 6
    forced = (blk == 0) | (blk == cur) | (blk == cur - 1)
    score = jnp.where(forced, FORCE_SCORE, imp)
    work = jnp.where(blk <= cur, score, NEG_INF)
    sel = jnp.zeros(imp.shape, F32)
    for _ in range(SEL_TOP):
        m = jnp.max(work, axis=-1, keepdims=True)
        cand = jnp.where((work == m) & (m > NEG_INF), blk, ns_lanes)
        first = jnp.min(cand, axis=-1, keepdims=True)
        hit = blk == first
        sel = jnp.where(hit, 1.0, sel)
        work = jnp.where(hit, NEG_INF, work)
    return sel


def _row_slopes(tq):
    head = _iota((NSA_HEADS * tq, 1), 0) >> (tq.bit_length() - 1)
    slope = jnp.zeros((NSA_HEADS * tq, 1), F32)
    for h in range(NSA_HEADS):
        slope = jnp.where(head == h, _slope(h), slope)
    return slope


def _rows_attend(q8, kt, vt, slope, dist, mask):
    s = _mm(q8, kt) - slope * dist.astype(F32)
    e, den = _softmax_rows(s, mask)
    p = e / den
    return _nt(p.astype(BF16), vt), p


def _rows_online(q8, kt, vt, slope, dist, mask, m_ref, l_ref, acc_ref):
    s = jnp.where(mask, _mm(q8, kt) - slope * dist.astype(F32), NEG_INF)
    m_old = m_ref[...]
    m_new = jnp.maximum(m_old, jnp.max(s, axis=-1, keepdims=True))
    m_use = jnp.where(m_new == NEG_INF, 0.0, m_new)
    alpha = jnp.exp(m_old - m_use)
    e = jnp.exp(s - m_use)
    l_ref[...] = alpha * l_ref[...] + jnp.sum(e, axis=-1, keepdims=True)
    acc_ref[...] = alpha * acc_ref[...] + _nt(e.astype(BF16), vt)
    m_ref[...] = m_new


def _merge_heads(o_c, o_s, o_w, gates, tq):
    lane = _iota((tq, LANES), 1)
    mixed = []
    for h in range(NSA_HEADS):
        rows = slice(h * tq, (h + 1) * tq)
        g0 = gates[:, SM_G + 3 * h:SM_G + 3 * h + 1]
        g1 = gates[:, SM_G + 3 * h + 1:SM_G + 3 * h + 2]
        g2 = gates[:, SM_G + 3 * h + 2:SM_G + 3 * h + 3]
        mixed.append(g0 * o_c[rows] + g1 * o_s[rows] + g2 * o_w[rows])
    cols = []
    for p in range(NSA_HEADS // 2):
        a, b = mixed[2 * p], mixed[2 * p + 1]
        if 2 * p < NSA_GROUP:
            cols.append(jnp.where(lane < NSA_HD, a, pltpu.roll(b, NSA_HD, 1)))
        else:
            cols.append(jnp.where(lane < NSA_HD, pltpu.roll(a, NSA_HD, 1), b))
    return jnp.concatenate(cols, axis=1)


def _softmax_cols(s, mask):
    s = jnp.where(mask, s, NEG_INF)
    m = jnp.max(s, axis=0, keepdims=True)
    m = jnp.where(m == NEG_INF, 0.0, m)
    e = jnp.exp(s - m)
    return e * (1.0 / jnp.maximum(jnp.sum(e, axis=0, keepdims=True), 1e-30))


def _split3(x):
    hi = x.astype(BF16)
    r1 = x - hi.astype(F32)
    mid = r1.astype(BF16)
    lo = (r1 - mid.astype(F32)).astype(BF16)
    return hi, mid, lo


def _top_blocks_cols(imp, qpos, nsp):
    blk = _iota(imp.shape, 0)
    cur = qpos >> 6
    forced = (blk == 0) | (blk == cur) | (blk == cur - 1)
    work = jnp.where(blk <= cur, jnp.where(forced, FORCE_SCORE, imp), NEG_INF)
    neg = jnp.full(imp.shape, NEG_INF, F32)
    for _ in range(SEL_TOP):
        m = jnp.max(work, axis=0, keepdims=True)
        cand = jnp.where((work == m) & (m > NEG_INF), blk, nsp)
        hit = blk == jnp.min(cand, axis=0, keepdims=True)
        neg = jnp.where(hit, 0.0, neg)
        work = jnp.where(hit, NEG_INF, work)
    return neg


def _nsa_prompt_kernel(q8_ref, sm_ref, kc_ref, vct_ref, ks_ref, vst_ref, kw_ref, vwt_ref, ovt_ref, y_ref,
                       neg_ref, m_ref, acc_ref, qta_ref, sa_ref, sb_ref, *, tq, tk):
    i = pl.program_id(1)
    start = i * tq
    r = NSA_HEADS * tq
    nsp, n_ch = ovt_ref.shape
    col = _iota((1, r), 1)
    qpos = start + (col & (tq - 1))

    q8 = jnp.concatenate([q8_ref[:, h * LANES:(h + 1) * LANES] for h in range(NSA_HEADS)], axis=0)
    eye = (_iota((LANES, LANES), 0) == _iota((LANES, LANES), 1)).astype(BF16)
    qt = _nt(eye, q8).astype(BF16)
    head = col >> (tq.bit_length() - 1)
    slope = jnp.zeros((1, r), F32)
    for h in range(NSA_HEADS):
        slope = jnp.where(head == h, _slope(h), slope)
    frow = _iota((2 * SUBLANES, r), 0)
    feat = jnp.where(frow < 2, slope, jnp.where(frow == 2, -slope * ((qpos >> 7) * LANES).astype(F32),
                                                jnp.where(frow == 3, -slope * (qpos & (LANES - 1)).astype(F32), 0.0)))
    qta = jnp.concatenate([qt, feat.astype(BF16), jnp.zeros((LANES - 2 * SUBLANES, r), BF16)], axis=0)

    blk_end = _iota((n_ch, 1), 0) * CMP_STRIDE + (CMP_LEN - 1)
    p = _softmax_cols(_mm(kc_ref[...], qta), blk_end <= qpos)
    o_c = _mm(vct_ref[...], p.astype(BF16))
    psum = []
    for g in range(NSA_KV_HEADS):
        acc = p[:, g * NSA_GROUP * tq:(g * NSA_GROUP + 1) * tq]
        for hh in range(1, NSA_GROUP):
            acc = acc + p[:, (g * NSA_GROUP + hh) * tq:(g * NSA_GROUP + hh + 1) * tq]
        psum.append(acc)
    ovt = ovt_ref[...]
    imp = sum(_mm(ovt, piece) for piece in _split3(jnp.concatenate(psum, axis=1)))
    qpos2 = start + (_iota((1, NSA_KV_HEADS * tq), 1) & (tq - 1))
    neg = _top_blocks_cols(imp, qpos2, nsp)
    for h in range(NSA_HEADS):
        g = h // NSA_GROUP
        neg_ref[:, h * tq:(h + 1) * tq] = neg[:, g * tq:(g + 1) * tq]

    m_ref[...] = jnp.full(m_ref.shape, NEG_INF, F32)
    acc_ref[...] = jnp.zeros(acc_ref.shape, F32)
    bpt = tk // SEL_BLOCK

    qta_ref[...] = qta

    def scores(t, buf):
        off = pl.multiple_of(t * tk, tk)
        buf[...] = _mm(ks_ref[pl.ds(off, tk), :], qta_ref[...])

    def tile(t, buf, causal):
        off = pl.multiple_of(t * tk, tk)
        parts = []
        for j in range(bpt):
            sj = buf[j * SEL_BLOCK:(j + 1) * SEL_BLOCK, :]
            neg = neg_ref[pl.ds(t * bpt + j, 1), :]
            if causal:
                sj = jnp.where(off + j * SEL_BLOCK + _iota((SEL_BLOCK, 1), 0) <= qpos, sj + neg, NEG_INF)
                parts.append(sj.astype(BF16))
            else:
                parts.append(sj.astype(BF16) + neg.astype(BF16))
        s = jnp.concatenate(parts, axis=0)
        m_old = m_ref[...]
        m_new = jnp.maximum(m_old, jnp.max(s, axis=0, keepdims=True).astype(F32))
        m_use = jnp.where(m_new == NEG_INF, 0.0, m_new)
        alpha = jnp.exp(m_old - m_use)
        e = jnp.exp(s - m_use.astype(BF16))
        acc_ref[...] = alpha * acc_ref[...] + _mm(vst_ref[:, pl.ds(off, tk)], e)
        m_ref[...] = m_new

    n_full = start // tk
    pairs = n_full // 2
    scores(0, sa_ref)

    def body(u, carry):
        scores(2 * u + 1, sb_ref)
        tile(2 * u, sa_ref, False)
        scores(2 * u + 2, sa_ref)
        tile(2 * u + 1, sb_ref, False)
        return carry

    lax.fori_loop(0, pairs, body, 0)

    @pl.when(n_full % 2 == 1)
    def _():
        scores(n_full, sb_ref)
        tile(n_full - 1, sa_ref, False)
        tile(n_full, sb_ref, True)

    @pl.when(n_full % 2 == 0)
    def _():
        tile(n_full, sa_ref, True)
    o_s = acc_ref[:NSA_KV, :] * (1.0 / jnp.maximum(acc_ref[NSA_KV:NSA_KV + 1, :], 1e-30))

    wl = tq + WINDOW
    ws = pl.multiple_of(jnp.maximum(start - WINDOW, 0), tq)
    dist = qpos - (ws + _iota((wl, 1), 0))
    pw = _softmax_cols(_mm(kw_ref[pl.ds(ws, wl), :], qta), (dist >= 0) & (dist <= WINDOW))
    o_w = _mm(vwt_ref[:, pl.ds(ws, wl)], pw.astype(BF16))

    gt = jnp.transpose(_sigmoid(sm_ref[...]))
    gate = [jnp.concatenate([gt[SM_G + 3 * h + c:SM_G + 3 * h + c + 1, :] for h in range(NSA_HEADS)], axis=1)
            for c in range(3)]
    o = gate[0] * o_c + gate[1] * o_s + gate[2] * o_w
    yt = jnp.concatenate([o[(h // NSA_GROUP) * NSA_HD:(h // NSA_GROUP + 1) * NSA_HD, h * tq:(h + 1) * tq]
                          for h in range(NSA_HEADS)], axis=0)
    y_ref[...] = jnp.transpose(yt)


def _nsa_prompt(q8, sm, kca, vct, ksa, vst, kwa, vwt, b, l):
    tq = 128
    tk = 256
    assert l % tk == 0 and l >= tq + WINDOW
    n_ch = l // CMP_STRIDE
    ns = -(-l // SEL_BLOCK)
    nsp = -(-ns // LANES) * LANES
    nq = l // tq
    w = 2 * NSA_KV
    ovt = jnp.transpose(_overlap(n_ch, nsp)).astype(BF16)
    row = lambda bi, i: (bi * nq + i, 0)
    seq_rows = lambda bi, i: (bi, 0)
    seq_cols = lambda bi, i: (0, bi)
    return pl.pallas_call(
        functools.partial(_nsa_prompt_kernel, tq=tq, tk=tk),
        grid=(b, nq),
        in_specs=[pl.BlockSpec((tq, NSA_HEADS * LANES), row),
                  pl.BlockSpec((tq, LANES), row),
                  pl.BlockSpec((n_ch, w), seq_rows),
                  pl.BlockSpec((NSA_KV, n_ch), seq_cols),
                  pl.BlockSpec((l, w), seq_rows),
                  pl.BlockSpec((VS_ROWS, l), seq_cols),
                  pl.BlockSpec((l, w), seq_rows),
                  pl.BlockSpec((NSA_KV, l), seq_cols),
                  pl.BlockSpec((nsp, n_ch), lambda bi, i: (0, 0))],
        out_specs=pl.BlockSpec((tq, NSA_Q), row),
        out_shape=jax.ShapeDtypeStruct((b * l, NSA_Q), F32),
        scratch_shapes=[pltpu.VMEM((nsp, NSA_HEADS * tq), F32),
                        pltpu.VMEM((1, NSA_HEADS * tq), F32),
                        pltpu.VMEM((VS_ROWS, NSA_HEADS * tq), F32),
                        pltpu.VMEM((2 * NSA_KV, NSA_HEADS * tq), BF16),
                        pltpu.VMEM((tk, NSA_HEADS * tq), F32),
                        pltpu.VMEM((tk, NSA_HEADS * tq), F32)],
        compiler_params=_cparams(("parallel", "arbitrary")),
        name="nsa_prompt",
    )(q8, sm, kca, vct, ksa, vst, kwa, vwt, ovt)


def _nsa_sample_kernel(*refs, pg, page, tq, past_len):
    pt_ref = refs[0]
    del pt_ref
    page_refs = refs[1:1 + pg]
    (q8_ref, sm_ref, kc_ref, vc_ref, knew_ref, kw_ref, ov_ref, y_ref,
     m_ref, l_ref, acc_ref, sel_ref, oc_ref) = refs[1 + pg:]
    j = pl.program_id(1)
    nsp = ov_ref.shape[1]
    r = NSA_HEADS * tq
    q8 = jnp.concatenate([q8_ref[:, h * LANES:(h + 1) * LANES].astype(F32) for h in range(NSA_HEADS)],
                         axis=0).astype(BF16)
    qpos = past_len + (_iota((r, 1), 0) & (tq - 1))
    slope = _row_slopes(tq)

    @pl.when(j == 0)
    def _():
        n_ch = kc_ref.shape[1]
        dist = qpos - (_iota((1, n_ch), 1) * CMP_STRIDE + (CMP_LEN - 1))
        o_c, p = _rows_attend(q8, kc_ref[...], vc_ref[...], slope, dist, dist >= 0)
        oc_ref[...] = o_c
        for g in range(NSA_KV_HEADS):
            psum = p[g * NSA_GROUP * tq:(g * NSA_GROUP + 1) * tq]
            for hh in range(1, NSA_GROUP):
                psum = psum + p[(g * NSA_GROUP + hh) * tq:(g * NSA_GROUP + hh + 1) * tq]
            sel = _top_blocks(_mm(psum, ov_ref[...], HI), qpos[:tq], nsp)
            for hh in range(NSA_GROUP):
                sel_ref[(g * NSA_GROUP + hh) * tq:(g * NSA_GROUP + hh + 1) * tq, :] = sel
        m_ref[...] = jnp.full(m_ref.shape, NEG_INF, F32)
        l_ref[...] = jnp.zeros(l_ref.shape, F32)
        acc_ref[...] = jnp.zeros(acc_ref.shape, F32)

    sel_bf = sel_ref[...].astype(BF16)

    def sel_step(kv, kpos):
        n = kv.shape[1]
        expand = jnp.where(_iota((nsp, n), 0) == (kpos >> 6), 1.0, 0.0).astype(BF16)
        dist = qpos - kpos
        mask = (_mm(sel_bf, expand) > 0.5) & (dist >= 0)
        _rows_online(q8, kv[:NSA_KV].astype(BF16), kv[NSA_KV:].astype(BF16), slope, dist, mask,
                     m_ref, l_ref, acc_ref)

    sel_step(jnp.concatenate([page_refs[k][...] for k in range(pg)], axis=1),
             j * (pg * page) + _iota((1, pg * page), 1))

    @pl.when(j == pl.num_programs(1) - 1)
    def _():
        sel_step(knew_ref[...], past_len + _iota((1, knew_ref.shape[1]), 1))
        o_s = acc_ref[...] / jnp.maximum(l_ref[...], 1e-30)
        kw = kw_ref[...]
        kwpos = past_len - WINDOW + _iota((1, kw.shape[1]), 1)
        dist = qpos - kwpos
        mask = (dist >= 0) & (dist <= WINDOW) & (kwpos >= 0) & (kwpos < past_len + tq)
        o_w, _ = _rows_attend(q8, kw[:NSA_KV].astype(BF16), kw[NSA_KV:].astype(BF16), slope, dist, mask)
        y_ref[...] = _merge_heads(oc_ref[...], o_s, o_w, _sigmoid(sm_ref[...]), tq)


def _nsa_sample(q8, sm, kct, vct, knew_t, win_t, cache_t, page_table, b, l):
    n_pages = page_table.shape[1]
    page = cache_t.shape[2]
    past_len = n_pages * page
    assert l <= SEL_BLOCK and past_len % SEL_BLOCK == 0 and past_len >= WINDOW
    pg = 16
    steps = n_pages // pg
    n_ch = past_len // CMP_STRIDE
    ns = -(-(past_len + l) // SEL_BLOCK)
    nsp = -(-ns // LANES) * LANES
    w = 2 * NSA_KV
    ov = _overlap(n_ch, nsp)
    wcols = win_t.shape[2]

    def page_map(k):
        return lambda bi, j, pt: (pt[bi, j * pg + k], 1, 0)

    per_b = lambda bi, j, pt: (bi, 0)
    per_b3 = lambda bi, j, pt: (bi, 0, 0)
    grid_spec = pltpu.PrefetchScalarGridSpec(
        num_scalar_prefetch=1,
        grid=(b, steps),
        in_specs=[pl.BlockSpec((None, w, page), page_map(k)) for k in range(pg)]
        + [pl.BlockSpec((None, l, NSA_HEADS * LANES), per_b3),
           pl.BlockSpec((l, LANES), per_b),
           pl.BlockSpec((None, NSA_KV, n_ch), per_b3),
           pl.BlockSpec((None, NSA_KV, n_ch), per_b3),
           pl.BlockSpec((None, w, LANES), per_b3),
           pl.BlockSpec((None, w, wcols), per_b3),
           pl.BlockSpec((n_ch, nsp), lambda bi, j, pt: (0, 0))],
        out_specs=pl.BlockSpec((l, NSA_Q), per_b),
        scratch_shapes=[pltpu.VMEM((NSA_HEADS * l, 1), F32),
                        pltpu.VMEM((NSA_HEADS * l, 1), F32),
                        pltpu.VMEM((NSA_HEADS * l, NSA_KV), F32),
                        pltpu.VMEM((NSA_HEADS * l, nsp), F32),
                        pltpu.VMEM((NSA_HEADS * l, NSA_KV), F32)],
    )
    return pl.pallas_call(
        functools.partial(_nsa_sample_kernel, pg=pg, page=page, tq=l, past_len=past_len),
        grid_spec=grid_spec,
        out_shape=jax.ShapeDtypeStruct((b * l, NSA_Q), F32),
        compiler_params=_cparams(("parallel", "arbitrary")),
        name="nsa_sample",
    )(page_table, *([cache_t] * pg), q8.reshape(b, l, NSA_HEADS * LANES), sm, kct, vct, knew_t, win_t, ov)


def _tail_kernel(x_ref, ya_ref, yb_ref, ga_ref, gb_ref, wa_ref, wb_ref, wo_ref, n2_ref, wr_ref, br_ref,
                 h_ref, hn_ref, cmb_ref):
    ma = _mm(ya_ref[...].astype(BF16), wa_ref[...])
    mb = _mm(yb_ref[...].astype(BF16), wb_ref[...])
    m = _sigmoid(ga_ref[...]) * ma + _sigmoid(gb_ref[...]) * mb
    h = x_ref[...] + _mm(m.astype(BF16), wo_ref[...])
    h_ref[...] = h
    hn = h * lax.rsqrt(jnp.mean(h * h, axis=-1, keepdims=True) + RMS_EPS) * n2_ref[...]
    hn_hi = hn.astype(BF16)
    hn_ref[...] = hn_hi

    hn_lo = (hn - hn_hi.astype(F32)).astype(BF16)
    logit = _mm(hn_hi, wr_ref[0]) + _mm(hn_lo, wr_ref[0]) + _mm(hn_hi, wr_ref[1]) + br_ref[...]
    lane = _iota(logit.shape, 1)
    is_grp = (lane >= N_EXPERTS) & (lane < N_EXPERTS + N_GROUPS)
    gl = jnp.where(is_grp, logit, NEG_INF)
    gmax = jnp.max(gl, axis=-1, keepdims=True)
    gidx = jnp.min(jnp.where(gl == gmax, lane, LANES), axis=-1, keepdims=True) - N_EXPERTS
    p_grp = 1.0 / jnp.sum(jnp.exp(gl - gmax), axis=-1, keepdims=True)
    el = jnp.where((lane >> 3) == gidx, logit, NEG_INF)
    t1 = jnp.max(el, axis=-1, keepdims=True)
    i1 = jnp.min(jnp.where(el == t1, lane, LANES), axis=-1, keepdims=True)
    el2 = jnp.where(lane == i1, NEG_INF, el)
    t2 = jnp.max(el2, axis=-1, keepdims=True)
    i2 = jnp.min(jnp.where(el2 == t2, lane, LANES), axis=-1, keepdims=True)
    e2 = jnp.exp(t2 - t1)
    w1 = p_grp / (1.0 + e2)
    w2 = p_grp * e2 / (1.0 + e2)
    cmb_ref[...] = jnp.where(lane == i1, w1, jnp.where(lane == i2, w2,
                                                        jnp.where(lane == gidx + N_EXPERTS, 1.0, 0.0)))


def _tail(x2d, ya, yb, ga, gb, wa, wb, wo, norm2_g, w_route, b_route):
    n = x2d.shape[0]
    tm = min(ROW_TILE, n)
    row = lambda i: (i, 0)
    fixed = lambda i: (0, 0)
    return pl.pallas_call(
        _tail_kernel,
        grid=(n // tm,),
        in_specs=[pl.BlockSpec((tm, D_MODEL), row),
                  pl.BlockSpec((tm, GDN_V), row),
                  pl.BlockSpec((tm, NSA_Q), row),
                  pl.BlockSpec((tm, D_MODEL), row),
                  pl.BlockSpec((tm, D_MODEL), row),
                  pl.BlockSpec((GDN_V, D_MODEL), fixed),
                  pl.BlockSpec((NSA_Q, D_MODEL), fixed),
                  pl.BlockSpec((D_MODEL, D_MODEL), fixed),
                  pl.BlockSpec((1, D_MODEL), fixed),
                  pl.BlockSpec((2, D_MODEL, LANES), lambda i: (0, 0, 0)),
                  pl.BlockSpec((1, LANES), fixed)],
        out_specs=[pl.BlockSpec((tm, D_MODEL), row),
                   pl.BlockSpec((tm, D_MODEL), row),
                   pl.BlockSpec((tm, LANES), row)],
        out_shape=[jax.ShapeDtypeStruct((n, D_MODEL), F32),
                   jax.ShapeDtypeStruct((n, D_MODEL), BF16),
                   jax.ShapeDtypeStruct((n, LANES), F32)],
        compiler_params=_cparams(("parallel",)),
        name="tail",
    )(x2d, ya, yb, ga, gb, wa, wb, wo, norm2_g.reshape(1, D_MODEL), w_route, b_route)


MOE_SLABS = 2 * N_GROUPS
MOE_SLAB_EXPERTS = N_EXPERTS // MOE_SLABS


def _moe_kernel(offs_ref, cnts_ref, h_ref, hn_ref, cmb_ref, wg_ref, wu_ref, wd_ref, nf_ref, y_ref,
                xs_ref, cs_ref, pt_ref, acc_ref, *, rb):
    i = pl.program_id(0)
    sl = pl.program_id(1)
    t = hn_ref.shape[0]
    lane = _iota((t, LANES), 1)

    @pl.when(sl == 0)
    def _():
        cmb = cmb_ref[...]
        oh = jnp.where((lane >= N_EXPERTS) & (lane < N_EXPERTS + N_GROUPS), cmb, 0.0)
        oh_bf = oh.astype(BF16)
        r_i = _iota((t, t), 0)
        c_i = _iota((t, t), 1)
        lt = _iota((LANES, LANES), 0) < _iota((LANES, LANES), 1)
        before = _mm(_ones_where(c_i < r_i), oh_bf)
        totals = jnp.broadcast_to(jnp.sum(oh, axis=0, keepdims=True), (SUBLANES, LANES))
        smaller = _mm(totals, lt.astype(F32), HI)[0:1]
        rank_col = jnp.sum((before + smaller) * oh, axis=1, keepdims=True)
        eye = (_iota((LANES, LANES), 0) == _iota((LANES, LANES), 1)).astype(BF16)
        oht = _nt(eye, oh_bf)
        before_t = _mm(oht.astype(BF16), _ones_where(r_i < c_i))
        totals_t = jnp.broadcast_to(jnp.sum(oht, axis=1, keepdims=True), (LANES, LANES))
        gt = _iota((LANES, LANES), 0) > _iota((LANES, LANES), 1)
        smaller_t = _mm(gt.astype(F32), totals_t, HI)[:, 0:1]
        rank_row = jnp.sum((before_t + smaller_t) * oht, axis=0, keepdims=True)
        perm = _ones_where(r_i == rank_row.astype(jnp.int32))
        pt_ref[...] = _ones_where(c_i == rank_col.astype(jnp.int32))
        xs_ref[...] = _mm(perm, hn_ref[...]).astype(BF16)
        c_hi = cmb.astype(BF16)
        c_lo = (cmb - c_hi.astype(F32)).astype(BF16)
        cs_ref[...] = _mm(perm, c_hi) + _mm(perm, c_lo)
        acc_ref[...] = jnp.zeros(acc_ref.shape, F32)

    g = sl // (MOE_SLABS // N_GROUPS)
    off = offs_ref[i, g]
    row0 = (off // (2 * SUBLANES)) * (2 * SUBLANES)
    nblk = (off + cnts_ref[i, g] - row0 + rb - 1) // rb
    lane_b = _iota((rb, LANES), 1)

    def block(j, carry):
        lo = row0 + j * rb
        r0 = pl.multiple_of(jnp.minimum(lo, t - rb), 2 * SUBLANES)
        x = xs_ref[pl.ds(r0, rb), :]
        cw = jnp.where(r0 + _iota((rb, 1), 0) >= lo, cs_ref[pl.ds(r0, rb), :], 0.0)
        parts = []
        for e in range(MOE_SLAB_EXPERTS):
            wgt = jnp.sum(jnp.where(lane_b == sl * MOE_SLAB_EXPERTS + e, cw, 0.0), axis=1, keepdims=True)
            parts.append((_silu(_mm(x, wg_ref[e])) * _mm(x, wu_ref[e]) * wgt).astype(BF16))
        acc_ref[pl.ds(r0, rb), :] += _mm(jnp.concatenate(parts, axis=1), wd_ref[0])
        return carry

    lax.fori_loop(0, nblk, block, 0)

    @pl.when(sl == pl.num_programs(1) - 1)
    def _():
        acc = acc_ref[...]
        a_hi = acc.astype(BF16)
        a_lo = (acc - a_hi.astype(F32)).astype(BF16)
        pt = pt_ref[...]
        v = h_ref[...] + _mm(pt, a_hi) + _mm(pt, a_lo)
        y_ref[...] = v * lax.rsqrt(jnp.mean(v * v, axis=-1, keepdims=True) + RMS_EPS) * nf_ref[...]


def _moe_weights(w_gate, w_up, w_down):
    return (w_gate.astype(BF16), w_up.astype(BF16),
            w_down.astype(BF16).reshape(MOE_SLABS, MOE_SLAB_EXPERTS * D_EXPERT, D_MODEL))


def _moe(h, hn, cmb, wg, wu, wd, norm_f_g):
    n = h.shape[0]
    t = min(1024, n)
    rb = min(256, t)
    tiles = n // t
    cnts = jnp.sum(cmb[:, N_EXPERTS:N_EXPERTS + N_GROUPS].reshape(tiles, t, N_GROUPS), axis=1).astype(jnp.int32)
    offs = jnp.cumsum(cnts, axis=1) - cnts
    ws = MOE_SLAB_EXPERTS * D_EXPERT
    row = lambda i, s, o, c: (i, 0)
    slab = lambda i, s, o, c: (s, 0, 0)
    grid_spec = pltpu.PrefetchScalarGridSpec(
        num_scalar_prefetch=2,
        grid=(tiles, MOE_SLABS),
        in_specs=[pl.BlockSpec((t, D_MODEL), row),
                  pl.BlockSpec((t, D_MODEL), row),
                  pl.BlockSpec((t, LANES), row),
                  pl.BlockSpec((MOE_SLAB_EXPERTS, D_MODEL, D_EXPERT), slab),
                  pl.BlockSpec((MOE_SLAB_EXPERTS, D_MODEL, D_EXPERT), slab),
                  pl.BlockSpec((1, ws, D_MODEL), slab),
                  pl.BlockSpec((1, D_MODEL), lambda i, s, o, c: (0, 0))],
        out_specs=pl.BlockSpec((t, D_MODEL), row),
        scratch_shapes=[pltpu.VMEM((t, D_MODEL), BF16),
                        pltpu.VMEM((t, LANES), F32),
                        pltpu.VMEM((t, t), BF16),
                        pltpu.VMEM((t, D_MODEL), F32)],
    )
    return pl.pallas_call(
        functools.partial(_moe_kernel, rb=rb),
        grid_spec=grid_spec,
        out_shape=jax.ShapeDtypeStruct((n, D_MODEL), F32),
        compiler_params=_cparams(("parallel", "arbitrary")),
        name="moe",
    )(offs, cnts, h, hn, cmb, wg, wu, wd, norm_f_g.reshape(1, D_MODEL))


def _route_weights(w_grp, b_grp, w_rt, b_rt):
    pad = LANES - N_EXPERTS - N_GROUPS
    w = jnp.concatenate([w_rt, w_grp, jnp.zeros((D_MODEL, pad), F32)], axis=1)
    bias = jnp.concatenate([b_rt, b_grp, jnp.zeros((pad,), F32)]).reshape(1, LANES)
    hi = w.astype(BF16)
    return jnp.stack([hi, (w - hi.astype(F32)).astype(BF16)]), bias


def _finish(x2d, ya, yb, ga, gb, wts):
    h, hn, cmb = _tail(x2d, ya, yb, ga, gb, wts["wa"], wts["wb"], wts["wo"], wts["norm2_g"],
                       wts["w_route"], wts["b_route"])
    return _moe(h, hn, cmb, wts["wg"], wts["wu"], wts["wd"], wts["norm_f_g"])


def _prompt_layer(x, wts):
    b, l, _ = x.shape
    x2d = x.reshape(b * l, D_MODEL)
    qkv, z, q8, ga, gb, sm, kv4t, kvwt, cmp_rows, ksa, kwa, vst, vwt = _project(
        x2d, wts["norm1_g"], wts["w_pk"], b, l)
    conv0 = jnp.zeros((b, GDN_CONV - 1, GDN_CONV_DIM), F32)
    s0 = jnp.zeros((b, GDN_HEADS, GDN_DK, GDN_DV), F32)
    ya, s_new, conv_new = _gdn(qkv, z, sm, conv0, s0, wts["conv_w"], wts["a_log"], wts["dt_bias"],
                               wts["gdn_norm_g"], b, l)
    head, tail = _pool_rows(cmp_rows, wts["pw"])
    kca, vct = _compress(head, tail, wts["cmp_wk"], wts["cmp_wv"], b)
    yb = _nsa_prompt(q8, sm, kca, vct, ksa, vst, kwa, vwt, b, l)
    y = _finish(x2d, ya, yb, ga, gb, wts)
    win_buf = min(WINDOW, l)
    kv_new = jnp.transpose(kv4t.reshape(b, 4, NSA_KV_HEADS, NSA_HD, l), (0, 4, 1, 2, 3))
    win_new = jnp.transpose(kvwt[:, :, l - win_buf:].reshape(b, 2, NSA_KV_HEADS, NSA_HD, win_buf), (0, 4, 1, 2, 3))
    return (y.reshape(b, l, D_MODEL), kv_new, win_new, s_new, conv_new)


def _sample_layer(x, cache_kv_l, page_table, cache_win_l, s0, conv_buf, wts):
    b, l, _ = x.shape
    x2d = x.reshape(b * l, D_MODEL)
    qkv, z, q8, ga, gb, sm, kv4, kvw = _project(x2d, wts["norm1_g"], wts["w_pk"], b, l)
    ya, s_new, conv_new = _gdn(qkv, z, sm, conv_buf, s0, wts["conv_w"], wts["a_log"], wts["dt_bias"],
                               wts["gdn_norm_g"], b, l)
    n_phys, page = cache_kv_l.shape[:2]
    w = 2 * NSA_KV
    cache_t = jnp.transpose(cache_kv_l, (0, 2, 3, 4, 1)).reshape(n_phys, 2 * w, page)
    win_buf = cache_win_l.shape[1]
    win_old_t = jnp.transpose(cache_win_l, (0, 2, 3, 4, 1)).reshape(b, w, win_buf)
    head_t, tail_t = _pool_pages(cache_t, page_table, wts["pos_wk"], wts["pos_wv"])
    kct, vct = _compress_t(head_t, tail_t, wts["cmp_wk"], wts["cmp_wv"])
    new_t = jnp.transpose(kv4.reshape(b, l, 2 * w), (0, 2, 1))
    knew_t = jnp.pad(new_t[:, w:], ((0, 0), (0, 0), (0, LANES - l)))
    win_new_t = jnp.transpose(kvw.reshape(b, l, w), (0, 2, 1))
    wcols = -(-(win_buf + l) // LANES) * LANES
    win_t = jnp.concatenate([win_old_t, win_new_t, jnp.zeros((b, w, wcols - win_buf - l), F32)], axis=2)
    yb = _nsa_sample(q8, sm, kct, vct, knew_t, win_t, cache_t, page_table, b, l)
    y = _finish(x2d, ya, yb, ga, gb, wts)
    win_new = jnp.transpose(win_t[:, :, l:l + win_buf].reshape(b, 2, NSA_KV_HEADS, NSA_HD, win_buf), (0, 4, 1, 2, 3))
    return (y.reshape(b, l, D_MODEL), kv4.reshape(b, l, 4, NSA_KV_HEADS, NSA_HD), win_new, s_new, conv_new)


def kernel(x_prompt, x_sample, cache_kv, page_table, cache_win, state_gdn, state_conv, norm1_g, w_in, gdn_conv_w, gdn_a_log, gdn_dt_bias, gdn_norm_g, cmp_pos_wk, cmp_pos_wv, cmp_wk, cmp_wv, w_branch_a, w_branch_b, w_out, norm2_g, w_grp, b_grp, w_rt, b_rt, w_e_gate, w_e_up, w_e_down, norm_f_g):
    assert w_in.shape[0] == 1, "single layer"
    w_route, b_route = _route_weights(w_grp[0], b_grp[0], w_rt[0], b_rt[0])
    wg, wu, wd = _moe_weights(w_e_gate[0], w_e_up[0], w_e_down[0])
    wts = dict(
        norm1_g=norm1_g[0], w_pk=_pack_w_in(w_in[0]),
        conv_w=gdn_conv_w[0], a_log=gdn_a_log[0], dt_bias=gdn_dt_bias[0], gdn_norm_g=gdn_norm_g[0],
        pw=_pos_weights(cmp_pos_wk[0], cmp_pos_wv[0]), pos_wk=cmp_pos_wk[0], pos_wv=cmp_pos_wv[0],
        cmp_wk=cmp_wk[0], cmp_wv=cmp_wv[0],
        wa=w_branch_a[0].astype(BF16), wb=w_branch_b[0].astype(BF16), wo=w_out[0].astype(BF16),
        norm2_g=norm2_g[0], w_route=w_route, b_route=b_route,
        wg=wg, wu=wu, wd=wd, norm_f_g=norm_f_g,
    )
    yp, kvp, winp, sp, cp = _prompt_layer(x_prompt, wts)
    ys, kvs, wins, ss, cs = _sample_layer(x_sample, cache_kv[0], page_table, cache_win[0], state_gdn[0],
                                          state_conv[0], wts)
    return (yp, ys, kvp[None], kvs[None], winp[None], wins[None], sp[None], ss[None], cp[None], cs[None])
```

```python
import functools
import math

import numpy as np
import jax
import jax.numpy as jnp
from jax import lax
from jax.experimental import pallas as pl
from jax.experimental.pallas import tpu as pltpu

F32 = jnp.float32
BF16 = jnp.bfloat16
HI = lax.Precision.HIGHEST

LANES = 128
SUBLANES = 8
VMEM_LIMIT = 56 * 1024 * 1024
ROW_TILE = 512

D_MODEL = 1024
GDN_HEADS = 4
GDN_DK = 128
GDN_DV = 128
GDN_QK = GDN_HEADS * GDN_DK
GDN_V = GDN_HEADS * GDN_DV
GDN_CONV_DIM = 2 * GDN_QK + GDN_V
GDN_CONV = 4
GDN_CHUNK = 64
GDN_SEQS_PER_STEP = 4
NSA_HEADS = 8
NSA_KV_HEADS = 2
NSA_GROUP = NSA_HEADS // NSA_KV_HEADS
NSA_HD = 64
NSA_Q = NSA_HEADS * NSA_HD
NSA_KV = NSA_KV_HEADS * NSA_HD
CMP_STRIDE = 16
CMP_LEN = 2 * CMP_STRIDE
SEL_BLOCK = 64
SEL_TOP = 16
WINDOW = 512
FORCE_SCORE = 1.0e4
N_GROUPS = 4
EXPERTS_PER_GROUP = 8
N_EXPERTS = N_GROUPS * EXPERTS_PER_GROUP
D_EXPERT = 256
RMS_EPS = 1e-6
VS_ROWS = NSA_KV + 2 * SUBLANES
NEG_INF = float("-inf")

_OFF_QKV = 0
_OFF_Z = _OFF_QKV + GDN_CONV_DIM
_OFF_B = _OFF_Z + GDN_V
_OFF_A = _OFF_B + GDN_HEADS
_OFF_Q = _OFF_A + GDN_HEADS
_OFF_KV = _OFF_Q + NSA_Q
_OFF_G = _OFF_KV + 6 * NSA_KV
_OFF_GA = _OFF_G + 3 * NSA_HEADS
_OFF_GB = _OFF_GA + D_MODEL

SM_B = 0
SM_A = 4
SM_G = 8

_PK = {}
_c = 0
for _n, _w in (("qkv", GDN_CONV_DIM), ("z", GDN_V), ("q8", NSA_HEADS * LANES), ("kv4", 4 * NSA_KV),
               ("kvw", 2 * NSA_KV), ("ga", D_MODEL), ("gb", D_MODEL), ("sm", LANES)):
    _PK[_n] = (_c, _w)
    _c += _w
PK_DIM = _c


def _cparams(sem):
    return pltpu.CompilerParams(dimension_semantics=sem, vmem_limit_bytes=VMEM_LIMIT)


def _nt(a, b, precision=None):
    return lax.dot_general(a, b, (((1,), (1,)), ((), ())), preferred_element_type=F32, precision=precision)


def _tn(a, b, precision=None):
    return lax.dot_general(a, b, (((0,), (0,)), ((), ())), preferred_element_type=F32, precision=precision)


def _mm(a, b, precision=None):
    return jnp.dot(a, b, preferred_element_type=F32, precision=precision)


def _mmb(a, b):
    return _mm(a.astype(BF16), b.astype(BF16))


def _sigmoid(x):
    return 1.0 / (1.0 + jnp.exp(-x))


def _silu(x):
    return x * _sigmoid(x)


def _iota(shape, dim):
    return lax.broadcasted_iota(jnp.int32, shape, dim)


def _ones_where(cond):
    return jnp.where(cond, 1.0, 0.0).astype(BF16)


def _pack_w_in(w_in):
    q = w_in[:, _OFF_Q:_OFF_Q + NSA_Q]
    zeros64 = jnp.zeros((D_MODEL, NSA_HD), w_in.dtype)
    q8 = []
    for h in range(NSA_HEADS):
        qh = q[:, h * NSA_HD:(h + 1) * NSA_HD]
        q8.append(jnp.concatenate([qh, zeros64] if h < NSA_GROUP else [zeros64, qh], axis=1))
    sm = jnp.concatenate([w_in[:, _OFF_B:_OFF_B + GDN_HEADS], w_in[:, _OFF_A:_OFF_A + GDN_HEADS],
                          w_in[:, _OFF_G:_OFF_G + 3 * NSA_HEADS],
                          jnp.zeros((D_MODEL, LANES - 2 * GDN_HEADS - 3 * NSA_HEADS), w_in.dtype)], axis=1)
    cols = [w_in[:, _OFF_QKV:_OFF_QKV + GDN_CONV_DIM], w_in[:, _OFF_Z:_OFF_Z + GDN_V]] + q8 + [
        w_in[:, _OFF_KV:_OFF_KV + 4 * NSA_KV], w_in[:, _OFF_KV + 4 * NSA_KV:_OFF_KV + 6 * NSA_KV],
        w_in[:, _OFF_GA:_OFF_GA + D_MODEL], w_in[:, _OFF_GB:_OFF_GB + D_MODEL], sm]
    return jnp.concatenate(cols, axis=1).astype(BF16)


def _pos_features(pos, shape):
    lane = _iota(shape, 1)
    feat = jnp.where(lane == 0, (pos >> 7) * LANES, jnp.where(lane == 1, pos & (LANES - 1),
                                                               jnp.where(lane < 4, 1, 0)))
    return feat.astype(F32).astype(BF16)


def _proj_kernel(x_ref, g_ref, w_ref, qkv_ref, z_ref, q8_ref, ga_ref, gb_ref, sm_ref, *kv_refs, tiles_per_seq):
    tm = x_ref.shape[0]
    x = x_ref[...]
    xn = (x * lax.rsqrt(jnp.mean(x * x, axis=-1, keepdims=True) + RMS_EPS) * g_ref[...]).astype(BF16)

    def seg(name):
        a, w = _PK[name]
        return _mm(xn, w_ref[:, a:a + w])

    qkv_ref[...] = seg("qkv")
    z_ref[...] = seg("z")
    q8_ref[...] = (seg("q8") * (NSA_HD ** -0.5)).astype(BF16)
    ga_ref[...] = seg("ga")
    gb_ref[...] = seg("gb")
    sm_ref[...] = seg("sm")
    kv4 = seg("kv4")
    kvw = seg("kvw")
    if tiles_per_seq == 0:
        kv4_ref, kvw_ref = kv_refs
        kv4_ref[...] = kv4
        kvw_ref[...] = kvw
        return
    kv4t_ref, kvwt_ref, cmp_ref, ksa_ref, kwa_ref, vst_ref, vwt_ref = kv_refs
    kv4t = jnp.transpose(kv4)
    kvwt = jnp.transpose(kvw)
    kv4t_ref[...] = kv4t
    kvwt_ref[...] = kvwt
    cmp_ref[...] = kv4[:, :2 * NSA_KV]
    pos = (pl.program_id(0) % tiles_per_seq) * tm + _iota((tm, 1), 0)
    feat = _pos_features(pos, (tm, LANES))
    ksa_ref[:, :NSA_KV] = kv4[:, 2 * NSA_KV:3 * NSA_KV].astype(BF16)
    ksa_ref[:, NSA_KV:] = feat
    kwa_ref[:, :NSA_KV] = kvw[:, :NSA_KV].astype(BF16)
    kwa_ref[:, NSA_KV:] = feat
    vst_ref[:NSA_KV, :] = kv4t[3 * NSA_KV:].astype(BF16)
    ones_row = _ones_where(_iota((VS_ROWS - NSA_KV, tm), 0) == 0)
    vst_ref[NSA_KV:, :] = ones_row
    vwt_ref[:NSA_KV, :] = kvwt[NSA_KV:].astype(BF16)
    vwt_ref[NSA_KV:, :] = ones_row


def _project(x2d, norm_g, w_pk, b, l):
    n = x2d.shape[0]
    tm = min(ROW_TILE, n)
    long_seq = l % tm == 0
    tps = l // tm if long_seq else 0
    w = 2 * NSA_KV
    row = lambda i: (i, 0)
    out_shape = [jax.ShapeDtypeStruct((n, _PK[k][1]), dt) for k, dt in
                 (("qkv", F32), ("z", F32), ("q8", BF16), ("ga", F32), ("gb", F32), ("sm", F32))]
    out_specs = [pl.BlockSpec((tm, s.shape[1]), row) for s in out_shape]
    if long_seq:
        seq_t = lambda i: (i // tps, 0, i % tps)
        out_shape += [jax.ShapeDtypeStruct((b, 2 * w, l), F32), jax.ShapeDtypeStruct((b, w, l), F32),
                      jax.ShapeDtypeStruct((n, w), F32), jax.ShapeDtypeStruct((n, w), BF16),
                      jax.ShapeDtypeStruct((n, w), BF16), jax.ShapeDtypeStruct((VS_ROWS, n), BF16),
                      jax.ShapeDtypeStruct((VS_ROWS, n), BF16)]
        out_specs += [pl.BlockSpec((None, 2 * w, tm), seq_t), pl.BlockSpec((None, w, tm), seq_t),
                      pl.BlockSpec((tm, w), row), pl.BlockSpec((tm, w), row), pl.BlockSpec((tm, w), row),
                      pl.BlockSpec((VS_ROWS, tm), lambda i: (0, i)), pl.BlockSpec((VS_ROWS, tm), lambda i: (0, i))]
    else:
        out_shape += [jax.ShapeDtypeStruct((n, 2 * w), F32), jax.ShapeDtypeStruct((n, w), F32)]
        out_specs += [pl.BlockSpec((tm, 2 * w), row), pl.BlockSpec((tm, w), row)]
    return pl.pallas_call(
        functools.partial(_proj_kernel, tiles_per_seq=tps),
        grid=(n // tm,),
        in_specs=[pl.BlockSpec((tm, D_MODEL), row),
                  pl.BlockSpec((1, D_MODEL), lambda i: (0, 0)),
                  pl.BlockSpec((D_MODEL, PK_DIM), lambda i: (0, 0), pipeline_mode=pl.Buffered(1))],
        out_specs=out_specs,
        out_shape=out_shape,
        compiler_params=_cparams(("parallel",)),
        name="proj",
    )(x2d, norm_g.reshape(1, D_MODEL), w_pk)


def _unit_lower_inverses(a_list, c):
    r = _iota((c, c), 0)
    col = _iota((c, c), 1)
    eye = (r == col).astype(F32)
    n1 = [jnp.where((r >> 3) == (col >> 3), -a, 0.0) for a in a_list]
    n2 = [_mmb(x, x) for x in n1]
    n4 = [_mmb(x, x) for x in n2]
    t = [_mmb(eye + x, eye + y) for x, y in zip(n1, n2)]
    t = [_mmb(x, eye + y) for x, y in zip(t, n4)]
    s = SUBLANES
    while s < c:
        sh = s.bit_length() - 1
        off = ((r >> (sh + 1)) == (col >> (sh + 1))) & ((r >> sh) != (col >> sh))
        ta = [_mmb(x, jnp.where(off, a, 0.0)) for x, a in zip(t, a_list)]
        t = [x - _mmb(y, x) for x, y in zip(t, ta)]
        s *= 2
    return t


def _gdn_kernel(qkv_ref, z_ref, sm_ref, cbuf_ref, s0_ref, cw_ref, alog_ref, dtb_ref, ng_ref,
                y_ref, snew_ref, cnew_ref, ext_ref, st_ref, *, chunk, nb):
    c = chunk
    ci = pl.program_id(1)

    @pl.when(ci == 0)
    def _():
        ext_ref[:, 0:SUBLANES, :] = cbuf_ref[...]
        st_ref[...] = s0_ref[...]

    r = _iota((c, c), 0)
    col = _iota((c, c), 1)
    tri_incl = r >= col
    tri_strict = r > col
    tri_f = tri_incl.astype(F32)
    pick = (_iota((SUBLANES, LANES), 1) == _iota((SUBLANES, LANES), 0) + SM_A).astype(F32)
    base = SUBLANES - (GDN_CONV - 1)
    ng = ng_ref[...]

    units = []
    for bb in range(nb):
        u = qkv_ref[bb]
        ext_ref[bb, SUBLANES:SUBLANES + c, :] = u
        conv = cw_ref[0:1, :] * ext_ref[bb, base:base + c, :]
        for i in range(1, GDN_CONV - 1):
            conv = conv + cw_ref[i:i + 1, :] * ext_ref[bb, base + i:base + i + c, :]
        conv = conv + cw_ref[GDN_CONV - 1:GDN_CONV, :] * u
        halo = ext_ref[bb, c:c + SUBLANES, :]
        ext_ref[bb, 0:SUBLANES, :] = halo
        cnew_ref[bb] = halo
        qkv = _silu(conv)

        sm = sm_ref[bb]
        beta_all = _sigmoid(sm)
        xa = sm + dtb_ref[...]
        softplus = jnp.maximum(xa, 0.0) + jnp.log(1.0 + jnp.exp(-jnp.abs(xa)))
        g_all = -jnp.exp(alog_ref[...]) * softplus
        gcum_all = _mm(tri_f, g_all, HI)
        gcum_rows = _nt(pick, gcum_all, HI)
        for h in range(GDN_HEADS):
            q = qkv[:, h * GDN_DK:(h + 1) * GDN_DK]
            k = qkv[:, GDN_QK + h * GDN_DK:GDN_QK + (h + 1) * GDN_DK]
            v = qkv[:, 2 * GDN_QK + h * GDN_DV:2 * GDN_QK + (h + 1) * GDN_DV]
            q = q * lax.rsqrt(jnp.sum(q * q, axis=-1, keepdims=True) + RMS_EPS) * (GDN_DK ** -0.5)
            k = k * lax.rsqrt(jnp.sum(k * k, axis=-1, keepdims=True) + RMS_EPS)
            beta = beta_all[:, SM_B + h:SM_B + h + 1]
            gc_col = gcum_all[:, SM_A + h:SM_A + h + 1]
            gc_row = gcum_rows[h:h + 1, :]
            gc_last = gcum_all[c - 1:c, SM_A + h:SM_A + h + 1]
            units.append(dict(bb=bb, h=h, q=q, k=k, v=v, beta=beta, gc_col=gc_col, gc_last=gc_last,
                              decay=jnp.exp(jnp.where(tri_incl, gc_col - gc_row, NEG_INF)),
                              eg=jnp.exp(gc_col), kb=k * beta, k_bf=k.astype(BF16)))

    a_list = [jnp.where(tri_strict, _nt(un["kb"].astype(BF16), un["k_bf"]) * un["decay"], 0.0) for un in units]
    t_list = _unit_lower_inverses(a_list, c)
    u_coef = [_mmb(t, un["v"] * un["beta"]) for t, un in zip(t_list, units)]
    w_coef = [_mmb(t, un["kb"] * un["eg"]) for t, un in zip(t_list, units)]
    qk = [(_nt(un["q"].astype(BF16), un["k_bf"]) * un["decay"]).astype(BF16) for un in units]
    s_old = [st_ref[un["bb"], un["h"]] for un in units]
    s_bf = [s.astype(BF16) for s in s_old]
    uu = [uc - _mm(wc.astype(BF16), s) for uc, wc, s in zip(u_coef, w_coef, s_bf)]
    uu_bf = [x.astype(BF16) for x in uu]
    o_list = [_mm((un["q"] * un["eg"]).astype(BF16), s) + _mm(a, x)
              for un, s, a, x in zip(units, s_bf, qk, uu_bf)]
    for un, s, x, o in zip(units, s_old, uu_bf, o_list):
        bb, h = un["bb"], un["h"]
        k_tail = un["k"] * jnp.exp(un["gc_last"] - un["gc_col"])
        st_ref[bb, h] = s * jnp.exp(un["gc_last"]) + _tn(k_tail.astype(BF16), x)
        on = o * lax.rsqrt(jnp.mean(o * o, axis=-1, keepdims=True) + RMS_EPS) * ng
        y_ref[bb, :, h * GDN_DV:(h + 1) * GDN_DV] = on * _silu(z_ref[bb, :, h * GDN_DV:(h + 1) * GDN_DV])

    @pl.when(ci == pl.num_programs(1) - 1)
    def _():
        snew_ref[...] = st_ref[...]


def _gdn(qkv, z, sm, conv_buf, s0, conv_w, a_log, dt_bias, norm_g, b, l):
    c = math.gcd(l, GDN_CHUNK)
    nc = l // c
    nb = math.gcd(b, GDN_SEQS_PER_STEP)
    cbuf8 = jnp.pad(conv_buf, ((0, 0), (SUBLANES - (GDN_CONV - 1), 0), (0, 0)))
    pad_a = (SM_A, LANES - SM_A - GDN_HEADS)
    alog_row = jnp.pad(a_log, pad_a).reshape(1, LANES)
    dtb_row = jnp.pad(dt_bias, pad_a).reshape(1, LANES)
    row = lambda bi, ci: (bi, ci, 0)
    per_seq3 = lambda bi, ci: (bi, 0, 0)
    per_seq4 = lambda bi, ci: (bi, 0, 0, 0)
    fixed = lambda bi, ci: (0, 0)
    y, s_new, c_new = pl.pallas_call(
        functools.partial(_gdn_kernel, chunk=c, nb=nb),
        grid=(b // nb, nc),
        in_specs=[pl.BlockSpec((nb, c, GDN_CONV_DIM), row),
                  pl.BlockSpec((nb, c, GDN_V), row),
                  pl.BlockSpec((nb, c, LANES), row),
                  pl.BlockSpec((nb, SUBLANES, GDN_CONV_DIM), per_seq3),
                  pl.BlockSpec((nb, GDN_HEADS, GDN_DK, GDN_DV), per_seq4),
                  pl.BlockSpec((GDN_CONV, GDN_CONV_DIM), fixed),
                  pl.BlockSpec((1, LANES), fixed),
                  pl.BlockSpec((1, LANES), fixed),
                  pl.BlockSpec((1, GDN_DV), fixed)],
        out_specs=[pl.BlockSpec((nb, c, GDN_V), row),
                   pl.BlockSpec((nb, GDN_HEADS, GDN_DK, GDN_DV), per_seq4),
                   pl.BlockSpec((nb, SUBLANES, GDN_CONV_DIM), per_seq3)],
        out_shape=[jax.ShapeDtypeStruct((b, l, GDN_V), F32),
                   jax.ShapeDtypeStruct((b, GDN_HEADS, GDN_DK, GDN_DV), F32),
                   jax.ShapeDtypeStruct((b, SUBLANES, GDN_CONV_DIM), F32)],
        scratch_shapes=[pltpu.VMEM((nb, c + SUBLANES, GDN_CONV_DIM), F32),
                        pltpu.VMEM((nb, GDN_HEADS, GDN_DK, GDN_DV), F32)],
        compiler_params=_cparams(("parallel", "arbitrary")),
        name="gdn",
    )(qkv.reshape(b, l, GDN_CONV_DIM), z.reshape(b, l, GDN_V), sm.reshape(b, l, LANES), cbuf8, s0, conv_w,
      alog_row, dtb_row, norm_g.reshape(1, GDN_DV))
    return y.reshape(b * l, GDN_V), s_new, c_new[:, SUBLANES - (GDN_CONV - 1):]


def _pool_kernel(*refs, n_in):
    refs = refs[len(refs) - 2 * n_in - 3:]
    x_refs, pw_ref, head_ref, tail_ref = refs[:2 * n_in], refs[2 * n_in], refs[2 * n_in + 1], refs[2 * n_in + 2]
    rows = x_refs[0].shape[0]
    n = rows // CMP_STRIDE
    for j, x_ref in enumerate(x_refs):
        half = slice((j % 2) * NSA_KV, (j % 2 + 1) * NSA_KV)
        head = None
        tail = None
        for p in range(CMP_STRIDE):
            xr = x_ref[pl.ds(p, n, stride=CMP_STRIDE), :]
            hp = xr * pw_ref[p:p + 1, half]
            tp = xr * pw_ref[CMP_STRIDE + p:CMP_STRIDE + p + 1, half]
            head = hp if head is None else head + hp
            tail = tp if tail is None else tail + tp
        head_ref[(j // 2) * n:(j // 2 + 1) * n, half] = head
        tail_ref[(j // 2) * n:(j // 2 + 1) * n, half] = tail


def _pos_weights(pos_wk, pos_wv):
    return jnp.concatenate([jnp.broadcast_to(pos_wk[:, None], (CMP_LEN, NSA_KV)),
                            jnp.broadcast_to(pos_wv[:, None], (CMP_LEN, NSA_KV))], axis=1).astype(F32)


def _pool_rows(kv4, pw):
    n = kv4.shape[0]
    r = min(2048, n)
    w = 2 * NSA_KV
    return pl.pallas_call(
        functools.partial(_pool_kernel, n_in=1),
        grid=(n // r,),
        in_specs=[pl.BlockSpec((r, NSA_KV), lambda i: (i, 0)),
                  pl.BlockSpec((r, NSA_KV), lambda i: (i, 1)),
                  pl.BlockSpec((CMP_LEN, w), lambda i: (0, 0))],
        out_specs=[pl.BlockSpec((r // CMP_STRIDE, w), lambda i: (i, 0))] * 2,
        out_shape=[jax.ShapeDtypeStruct((n // CMP_STRIDE, w), F32)] * 2,
        compiler_params=_cparams(("parallel",)),
        name="pool_rows",
    )(kv4, kv4, pw)


def _pool_pages_kernel(*refs, pg):
    page_refs = refs[1:1 + pg]
    pk_hi, pk_lo, pv_hi, pv_lo, head_ref, tail_ref = refs[1 + pg:]
    x = jnp.concatenate([r[...] for r in page_refs], axis=1)
    n = head_ref.shape[1]
    for half, (p_hi, p_lo) in enumerate(((pk_hi, pk_lo), (pv_hi, pv_lo))):
        xs = x[half * NSA_KV:(half + 1) * NSA_KV]
        hi = xs.astype(BF16)
        out = _mm(hi, p_hi[...]) + _mm(hi, p_lo[...])
        head_ref[half * NSA_KV:(half + 1) * NSA_KV, :] = out[:, :n]
        tail_ref[half * NSA_KV:(half + 1) * NSA_KV, :] = out[:, n:]


def _pool_matrix(pos_w, positions):
    chunks = positions // CMP_STRIDE
    p = jnp.arange(positions)[:, None]
    c = jnp.arange(chunks)[None, :]
    inside = (p // CMP_STRIDE) == c
    head = jnp.where(inside, pos_w[:CMP_STRIDE][p % CMP_STRIDE], 0.0)
    tail = jnp.where(inside, pos_w[CMP_STRIDE:][p % CMP_STRIDE], 0.0)
    m = jnp.concatenate([head, tail], axis=1).astype(F32)
    hi = m.astype(BF16)
    return hi, (m - hi.astype(F32)).astype(BF16)


def _pool_pages(cache_t, page_table, pos_wk, pos_wv):
    b, n_pages = page_table.shape
    page = cache_t.shape[2]
    pg = LANES * CMP_STRIDE // page
    w = 2 * NSA_KV
    steps = n_pages // pg
    n_ch = n_pages * page // CMP_STRIDE
    mats = _pool_matrix(pos_wk, pg * page) + _pool_matrix(pos_wv, pg * page)

    def page_map(k):
        return lambda bi, j, pt: (pt[bi, j * pg + k], 0, 0)

    fixed = lambda bi, j, pt: (0, 0)
    grid_spec = pltpu.PrefetchScalarGridSpec(
        num_scalar_prefetch=1,
        grid=(b, steps),
        in_specs=[pl.BlockSpec((None, w, page), page_map(k)) for k in range(pg)]
        + [pl.BlockSpec((pg * page, 2 * LANES), fixed)] * 4,
        out_specs=[pl.BlockSpec((None, w, LANES), lambda bi, j, pt: (bi, 0, j))] * 2,
    )
    return pl.pallas_call(
        functools.partial(_pool_pages_kernel, pg=pg),
        grid_spec=grid_spec,
        out_shape=[jax.ShapeDtypeStruct((b, w, n_ch), F32)] * 2,
        compiler_params=_cparams(("parallel", "arbitrary")),
        name="pool_pages",
    )(page_table, *([cache_t] * pg), *mats)


def _cmp_t_kernel(head_ref, tail_ref, wk_ref, wv_ref, kct_ref, vct_ref):
    n = head_ref.shape[1]
    blocks = head_ref[...] + pltpu.roll(tail_ref[...], n - 1, 1)
    kct_ref[...] = _mm(wk_ref[...], blocks[:NSA_KV].astype(BF16)).astype(BF16)
    vct_ref[...] = _mm(wv_ref[...], blocks[NSA_KV:].astype(BF16)).astype(BF16)


def _compress_t(head_t, tail_t, cmp_wk, cmp_wv):
    b, w, n_ch = head_t.shape
    seq = lambda i: (i, 0, 0)
    return pl.pallas_call(
        _cmp_t_kernel,
        grid=(b,),
        in_specs=[pl.BlockSpec((None, w, n_ch), seq)] * 2 + [pl.BlockSpec((NSA_KV, NSA_KV), lambda i: (0, 0))] * 2,
        out_specs=[pl.BlockSpec((None, NSA_KV, n_ch), seq)] * 2,
        out_shape=[jax.ShapeDtypeStruct((b, NSA_KV, n_ch), BF16)] * 2,
        compiler_params=_cparams(("parallel",)),
        name="compress_t",
    )(head_t, tail_t, jnp.transpose(_block_diag2(cmp_wk)), jnp.transpose(_block_diag2(cmp_wv)))


def _cmp_kernel(head_ref, tail_ref, wk_ref, wv_ref, kca_ref, vct_ref):
    n = head_ref.shape[0]
    blocks = head_ref[...] + pltpu.roll(tail_ref[...], n - 1, 0)
    kca_ref[:, :NSA_KV] = _mm(blocks[:, :NSA_KV].astype(BF16), wk_ref[...]).astype(BF16)
    blk_end = _iota((n, 1), 0) * CMP_STRIDE + (CMP_LEN - 1)
    kca_ref[:, NSA_KV:] = _pos_features(blk_end, (n, LANES))
    vct_ref[...] = _nt(wv_ref[...], blocks[:, NSA_KV:].astype(BF16)).astype(BF16)


def _block_diag2(w):
    z = jnp.zeros_like(w)
    return jnp.concatenate([jnp.concatenate([w, z], axis=1), jnp.concatenate([z, w], axis=1)], axis=0).astype(BF16)


def _compress(head, tail, cmp_wk, cmp_wv, b):
    n_ch = head.shape[0] // b
    w = 2 * NSA_KV
    return pl.pallas_call(
        _cmp_kernel,
        grid=(b,),
        in_specs=[pl.BlockSpec((n_ch, w), lambda i: (i, 0))] * 2
        + [pl.BlockSpec((NSA_KV, NSA_KV), lambda i: (0, 0))] * 2,
        out_specs=[pl.BlockSpec((n_ch, w), lambda i: (i, 0)),
                   pl.BlockSpec((NSA_KV, n_ch), lambda i: (0, i))],
        out_shape=[jax.ShapeDtypeStruct((b * n_ch, w), BF16),
                   jax.ShapeDtypeStruct((NSA_KV, b * n_ch), BF16)],
        compiler_params=_cparams(("parallel",)),
        name="compress",
    )(head, tail, _block_diag2(cmp_wk), jnp.transpose(_block_diag2(cmp_wv)))


def _overlap(n_ch, nsp):
    i = jnp.arange(n_ch)[:, None]
    j = jnp.arange(nsp)[None, :]
    lo = jnp.maximum(i * CMP_STRIDE, j * SEL_BLOCK)
    hi = jnp.minimum(i * CMP_STRIDE + CMP_LEN, (j + 1) * SEL_BLOCK)
    ov = jnp.maximum(hi - lo, 0).astype(F32) / CMP_LEN
    return jnp.where(i < n_ch - 1, ov, 0.0)


def _slope(h):
    return 2.0 ** (-(h + 1))


def _softmax_rows(s, mask):
    s = jnp.where(mask, s, NEG_INF)
    m = jnp.max(s, axis=-1, keepdims=True)
    m = jnp.where(m == NEG_INF, 0.0, m)
    e = jnp.where(mask, jnp.exp(s - m), 0.0)
    return e, jnp.maximum(jnp.sum(e, axis=-1, keepdims=True), 1e-30)


def _top_blocks(imp, qpos, ns_lanes):
    blk = _iota(imp.shape, 1)
    cur = qpos >> 6
    forced = (blk == 0) | (blk == cur) | (blk == cur - 1)
    score = jnp.where(forced, FORCE_SCORE, imp)
    work = jnp.where(blk <= cur, score, NEG_INF)
    sel = jnp.zeros(imp.shape, F32)
    for _ in range(SEL_TOP):
        m = jnp.max(work, axis=-1, keepdims=True)
        cand = jnp.where((work == m) & (m > NEG_INF), blk, ns_lanes)
        first = jnp.min(cand, axis=-1, keepdims=True)
        hit = blk == first
        sel = jnp.where(hit, 1.0, sel)
        work = jnp.where(hit, NEG_INF, work)
    return sel


def _row_slopes(tq):
    head = _iota((NSA_HEADS * tq, 1), 0) >> (tq.bit_length() - 1)
    slope = jnp.zeros((NSA_HEADS * tq, 1), F32)
    for h in range(NSA_HEADS):
        slope = jnp.where(head == h, _slope(h), slope)
    return slope


def _rows_attend(q8, kt, vt, slope, dist, mask):
    s = _mm(q8, kt) - slope * dist.astype(F32)
    e, den = _softmax_rows(s, mask)
    p = e / den
    return _nt(p.astype(BF16), vt), p


def _rows_online(q8, kt, vt, slope, dist, mask, m_ref, l_ref, acc_ref):
    s = jnp.where(mask, _mm(q8, kt) - slope * dist.astype(F32), NEG_INF)
    m_old = m_ref[...]
    m_new = jnp.maximum(m_old, jnp.max(s, axis=-1, keepdims=True))
    m_use = jnp.where(m_new == NEG_INF, 0.0, m_new)
    alpha = jnp.exp(m_old - m_use)
    e = jnp.exp(s - m_use)
    l_ref[...] = alpha * l_ref[...] + jnp.sum(e, axis=-1, keepdims=True)
    acc_ref[...] = alpha * acc_ref[...] + _nt(e.astype(BF16), vt)
    m_ref[...] = m_new


def _merge_heads(o_c, o_s, o_w, gates, tq):
    lane = _iota((tq, LANES), 1)
    mixed = []
    for h in range(NSA_HEADS):
        rows = slice(h * tq, (h + 1) * tq)
        g0 = gates[:, SM_G + 3 * h:SM_G + 3 * h + 1]
        g1 = gates[:, SM_G + 3 * h + 1:SM_G + 3 * h + 2]
        g2 = gates[:, SM_G + 3 * h + 2:SM_G + 3 * h + 3]
        mixed.append(g0 * o_c[rows] + g1 * o_s[rows] + g2 * o_w[rows])
    cols = []
    for p in range(NSA_HEADS // 2):
        a, b = mixed[2 * p], mixed[2 * p + 1]
        if 2 * p < NSA_GROUP:
            cols.append(jnp.where(lane < NSA_HD, a, pltpu.roll(b, NSA_HD, 1)))
        else:
            cols.append(jnp.where(lane < NSA_HD, pltpu.roll(a, NSA_HD, 1), b))
    return jnp.concatenate(cols, axis=1)


def _softmax_cols(s, mask):
    s = jnp.where(mask, s, NEG_INF)
    m = jnp.max(s, axis=0, keepdims=True)
    m = jnp.where(m == NEG_INF, 0.0, m)
    e = jnp.exp(s - m)
    return e * (1.0 / jnp.maximum(jnp.sum(e, axis=0, keepdims=True), 1e-30))


def _split3(x):
    hi = x.astype(BF16)
    r1 = x - hi.astype(F32)
    mid = r1.astype(BF16)
    lo = (r1 - mid.astype(F32)).astype(BF16)
    return hi, mid, lo


def _top_blocks_cols(imp, qpos, nsp):
    blk = _iota(imp.shape, 0)
    cur = qpos >> 6
    forced = (blk == 0) | (blk == cur) | (blk == cur - 1)
    work = jnp.where(blk <= cur, jnp.where(forced, FORCE_SCORE, imp), NEG_INF)
    neg = jnp.full(imp.shape, NEG_INF, F32)
    for _ in range(SEL_TOP):
        m = jnp.max(work, axis=0, keepdims=True)
        cand = jnp.where((work == m) & (m > NEG_INF), blk, nsp)
        hit = blk == jnp.min(cand, axis=0, keepdims=True)
        neg = jnp.where(hit, 0.0, neg)
        work = jnp.where(hit, NEG_INF, work)
    return neg


def _nsa_prompt_kernel(q8_ref, sm_ref, kc_ref, vct_ref, ks_ref, vst_ref, kw_ref, vwt_ref, ovt_ref, y_ref,
                       neg_ref, m_ref, acc_ref, qta_ref, sa_ref, sb_ref, *, tq, tk):
    i = pl.program_id(1)
    start = i * tq
    r = NSA_HEADS * tq
    nsp, n_ch = ovt_ref.shape
    col = _iota((1, r), 1)
    qpos = start + (col & (tq - 1))

    q8 = jnp.concatenate([q8_ref[:, h * LANES:(h + 1) * LANES] for h in range(NSA_HEADS)], axis=0)
    eye = (_iota((LANES, LANES), 0) == _iota((LANES, LANES), 1)).astype(BF16)
    qt = _nt(eye, q8).astype(BF16)
    head = col >> (tq.bit_length() - 1)
    slope = jnp.zeros((1, r), F32)
    for h in range(NSA_HEADS):
        slope = jnp.where(head == h, _slope(h), slope)
    frow = _iota((2 * SUBLANES, r), 0)
    feat = jnp.where(frow < 2, slope, jnp.where(frow == 2, -slope * ((qpos >> 7) * LANES).astype(F32),
                                                jnp.where(frow == 3, -slope * (qpos & (LANES - 1)).astype(F32), 0.0)))
    qta = jnp.concatenate([qt, feat.astype(BF16), jnp.zeros((LANES - 2 * SUBLANES, r), BF16)], axis=0)

    blk_end = _iota((n_ch, 1), 0) * CMP_STRIDE + (CMP_LEN - 1)
    p = _softmax_cols(_mm(kc_ref[...], qta), blk_end <= qpos)
    o_c = _mm(vct_ref[...], p.astype(BF16))
    psum = []
    for g in range(NSA_KV_HEADS):
        acc = p[:, g * NSA_GROUP * tq:(g * NSA_GROUP + 1) * tq]
        for hh in range(1, NSA_GROUP):
            acc = acc + p[:, (g * NSA_GROUP + hh) * tq:(g * NSA_GROUP + hh + 1) * tq]
        psum.append(acc)
    ovt = ovt_ref[...]
    imp = sum(_mm(ovt, piece) for piece in _split3(jnp.concatenate(psum, axis=1)))
    qpos2 = start + (_iota((1, NSA_KV_HEADS * tq), 1) & (tq - 1))
    neg = _top_blocks_cols(imp, qpos2, nsp)
    for h in range(NSA_HEADS):
        g = h // NSA_GROUP
        neg_ref[:, h * tq:(h + 1) * tq] = neg[:, g * tq:(g + 1) * tq]

    m_ref[...] = jnp.full(m_ref.shape, NEG_INF, F32)
    acc_ref[...] = jnp.zeros(acc_ref.shape, F32)
    bpt = tk // SEL_BLOCK

    qta_ref[...] = qta

    def scores(t, buf):
        off = pl.multiple_of(t * tk, tk)
        buf[...] = _mm(ks_ref[pl.ds(off, tk), :], qta_ref[...])

    def tile(t, buf, causal):
        off = pl.multiple_of(t * tk, tk)
        parts = []
        for j in range(bpt):
            sj = buf[j * SEL_BLOCK:(j + 1) * SEL_BLOCK, :]
            neg = neg_ref[pl.ds(t * bpt + j, 1), :]
            if causal:
                sj = jnp.where(off + j * SEL_BLOCK + _iota((SEL_BLOCK, 1), 0) <= qpos, sj + neg, NEG_INF)
                parts.append(sj.astype(BF16))
            else:
                parts.append(sj.astype(BF16) + neg.astype(BF16))
        s = jnp.concatenate(parts, axis=0)
        m_old = m_ref[...]
        m_new = jnp.maximum(m_old, jnp.max(s, axis=0, keepdims=True).astype(F32))
        m_use = jnp.where(m_new == NEG_INF, 0.0, m_new)
        alpha = jnp.exp(m_old - m_use)
        e = jnp.exp(s - m_use.astype(BF16))
        acc_ref[...] = alpha * acc_ref[...] + _mm(vst_ref[:, pl.ds(off, tk)], e)
        m_ref[...] = m_new

    n_full = start // tk
    pairs = n_full // 2
    scores(0, sa_ref)

    def body(u, carry):
        scores(2 * u + 1, sb_ref)
        tile(2 * u, sa_ref, False)
        scores(2 * u + 2, sa_ref)
        tile(2 * u + 1, sb_ref, False)
        return carry

    lax.fori_loop(0, pairs, body, 0)

    @pl.when(n_full % 2 == 1)
    def _():
        scores(n_full, sb_ref)
        tile(n_full - 1, sa_ref, False)
        tile(n_full, sb_ref, True)

    @pl.when(n_full % 2 == 0)
    def _():
        tile(n_full, sa_ref, True)
    o_s = acc_ref[:NSA_KV, :] * (1.0 / jnp.maximum(acc_ref[NSA_KV:NSA_KV + 1, :], 1e-30))

    wl = tq + WINDOW
    ws = pl.multiple_of(jnp.maximum(start - WINDOW, 0), tq)
    dist = qpos - (ws + _iota((wl, 1), 0))
    sw = jnp.where((dist >= 0) & (dist <= WINDOW), _mm(kw_ref[pl.ds(ws, wl), :], qta), NEG_INF).astype(BF16)
    ew = jnp.exp(sw - jnp.max(sw, axis=0, keepdims=True))
    ow = _mm(vwt_ref[:, pl.ds(ws, wl)], ew)
    o_w = ow[:NSA_KV] * (1.0 / jnp.maximum(ow[NSA_KV:NSA_KV + 1], 1e-30))

    gt = jnp.transpose(_sigmoid(sm_ref[...]))
    gate = [jnp.concatenate([gt[SM_G + 3 * h + c:SM_G + 3 * h + c + 1, :] for h in range(NSA_HEADS)], axis=1)
            for c in range(3)]
    o = gate[0] * o_c + gate[1] * o_s + gate[2] * o_w
    yt = jnp.concatenate([o[(h // NSA_GROUP) * NSA_HD:(h // NSA_GROUP + 1) * NSA_HD, h * tq:(h + 1) * tq]
                          for h in range(NSA_HEADS)], axis=0)
    y_ref[...] = jnp.transpose(yt)


def _nsa_prompt(q8, sm, kca, vct, ksa, vst, kwa, vwt, b, l):
    tq = 128
    tk = 512
    assert l % tk == 0 and l >= tq + WINDOW
    n_ch = l // CMP_STRIDE
    ns = -(-l // SEL_BLOCK)
    nsp = -(-ns // LANES) * LANES
    nq = l // tq
    w = 2 * NSA_KV
    ovt = jnp.transpose(_overlap(n_ch, nsp)).astype(BF16)
    row = lambda bi, i: (bi * nq + i, 0)
    seq_rows = lambda bi, i: (bi, 0)
    seq_cols = lambda bi, i: (0, bi)
    return pl.pallas_call(
        functools.partial(_nsa_prompt_kernel, tq=tq, tk=tk),
        grid=(b, nq),
        in_specs=[pl.BlockSpec((tq, NSA_HEADS * LANES), row),
                  pl.BlockSpec((tq, LANES), row),
                  pl.BlockSpec((n_ch, w), seq_rows),
                  pl.BlockSpec((NSA_KV, n_ch), seq_cols),
                  pl.BlockSpec((l, w), seq_rows),
                  pl.BlockSpec((VS_ROWS, l), seq_cols),
                  pl.BlockSpec((l, w), seq_rows),
                  pl.BlockSpec((VS_ROWS, l), seq_cols),
                  pl.BlockSpec((nsp, n_ch), lambda bi, i: (0, 0))],
        out_specs=pl.BlockSpec((tq, NSA_Q), row),
        out_shape=jax.ShapeDtypeStruct((b * l, NSA_Q), F32),
        scratch_shapes=[pltpu.VMEM((nsp, NSA_HEADS * tq), F32),
                        pltpu.VMEM((1, NSA_HEADS * tq), F32),
                        pltpu.VMEM((VS_ROWS, NSA_HEADS * tq), F32),
                        pltpu.VMEM((2 * NSA_KV, NSA_HEADS * tq), BF16),
                        pltpu.VMEM((tk, NSA_HEADS * tq), F32),
                        pltpu.VMEM((tk, NSA_HEADS * tq), F32)],
        compiler_params=_cparams(("parallel", "arbitrary")),
        name="nsa_prompt",
    )(q8, sm, kca, vct, ksa, vst, kwa, vwt, ovt)


def _nsa_sample_kernel(*refs, pg, page, tq, past_len):
    pt_ref = refs[0]
    del pt_ref
    page_refs = refs[1:1 + pg]
    (q8_ref, sm_ref, kc_ref, vc_ref, knew_ref, kw_ref, ov_ref, y_ref,
     m_ref, l_ref, acc_ref, sel_ref, oc_ref) = refs[1 + pg:]
    j = pl.program_id(1)
    nsp = ov_ref.shape[1]
    r = NSA_HEADS * tq
    q8 = jnp.concatenate([q8_ref[:, h * LANES:(h + 1) * LANES].astype(F32) for h in range(NSA_HEADS)],
                         axis=0).astype(BF16)
    qpos = past_len + (_iota((r, 1), 0) & (tq - 1))
    slope = _row_slopes(tq)

    @pl.when(j == 0)
    def _():
        n_ch = kc_ref.shape[1]
        dist = qpos - (_iota((1, n_ch), 1) * CMP_STRIDE + (CMP_LEN - 1))
        o_c, p = _rows_attend(q8, kc_ref[...], vc_ref[...], slope, dist, dist >= 0)
        oc_ref[...] = o_c
        for g in range(NSA_KV_HEADS):
            psum = p[g * NSA_GROUP * tq:(g * NSA_GROUP + 1) * tq]
            for hh in range(1, NSA_GROUP):
                psum = psum + p[(g * NSA_GROUP + hh) * tq:(g * NSA_GROUP + hh + 1) * tq]
            sel = _top_blocks(_mm(psum, ov_ref[...], HI), qpos[:tq], nsp)
            for hh in range(NSA_GROUP):
                sel_ref[(g * NSA_GROUP + hh) * tq:(g * NSA_GROUP + hh + 1) * tq, :] = sel
        m_ref[...] = jnp.full(m_ref.shape, NEG_INF, F32)
        l_ref[...] = jnp.zeros(l_ref.shape, F32)
        acc_ref[...] = jnp.zeros(acc_ref.shape, F32)

    sel_bf = sel_ref[...].astype(BF16)

    def sel_step(kv, kpos):
        n = kv.shape[1]
        expand = jnp.where(_iota((nsp, n), 0) == (kpos >> 6), 1.0, 0.0).astype(BF16)
        dist = qpos - kpos
        mask = (_mm(sel_bf, expand) > 0.5) & (dist >= 0)
        _rows_online(q8, kv[:NSA_KV].astype(BF16), kv[NSA_KV:].astype(BF16), slope, dist, mask,
                     m_ref, l_ref, acc_ref)

    sel_step(jnp.concatenate([page_refs[k][...] for k in range(pg)], axis=1),
             j * (pg * page) + _iota((1, pg * page), 1))

    @pl.when(j == pl.num_programs(1) - 1)
    def _():
        sel_step(knew_ref[...], past_len + _iota((1, knew_ref.shape[1]), 1))
        o_s = acc_ref[...] / jnp.maximum(l_ref[...], 1e-30)
        kw = kw_ref[...]
        kwpos = past_len - WINDOW + _iota((1, kw.shape[1]), 1)
        dist = qpos - kwpos
        mask = (dist >= 0) & (dist <= WINDOW) & (kwpos >= 0) & (kwpos < past_len + tq)
        o_w, _ = _rows_attend(q8, kw[:NSA_KV].astype(BF16), kw[NSA_KV:].astype(BF16), slope, dist, mask)
        y_ref[...] = _merge_heads(oc_ref[...], o_s, o_w, _sigmoid(sm_ref[...]), tq)


def _nsa_sample(q8, sm, kct, vct, knew_t, win_t, cache_t, page_table, b, l):
    n_pages = page_table.shape[1]
    page = cache_t.shape[2]
    past_len = n_pages * page
    assert l <= SEL_BLOCK and past_len % SEL_BLOCK == 0 and past_len >= WINDOW
    pg = 16
    steps = n_pages // pg
    n_ch = past_len // CMP_STRIDE
    ns = -(-(past_len + l) // SEL_BLOCK)
    nsp = -(-ns // LANES) * LANES
    w = 2 * NSA_KV
    ov = _overlap(n_ch, nsp)
    wcols = win_t.shape[2]

    def page_map(k):
        return lambda bi, j, pt: (pt[bi, j * pg + k], 1, 0)

    per_b = lambda bi, j, pt: (bi, 0)
    per_b3 = lambda bi, j, pt: (bi, 0, 0)
    grid_spec = pltpu.PrefetchScalarGridSpec(
        num_scalar_prefetch=1,
        grid=(b, steps),
        in_specs=[pl.BlockSpec((None, w, page), page_map(k)) for k in range(pg)]
        + [pl.BlockSpec((None, l, NSA_HEADS * LANES), per_b3),
           pl.BlockSpec((l, LANES), per_b),
           pl.BlockSpec((None, NSA_KV, n_ch), per_b3),
           pl.BlockSpec((None, NSA_KV, n_ch), per_b3),
           pl.BlockSpec((None, w, LANES), per_b3),
           pl.BlockSpec((None, w, wcols), per_b3),
           pl.BlockSpec((n_ch, nsp), lambda bi, j, pt: (0, 0))],
        out_specs=pl.BlockSpec((l, NSA_Q), per_b),
        scratch_shapes=[pltpu.VMEM((NSA_HEADS * l, 1), F32),
                        pltpu.VMEM((NSA_HEADS * l, 1), F32),
                        pltpu.VMEM((NSA_HEADS * l, NSA_KV), F32),
                        pltpu.VMEM((NSA_HEADS * l, nsp), F32),
                        pltpu.VMEM((NSA_HEADS * l, NSA_KV), F32)],
    )
    return pl.pallas_call(
        functools.partial(_nsa_sample_kernel, pg=pg, page=page, tq=l, past_len=past_len),
        grid_spec=grid_spec,
        out_shape=jax.ShapeDtypeStruct((b * l, NSA_Q), F32),
        compiler_params=_cparams(("parallel", "arbitrary")),
        name="nsa_sample",
    )(page_table, *([cache_t] * pg), q8.reshape(b, l, NSA_HEADS * LANES), sm, kct, vct, knew_t, win_t, ov)


def _tail_kernel(x_ref, ya_ref, yb_ref, ga_ref, gb_ref, wa_ref, wb_ref, wo_ref, n2_ref, wr_ref, br_ref,
                 h_ref, hn_ref, cmb_ref):
    ma = _mm(ya_ref[...].astype(BF16), wa_ref[...])
    mb = _mm(yb_ref[...].astype(BF16), wb_ref[...])
    m = _sigmoid(ga_ref[...]) * ma + _sigmoid(gb_ref[...]) * mb
    h = x_ref[...] + _mm(m.astype(BF16), wo_ref[...])
    h_ref[...] = h
    hn = h * lax.rsqrt(jnp.mean(h * h, axis=-1, keepdims=True) + RMS_EPS) * n2_ref[...]
    hn_hi = hn.astype(BF16)
    hn_ref[...] = hn_hi

    hn_lo = (hn - hn_hi.astype(F32)).astype(BF16)
    logit = _mm(hn_hi, wr_ref[0]) + _mm(hn_lo, wr_ref[0]) + _mm(hn_hi, wr_ref[1]) + br_ref[...]
    lane = _iota(logit.shape, 1)
    is_grp = (lane >= N_EXPERTS) & (lane < N_EXPERTS + N_GROUPS)
    gl = jnp.where(is_grp, logit, NEG_INF)
    gmax = jnp.max(gl, axis=-1, keepdims=True)
    gidx = jnp.min(jnp.where(gl == gmax, lane, LANES), axis=-1, keepdims=True) - N_EXPERTS
    p_grp = 1.0 / jnp.sum(jnp.exp(gl - gmax), axis=-1, keepdims=True)
    el = jnp.where((lane >> 3) == gidx, logit, NEG_INF)
    t1 = jnp.max(el, axis=-1, keepdims=True)
    i1 = jnp.min(jnp.where(el == t1, lane, LANES), axis=-1, keepdims=True)
    el2 = jnp.where(lane == i1, NEG_INF, el)
    t2 = jnp.max(el2, axis=-1, keepdims=True)
    i2 = jnp.min(jnp.where(el2 == t2, lane, LANES), axis=-1, keepdims=True)
    e2 = jnp.exp(t2 - t1)
    w1 = p_grp / (1.0 + e2)
    w2 = p_grp * e2 / (1.0 + e2)
    cmb_ref[...] = jnp.where(lane == i1, w1, jnp.where(lane == i2, w2,
                                                        jnp.where(lane == gidx + N_EXPERTS, 1.0, 0.0)))


def _tail(x2d, ya, yb, ga, gb, wa, wb, wo, norm2_g, w_route, b_route):
    n = x2d.shape[0]
    tm = min(ROW_TILE, n)
    row = lambda i: (i, 0)
    fixed = lambda i: (0, 0)
    return pl.pallas_call(
        _tail_kernel,
        grid=(n // tm,),
        in_specs=[pl.BlockSpec((tm, D_MODEL), row),
                  pl.BlockSpec((tm, GDN_V), row),
                  pl.BlockSpec((tm, NSA_Q), row),
                  pl.BlockSpec((tm, D_MODEL), row),
                  pl.BlockSpec((tm, D_MODEL), row),
                  pl.BlockSpec((GDN_V, D_MODEL), fixed),
                  pl.BlockSpec((NSA_Q, D_MODEL), fixed),
                  pl.BlockSpec((D_MODEL, D_MODEL), fixed),
                  pl.BlockSpec((1, D_MODEL), fixed),
                  pl.BlockSpec((2, D_MODEL, LANES), lambda i: (0, 0, 0)),
                  pl.BlockSpec((1, LANES), fixed)],
        out_specs=[pl.BlockSpec((tm, D_MODEL), row),
                   pl.BlockSpec((tm, D_MODEL), row),
                   pl.BlockSpec((tm, LANES), row)],
        out_shape=[jax.ShapeDtypeStruct((n, D_MODEL), F32),
                   jax.ShapeDtypeStruct((n, D_MODEL), BF16),
                   jax.ShapeDtypeStruct((n, LANES), F32)],
        compiler_params=_cparams(("parallel",)),
        name="tail",
    )(x2d, ya, yb, ga, gb, wa, wb, wo, norm2_g.reshape(1, D_MODEL), w_route, b_route)


MOE_SLABS = 2 * N_GROUPS
MOE_SLAB_EXPERTS = N_EXPERTS // MOE_SLABS


def _moe_kernel(offs_ref, cnts_ref, h_ref, hn_ref, cmb_ref, wg_ref, wu_ref, wd_ref, nf_ref, y_ref,
                xs_ref, cs_ref, pt_ref, acc_ref, *, rb):
    i = pl.program_id(0)
    sl = pl.program_id(1)
    t = hn_ref.shape[0]
    lane = _iota((t, LANES), 1)

    @pl.when(sl == 0)
    def _():
        cmb = cmb_ref[...]
        oh = jnp.where((lane >= N_EXPERTS) & (lane < N_EXPERTS + N_GROUPS), cmb, 0.0)
        oh_bf = oh.astype(BF16)
        r_i = _iota((t, t), 0)
        c_i = _iota((t, t), 1)
        lt = _iota((LANES, LANES), 0) < _iota((LANES, LANES), 1)
        before = _mm(_ones_where(c_i < r_i), oh_bf)
        totals = jnp.broadcast_to(jnp.sum(oh, axis=0, keepdims=True), (SUBLANES, LANES))
        smaller = _mm(totals, lt.astype(F32), HI)[0:1]
        rank_col = jnp.sum((before + smaller) * oh, axis=1, keepdims=True)
        eye = (_iota((LANES, LANES), 0) == _iota((LANES, LANES), 1)).astype(BF16)
        oht = _nt(eye, oh_bf)
        before_t = _mm(oht.astype(BF16), _ones_where(r_i < c_i))
        totals_t = jnp.broadcast_to(jnp.sum(oht, axis=1, keepdims=True), (LANES, LANES))
        gt = _iota((LANES, LANES), 0) > _iota((LANES, LANES), 1)
        smaller_t = _mm(gt.astype(F32), totals_t, HI)[:, 0:1]
        rank_row = jnp.sum((before_t + smaller_t) * oht, axis=0, keepdims=True)
        perm = _ones_where(r_i == rank_row.astype(jnp.int32))
        pt_ref[...] = _ones_where(c_i == rank_col.astype(jnp.int32))
        xs_ref[...] = _mm(perm, hn_ref[...]).astype(BF16)
        c_hi = cmb.astype(BF16)
        c_lo = (cmb - c_hi.astype(F32)).astype(BF16)
        cs_ref[...] = _mm(perm, c_hi) + _mm(perm, c_lo)
        acc_ref[...] = jnp.zeros(acc_ref.shape, F32)

    g = sl // (MOE_SLABS // N_GROUPS)
    off = offs_ref[i, g]
    row0 = (off // (2 * SUBLANES)) * (2 * SUBLANES)
    nblk = (off + cnts_ref[i, g] - row0 + rb - 1) // rb
    lane_b = _iota((rb, LANES), 1)

    def block(j, carry):
        lo = row0 + j * rb
        r0 = pl.multiple_of(jnp.minimum(lo, t - rb), 2 * SUBLANES)
        x = xs_ref[pl.ds(r0, rb), :]
        cw = jnp.where(r0 + _iota((rb, 1), 0) >= lo, cs_ref[pl.ds(r0, rb), :], 0.0)
        parts = []
        for e in range(MOE_SLAB_EXPERTS):
            wgt = jnp.sum(jnp.where(lane_b == sl * MOE_SLAB_EXPERTS + e, cw, 0.0), axis=1, keepdims=True)
            parts.append((_silu(_mm(x, wg_ref[e])) * _mm(x, wu_ref[e]) * wgt).astype(BF16))
        acc_ref[pl.ds(r0, rb), :] += _mm(jnp.concatenate(parts, axis=1), wd_ref[0])
        return carry

    lax.fori_loop(0, nblk, block, 0)

    @pl.when(sl == pl.num_programs(1) - 1)
    def _():
        acc = acc_ref[...]
        a_hi = acc.astype(BF16)
        a_lo = (acc - a_hi.astype(F32)).astype(BF16)
        pt = pt_ref[...]
        v = h_ref[...] + _mm(pt, a_hi) + _mm(pt, a_lo)
        y_ref[...] = v * lax.rsqrt(jnp.mean(v * v, axis=-1, keepdims=True) + RMS_EPS) * nf_ref[...]


def _moe_weights(w_gate, w_up, w_down):
    return (w_gate.astype(BF16), w_up.astype(BF16),
            w_down.astype(BF16).reshape(MOE_SLABS, MOE_SLAB_EXPERTS * D_EXPERT, D_MODEL))


def _moe(h, hn, cmb, wg, wu, wd, norm_f_g):
    n = h.shape[0]
    t = min(1024, n)
    rb = min(256, t)
    tiles = n // t
    cnts = jnp.sum(cmb[:, N_EXPERTS:N_EXPERTS + N_GROUPS].reshape(tiles, t, N_GROUPS), axis=1).astype(jnp.int32)
    offs = jnp.cumsum(cnts, axis=1) - cnts
    ws = MOE_SLAB_EXPERTS * D_EXPERT
    row = lambda i, s, o, c: (i, 0)
    slab = lambda i, s, o, c: (s, 0, 0)
    grid_spec = pltpu.PrefetchScalarGridSpec(
        num_scalar_prefetch=2,
        grid=(tiles, MOE_SLABS),
        in_specs=[pl.BlockSpec((t, D_MODEL), row),
                  pl.BlockSpec((t, D_MODEL), row),
                  pl.BlockSpec((t, LANES), row),
                  pl.BlockSpec((MOE_SLAB_EXPERTS, D_MODEL, D_EXPERT), slab),
                  pl.BlockSpec((MOE_SLAB_EXPERTS, D_MODEL, D_EXPERT), slab),
                  pl.BlockSpec((1, ws, D_MODEL), slab),
                  pl.BlockSpec((1, D_MODEL), lambda i, s, o, c: (0, 0))],
        out_specs=pl.BlockSpec((t, D_MODEL), row),
        scratch_shapes=[pltpu.VMEM((t, D_MODEL), BF16),
                        pltpu.VMEM((t, LANES), F32),
                        pltpu.VMEM((t, t), BF16),
                        pltpu.VMEM((t, D_MODEL), F32)],
    )
    return pl.pallas_call(
        functools.partial(_moe_kernel, rb=rb),
        grid_spec=grid_spec,
        out_shape=jax.ShapeDtypeStruct((n, D_MODEL), F32),
        compiler_params=_cparams(("parallel", "arbitrary")),
        name="moe",
    )(offs, cnts, h, hn, cmb, wg, wu, wd, norm_f_g.reshape(1, D_MODEL))


def _route_weights(w_grp, b_grp, w_rt, b_rt):
    pad = LANES - N_EXPERTS - N_GROUPS
    w = jnp.concatenate([w_rt, w_grp, jnp.zeros((D_MODEL, pad), F32)], axis=1)
    bias = jnp.concatenate([b_rt, b_grp, jnp.zeros((pad,), F32)]).reshape(1, LANES)
    hi = w.astype(BF16)
    return jnp.stack([hi, (w - hi.astype(F32)).astype(BF16)]), bias


def _finish(x2d, ya, yb, ga, gb, wts):
    h, hn, cmb = _tail(x2d, ya, yb, ga, gb, wts["wa"], wts["wb"], wts["wo"], wts["norm2_g"],
                       wts["w_route"], wts["b_route"])
    return _moe(h, hn, cmb, wts["wg"], wts["wu"], wts["wd"], wts["norm_f_g"])


def _prompt_layer(x, wts):
    b, l, _ = x.shape
    x2d = x.reshape(b * l, D_MODEL)
    qkv, z, q8, ga, gb, sm, kv4t, kvwt, cmp_rows, ksa, kwa, vst, vwt = _project(
        x2d, wts["norm1_g"], wts["w_pk"], b, l)
    conv0 = jnp.zeros((b, GDN_CONV - 1, GDN_CONV_DIM), F32)
    s0 = jnp.zeros((b, GDN_HEADS, GDN_DK, GDN_DV), F32)
    ya, s_new, conv_new = _gdn(qkv, z, sm, conv0, s0, wts["conv_w"], wts["a_log"], wts["dt_bias"],
                               wts["gdn_norm_g"], b, l)
    head, tail = _pool_rows(cmp_rows, wts["pw"])
    kca, vct = _compress(head, tail, wts["cmp_wk"], wts["cmp_wv"], b)
    yb = _nsa_prompt(q8, sm, kca, vct, ksa, vst, kwa, vwt, b, l)
    y = _finish(x2d, ya, yb, ga, gb, wts)
    win_buf = min(WINDOW, l)
    kv_new = jnp.transpose(kv4t.reshape(b, 4, NSA_KV_HEADS, NSA_HD, l), (0, 4, 1, 2, 3))
    win_new = jnp.transpose(kvwt[:, :, l - win_buf:].reshape(b, 2, NSA_KV_HEADS, NSA_HD, win_buf), (0, 4, 1, 2, 3))
    return (y.reshape(b, l, D_MODEL), kv_new, win_new, s_new, conv_new)


def _sample_layer(x, cache_kv_l, page_table, cache_win_l, s0, conv_buf, wts):
    b, l, _ = x.shape
    x2d = x.reshape(b * l, D_MODEL)
    qkv, z, q8, ga, gb, sm, kv4, kvw = _project(x2d, wts["norm1_g"], wts["w_pk"], b, l)
    ya, s_new, conv_new = _gdn(qkv, z, sm, conv_buf, s0, wts["conv_w"], wts["a_log"], wts["dt_bias"],
                               wts["gdn_norm_g"], b, l)
    n_phys, page = cache_kv_l.shape[:2]
    w = 2 * NSA_KV
    cache_t = jnp.transpose(cache_kv_l, (0, 2, 3, 4, 1)).reshape(n_phys, 2 * w, page)
    win_buf = cache_win_l.shape[1]
    win_old_t = jnp.transpose(cache_win_l, (0, 2, 3, 4, 1)).reshape(b, w, win_buf)
    head_t, tail_t = _pool_pages(cache_t, page_table, wts["pos_wk"], wts["pos_wv"])
    kct, vct = _compress_t(head_t, tail_t, wts["cmp_wk"], wts["cmp_wv"])
    new_t = jnp.transpose(kv4.reshape(b, l, 2 * w), (0, 2, 1))
    knew_t = jnp.pad(new_t[:, w:], ((0, 0), (0, 0), (0, LANES - l)))
    win_new_t = jnp.transpose(kvw.reshape(b, l, w), (0, 2, 1))
    wcols = -(-(win_buf + l) // LANES) * LANES
    win_t = jnp.concatenate([win_old_t, win_new_t, jnp.zeros((b, w, wcols - win_buf - l), F32)], axis=2)
    yb = _nsa_sample(q8, sm, kct, vct, knew_t, win_t, cache_t, page_table, b, l)
    y = _finish(x2d, ya, yb, ga, gb, wts)
    win_new = jnp.transpose(win_t[:, :, l:l + win_buf].reshape(b, 2, NSA_KV_HEADS, NSA_HD, win_buf), (0, 4, 1, 2, 3))
    return (y.reshape(b, l, D_MODEL), kv4.reshape(b, l, 4, NSA_KV_HEADS, NSA_HD), win_new, s_new, conv_new)


def kernel(x_prompt, x_sample, cache_kv, page_table, cache_win, state_gdn, state_conv, norm1_g, w_in, gdn_conv_w, gdn_a_log, gdn_dt_bias, gdn_norm_g, cmp_pos_wk, cmp_pos_wv, cmp_wk, cmp_wv, w_branch_a, w_branch_b, w_out, norm2_g, w_grp, b_grp, w_rt, b_rt, w_e_gate, w_e_up, w_e_down, norm_f_g):
    assert w_in.shape[0] == 1, "single layer"
    w_route, b_route = _route_weights(w_grp[0], b_grp[0], w_rt[0], b_rt[0])
    wg, wu, wd = _moe_weights(w_e_gate[0], w_e_up[0], w_e_down[0])
    wts = dict(
        norm1_g=norm1_g[0], w_pk=_pack_w_in(w_in[0]),
        conv_w=gdn_conv_w[0], a_log=gdn_a_log[0], dt_bias=gdn_dt_bias[0], gdn_norm_g=gdn_norm_g[0],
        pw=_pos_weights(cmp_pos_wk[0], cmp_pos_wv[0]), pos_wk=cmp_pos_wk[0], pos_wv=cmp_pos_wv[0],
        cmp_wk=cmp_wk[0], cmp_wv=cmp_wv[0],
        wa=w_branch_a[0].astype(BF16), wb=w_branch_b[0].astype(BF16), wo=w_out[0].astype(BF16),
        norm2_g=norm2_g[0], w_route=w_route, b_route=b_route,
        wg=wg, wu=wu, wd=wd, norm_f_g=norm_f_g,
    )
    yp, kvp, winp, sp, cp = _prompt_layer(x_prompt, wts)
    ys, kvs, wins, ss, cs = _sample_layer(x_sample, cache_kv[0], page_table, cache_win[0], state_gdn[0],
                                          state_conv[0], wts)
    return (yp, ys, kvp[None], kvs[None], winp[None], wins[None], sp[None], ss[None], cp[None], cs[None])
```

```python
import functools
import math

import numpy as np
import jax
import jax.numpy as jnp
from jax import lax
from jax.experimental import pallas as pl
from jax.experimental.pallas import tpu as pltpu

F32 = jnp.float32
BF16 = jnp.bfloat16
HI = lax.Precision.HIGHEST

LANES = 128
SUBLANES = 8
VMEM_LIMIT = 56 * 1024 * 1024
ROW_TILE = 512

D_MODEL = 1024
GDN_HEADS = 4
GDN_DK = 128
GDN_DV = 128
GDN_QK = GDN_HEADS * GDN_DK
GDN_V = GDN_HEADS * GDN_DV
GDN_CONV_DIM = 2 * GDN_QK + GDN_V
GDN_CONV = 4
GDN_CHUNK = 64
GDN_SEQS_PER_STEP = 4
NSA_HEADS = 8
NSA_KV_HEADS = 2
NSA_GROUP = NSA_HEADS // NSA_KV_HEADS
NSA_HD = 64
NSA_Q = NSA_HEADS * NSA_HD
NSA_KV = NSA_KV_HEADS * NSA_HD
CMP_STRIDE = 16
CMP_LEN = 2 * CMP_STRIDE
SEL_BLOCK = 64
SEL_TOP = 16
WINDOW = 512
FORCE_SCORE = 1.0e4
N_GROUPS = 4
EXPERTS_PER_GROUP = 8
N_EXPERTS = N_GROUPS * EXPERTS_PER_GROUP
D_EXPERT = 256
RMS_EPS = 1e-6
VS_ROWS = NSA_KV + 2 * SUBLANES
NEG_INF = float("-inf")

_OFF_QKV = 0
_OFF_Z = _OFF_QKV + GDN_CONV_DIM
_OFF_B = _OFF_Z + GDN_V
_OFF_A = _OFF_B + GDN_HEADS
_OFF_Q = _OFF_A + GDN_HEADS
_OFF_KV = _OFF_Q + NSA_Q
_OFF_G = _OFF_KV + 6 * NSA_KV
_OFF_GA = _OFF_G + 3 * NSA_HEADS
_OFF_GB = _OFF_GA + D_MODEL

SM_B = 0
SM_A = 4
SM_G = 8

_PK = {}
_c = 0
for _n, _w in (("qkv", GDN_CONV_DIM), ("z", GDN_V), ("q8", NSA_HEADS * LANES), ("kv4", 4 * NSA_KV),
               ("kvw", 2 * NSA_KV), ("ga", D_MODEL), ("gb", D_MODEL), ("sm", LANES)):
    _PK[_n] = (_c, _w)
    _c += _w
PK_DIM = _c


def _cparams(sem):
    return pltpu.CompilerParams(dimension_semantics=sem, vmem_limit_bytes=VMEM_LIMIT)


def _nt(a, b, precision=None):
    return lax.dot_general(a, b, (((1,), (1,)), ((), ())), preferred_element_type=F32, precision=precision)


def _tn(a, b, precision=None):
    return lax.dot_general(a, b, (((0,), (0,)), ((), ())), preferred_element_type=F32, precision=precision)


def _mm(a, b, precision=None):
    return jnp.dot(a, b, preferred_element_type=F32, precision=precision)


def _mmb(a, b):
    return _mm(a.astype(BF16), b.astype(BF16))


def _sigmoid(x):
    return 1.0 / (1.0 + jnp.exp(-x))


def _silu(x):
    return x * _sigmoid(x)


def _iota(shape, dim):
    return lax.broadcasted_iota(jnp.int32, shape, dim)


def _ones_where(cond):
    return jnp.where(cond, 1.0, 0.0).astype(BF16)


def _pack_w_in(w_in):
    q = w_in[:, _OFF_Q:_OFF_Q + NSA_Q]
    zeros64 = jnp.zeros((D_MODEL, NSA_HD), w_in.dtype)
    q8 = []
    for h in range(NSA_HEADS):
        qh = q[:, h * NSA_HD:(h + 1) * NSA_HD]
        q8.append(jnp.concatenate([qh, zeros64] if h < NSA_GROUP else [zeros64, qh], axis=1))
    sm = jnp.concatenate([w_in[:, _OFF_B:_OFF_B + GDN_HEADS], w_in[:, _OFF_A:_OFF_A + GDN_HEADS],
                          w_in[:, _OFF_G:_OFF_G + 3 * NSA_HEADS],
                          jnp.zeros((D_MODEL, LANES - 2 * GDN_HEADS - 3 * NSA_HEADS), w_in.dtype)], axis=1)
    cols = [w_in[:, _OFF_QKV:_OFF_QKV + GDN_CONV_DIM], w_in[:, _OFF_Z:_OFF_Z + GDN_V]] + q8 + [
        w_in[:, _OFF_KV:_OFF_KV + 4 * NSA_KV], w_in[:, _OFF_KV + 4 * NSA_KV:_OFF_KV + 6 * NSA_KV],
        w_in[:, _OFF_GA:_OFF_GA + D_MODEL], w_in[:, _OFF_GB:_OFF_GB + D_MODEL], sm]
    return jnp.concatenate(cols, axis=1).astype(BF16)


def _pos_features(pos, shape):
    lane = _iota(shape, 1)
    feat = jnp.where(lane == 0, (pos >> 7) * LANES, jnp.where(lane == 1, pos & (LANES - 1),
                                                               jnp.where(lane < 4, 1, 0)))
    return feat.astype(F32).astype(BF16)


def _proj_kernel(x_ref, g_ref, w_ref, qkv_ref, z_ref, q8_ref, ga_ref, gb_ref, sm_ref, *kv_refs, tiles_per_seq):
    tm = x_ref.shape[0]
    x = x_ref[...]
    xn = (x * lax.rsqrt(jnp.mean(x * x, axis=-1, keepdims=True) + RMS_EPS) * g_ref[...]).astype(BF16)

    def seg(name):
        a, w = _PK[name]
        return _mm(xn, w_ref[:, a:a + w])

    qkv_ref[...] = seg("qkv")
    z_ref[...] = seg("z")
    q8_ref[...] = (seg("q8") * (NSA_HD ** -0.5)).astype(BF16)
    ga_ref[...] = seg("ga")
    gb_ref[...] = seg("gb")
    sm_ref[...] = seg("sm")
    kv4 = seg("kv4")
    kvw = seg("kvw")
    if tiles_per_seq == 0:
        kv4_ref, kvw_ref = kv_refs
        kv4_ref[...] = kv4
        kvw_ref[...] = kvw
        return
    kv4t_ref, kvwt_ref, cmp_ref, ksa_ref, kwa_ref, vst_ref, vwt_ref = kv_refs
    kv4t = jnp.transpose(kv4)
    kvwt = jnp.transpose(kvw)
    kv4t_ref[...] = kv4t
    kvwt_ref[...] = kvwt
    cmp_ref[...] = kv4[:, :2 * NSA_KV]
    pos = (pl.program_id(0) % tiles_per_seq) * tm + _iota((tm, 1), 0)
    feat = _pos_features(pos, (tm, LANES))
    ksa_ref[:, :NSA_KV] = kv4[:, 2 * NSA_KV:3 * NSA_KV].astype(BF16)
    ksa_ref[:, NSA_KV:] = feat
    kwa_ref[:, :NSA_KV] = kvw[:, :NSA_KV].astype(BF16)
    kwa_ref[:, NSA_KV:] = feat
    vst_ref[:NSA_KV, :] = kv4t[3 * NSA_KV:].astype(BF16)
    ones_row = _ones_where(_iota((VS_ROWS - NSA_KV, tm), 0) == 0)
    vst_ref[NSA_KV:, :] = ones_row
    vwt_ref[:NSA_KV, :] = kvwt[NSA_KV:].astype(BF16)
    vwt_ref[NSA_KV:, :] = ones_row


def _project(x2d, norm_g, w_pk, b, l):
    n = x2d.shape[0]
    tm = min(ROW_TILE, n)
    long_seq = l % tm == 0
    tps = l // tm if long_seq else 0
    w = 2 * NSA_KV
    row = lambda i: (i, 0)
    out_shape = [jax.ShapeDtypeStruct((n, _PK[k][1]), dt) for k, dt in
                 (("qkv", F32), ("z", F32), ("q8", BF16), ("ga", F32), ("gb", F32), ("sm", F32))]
    out_specs = [pl.BlockSpec((tm, s.shape[1]), row) for s in out_shape]
    if long_seq:
        seq_t = lambda i: (i // tps, 0, i % tps)
        out_shape += [jax.ShapeDtypeStruct((b, 2 * w, l), F32), jax.ShapeDtypeStruct((b, w, l), F32),
                      jax.ShapeDtypeStruct((n, w), F32), jax.ShapeDtypeStruct((n, w), BF16),
                      jax.ShapeDtypeStruct((n, w), BF16), jax.ShapeDtypeStruct((VS_ROWS, n), BF16),
                      jax.ShapeDtypeStruct((VS_ROWS, n), BF16)]
        out_specs += [pl.BlockSpec((None, 2 * w, tm), seq_t), pl.BlockSpec((None, w, tm), seq_t),
                      pl.BlockSpec((tm, w), row), pl.BlockSpec((tm, w), row), pl.BlockSpec((tm, w), row),
                      pl.BlockSpec((VS_ROWS, tm), lambda i: (0, i)), pl.BlockSpec((VS_ROWS, tm), lambda i: (0, i))]
    else:
        out_shape += [jax.ShapeDtypeStruct((n, 2 * w), F32), jax.ShapeDtypeStruct((n, w), F32)]
        out_specs += [pl.BlockSpec((tm, 2 * w), row), pl.BlockSpec((tm, w), row)]
    return pl.pallas_call(
        functools.partial(_proj_kernel, tiles_per_seq=tps),
        grid=(n // tm,),
        in_specs=[pl.BlockSpec((tm, D_MODEL), row),
                  pl.BlockSpec((1, D_MODEL), lambda i: (0, 0)),
                  pl.BlockSpec((D_MODEL, PK_DIM), lambda i: (0, 0), pipeline_mode=pl.Buffered(1))],
        out_specs=out_specs,
        out_shape=out_shape,
        compiler_params=_cparams(("parallel",)),
        name="proj",
    )(x2d, norm_g.reshape(1, D_MODEL), w_pk)


def _unit_lower_inverses(a_list, c):
    r = _iota((c, c), 0)
    col = _iota((c, c), 1)
    eye = (r == col).astype(F32)
    n1 = [jnp.where((r >> 3) == (col >> 3), -a, 0.0) for a in a_list]
    n2 = [_mmb(x, x) for x in n1]
    n4 = [_mmb(x, x) for x in n2]
    t = [_mmb(eye + x, eye + y) for x, y in zip(n1, n2)]
    t = [_mmb(x, eye + y) for x, y in zip(t, n4)]
    s = SUBLANES
    while s < c:
        sh = s.bit_length() - 1
        off = ((r >> (sh + 1)) == (col >> (sh + 1))) & ((r >> sh) != (col >> sh))
        ta = [_mmb(x, jnp.where(off, a, 0.0)) for x, a in zip(t, a_list)]
        t = [x - _mmb(y, x) for x, y in zip(t, ta)]
        s *= 2
    return t


def _gdn_kernel(qkv_ref, z_ref, sm_ref, cbuf_ref, s0_ref, cw_ref, alog_ref, dtb_ref, ng_ref,
                y_ref, snew_ref, cnew_ref, ext_ref, st_ref, *, chunk, nb):
    c = chunk
    ci = pl.program_id(1)

    @pl.when(ci == 0)
    def _():
        ext_ref[:, 0:SUBLANES, :] = cbuf_ref[...]
        st_ref[...] = s0_ref[...]

    r = _iota((c, c), 0)
    col = _iota((c, c), 1)
    tri_incl = r >= col
    tri_strict = r > col
    tri_f = tri_incl.astype(F32)
    pick = (_iota((SUBLANES, LANES), 1) == _iota((SUBLANES, LANES), 0) + SM_A).astype(F32)
    base = SUBLANES - (GDN_CONV - 1)
    ng = ng_ref[...]

    units = []
    for bb in range(nb):
        u = qkv_ref[bb]
        ext_ref[bb, SUBLANES:SUBLANES + c, :] = u
        conv = cw_ref[0:1, :] * ext_ref[bb, base:base + c, :]
        for i in range(1, GDN_CONV - 1):
            conv = conv + cw_ref[i:i + 1, :] * ext_ref[bb, base + i:base + i + c, :]
        conv = conv + cw_ref[GDN_CONV - 1:GDN_CONV, :] * u
        halo = ext_ref[bb, c:c + SUBLANES, :]
        ext_ref[bb, 0:SUBLANES, :] = halo
        cnew_ref[bb] = halo
        qkv = _silu(conv)

        sm = sm_ref[bb]
        beta_all = _sigmoid(sm)
        xa = sm + dtb_ref[...]
        softplus = jnp.maximum(xa, 0.0) + jnp.log(1.0 + jnp.exp(-jnp.abs(xa)))
        g_all = -jnp.exp(alog_ref[...]) * softplus
        gcum_all = _mm(tri_f, g_all, HI)
        gcum_rows = _nt(pick, gcum_all, HI)
        for h in range(GDN_HEADS):
            q = qkv[:, h * GDN_DK:(h + 1) * GDN_DK]
            k = qkv[:, GDN_QK + h * GDN_DK:GDN_QK + (h + 1) * GDN_DK]
            v = qkv[:, 2 * GDN_QK + h * GDN_DV:2 * GDN_QK + (h + 1) * GDN_DV]
            q = q * lax.rsqrt(jnp.sum(q * q, axis=-1, keepdims=True) + RMS_EPS) * (GDN_DK ** -0.5)
            k = k * lax.rsqrt(jnp.sum(k * k, axis=-1, keepdims=True) + RMS_EPS)
            beta = beta_all[:, SM_B + h:SM_B + h + 1]
            gc_col = gcum_all[:, SM_A + h:SM_A + h + 1]
            gc_row = gcum_rows[h:h + 1, :]
            gc_last = gcum_all[c - 1:c, SM_A + h:SM_A + h + 1]
            units.append(dict(bb=bb, h=h, q=q, k=k, v=v, beta=beta, gc_col=gc_col, gc_last=gc_last,
                              decay=jnp.exp(jnp.where(tri_incl, gc_col - gc_row, NEG_INF)),
                              eg=jnp.exp(gc_col), kb=k * beta, k_bf=k.astype(BF16)))

    a_list = [jnp.where(tri_strict, _nt(un["kb"].astype(BF16), un["k_bf"]) * un["decay"], 0.0) for un in units]
    t_list = _unit_lower_inverses(a_list, c)
    u_coef = [_mmb(t, un["v"] * un["beta"]) for t, un in zip(t_list, units)]
    w_coef = [_mmb(t, un["kb"] * un["eg"]) for t, un in zip(t_list, units)]
    qk = [(_nt(un["q"].astype(BF16), un["k_bf"]) * un["decay"]).astype(BF16) for un in units]
    s_old = [st_ref[un["bb"], un["h"]] for un in units]
    s_bf = [s.astype(BF16) for s in s_old]
    uu = [uc - _mm(wc.astype(BF16), s) for uc, wc, s in zip(u_coef, w_coef, s_bf)]
    uu_bf = [x.astype(BF16) for x in uu]
    o_list = [_mm((un["q"] * un["eg"]).astype(BF16), s) + _mm(a, x)
              for un, s, a, x in zip(units, s_bf, qk, uu_bf)]
    for un, s, x, o in zip(units, s_old, uu_bf, o_list):
        bb, h = un["bb"], un["h"]
        k_tail = un["k"] * jnp.exp(un["gc_last"] - un["gc_col"])
        st_ref[bb, h] = s * jnp.exp(un["gc_last"]) + _tn(k_tail.astype(BF16), x)
        on = o * lax.rsqrt(jnp.mean(o * o, axis=-1, keepdims=True) + RMS_EPS) * ng
        y_ref[bb, :, h * GDN_DV:(h + 1) * GDN_DV] = on * _silu(z_ref[bb, :, h * GDN_DV:(h + 1) * GDN_DV])

    @pl.when(ci == pl.num_programs(1) - 1)
    def _():
        snew_ref[...] = st_ref[...]


def _gdn(qkv, z, sm, conv_buf, s0, conv_w, a_log, dt_bias, norm_g, b, l):
    c = math.gcd(l, GDN_CHUNK)
    nc = l // c
    nb = math.gcd(b, GDN_SEQS_PER_STEP)
    cbuf8 = jnp.pad(conv_buf, ((0, 0), (SUBLANES - (GDN_CONV - 1), 0), (0, 0)))
    pad_a = (SM_A, LANES - SM_A - GDN_HEADS)
    alog_row = jnp.pad(a_log, pad_a).reshape(1, LANES)
    dtb_row = jnp.pad(dt_bias, pad_a).reshape(1, LANES)
    row = lambda bi, ci: (bi, ci, 0)
    per_seq3 = lambda bi, ci: (bi, 0, 0)
    per_seq4 = lambda bi, ci: (bi, 0, 0, 0)
    fixed = lambda bi, ci: (0, 0)
    y, s_new, c_new = pl.pallas_call(
        functools.partial(_gdn_kernel, chunk=c, nb=nb),
        grid=(b // nb, nc),
        in_specs=[pl.BlockSpec((nb, c, GDN_CONV_DIM), row),
                  pl.BlockSpec((nb, c, GDN_V), row),
                  pl.BlockSpec((nb, c, LANES), row),
                  pl.BlockSpec((nb, SUBLANES, GDN_CONV_DIM), per_seq3),
                  pl.BlockSpec((nb, GDN_HEADS, GDN_DK, GDN_DV), per_seq4),
                  pl.BlockSpec((GDN_CONV, GDN_CONV_DIM), fixed),
                  pl.BlockSpec((1, LANES), fixed),
                  pl.BlockSpec((1, LANES), fixed),
                  pl.BlockSpec((1, GDN_DV), fixed)],
        out_specs=[pl.BlockSpec((nb, c, GDN_V), row),
                   pl.BlockSpec((nb, GDN_HEADS, GDN_DK, GDN_DV), per_seq4),
                   pl.BlockSpec((nb, SUBLANES, GDN_CONV_DIM), per_seq3)],
        out_shape=[jax.ShapeDtypeStruct((b, l, GDN_V), F32),
                   jax.ShapeDtypeStruct((b, GDN_HEADS, GDN_DK, GDN_DV), F32),
                   jax.ShapeDtypeStruct((b, SUBLANES, GDN_CONV_DIM), F32)],
        scratch_shapes=[pltpu.VMEM((nb, c + SUBLANES, GDN_CONV_DIM), F32),
                        pltpu.VMEM((nb, GDN_HEADS, GDN_DK, GDN_DV), F32)],
        compiler_params=_cparams(("parallel", "arbitrary")),
        name="gdn",
    )(qkv.reshape(b, l, GDN_CONV_DIM), z.reshape(b, l, GDN_V), sm.reshape(b, l, LANES), cbuf8, s0, conv_w,
      alog_row, dtb_row, norm_g.reshape(1, GDN_DV))
    return y.reshape(b * l, GDN_V), s_new, c_new[:, SUBLANES - (GDN_CONV - 1):]


def _pool_kernel(*refs, n_in):
    refs = refs[len(refs) - 2 * n_in - 3:]
    x_refs, pw_ref, head_ref, tail_ref = refs[:2 * n_in], refs[2 * n_in], refs[2 * n_in + 1], refs[2 * n_in + 2]
    rows = x_refs[0].shape[0]
    n = rows // CMP_STRIDE
    for j, x_ref in enumerate(x_refs):
        half = slice((j % 2) * NSA_KV, (j % 2 + 1) * NSA_KV)
        head = None
        tail = None
        for p in range(CMP_STRIDE):
            xr = x_ref[pl.ds(p, n, stride=CMP_STRIDE), :]
            hp = xr * pw_ref[p:p + 1, half]
            tp = xr * pw_ref[CMP_STRIDE + p:CMP_STRIDE + p + 1, half]
            head = hp if head is None else head + hp
            tail = tp if tail is None else tail + tp
        head_ref[(j // 2) * n:(j // 2 + 1) * n, half] = head
        tail_ref[(j // 2) * n:(j // 2 + 1) * n, half] = tail


def _pos_weights(pos_wk, pos_wv):
    return jnp.concatenate([jnp.broadcast_to(pos_wk[:, None], (CMP_LEN, NSA_KV)),
                            jnp.broadcast_to(pos_wv[:, None], (CMP_LEN, NSA_KV))], axis=1).astype(F32)


def _pool_rows(kv4, pw):
    n = kv4.shape[0]
    r = min(2048, n)
    w = 2 * NSA_KV
    return pl.pallas_call(
        functools.partial(_pool_kernel, n_in=1),
        grid=(n // r,),
        in_specs=[pl.BlockSpec((r, NSA_KV), lambda i: (i, 0)),
                  pl.BlockSpec((r, NSA_KV), lambda i: (i, 1)),
                  pl.BlockSpec((CMP_LEN, w), lambda i: (0, 0))],
        out_specs=[pl.BlockSpec((r // CMP_STRIDE, w), lambda i: (i, 0))] * 2,
        out_shape=[jax.ShapeDtypeStruct((n // CMP_STRIDE, w), F32)] * 2,
        compiler_params=_cparams(("parallel",)),
        name="pool_rows",
    )(kv4, kv4, pw)


def _pool_pages_kernel(*refs, pg):
    page_refs = refs[1:1 + pg]
    pk, pv, head_ref, tail_ref = refs[1 + pg:]
    x = jnp.concatenate([r[...] for r in page_refs], axis=1)
    n = head_ref.shape[1]
    for half, pool in enumerate((pk, pv)):
        xs = x[half * NSA_KV:(half + 1) * NSA_KV]
        out = _mm(xs.astype(BF16), pool[...])
        head_ref[half * NSA_KV:(half + 1) * NSA_KV, :] = out[:, :n]
        tail_ref[half * NSA_KV:(half + 1) * NSA_KV, :] = out[:, n:]


def _pool_matrix(pos_w, positions):
    chunks = positions // CMP_STRIDE
    p = jnp.arange(positions)[:, None]
    c = jnp.arange(chunks)[None, :]
    inside = (p // CMP_STRIDE) == c
    head = jnp.where(inside, pos_w[:CMP_STRIDE][p % CMP_STRIDE], 0.0)
    tail = jnp.where(inside, pos_w[CMP_STRIDE:][p % CMP_STRIDE], 0.0)
    return jnp.concatenate([head, tail], axis=1).astype(BF16)


def _pool_pages(cache_t, page_table, pos_wk, pos_wv):
    b, n_pages = page_table.shape
    page = cache_t.shape[2]
    pg = LANES * CMP_STRIDE // page
    w = 2 * NSA_KV
    steps = n_pages // pg
    n_ch = n_pages * page // CMP_STRIDE
    mats = (_pool_matrix(pos_wk, pg * page), _pool_matrix(pos_wv, pg * page))

    def page_map(k):
        return lambda bi, j, pt: (pt[bi, j * pg + k], 0, 0)

    fixed = lambda bi, j, pt: (0, 0)
    grid_spec = pltpu.PrefetchScalarGridSpec(
        num_scalar_prefetch=1,
        grid=(b, steps),
        in_specs=[pl.BlockSpec((None, w, page), page_map(k)) for k in range(pg)]
        + [pl.BlockSpec((pg * page, 2 * LANES), fixed)] * 2,
        out_specs=[pl.BlockSpec((None, w, LANES), lambda bi, j, pt: (bi, 0, j))] * 2,
    )
    return pl.pallas_call(
        functools.partial(_pool_pages_kernel, pg=pg),
        grid_spec=grid_spec,
        out_shape=[jax.ShapeDtypeStruct((b, w, n_ch), F32)] * 2,
        compiler_params=_cparams(("parallel", "arbitrary")),
        name="pool_pages",
    )(page_table, *([cache_t] * pg), *mats)


def _cmp_t_kernel(head_ref, tail_ref, wk_ref, wv_ref, kct_ref, vct_ref):
    n = head_ref.shape[1]
    blocks = head_ref[...] + pltpu.roll(tail_ref[...], n - 1, 1)
    kct_ref[...] = _mm(wk_ref[...], blocks[:NSA_KV].astype(BF16)).astype(BF16)
    vct_ref[...] = _mm(wv_ref[...], blocks[NSA_KV:].astype(BF16)).astype(BF16)


def _compress_t(head_t, tail_t, cmp_wk, cmp_wv):
    b, w, n_ch = head_t.shape
    seq = lambda i: (i, 0, 0)
    return pl.pallas_call(
        _cmp_t_kernel,
        grid=(b,),
        in_specs=[pl.BlockSpec((None, w, n_ch), seq)] * 2 + [pl.BlockSpec((NSA_KV, NSA_KV), lambda i: (0, 0))] * 2,
        out_specs=[pl.BlockSpec((None, NSA_KV, n_ch), seq)] * 2,
        out_shape=[jax.ShapeDtypeStruct((b, NSA_KV, n_ch), BF16)] * 2,
        compiler_params=_cparams(("parallel",)),
        name="compress_t",
    )(head_t, tail_t, jnp.transpose(_block_diag2(cmp_wk)), jnp.transpose(_block_diag2(cmp_wv)))


def _cmp_kernel(head_ref, tail_ref, wk_ref, wv_ref, kca_ref, vct_ref):
    n = head_ref.shape[0]
    blocks = head_ref[...] + pltpu.roll(tail_ref[...], n - 1, 0)
    kca_ref[:, :NSA_KV] = _mm(blocks[:, :NSA_KV].astype(BF16), wk_ref[...]).astype(BF16)
    blk_end = _iota((n, 1), 0) * CMP_STRIDE + (CMP_LEN - 1)
    kca_ref[:, NSA_KV:] = _pos_features(blk_end, (n, LANES))
    vct_ref[...] = _nt(wv_ref[...], blocks[:, NSA_KV:].astype(BF16)).astype(BF16)


def _block_diag2(w):
    z = jnp.zeros_like(w)
    return jnp.concatenate([jnp.concatenate([w, z], axis=1), jnp.concatenate([z, w], axis=1)], axis=0).astype(BF16)


def _compress(head, tail, cmp_wk, cmp_wv, b):
    n_ch = head.shape[0] // b
    w = 2 * NSA_KV
    return pl.pallas_call(
        _cmp_kernel,
        grid=(b,),
        in_specs=[pl.BlockSpec((n_ch, w), lambda i: (i, 0))] * 2
        + [pl.BlockSpec((NSA_KV, NSA_KV), lambda i: (0, 0))] * 2,
        out_specs=[pl.BlockSpec((n_ch, w), lambda i: (i, 0)),
                   pl.BlockSpec((NSA_KV, n_ch), lambda i: (0, i))],
        out_shape=[jax.ShapeDtypeStruct((b * n_ch, w), BF16),
                   jax.ShapeDtypeStruct((NSA_KV, b * n_ch), BF16)],
        compiler_params=_cparams(("parallel",)),
        name="compress",
    )(head, tail, _block_diag2(cmp_wk), jnp.transpose(_block_diag2(cmp_wv)))


def _overlap(n_ch, nsp):
    i = jnp.arange(n_ch)[:, None]
    j = jnp.arange(nsp)[None, :]
    lo = jnp.maximum(i * CMP_STRIDE, j * SEL_BLOCK)
    hi = jnp.minimum(i * CMP_STRIDE + CMP_LEN, (j + 1) * SEL_BLOCK)
    ov = jnp.maximum(hi - lo, 0).astype(F32) / CMP_LEN
    return jnp.where(i < n_ch - 1, ov, 0.0)


def _slope(h):
    return 2.0 ** (-(h + 1))


def _softmax_rows(s, mask):
    s = jnp.where(mask, s, NEG_INF)
    m = jnp.max(s, axis=-1, keepdims=True)
    m = jnp.where(m == NEG_INF, 0.0, m)
    e = jnp.where(mask, jnp.exp(s - m), 0.0)
    return e, jnp.maximum(jnp.sum(e, axis=-1, keepdims=True), 1e-30)


def _top_blocks(imp, qpos, ns_lanes):
    blk = _iota(imp.shape, 1)
    cur = qpos >> 6
    forced = (blk == 0) | (blk == cur) | (blk == cur - 1)
    score = jnp.where(forced, FORCE_SCORE, imp)
    work = jnp.where(blk <= cur, score, NEG_INF)
    sel = jnp.zeros(imp.shape, F32)
    for _ in range(SEL_TOP):
        m = jnp.max(work, axis=-1, keepdims=True)
        cand = jnp.where((work == m) & (m > NEG_INF), blk, ns_lanes)
        first = jnp.min(cand, axis=-1, keepdims=True)
        hit = blk == first
        sel = jnp.where(hit, 1.0, sel)
        work = jnp.where(hit, NEG_INF, work)
    return sel


def _row_slopes(tq):
    head = _iota((NSA_HEADS * tq, 1), 0) >> (tq.bit_length() - 1)
    slope = jnp.zeros((NSA_HEADS * tq, 1), F32)
    for h in range(NSA_HEADS):
        slope = jnp.where(head == h, _slope(h), slope)
    return slope


def _rows_attend(q8, kt, vt, slope, dist, mask):
    s = _mm(q8, kt) - slope * dist.astype(F32)
    e, den = _softmax_rows(s, mask)
    p = e / den
    return _nt(p.astype(BF16), vt), p


def _rows_online(q8, kt, vt, slope, dist, mask, m_ref, l_ref, acc_ref):
    s = jnp.where(mask, _mm(q8, kt) - slope * dist.astype(F32), NEG_INF)
    m_old = m_ref[...]
    m_new = jnp.maximum(m_old, jnp.max(s, axis=-1, keepdims=True))
    m_use = jnp.where(m_new == NEG_INF, 0.0, m_new)
    alpha = jnp.exp(m_old - m_use)
    e = jnp.exp(s - m_use)
    l_ref[...] = alpha * l_ref[...] + jnp.sum(e, axis=-1, keepdims=True)
    acc_ref[...] = alpha * acc_ref[...] + _nt(e.astype(BF16), vt)
    m_ref[...] = m_new


def _merge_heads(o_c, o_s, o_w, gates, tq):
    lane = _iota((tq, LANES), 1)
    mixed = []
    for h in range(NSA_HEADS):
        rows = slice(h * tq, (h + 1) * tq)
        g0 = gates[:, SM_G + 3 * h:SM_G + 3 * h + 1]
        g1 = gates[:, SM_G + 3 * h + 1:SM_G + 3 * h + 2]
        g2 = gates[:, SM_G + 3 * h + 2:SM_G + 3 * h + 3]
        mixed.append(g0 * o_c[rows] + g1 * o_s[rows] + g2 * o_w[rows])
    cols = []
    for p in range(NSA_HEADS // 2):
        a, b = mixed[2 * p], mixed[2 * p + 1]
        if 2 * p < NSA_GROUP:
            cols.append(jnp.where(lane < NSA_HD, a, pltpu.roll(b, NSA_HD, 1)))
        else:
            cols.append(jnp.where(lane < NSA_HD, pltpu.roll(a, NSA_HD, 1), b))
    return jnp.concatenate(cols, axis=1)


def _softmax_cols(s, mask):
    s = jnp.where(mask, s, NEG_INF)
    m = jnp.max(s, axis=0, keepdims=True)
    m = jnp.where(m == NEG_INF, 0.0, m)
    e = jnp.exp(s - m)
    return e * (1.0 / jnp.maximum(jnp.sum(e, axis=0, keepdims=True), 1e-30))


def _split3(x):
    hi = x.astype(BF16)
    r1 = x - hi.astype(F32)
    mid = r1.astype(BF16)
    lo = (r1 - mid.astype(F32)).astype(BF16)
    return hi, mid, lo


def _top_blocks_cols(imp, qpos, nsp):
    blk = _iota(imp.shape, 0)
    cur = qpos >> 6
    forced = (blk == 0) | (blk == cur) | (blk == cur - 1)
    work = jnp.where(blk <= cur, jnp.where(forced, FORCE_SCORE, imp), NEG_INF)
    neg = jnp.full(imp.shape, NEG_INF, F32)
    for _ in range(SEL_TOP):
        m = jnp.max(work, axis=0, keepdims=True)
        cand = jnp.where((work == m) & (m > NEG_INF), blk, nsp)
        hit = blk == jnp.min(cand, axis=0, keepdims=True)
        neg = jnp.where(hit, 0.0, neg)
        work = jnp.where(hit, NEG_INF, work)
    return neg


def _nsa_prompt_kernel(q8_ref, sm_ref, kc_ref, vct_ref, ks_ref, vst_ref, kw_ref, vwt_ref, ovt_ref, y_ref,
                       neg_ref, m_ref, acc_ref, qta_ref, sa_ref, sb_ref, act_ref, *, tq, tk):
    i = pl.program_id(1)
    start = i * tq
    r = NSA_HEADS * tq
    nsp, n_ch = ovt_ref.shape
    col = _iota((1, r), 1)
    qpos = start + (col & (tq - 1))

    q8 = jnp.concatenate([q8_ref[:, h * LANES:(h + 1) * LANES] for h in range(NSA_HEADS)], axis=0)
    eye = (_iota((LANES, LANES), 0) == _iota((LANES, LANES), 1)).astype(BF16)
    qt = _nt(eye, q8).astype(BF16)
    head = col >> (tq.bit_length() - 1)
    slope = jnp.zeros((1, r), F32)
    for h in range(NSA_HEADS):
        slope = jnp.where(head == h, _slope(h), slope)
    frow = _iota((2 * SUBLANES, r), 0)
    feat = jnp.where(frow < 2, slope, jnp.where(frow == 2, -slope * ((qpos >> 7) * LANES).astype(F32),
                                                jnp.where(frow == 3, -slope * (qpos & (LANES - 1)).astype(F32), 0.0)))
    qta = jnp.concatenate([qt, feat.astype(BF16), jnp.zeros((LANES - 2 * SUBLANES, r), BF16)], axis=0)

    blk_end = _iota((n_ch, 1), 0) * CMP_STRIDE + (CMP_LEN - 1)
    p = _softmax_cols(_mm(kc_ref[...], qta), blk_end <= qpos)
    o_c = _mm(vct_ref[...], p.astype(BF16))
    psum = []
    for g in range(NSA_KV_HEADS):
        acc = p[:, g * NSA_GROUP * tq:(g * NSA_GROUP + 1) * tq]
        for hh in range(1, NSA_GROUP):
            acc = acc + p[:, (g * NSA_GROUP + hh) * tq:(g * NSA_GROUP + hh + 1) * tq]
        psum.append(acc)
    ovt = ovt_ref[...]
    imp = sum(_mm(ovt, piece) for piece in _split3(jnp.concatenate(psum, axis=1)))
    qpos2 = start + (_iota((1, NSA_KV_HEADS * tq), 1) & (tq - 1))
    neg = _top_blocks_cols(imp, qpos2, nsp)
    for h in range(NSA_HEADS):
        g = h // NSA_GROUP
        neg_ref[:, h * tq:(h + 1) * tq] = neg[:, g * tq:(g + 1) * tq]

    m_ref[...] = jnp.full(m_ref.shape, NEG_INF, F32)
    acc_ref[...] = jnp.zeros(acc_ref.shape, F32)
    bpt = tk // SEL_BLOCK

    qta_ref[...] = qta

    def scores(t, buf):
        off = pl.multiple_of(t * tk, tk)
        buf[...] = _mm(ks_ref[pl.ds(off, tk), :], qta_ref[...])

    def tile(t, buf, causal):
        off = pl.multiple_of(t * tk, tk)
        parts = []
        for j in range(bpt):
            sj = buf[j * SEL_BLOCK:(j + 1) * SEL_BLOCK, :]
            neg = neg_ref[pl.ds(t * bpt + j, 1), :]
            if causal:
                sj = jnp.where(off + j * SEL_BLOCK + _iota((SEL_BLOCK, 1), 0) <= qpos, sj + neg, NEG_INF)
                parts.append(sj.astype(BF16))
            else:
                parts.append(sj.astype(BF16) + neg.astype(BF16))
        s = jnp.concatenate(parts, axis=0)
        m_old = m_ref[...]
        m_new = jnp.maximum(m_old, jnp.max(s, axis=0, keepdims=True).astype(F32))
        m_use = jnp.where(m_new == NEG_INF, 0.0, m_new)
        alpha = jnp.exp(m_old - m_use)
        e = jnp.exp(s - m_use.astype(BF16))
        acc_ref[...] = alpha * acc_ref[...] + _mm(vst_ref[:, pl.ds(off, tk)], e)
        m_ref[...] = m_new

    n_full = start // tk
    blk_any = jnp.max(neg, axis=1, keepdims=True)
    cnt = jnp.int32(0)
    for t in range(ks_ref.shape[0] // tk):
        live = (jnp.max(blk_any[t * bpt:(t + 1) * bpt]) > NEG_INF) & (t < n_full)
        act_ref[cnt] = t
        cnt = cnt + live.astype(jnp.int32)
    act_ref[cnt] = n_full
    pairs = cnt // 2
    scores(act_ref[0], sa_ref)

    def body(u, carry):
        scores(act_ref[2 * u + 1], sb_ref)
        tile(act_ref[2 * u], sa_ref, False)
        scores(act_ref[2 * u + 2], sa_ref)
        tile(act_ref[2 * u + 1], sb_ref, False)
        return carry

    lax.fori_loop(0, pairs, body, 0)

    @pl.when(cnt % 2 == 1)
    def _():
        scores(n_full, sb_ref)
        tile(act_ref[cnt - 1], sa_ref, False)
        tile(n_full, sb_ref, True)

    @pl.when(cnt % 2 == 0)
    def _():
        tile(n_full, sa_ref, True)
    o_s = acc_ref[:NSA_KV, :] * (1.0 / jnp.maximum(acc_ref[NSA_KV:NSA_KV + 1, :], 1e-30))

    wl = tq + WINDOW
    ws = pl.multiple_of(jnp.maximum(start - WINDOW, 0), tq)
    dist = qpos - (ws + _iota((wl, 1), 0))
    sw = jnp.where((dist >= 0) & (dist <= WINDOW), _mm(kw_ref[pl.ds(ws, wl), :], qta), NEG_INF).astype(BF16)
    ew = jnp.exp(sw - jnp.max(sw, axis=0, keepdims=True))
    ow = _mm(vwt_ref[:, pl.ds(ws, wl)], ew)
    o_w = ow[:NSA_KV] * (1.0 / jnp.maximum(ow[NSA_KV:NSA_KV + 1], 1e-30))

    gt = jnp.transpose(_sigmoid(sm_ref[...]))
    gate = [jnp.concatenate([gt[SM_G + 3 * h + c:SM_G + 3 * h + c + 1, :] for h in range(NSA_HEADS)], axis=1)
            for c in range(3)]
    o = gate[0] * o_c + gate[1] * o_s + gate[2] * o_w
    yt = jnp.concatenate([o[(h // NSA_GROUP) * NSA_HD:(h // NSA_GROUP + 1) * NSA_HD, h * tq:(h + 1) * tq]
                          for h in range(NSA_HEADS)], axis=0)
    y_ref[...] = jnp.transpose(yt)


def _nsa_prompt(q8, sm, kca, vct, ksa, vst, kwa, vwt, b, l):
    tq = 128
    tk = 512
    assert l % tk == 0 and l >= tq + WINDOW
    n_ch = l // CMP_STRIDE
    ns = -(-l // SEL_BLOCK)
    nsp = -(-ns // LANES) * LANES
    nq = l // tq
    w = 2 * NSA_KV
    ovt = jnp.transpose(_overlap(n_ch, nsp)).astype(BF16)
    row = lambda bi, i: (bi * nq + i, 0)
    seq_rows = lambda bi, i: (bi, 0)
    seq_cols = lambda bi, i: (0, bi)
    return pl.pallas_call(
        functools.partial(_nsa_prompt_kernel, tq=tq, tk=tk),
        grid=(b, nq),
        in_specs=[pl.BlockSpec((tq, NSA_HEADS * LANES), row),
                  pl.BlockSpec((tq, LANES), row),
                  pl.BlockSpec((n_ch, w), seq_rows),
                  pl.BlockSpec((NSA_KV, n_ch), seq_cols),
                  pl.BlockSpec((l, w), seq_rows),
                  pl.BlockSpec((VS_ROWS, l), seq_cols),
                  pl.BlockSpec((l, w), seq_rows),
                  pl.BlockSpec((VS_ROWS, l), seq_cols),
                  pl.BlockSpec((nsp, n_ch), lambda bi, i: (0, 0))],
        out_specs=pl.BlockSpec((tq, NSA_Q), row),
        out_shape=jax.ShapeDtypeStruct((b * l, NSA_Q), F32),
        scratch_shapes=[pltpu.VMEM((nsp, NSA_HEADS * tq), F32),
                        pltpu.VMEM((1, NSA_HEADS * tq), F32),
                        pltpu.VMEM((VS_ROWS, NSA_HEADS * tq), F32),
                        pltpu.VMEM((2 * NSA_KV, NSA_HEADS * tq), BF16),
                        pltpu.VMEM((tk, NSA_HEADS * tq), F32),
                        pltpu.VMEM((tk, NSA_HEADS * tq), F32),
                        pltpu.SMEM((l // tk + 1,), jnp.int32)],
        compiler_params=_cparams(("parallel", "arbitrary")),
        name="nsa_prompt",
    )(q8, sm, kca, vct, ksa, vst, kwa, vwt, ovt)


def _nsa_sample_kernel(*refs, pg, page, tq, past_len):
    pt_ref = refs[0]
    del pt_ref
    page_refs = refs[1:1 + pg]
    (q8_ref, sm_ref, kc_ref, vc_ref, knew_ref, kw_ref, ov_ref, y_ref,
     m_ref, l_ref, acc_ref, sel_ref, oc_ref) = refs[1 + pg:]
    j = pl.program_id(1)
    nsp = ov_ref.shape[1]
    r = NSA_HEADS * tq
    q8 = jnp.concatenate([q8_ref[:, h * LANES:(h + 1) * LANES].astype(F32) for h in range(NSA_HEADS)],
                         axis=0).astype(BF16)
    qpos = past_len + (_iota((r, 1), 0) & (tq - 1))
    slope = _row_slopes(tq)

    @pl.when(j == 0)
    def _():
        n_ch = kc_ref.shape[1]
        dist = qpos - (_iota((1, n_ch), 1) * CMP_STRIDE + (CMP_LEN - 1))
        o_c, p = _rows_attend(q8, kc_ref[...], vc_ref[...], slope, dist, dist >= 0)
        oc_ref[...] = o_c
        psums = []
        for g in range(NSA_KV_HEADS):
            psum = p[g * NSA_GROUP * tq:(g * NSA_GROUP + 1) * tq]
            for hh in range(1, NSA_GROUP):
                psum = psum + p[(g * NSA_GROUP + hh) * tq:(g * NSA_GROUP + hh + 1) * tq]
            psums.append(psum)
        sel = _top_blocks(_mm(jnp.concatenate(psums, axis=0), ov_ref[...], HI), qpos[:NSA_KV_HEADS * tq], nsp)
        for h in range(NSA_HEADS):
            g = h // NSA_GROUP
            sel_ref[h * tq:(h + 1) * tq, :] = sel[g * tq:(g + 1) * tq]
        m_ref[...] = jnp.full(m_ref.shape, NEG_INF, F32)
        l_ref[...] = jnp.zeros(l_ref.shape, F32)
        acc_ref[...] = jnp.zeros(acc_ref.shape, F32)

    sel_bf = sel_ref[...].astype(BF16)

    def sel_step(kv, kpos):
        n = kv.shape[1]
        expand = jnp.where(_iota((nsp, n), 0) == (kpos >> 6), 1.0, 0.0).astype(BF16)
        dist = qpos - kpos
        mask = (_mm(sel_bf, expand) > 0.5) & (dist >= 0)
        _rows_online(q8, kv[:NSA_KV].astype(BF16), kv[NSA_KV:].astype(BF16), slope, dist, mask,
                     m_ref, l_ref, acc_ref)

    sel_step(jnp.concatenate([page_refs[k][...] for k in range(pg)], axis=1),
             j * (pg * page) + _iota((1, pg * page), 1))

    @pl.when(j == pl.num_programs(1) - 1)
    def _():
        sel_step(knew_ref[...], past_len + _iota((1, knew_ref.shape[1]), 1))
        o_s = acc_ref[...] / jnp.maximum(l_ref[...], 1e-30)
        kw = kw_ref[...]
        kwpos = past_len - WINDOW + _iota((1, kw.shape[1]), 1)
        dist = qpos - kwpos
        mask = (dist >= 0) & (dist <= WINDOW) & (kwpos >= 0) & (kwpos < past_len + tq)
        o_w, _ = _rows_attend(q8, kw[:NSA_KV].astype(BF16), kw[NSA_KV:].astype(BF16), slope, dist, mask)
        y_ref[...] = _merge_heads(oc_ref[...], o_s, o_w, _sigmoid(sm_ref[...]), tq)


def _nsa_sample(q8, sm, kct, vct, knew_t, win_t, cache_t, page_table, b, l):
    n_pages = page_table.shape[1]
    page = cache_t.shape[2]
    past_len = n_pages * page
    assert l <= SEL_BLOCK and past_len % SEL_BLOCK == 0 and past_len >= WINDOW
    pg = 16
    steps = n_pages // pg
    n_ch = past_len // CMP_STRIDE
    ns = -(-(past_len + l) // SEL_BLOCK)
    nsp = -(-ns // LANES) * LANES
    w = 2 * NSA_KV
    ov = _overlap(n_ch, nsp)
    wcols = win_t.shape[2]

    def page_map(k):
        return lambda bi, j, pt: (pt[bi, j * pg + k], 1, 0)

    per_b = lambda bi, j, pt: (bi, 0)
    per_b3 = lambda bi, j, pt: (bi, 0, 0)
    grid_spec = pltpu.PrefetchScalarGridSpec(
        num_scalar_prefetch=1,
        grid=(b, steps),
        in_specs=[pl.BlockSpec((None, w, page), page_map(k)) for k in range(pg)]
        + [pl.BlockSpec((None, l, NSA_HEADS * LANES), per_b3),
           pl.BlockSpec((l, LANES), per_b),
           pl.BlockSpec((None, NSA_KV, n_ch), per_b3),
           pl.BlockSpec((None, NSA_KV, n_ch), per_b3),
           pl.BlockSpec((None, w, LANES), per_b3),
           pl.BlockSpec((None, w, wcols), per_b3),
           pl.BlockSpec((n_ch, nsp), lambda bi, j, pt: (0, 0))],
        out_specs=pl.BlockSpec((l, NSA_Q), per_b),
        scratch_shapes=[pltpu.VMEM((NSA_HEADS * l, 1), F32),
                        pltpu.VMEM((NSA_HEADS * l, 1), F32),
                        pltpu.VMEM((NSA_HEADS * l, NSA_KV), F32),
                        pltpu.VMEM((NSA_HEADS * l, nsp), F32),
                        pltpu.VMEM((NSA_HEADS * l, NSA_KV), F32)],
    )
    return pl.pallas_call(
        functools.partial(_nsa_sample_kernel, pg=pg, page=page, tq=l, past_len=past_len),
        grid_spec=grid_spec,
        out_shape=jax.ShapeDtypeStruct((b * l, NSA_Q), F32),
        compiler_params=_cparams(("parallel", "arbitrary")),
        name="nsa_sample",
    )(page_table, *([cache_t] * pg), q8.reshape(b, l, NSA_HEADS * LANES), sm, kct, vct, knew_t, win_t, ov)


def _tail_kernel(x_ref, ya_ref, yb_ref, ga_ref, gb_ref, wa_ref, wb_ref, wo_ref, n2_ref, wr_ref, br_ref,
                 h_ref, hn_ref, cmb_ref):
    ma = _mm(ya_ref[...].astype(BF16), wa_ref[...])
    mb = _mm(yb_ref[...].astype(BF16), wb_ref[...])
    m = _sigmoid(ga_ref[...]) * ma + _sigmoid(gb_ref[...]) * mb
    h = x_ref[...] + _mm(m.astype(BF16), wo_ref[...])
    h_ref[...] = h
    hn = h * lax.rsqrt(jnp.mean(h * h, axis=-1, keepdims=True) + RMS_EPS) * n2_ref[...]
    hn_hi = hn.astype(BF16)
    hn_ref[...] = hn_hi

    hn_lo = (hn - hn_hi.astype(F32)).astype(BF16)
    logit = _mm(hn_hi, wr_ref[0]) + _mm(hn_lo, wr_ref[0]) + _mm(hn_hi, wr_ref[1]) + br_ref[...]
    lane = _iota(logit.shape, 1)
    is_grp = (lane >= N_EXPERTS) & (lane < N_EXPERTS + N_GROUPS)
    gl = jnp.where(is_grp, logit, NEG_INF)
    gmax = jnp.max(gl, axis=-1, keepdims=True)
    gidx = jnp.min(jnp.where(gl == gmax, lane, LANES), axis=-1, keepdims=True) - N_EXPERTS
    p_grp = 1.0 / jnp.sum(jnp.exp(gl - gmax), axis=-1, keepdims=True)
    el = jnp.where((lane >> 3) == gidx, logit, NEG_INF)
    t1 = jnp.max(el, axis=-1, keepdims=True)
    i1 = jnp.min(jnp.where(el == t1, lane, LANES), axis=-1, keepdims=True)
    el2 = jnp.where(lane == i1, NEG_INF, el)
    t2 = jnp.max(el2, axis=-1, keepdims=True)
    i2 = jnp.min(jnp.where(el2 == t2, lane, LANES), axis=-1, keepdims=True)
    e2 = jnp.exp(t2 - t1)
    w1 = p_grp / (1.0 + e2)
    w2 = p_grp * e2 / (1.0 + e2)
    cmb_ref[...] = jnp.where(lane == i1, w1, jnp.where(lane == i2, w2,
                                                        jnp.where(lane == gidx + N_EXPERTS, 1.0, 0.0)))


def _tail(x2d, ya, yb, ga, gb, wa, wb, wo, norm2_g, w_route, b_route):
    n = x2d.shape[0]
    tm = min(ROW_TILE, n)
    row = lambda i: (i, 0)
    fixed = lambda i: (0, 0)
    return pl.pallas_call(
        _tail_kernel,
        grid=(n // tm,),
        in_specs=[pl.BlockSpec((tm, D_MODEL), row),
                  pl.BlockSpec((tm, GDN_V), row),
                  pl.BlockSpec((tm, NSA_Q), row),
                  pl.BlockSpec((tm, D_MODEL), row),
                  pl.BlockSpec((tm, D_MODEL), row),
                  pl.BlockSpec((GDN_V, D_MODEL), fixed),
                  pl.BlockSpec((NSA_Q, D_MODEL), fixed),
                  pl.BlockSpec((D_MODEL, D_MODEL), fixed),
                  pl.BlockSpec((1, D_MODEL), fixed),
                  pl.BlockSpec((2, D_MODEL, LANES), lambda i: (0, 0, 0)),
                  pl.BlockSpec((1, LANES), fixed)],
        out_specs=[pl.BlockSpec((tm, D_MODEL), row),
                   pl.BlockSpec((tm, D_MODEL), row),
                   pl.BlockSpec((tm, LANES), row)],
        out_shape=[jax.ShapeDtypeStruct((n, D_MODEL), F32),
                   jax.ShapeDtypeStruct((n, D_MODEL), BF16),
                   jax.ShapeDtypeStruct((n, LANES), F32)],
        compiler_params=_cparams(("parallel",)),
        name="tail",
    )(x2d, ya, yb, ga, gb, wa, wb, wo, norm2_g.reshape(1, D_MODEL), w_route, b_route)


MOE_SLABS = 2 * N_GROUPS
MOE_SLAB_EXPERTS = N_EXPERTS // MOE_SLABS


def _moe_kernel(offs_ref, cnts_ref, h_ref, hn_ref, cmb_ref, wg_ref, wu_ref, wd_ref, nf_ref, y_ref,
                xs_ref, cs_ref, pt_ref, acc_ref, *, rb):
    i = pl.program_id(0)
    sl = pl.program_id(1)
    t = hn_ref.shape[0]
    lane = _iota((t, LANES), 1)

    @pl.when(sl == 0)
    def _():
        cmb = cmb_ref[...]
        oh = jnp.where((lane >= N_EXPERTS) & (lane < N_EXPERTS + N_GROUPS), cmb, 0.0)
        oh_bf = oh.astype(BF16)
        r_i = _iota((t, t), 0)
        c_i = _iota((t, t), 1)
        lt = _iota((LANES, LANES), 0) < _iota((LANES, LANES), 1)
        before = _mm(_ones_where(c_i < r_i), oh_bf)
        totals = jnp.broadcast_to(jnp.sum(oh, axis=0, keepdims=True), (SUBLANES, LANES))
        smaller = _mm(totals, lt.astype(F32), HI)[0:1]
        rank_col = jnp.sum((before + smaller) * oh, axis=1, keepdims=True)
        eye = (_iota((LANES, LANES), 0) == _iota((LANES, LANES), 1)).astype(BF16)
        oht = _nt(eye, oh_bf)
        before_t = _mm(oht.astype(BF16), _ones_where(r_i < c_i))
        totals_t = jnp.broadcast_to(jnp.sum(oht, axis=1, keepdims=True), (LANES, LANES))
        gt = _iota((LANES, LANES), 0) > _iota((LANES, LANES), 1)
        smaller_t = _mm(gt.astype(F32), totals_t, HI)[:, 0:1]
        rank_row = jnp.sum((before_t + smaller_t) * oht, axis=0, keepdims=True)
        perm = _ones_where(r_i == rank_row.astype(jnp.int32))
        pt_ref[...] = _ones_where(c_i == rank_col.astype(jnp.int32))
        xs_ref[...] = _mm(perm, hn_ref[...]).astype(BF16)
        c_hi = cmb.astype(BF16)
        c_lo = (cmb - c_hi.astype(F32)).astype(BF16)
        cs_ref[...] = _mm(perm, c_hi) + _mm(perm, c_lo)
        acc_ref[...] = jnp.zeros(acc_ref.shape, F32)

    g = sl // (MOE_SLABS // N_GROUPS)
    off = offs_ref[i, g]
    row0 = (off // (2 * SUBLANES)) * (2 * SUBLANES)
    nblk = (off + cnts_ref[i, g] - row0 + rb - 1) // rb
    lane_b = _iota((rb, LANES), 1)

    def block(j, carry):
        lo = row0 + j * rb
        r0 = pl.multiple_of(jnp.minimum(lo, t - rb), 2 * SUBLANES)
        x = xs_ref[pl.ds(r0, rb), :]
        cw = jnp.where(r0 + _iota((rb, 1), 0) >= lo, cs_ref[pl.ds(r0, rb), :], 0.0)
        parts = []
        for e in range(MOE_SLAB_EXPERTS):
            wgt = jnp.sum(jnp.where(lane_b == sl * MOE_SLAB_EXPERTS + e, cw, 0.0), axis=1, keepdims=True)
            parts.append((_silu(_mm(x, wg_ref[e])) * _mm(x, wu_ref[e]) * wgt).astype(BF16))
        acc_ref[pl.ds(r0, rb), :] += _mm(jnp.concatenate(parts, axis=1), wd_ref[0])
        return carry

    lax.fori_loop(0, nblk, block, 0)

    @pl.when(sl == pl.num_programs(1) - 1)
    def _():
        acc = acc_ref[...]
        a_hi = acc.astype(BF16)
        a_lo = (acc - a_hi.astype(F32)).astype(BF16)
        pt = pt_ref[...]
        v = h_ref[...] + _mm(pt, a_hi) + _mm(pt, a_lo)
        y_ref[...] = v * lax.rsqrt(jnp.mean(v * v, axis=-1, keepdims=True) + RMS_EPS) * nf_ref[...]


def _moe_weights(w_gate, w_up, w_down):
    return (w_gate.astype(BF16), w_up.astype(BF16),
            w_down.astype(BF16).reshape(MOE_SLABS, MOE_SLAB_EXPERTS * D_EXPERT, D_MODEL))


def _moe(h, hn, cmb, wg, wu, wd, norm_f_g):
    n = h.shape[0]
    t = min(1024, n)
    rb = min(256, t)
    tiles = n // t
    cnts = jnp.sum(cmb[:, N_EXPERTS:N_EXPERTS + N_GROUPS].reshape(tiles, t, N_GROUPS), axis=1).astype(jnp.int32)
    offs = jnp.cumsum(cnts, axis=1) - cnts
    ws = MOE_SLAB_EXPERTS * D_EXPERT
    row = lambda i, s, o, c: (i, 0)
    slab = lambda i, s, o, c: (s, 0, 0)
    grid_spec = pltpu.PrefetchScalarGridSpec(
        num_scalar_prefetch=2,
        grid=(tiles, MOE_SLABS),
        in_specs=[pl.BlockSpec((t, D_MODEL), row),
                  pl.BlockSpec((t, D_MODEL), row),
                  pl.BlockSpec((t, LANES), row),
                  pl.BlockSpec((MOE_SLAB_EXPERTS, D_MODEL, D_EXPERT), slab),
                  pl.BlockSpec((MOE_SLAB_EXPERTS, D_MODEL, D_EXPERT), slab),
                  pl.BlockSpec((1, ws, D_MODEL), slab),
                  pl.BlockSpec((1, D_MODEL), lambda i, s, o, c: (0, 0))],
        out_specs=pl.BlockSpec((t, D_MODEL), row),
        scratch_shapes=[pltpu.VMEM((t, D_MODEL), BF16),
                        pltpu.VMEM((t, LANES), F32),
                        pltpu.VMEM((t, t), BF16),
                        pltpu.VMEM((t, D_MODEL), F32)],
    )
    return pl.pallas_call(
        functools.partial(_moe_kernel, rb=rb),
        grid_spec=grid_spec,
        out_shape=jax.ShapeDtypeStruct((n, D_MODEL), F32),
        compiler_params=_cparams(("parallel", "arbitrary")),
        name="moe",
    )(offs, cnts, h, hn, cmb, wg, wu, wd, norm_f_g.reshape(1, D_MODEL))


def _route_weights(w_grp, b_grp, w_rt, b_rt):
    pad = LANES - N_EXPERTS - N_GROUPS
    w = jnp.concatenate([w_rt, w_grp, jnp.zeros((D_MODEL, pad), F32)], axis=1)
    bias = jnp.concatenate([b_rt, b_grp, jnp.zeros((pad,), F32)]).reshape(1, LANES)
    hi = w.astype(BF16)
    return jnp.stack([hi, (w - hi.astype(F32)).astype(BF16)]), bias


def _finish(x2d, ya, yb, ga, gb, wts):
    h, hn, cmb = _tail(x2d, ya, yb, ga, gb, wts["wa"], wts["wb"], wts["wo"], wts["norm2_g"],
                       wts["w_route"], wts["b_route"])
    return _moe(h, hn, cmb, wts["wg"], wts["wu"], wts["wd"], wts["norm_f_g"])


def _prompt_layer(x, wts):
    b, l, _ = x.shape
    x2d = x.reshape(b * l, D_MODEL)
    qkv, z, q8, ga, gb, sm, kv4t, kvwt, cmp_rows, ksa, kwa, vst, vwt = _project(
        x2d, wts["norm1_g"], wts["w_pk"], b, l)
    conv0 = jnp.zeros((b, GDN_CONV - 1, GDN_CONV_DIM), F32)
    s0 = jnp.zeros((b, GDN_HEADS, GDN_DK, GDN_DV), F32)
    ya, s_new, conv_new = _gdn(qkv, z, sm, conv0, s0, wts["conv_w"], wts["a_log"], wts["dt_bias"],
                               wts["gdn_norm_g"], b, l)
    head, tail = _pool_rows(cmp_rows, wts["pw"])
    kca, vct = _compress(head, tail, wts["cmp_wk"], wts["cmp_wv"], b)
    yb = _nsa_prompt(q8, sm, kca, vct, ksa, vst, kwa, vwt, b, l)
    y = _finish(x2d, ya, yb, ga, gb, wts)
    win_buf = min(WINDOW, l)
    kv_new = jnp.transpose(kv4t.reshape(b, 4, NSA_KV_HEADS, NSA_HD, l), (0, 4, 1, 2, 3))
    win_new = jnp.transpose(kvwt[:, :, l - win_buf:].reshape(b, 2, NSA_KV_HEADS, NSA_HD, win_buf), (0, 4, 1, 2, 3))
    return (y.reshape(b, l, D_MODEL), kv_new, win_new, s_new, conv_new)


def _sample_layer(x, cache_kv_l, page_table, cache_win_l, s0, conv_buf, wts):
    b, l, _ = x.shape
    x2d = x.reshape(b * l, D_MODEL)
    qkv, z, q8, ga, gb, sm, kv4, kvw = _project(x2d, wts["norm1_g"], wts["w_pk"], b, l)
    ya, s_new, conv_new = _gdn(qkv, z, sm, conv_buf, s0, wts["conv_w"], wts["a_log"], wts["dt_bias"],
                               wts["gdn_norm_g"], b, l)
    n_phys, page = cache_kv_l.shape[:2]
    w = 2 * NSA_KV
    cache_t = jnp.transpose(cache_kv_l, (0, 2, 3, 4, 1)).reshape(n_phys, 2 * w, page)
    win_buf = cache_win_l.shape[1]
    win_old_t = jnp.transpose(cache_win_l, (0, 2, 3, 4, 1)).reshape(b, w, win_buf)
    head_t, tail_t = _pool_pages(cache_t, page_table, wts["pos_wk"], wts["pos_wv"])
    kct, vct = _compress_t(head_t, tail_t, wts["cmp_wk"], wts["cmp_wv"])
    new_t = jnp.transpose(kv4.reshape(b, l, 2 * w), (0, 2, 1))
    knew_t = jnp.pad(new_t[:, w:], ((0, 0), (0, 0), (0, LANES - l)))
    win_new_t = jnp.transpose(kvw.reshape(b, l, w), (0, 2, 1))
    wcols = -(-(win_buf + l) // LANES) * LANES
    win_t = jnp.concatenate([win_old_t, win_new_t, jnp.zeros((b, w, wcols - win_buf - l), F32)], axis=2)
    yb = _nsa_sample(q8, sm, kct, vct, knew_t, win_t, cache_t, page_table, b, l)
    y = _finish(x2d, ya, yb, ga, gb, wts)
    win_new = jnp.transpose(win_t[:, :, l:l + win_buf].reshape(b, 2, NSA_KV_HEADS, NSA_HD, win_buf), (0, 4, 1, 2, 3))
    return (y.reshape(b, l, D_MODEL), kv4.reshape(b, l, 4, NSA_KV_HEADS, NSA_HD), win_new, s_new, conv_new)


def kernel(x_prompt, x_sample, cache_kv, page_table, cache_win, state_gdn, state_conv, norm1_g, w_in, gdn_conv_w, gdn_a_log, gdn_dt_bias, gdn_norm_g, cmp_pos_wk, cmp_pos_wv, cmp_wk, cmp_wv, w_branch_a, w_branch_b, w_out, norm2_g, w_grp, b_grp, w_rt, b_rt, w_e_gate, w_e_up, w_e_down, norm_f_g):
    assert w_in.shape[0] == 1, "single layer"
    w_route, b_route = _route_weights(w_grp[0], b_grp[0], w_rt[0], b_rt[0])
    wg, wu, wd = _moe_weights(w_e_gate[0], w_e_up[0], w_e_down[0])
    wts = dict(
        norm1_g=norm1_g[0], w_pk=_pack_w_in(w_in[0]),
        conv_w=gdn_conv_w[0], a_log=gdn_a_log[0], dt_bias=gdn_dt_bias[0], gdn_norm_g=gdn_norm_g[0],
        pw=_pos_weights(cmp_pos_wk[0], cmp_pos_wv[0]), pos_wk=cmp_pos_wk[0], pos_wv=cmp_pos_wv[0],
        cmp_wk=cmp_wk[0], cmp_wv=cmp_wv[0],
        wa=w_branch_a[0].astype(BF16), wb=w_branch_b[0].astype(BF16), wo=w_out[0].astype(BF16),
        norm2_g=norm2_g[0], w_route=w_route, b_route=b_route,
        wg=wg, wu=wu, wd=wd, norm_f_g=norm_f_g,
    )
    yp, kvp, winp, sp, cp = _prompt_layer(x_prompt, wts)
    ys, kvs, wins, ss, cs = _sample_layer(x_sample, cache_kv[0], page_table, cache_win[0], state_gdn[0],
                                          state_conv[0], wts)
    return (yp, ys, kvp[None], kvs[None], winp[None], wins[None], sp[None], ss[None], cp[None], cs[None])
```

```python
import functools
import math

import numpy as np
import jax
import jax.numpy as jnp
from jax import lax
from jax.experimental import pallas as pl
from jax.experimental.pallas import tpu as pltpu

F32 = jnp.float32
BF16 = jnp.bfloat16
HI = lax.Precision.HIGHEST

LANES = 128
SUBLANES = 8
VMEM_LIMIT = 56 * 1024 * 1024
ROW_TILE = 512

D_MODEL = 1024
GDN_HEADS = 4
GDN_DK = 128
GDN_DV = 128
GDN_QK = GDN_HEADS * GDN_DK
GDN_V = GDN_HEADS * GDN_DV
GDN_CONV_DIM = 2 * GDN_QK + GDN_V
GDN_CONV = 4
GDN_CHUNK = 64
GDN_SEQS_PER_STEP = 4
NSA_HEADS = 8
NSA_KV_HEADS = 2
NSA_GROUP = NSA_HEADS // NSA_KV_HEADS
NSA_HD = 64
NSA_Q = NSA_HEADS * NSA_HD
NSA_KV = NSA_KV_HEADS * NSA_HD
CMP_STRIDE = 16
CMP_LEN = 2 * CMP_STRIDE
SEL_BLOCK = 64
SEL_TOP = 16
WINDOW = 512
FORCE_SCORE = 1.0e4
N_GROUPS = 4
EXPERTS_PER_GROUP = 8
N_EXPERTS = N_GROUPS * EXPERTS_PER_GROUP
D_EXPERT = 256
RMS_EPS = 1e-6
VS_ROWS = NSA_KV + 2 * SUBLANES
NEG_INF = float("-inf")

_OFF_QKV = 0
_OFF_Z = _OFF_QKV + GDN_CONV_DIM
_OFF_B = _OFF_Z + GDN_V
_OFF_A = _OFF_B + GDN_HEADS
_OFF_Q = _OFF_A + GDN_HEADS
_OFF_KV = _OFF_Q + NSA_Q
_OFF_G = _OFF_KV + 6 * NSA_KV
_OFF_GA = _OFF_G + 3 * NSA_HEADS
_OFF_GB = _OFF_GA + D_MODEL

SM_B = 0
SM_A = 4
SM_G = 8

_PK = {}
_c = 0
for _n, _w in (("qkv", GDN_CONV_DIM), ("z", GDN_V), ("q8", NSA_HEADS * LANES), ("kv4", 4 * NSA_KV),
               ("kvw", 2 * NSA_KV), ("ga", D_MODEL), ("gb", D_MODEL), ("sm", LANES)):
    _PK[_n] = (_c, _w)
    _c += _w
PK_DIM = _c


def _cparams(sem):
    return pltpu.CompilerParams(dimension_semantics=sem, vmem_limit_bytes=VMEM_LIMIT)


def _nt(a, b, precision=None):
    return lax.dot_general(a, b, (((1,), (1,)), ((), ())), preferred_element_type=F32, precision=precision)


def _tn(a, b, precision=None):
    return lax.dot_general(a, b, (((0,), (0,)), ((), ())), preferred_element_type=F32, precision=precision)


def _mm(a, b, precision=None):
    return jnp.dot(a, b, preferred_element_type=F32, precision=precision)


def _mmb(a, b):
    return _mm(a.astype(BF16), b.astype(BF16))


def _sigmoid(x):
    return 1.0 / (1.0 + jnp.exp(-x))


def _silu(x):
    return x * _sigmoid(x)


def _iota(shape, dim):
    return lax.broadcasted_iota(jnp.int32, shape, dim)


def _ones_where(cond):
    return jnp.where(cond, 1.0, 0.0).astype(BF16)


def _pack_w_in(w_in):
    q = w_in[:, _OFF_Q:_OFF_Q + NSA_Q]
    zeros64 = jnp.zeros((D_MODEL, NSA_HD), w_in.dtype)
    q8 = []
    for h in range(NSA_HEADS):
        qh = q[:, h * NSA_HD:(h + 1) * NSA_HD]
        q8.append(jnp.concatenate([qh, zeros64] if h < NSA_GROUP else [zeros64, qh], axis=1))
    sm = jnp.concatenate([w_in[:, _OFF_B:_OFF_B + GDN_HEADS], w_in[:, _OFF_A:_OFF_A + GDN_HEADS],
                          w_in[:, _OFF_G:_OFF_G + 3 * NSA_HEADS],
                          jnp.zeros((D_MODEL, LANES - 2 * GDN_HEADS - 3 * NSA_HEADS), w_in.dtype)], axis=1)
    cols = [w_in[:, _OFF_QKV:_OFF_QKV + GDN_CONV_DIM], w_in[:, _OFF_Z:_OFF_Z + GDN_V]] + q8 + [
        w_in[:, _OFF_KV:_OFF_KV + 4 * NSA_KV], w_in[:, _OFF_KV + 4 * NSA_KV:_OFF_KV + 6 * NSA_KV],
        w_in[:, _OFF_GA:_OFF_GA + D_MODEL], w_in[:, _OFF_GB:_OFF_GB + D_MODEL], sm]
    return jnp.concatenate(cols, axis=1).astype(BF16)


def _pos_features(pos, shape):
    lane = _iota(shape, 1)
    feat = jnp.where(lane == 0, (pos >> 7) * LANES, jnp.where(lane == 1, pos & (LANES - 1),
                                                               jnp.where(lane < 4, 1, 0)))
    return feat.astype(F32).astype(BF16)


def _proj_kernel(x_ref, g_ref, w_ref, qkv_ref, z_ref, q8_ref, ga_ref, gb_ref, sm_ref, *kv_refs, tiles_per_seq):
    tm = x_ref.shape[0]
    x = x_ref[...]
    xn = (x * lax.rsqrt(jnp.mean(x * x, axis=-1, keepdims=True) + RMS_EPS) * g_ref[...]).astype(BF16)

    def seg(name):
        a, w = _PK[name]
        return _mm(xn, w_ref[:, a:a + w])

    qkv_ref[...] = seg("qkv")
    z_ref[...] = seg("z")
    q8_ref[...] = (seg("q8") * (NSA_HD ** -0.5)).astype(BF16)
    ga_ref[...] = seg("ga")
    gb_ref[...] = seg("gb")
    sm_ref[...] = seg("sm")
    kv4 = seg("kv4")
    kvw = seg("kvw")
    if tiles_per_seq == 0:
        kv4_ref, kvw_ref = kv_refs
        kv4_ref[...] = kv4
        kvw_ref[...] = kvw
        return
    kv4t_ref, kvwt_ref, cmp_ref, ksa_ref, kwa_ref, vst_ref, vwt_ref = kv_refs
    kv4t = jnp.transpose(kv4)
    kvwt = jnp.transpose(kvw)
    kv4t_ref[...] = kv4t
    kvwt_ref[...] = kvwt
    cmp_ref[...] = kv4[:, :2 * NSA_KV]
    pos = (pl.program_id(0) % tiles_per_seq) * tm + _iota((tm, 1), 0)
    feat = _pos_features(pos, (tm, LANES))
    ksa_ref[:, :NSA_KV] = kv4[:, 2 * NSA_KV:3 * NSA_KV].astype(BF16)
    ksa_ref[:, NSA_KV:] = feat
    kwa_ref[:, :NSA_KV] = kvw[:, :NSA_KV].astype(BF16)
    kwa_ref[:, NSA_KV:] = feat
    vst_ref[:NSA_KV, :] = kv4t[3 * NSA_KV:].astype(BF16)
    ones_row = _ones_where(_iota((VS_ROWS - NSA_KV, tm), 0) == 0)
    vst_ref[NSA_KV:, :] = ones_row
    vwt_ref[:NSA_KV, :] = kvwt[NSA_KV:].astype(BF16)
    vwt_ref[NSA_KV:, :] = ones_row


def _project(x2d, norm_g, w_pk, b, l):
    n = x2d.shape[0]
    tm = min(ROW_TILE, n)
    long_seq = l % tm == 0
    tps = l // tm if long_seq else 0
    w = 2 * NSA_KV
    row = lambda i: (i, 0)
    out_shape = [jax.ShapeDtypeStruct((n, _PK[k][1]), dt) for k, dt in
                 (("qkv", F32), ("z", F32), ("q8", BF16), ("ga", F32), ("gb", F32), ("sm", F32))]
    out_specs = [pl.BlockSpec((tm, s.shape[1]), row) for s in out_shape]
    if long_seq:
        seq_t = lambda i: (i // tps, 0, i % tps)
        out_shape += [jax.ShapeDtypeStruct((b, 2 * w, l), F32), jax.ShapeDtypeStruct((b, w, l), F32),
                      jax.ShapeDtypeStruct((n, w), F32), jax.ShapeDtypeStruct((n, w), BF16),
                      jax.ShapeDtypeStruct((n, w), BF16), jax.ShapeDtypeStruct((VS_ROWS, n), BF16),
                      jax.ShapeDtypeStruct((VS_ROWS, n), BF16)]
        out_specs += [pl.BlockSpec((None, 2 * w, tm), seq_t), pl.BlockSpec((None, w, tm), seq_t),
                      pl.BlockSpec((tm, w), row), pl.BlockSpec((tm, w), row), pl.BlockSpec((tm, w), row),
                      pl.BlockSpec((VS_ROWS, tm), lambda i: (0, i)), pl.BlockSpec((VS_ROWS, tm), lambda i: (0, i))]
    else:
        out_shape += [jax.ShapeDtypeStruct((n, 2 * w), F32), jax.ShapeDtypeStruct((n, w), F32)]
        out_specs += [pl.BlockSpec((tm, 2 * w), row), pl.BlockSpec((tm, w), row)]
    return pl.pallas_call(
        functools.partial(_proj_kernel, tiles_per_seq=tps),
        grid=(n // tm,),
        in_specs=[pl.BlockSpec((tm, D_MODEL), row),
                  pl.BlockSpec((1, D_MODEL), lambda i: (0, 0)),
                  pl.BlockSpec((D_MODEL, PK_DIM), lambda i: (0, 0), pipeline_mode=pl.Buffered(1))],
        out_specs=out_specs,
        out_shape=out_shape,
        compiler_params=_cparams(("parallel",)),
        name="proj",
    )(x2d, norm_g.reshape(1, D_MODEL), w_pk)


def _unit_lower_inverses(a_list, c):
    r = _iota((c, c), 0)
    col = _iota((c, c), 1)
    eye = (r == col).astype(F32)
    n1 = [jnp.where((r >> 3) == (col >> 3), -a, 0.0) for a in a_list]
    n2 = [_mmb(x, x) for x in n1]
    n4 = [_mmb(x, x) for x in n2]
    t = [_mmb(eye + x, eye + y) for x, y in zip(n1, n2)]
    t = [_mmb(x, eye + y) for x, y in zip(t, n4)]
    s = SUBLANES
    while s < c:
        sh = s.bit_length() - 1
        off = ((r >> (sh + 1)) == (col >> (sh + 1))) & ((r >> sh) != (col >> sh))
        ta = [_mmb(x, jnp.where(off, a, 0.0)) for x, a in zip(t, a_list)]
        t = [x - _mmb(y, x) for x, y in zip(t, ta)]
        s *= 2
    return t


def _gdn_kernel(qkv_ref, z_ref, sm_ref, cbuf_ref, s0_ref, cw_ref, alog_ref, dtb_ref, ng_ref,
                y_ref, snew_ref, cnew_ref, ext_ref, st_ref, *, chunk, nb):
    c = chunk
    ci = pl.program_id(1)

    @pl.when(ci == 0)
    def _():
        ext_ref[:, 0:SUBLANES, :] = cbuf_ref[...]
        st_ref[...] = s0_ref[...]

    r = _iota((c, c), 0)
    col = _iota((c, c), 1)
    tri_incl = r >= col
    tri_strict = r > col
    tri_f = tri_incl.astype(F32)
    pick = (_iota((SUBLANES, LANES), 1) == _iota((SUBLANES, LANES), 0) + SM_A).astype(F32)
    base = SUBLANES - (GDN_CONV - 1)
    ng = ng_ref[...]

    units = []
    for bb in range(nb):
        u = qkv_ref[bb]
        ext_ref[bb, SUBLANES:SUBLANES + c, :] = u
        conv = cw_ref[0:1, :] * ext_ref[bb, base:base + c, :]
        for i in range(1, GDN_CONV - 1):
            conv = conv + cw_ref[i:i + 1, :] * ext_ref[bb, base + i:base + i + c, :]
        conv = conv + cw_ref[GDN_CONV - 1:GDN_CONV, :] * u
        halo = ext_ref[bb, c:c + SUBLANES, :]
        ext_ref[bb, 0:SUBLANES, :] = halo
        cnew_ref[bb] = halo
        qkv = _silu(conv)

        sm = sm_ref[bb]
        beta_all = _sigmoid(sm)
        xa = sm + dtb_ref[...]
        softplus = jnp.maximum(xa, 0.0) + jnp.log(1.0 + jnp.exp(-jnp.abs(xa)))
        g_all = -jnp.exp(alog_ref[...]) * softplus
        gcum_all = _mm(tri_f, g_all, HI)
        gcum_rows = _nt(pick, gcum_all, HI)
        for h in range(GDN_HEADS):
            q = qkv[:, h * GDN_DK:(h + 1) * GDN_DK]
            k = qkv[:, GDN_QK + h * GDN_DK:GDN_QK + (h + 1) * GDN_DK]
            v = qkv[:, 2 * GDN_QK + h * GDN_DV:2 * GDN_QK + (h + 1) * GDN_DV]
            q = q * lax.rsqrt(jnp.sum(q * q, axis=-1, keepdims=True) + RMS_EPS) * (GDN_DK ** -0.5)
            k = k * lax.rsqrt(jnp.sum(k * k, axis=-1, keepdims=True) + RMS_EPS)
            beta = beta_all[:, SM_B + h:SM_B + h + 1]
            gc_col = gcum_all[:, SM_A + h:SM_A + h + 1]
            gc_row = gcum_rows[h:h + 1, :]
            gc_last = gcum_all[c - 1:c, SM_A + h:SM_A + h + 1]
            units.append(dict(bb=bb, h=h, q=q, k=k, v=v, beta=beta, gc_col=gc_col, gc_last=gc_last,
                              decay=jnp.exp(jnp.where(tri_incl, gc_col - gc_row, NEG_INF)),
                              eg=jnp.exp(gc_col), kb=k * beta, k_bf=k.astype(BF16)))

    a_list = [jnp.where(tri_strict, _nt(un["kb"].astype(BF16), un["k_bf"]) * un["decay"], 0.0) for un in units]
    t_list = _unit_lower_inverses(a_list, c)
    u_coef = [_mmb(t, un["v"] * un["beta"]) for t, un in zip(t_list, units)]
    w_coef = [_mmb(t, un["kb"] * un["eg"]) for t, un in zip(t_list, units)]
    qk = [(_nt(un["q"].astype(BF16), un["k_bf"]) * un["decay"]).astype(BF16) for un in units]
    s_old = [st_ref[un["bb"], un["h"]] for un in units]
    s_bf = [s.astype(BF16) for s in s_old]
    uu = [uc - _mm(wc.astype(BF16), s) for uc, wc, s in zip(u_coef, w_coef, s_bf)]
    uu_bf = [x.astype(BF16) for x in uu]
    o_list = [_mm((un["q"] * un["eg"]).astype(BF16), s) + _mm(a, x)
              for un, s, a, x in zip(units, s_bf, qk, uu_bf)]
    for un, s, x, o in zip(units, s_old, uu_bf, o_list):
        bb, h = un["bb"], un["h"]
        k_tail = un["k"] * jnp.exp(un["gc_last"] - un["gc_col"])
        st_ref[bb, h] = s * jnp.exp(un["gc_last"]) + _tn(k_tail.astype(BF16), x)
        on = o * lax.rsqrt(jnp.mean(o * o, axis=-1, keepdims=True) + RMS_EPS) * ng
        y_ref[bb, :, h * GDN_DV:(h + 1) * GDN_DV] = on * _silu(z_ref[bb, :, h * GDN_DV:(h + 1) * GDN_DV])

    @pl.when(ci == pl.num_programs(1) - 1)
    def _():
        snew_ref[...] = st_ref[...]


def _gdn(qkv, z, sm, conv_buf, s0, conv_w, a_log, dt_bias, norm_g, b, l):
    c = math.gcd(l, GDN_CHUNK)
    nc = l // c
    nb = math.gcd(b, GDN_SEQS_PER_STEP)
    cbuf8 = jnp.pad(conv_buf, ((0, 0), (SUBLANES - (GDN_CONV - 1), 0), (0, 0)))
    pad_a = (SM_A, LANES - SM_A - GDN_HEADS)
    alog_row = jnp.pad(a_log, pad_a).reshape(1, LANES)
    dtb_row = jnp.pad(dt_bias, pad_a).reshape(1, LANES)
    row = lambda bi, ci: (bi, ci, 0)
    per_seq3 = lambda bi, ci: (bi, 0, 0)
    per_seq4 = lambda bi, ci: (bi, 0, 0, 0)
    fixed = lambda bi, ci: (0, 0)
    y, s_new, c_new = pl.pallas_call(
        functools.partial(_gdn_kernel, chunk=c, nb=nb),
        grid=(b // nb, nc),
        in_specs=[pl.BlockSpec((nb, c, GDN_CONV_DIM), row),
                  pl.BlockSpec((nb, c, GDN_V), row),
                  pl.BlockSpec((nb, c, LANES), row),
                  pl.BlockSpec((nb, SUBLANES, GDN_CONV_DIM), per_seq3),
                  pl.BlockSpec((nb, GDN_HEADS, GDN_DK, GDN_DV), per_seq4),
                  pl.BlockSpec((GDN_CONV, GDN_CONV_DIM), fixed),
                  pl.BlockSpec((1, LANES), fixed),
                  pl.BlockSpec((1, LANES), fixed),
                  pl.BlockSpec((1, GDN_DV), fixed)],
        out_specs=[pl.BlockSpec((nb, c, GDN_V), row),
                   pl.BlockSpec((nb, GDN_HEADS, GDN_DK, GDN_DV), per_seq4),
                   pl.BlockSpec((nb, SUBLANES, GDN_CONV_DIM), per_seq3)],
        out_shape=[jax.ShapeDtypeStruct((b, l, GDN_V), F32),
                   jax.ShapeDtypeStruct((b, GDN_HEADS, GDN_DK, GDN_DV), F32),
                   jax.ShapeDtypeStruct((b, SUBLANES, GDN_CONV_DIM), F32)],
        scratch_shapes=[pltpu.VMEM((nb, c + SUBLANES, GDN_CONV_DIM), F32),
                        pltpu.VMEM((nb, GDN_HEADS, GDN_DK, GDN_DV), F32)],
        compiler_params=_cparams(("parallel", "arbitrary")),
        name="gdn",
    )(qkv.reshape(b, l, GDN_CONV_DIM), z.reshape(b, l, GDN_V), sm.reshape(b, l, LANES), cbuf8, s0, conv_w,
      alog_row, dtb_row, norm_g.reshape(1, GDN_DV))
    return y.reshape(b * l, GDN_V), s_new, c_new[:, SUBLANES - (GDN_CONV - 1):]


def _pool_kernel(*refs, n_in):
    refs = refs[len(refs) - 2 * n_in - 3:]
    x_refs, pw_ref, head_ref, tail_ref = refs[:2 * n_in], refs[2 * n_in], refs[2 * n_in + 1], refs[2 * n_in + 2]
    rows = x_refs[0].shape[0]
    n = rows // CMP_STRIDE
    for j, x_ref in enumerate(x_refs):
        half = slice((j % 2) * NSA_KV, (j % 2 + 1) * NSA_KV)
        head = None
        tail = None
        for p in range(CMP_STRIDE):
            xr = x_ref[pl.ds(p, n, stride=CMP_STRIDE), :]
            hp = xr * pw_ref[p:p + 1, half]
            tp = xr * pw_ref[CMP_STRIDE + p:CMP_STRIDE + p + 1, half]
            head = hp if head is None else head + hp
            tail = tp if tail is None else tail + tp
        head_ref[(j // 2) * n:(j // 2 + 1) * n, half] = head
        tail_ref[(j // 2) * n:(j // 2 + 1) * n, half] = tail


def _pos_weights(pos_wk, pos_wv):
    return jnp.concatenate([jnp.broadcast_to(pos_wk[:, None], (CMP_LEN, NSA_KV)),
                            jnp.broadcast_to(pos_wv[:, None], (CMP_LEN, NSA_KV))], axis=1).astype(F32)


def _pool_rows(kv4, pw):
    n = kv4.shape[0]
    r = min(2048, n)
    w = 2 * NSA_KV
    return pl.pallas_call(
        functools.partial(_pool_kernel, n_in=1),
        grid=(n // r,),
        in_specs=[pl.BlockSpec((r, NSA_KV), lambda i: (i, 0)),
                  pl.BlockSpec((r, NSA_KV), lambda i: (i, 1)),
                  pl.BlockSpec((CMP_LEN, w), lambda i: (0, 0))],
        out_specs=[pl.BlockSpec((r // CMP_STRIDE, w), lambda i: (i, 0))] * 2,
        out_shape=[jax.ShapeDtypeStruct((n // CMP_STRIDE, w), F32)] * 2,
        compiler_params=_cparams(("parallel",)),
        name="pool_rows",
    )(kv4, kv4, pw)


def _pool_pages_kernel(*refs, pg):
    page_refs = refs[1:1 + pg]
    pk, pv, head_ref, tail_ref = refs[1 + pg:]
    x = jnp.concatenate([r[...] for r in page_refs], axis=1)
    n = head_ref.shape[1]
    for half, pool in enumerate((pk, pv)):
        xs = x[half * NSA_KV:(half + 1) * NSA_KV]
        out = _mm(xs.astype(BF16), pool[...])
        head_ref[half * NSA_KV:(half + 1) * NSA_KV, :] = out[:, :n]
        tail_ref[half * NSA_KV:(half + 1) * NSA_KV, :] = out[:, n:]


def _pool_matrix(pos_w, positions):
    chunks = positions // CMP_STRIDE
    p = jnp.arange(positions)[:, None]
    c = jnp.arange(chunks)[None, :]
    inside = (p // CMP_STRIDE) == c
    head = jnp.where(inside, pos_w[:CMP_STRIDE][p % CMP_STRIDE], 0.0)
    tail = jnp.where(inside, pos_w[CMP_STRIDE:][p % CMP_STRIDE], 0.0)
    return jnp.concatenate([head, tail], axis=1).astype(BF16)


def _pool_pages(cache_t, page_table, pos_wk, pos_wv):
    b, n_pages = page_table.shape
    page = cache_t.shape[2]
    pg = LANES * CMP_STRIDE // page
    w = 2 * NSA_KV
    steps = n_pages // pg
    n_ch = n_pages * page // CMP_STRIDE
    mats = (_pool_matrix(pos_wk, pg * page), _pool_matrix(pos_wv, pg * page))

    def page_map(k):
        return lambda bi, j, pt: (pt[bi, j * pg + k], 0, 0)

    fixed = lambda bi, j, pt: (0, 0)
    grid_spec = pltpu.PrefetchScalarGridSpec(
        num_scalar_prefetch=1,
        grid=(b, steps),
        in_specs=[pl.BlockSpec((None, w, page), page_map(k)) for k in range(pg)]
        + [pl.BlockSpec((pg * page, 2 * LANES), fixed)] * 2,
        out_specs=[pl.BlockSpec((None, w, LANES), lambda bi, j, pt: (bi, 0, j))] * 2,
    )
    return pl.pallas_call(
        functools.partial(_pool_pages_kernel, pg=pg),
        grid_spec=grid_spec,
        out_shape=[jax.ShapeDtypeStruct((b, w, n_ch), F32)] * 2,
        compiler_params=_cparams(("parallel", "arbitrary")),
        name="pool_pages",
    )(page_table, *([cache_t] * pg), *mats)


def _cmp_t_kernel(head_ref, tail_ref, wk_ref, wv_ref, kct_ref, vct_ref):
    n = head_ref.shape[1]
    blocks = head_ref[...] + pltpu.roll(tail_ref[...], n - 1, 1)
    kct_ref[...] = _mm(wk_ref[...], blocks[:NSA_KV].astype(BF16)).astype(BF16)
    vct_ref[...] = _mm(wv_ref[...], blocks[NSA_KV:].astype(BF16)).astype(BF16)


def _compress_t(head_t, tail_t, cmp_wk, cmp_wv):
    b, w, n_ch = head_t.shape
    seq = lambda i: (i, 0, 0)
    return pl.pallas_call(
        _cmp_t_kernel,
        grid=(b,),
        in_specs=[pl.BlockSpec((None, w, n_ch), seq)] * 2 + [pl.BlockSpec((NSA_KV, NSA_KV), lambda i: (0, 0))] * 2,
        out_specs=[pl.BlockSpec((None, NSA_KV, n_ch), seq)] * 2,
        out_shape=[jax.ShapeDtypeStruct((b, NSA_KV, n_ch), BF16)] * 2,
        compiler_params=_cparams(("parallel",)),
        name="compress_t",
    )(head_t, tail_t, jnp.transpose(_block_diag2(cmp_wk)), jnp.transpose(_block_diag2(cmp_wv)))


def _cmp_kernel(head_ref, tail_ref, wk_ref, wv_ref, kca_ref, vct_ref):
    n = head_ref.shape[0]
    blocks = head_ref[...] + pltpu.roll(tail_ref[...], n - 1, 0)
    kca_ref[:, :NSA_KV] = _mm(blocks[:, :NSA_KV].astype(BF16), wk_ref[...]).astype(BF16)
    blk_end = _iota((n, 1), 0) * CMP_STRIDE + (CMP_LEN - 1)
    kca_ref[:, NSA_KV:] = _pos_features(blk_end, (n, LANES))
    vct_ref[...] = _nt(wv_ref[...], blocks[:, NSA_KV:].astype(BF16)).astype(BF16)


def _block_diag2(w):
    z = jnp.zeros_like(w)
    return jnp.concatenate([jnp.concatenate([w, z], axis=1), jnp.concatenate([z, w], axis=1)], axis=0).astype(BF16)


def _compress(head, tail, cmp_wk, cmp_wv, b):
    n_ch = head.shape[0] // b
    w = 2 * NSA_KV
    return pl.pallas_call(
        _cmp_kernel,
        grid=(b,),
        in_specs=[pl.BlockSpec((n_ch, w), lambda i: (i, 0))] * 2
        + [pl.BlockSpec((NSA_KV, NSA_KV), lambda i: (0, 0))] * 2,
        out_specs=[pl.BlockSpec((n_ch, w), lambda i: (i, 0)),
                   pl.BlockSpec((NSA_KV, n_ch), lambda i: (0, i))],
        out_shape=[jax.ShapeDtypeStruct((b * n_ch, w), BF16),
                   jax.ShapeDtypeStruct((NSA_KV, b * n_ch), BF16)],
        compiler_params=_cparams(("parallel",)),
        name="compress",
    )(head, tail, _block_diag2(cmp_wk), jnp.transpose(_block_diag2(cmp_wv)))


def _overlap(n_ch, nsp):
    i = jnp.arange(n_ch)[:, None]
    j = jnp.arange(nsp)[None, :]
    lo = jnp.maximum(i * CMP_STRIDE, j * SEL_BLOCK)
    hi = jnp.minimum(i * CMP_STRIDE + CMP_LEN, (j + 1) * SEL_BLOCK)
    ov = jnp.maximum(hi - lo, 0).astype(F32) / CMP_LEN
    return jnp.where(i < n_ch - 1, ov, 0.0)


def _slope(h):
    return 2.0 ** (-(h + 1))


def _softmax_rows(s, mask):
    s = jnp.where(mask, s, NEG_INF)
    m = jnp.max(s, axis=-1, keepdims=True)
    m = jnp.where(m == NEG_INF, 0.0, m)
    e = jnp.where(mask, jnp.exp(s - m), 0.0)
    return e, jnp.maximum(jnp.sum(e, axis=-1, keepdims=True), 1e-30)


def _top_blocks(imp, qpos, ns_lanes):
    blk = _iota(imp.shape, 1)
    cur = qpos >> 6
    forced = (blk == 0) | (blk == cur) | (blk == cur - 1)
    score = jnp.where(forced, FORCE_SCORE, imp)
    work = jnp.where(blk <= cur, score, NEG_INF)
    sel = jnp.zeros(imp.shape, F32)
    for _ in range(SEL_TOP):
        m = jnp.max(work, axis=-1, keepdims=True)
        cand = jnp.where((work == m) & (m > NEG_INF), blk, ns_lanes)
        first = jnp.min(cand, axis=-1, keepdims=True)
        hit = blk == first
        sel = jnp.where(hit, 1.0, sel)
        work = jnp.where(hit, NEG_INF, work)
    return sel


def _row_slopes(tq):
    head = _iota((NSA_HEADS * tq, 1), 0) >> (tq.bit_length() - 1)
    slope = jnp.zeros((NSA_HEADS * tq, 1), F32)
    for h in range(NSA_HEADS):
        slope = jnp.where(head == h, _slope(h), slope)
    return slope


def _rows_attend(q8, kt, vt, slope, dist, mask):
    s = _mm(q8, kt) - slope * dist.astype(F32)
    e, den = _softmax_rows(s, mask)
    p = e / den
    return _nt(p.astype(BF16), vt), p


def _rows_online(q8, kt, vt, slope, dist, mask, m_ref, l_ref, acc_ref):
    s = jnp.where(mask, _mm(q8, kt) - slope * dist.astype(F32), NEG_INF)
    m_old = m_ref[...]
    m_new = jnp.maximum(m_old, jnp.max(s, axis=-1, keepdims=True))
    m_use = jnp.where(m_new == NEG_INF, 0.0, m_new)
    alpha = jnp.exp(m_old - m_use)
    e = jnp.exp(s - m_use)
    l_ref[...] = alpha * l_ref[...] + jnp.sum(e, axis=-1, keepdims=True)
    acc_ref[...] = alpha * acc_ref[...] + _nt(e.astype(BF16), vt)
    m_ref[...] = m_new


def _merge_heads(o_c, o_s, o_w, gates, tq):
    lane = _iota((tq, LANES), 1)
    mixed = []
    for h in range(NSA_HEADS):
        rows = slice(h * tq, (h + 1) * tq)
        g0 = gates[:, SM_G + 3 * h:SM_G + 3 * h + 1]
        g1 = gates[:, SM_G + 3 * h + 1:SM_G + 3 * h + 2]
        g2 = gates[:, SM_G + 3 * h + 2:SM_G + 3 * h + 3]
        mixed.append(g0 * o_c[rows] + g1 * o_s[rows] + g2 * o_w[rows])
    cols = []
    for p in range(NSA_HEADS // 2):
        a, b = mixed[2 * p], mixed[2 * p + 1]
        if 2 * p < NSA_GROUP:
            cols.append(jnp.where(lane < NSA_HD, a, pltpu.roll(b, NSA_HD, 1)))
        else:
            cols.append(jnp.where(lane < NSA_HD, pltpu.roll(a, NSA_HD, 1), b))
    return jnp.concatenate(cols, axis=1)


def _softmax_cols(s, mask):
    s = jnp.where(mask, s, NEG_INF)
    m = jnp.max(s, axis=0, keepdims=True)
    m = jnp.where(m == NEG_INF, 0.0, m)
    e = jnp.exp(s - m)
    return e * (1.0 / jnp.maximum(jnp.sum(e, axis=0, keepdims=True), 1e-30))


def _split3(x):
    hi = x.astype(BF16)
    r1 = x - hi.astype(F32)
    mid = r1.astype(BF16)
    lo = (r1 - mid.astype(F32)).astype(BF16)
    return hi, mid, lo


def _top_blocks_cols(imp, qpos, nsp):
    blk = _iota(imp.shape, 0)
    cur = qpos >> 6
    forced = (blk == 0) | (blk == cur) | (blk == cur - 1)
    work = jnp.where(blk <= cur, jnp.where(forced, FORCE_SCORE, imp), NEG_INF)
    neg = jnp.full(imp.shape, NEG_INF, F32)
    for _ in range(SEL_TOP):
        m = jnp.max(work, axis=0, keepdims=True)
        cand = jnp.where((work == m) & (m > NEG_INF), blk, nsp)
        hit = blk == jnp.min(cand, axis=0, keepdims=True)
        neg = jnp.where(hit, 0.0, neg)
        work = jnp.where(hit, NEG_INF, work)
    return neg


def _nsa_prompt_kernel(q8_ref, sm_ref, kc_ref, vct_ref, ks_ref, vst_ref, kw_ref, vwt_ref, ovt_ref, y_ref,
                       neg_ref, m_ref, acc_ref, qta_ref, sa_ref, sb_ref, act_ref, *, tq, tk):
    i = pl.program_id(1)
    start = i * tq
    r = NSA_HEADS * tq
    nsp, n_ch = ovt_ref.shape
    col = _iota((1, r), 1)
    qpos = start + (col & (tq - 1))

    q8 = jnp.concatenate([q8_ref[:, h * LANES:(h + 1) * LANES] for h in range(NSA_HEADS)], axis=0)
    eye = (_iota((LANES, LANES), 0) == _iota((LANES, LANES), 1)).astype(BF16)
    qt = _nt(eye, q8).astype(BF16)
    head = col >> (tq.bit_length() - 1)
    slope = jnp.zeros((1, r), F32)
    for h in range(NSA_HEADS):
        slope = jnp.where(head == h, _slope(h), slope)
    frow = _iota((2 * SUBLANES, r), 0)
    feat = jnp.where(frow < 2, slope, jnp.where(frow == 2, -slope * ((qpos >> 7) * LANES).astype(F32),
                                                jnp.where(frow == 3, -slope * (qpos & (LANES - 1)).astype(F32), 0.0)))
    qta = jnp.concatenate([qt, feat.astype(BF16), jnp.zeros((LANES - 2 * SUBLANES, r), BF16)], axis=0)

    blk_end = _iota((n_ch, 1), 0) * CMP_STRIDE + (CMP_LEN - 1)
    p = _softmax_cols(_mm(kc_ref[...], qta), blk_end <= qpos)
    o_c = _mm(vct_ref[...], p.astype(BF16))
    psum = []
    for g in range(NSA_KV_HEADS):
        acc = p[:, g * NSA_GROUP * tq:(g * NSA_GROUP + 1) * tq]
        for hh in range(1, NSA_GROUP):
            acc = acc + p[:, (g * NSA_GROUP + hh) * tq:(g * NSA_GROUP + hh + 1) * tq]
        psum.append(acc)
    ovt = ovt_ref[...]
    imp = sum(_mm(ovt, piece) for piece in _split3(jnp.concatenate(psum, axis=1)))
    qpos2 = start + (_iota((1, NSA_KV_HEADS * tq), 1) & (tq - 1))
    neg = _top_blocks_cols(imp, qpos2, nsp)
    for h in range(NSA_HEADS):
        g = h // NSA_GROUP
        neg_ref[:, h * tq:(h + 1) * tq] = neg[:, g * tq:(g + 1) * tq]

    m_ref[...] = jnp.full(m_ref.shape, NEG_INF, F32)
    acc_ref[...] = jnp.zeros(acc_ref.shape, F32)
    bpt = tk // SEL_BLOCK

    qta_ref[...] = qta

    def scores(t, buf):
        off = pl.multiple_of(t * tk, tk)
        buf[...] = _mm(ks_ref[pl.ds(off, tk), :], qta_ref[...])

    def tile(t, buf, causal):
        off = pl.multiple_of(t * tk, tk)
        parts = []
        for j in range(bpt):
            sj = buf[j * SEL_BLOCK:(j + 1) * SEL_BLOCK, :]
            neg = neg_ref[pl.ds(t * bpt + j, 1), :]
            if causal:
                sj = jnp.where(off + j * SEL_BLOCK + _iota((SEL_BLOCK, 1), 0) <= qpos, sj + neg, NEG_INF)
                parts.append(sj.astype(BF16))
            else:
                parts.append(sj.astype(BF16) + neg.astype(BF16))
        s = jnp.concatenate(parts, axis=0)
        m_old = m_ref[...]
        m_new = jnp.maximum(m_old, jnp.max(s, axis=0, keepdims=True).astype(F32))
        m_use = jnp.where(m_new == NEG_INF, 0.0, m_new)
        alpha = jnp.exp(m_old - m_use)
        e = jnp.exp(s - m_use.astype(BF16))
        acc_ref[...] = alpha * acc_ref[...] + _mm(vst_ref[:, pl.ds(off, tk)], e)
        m_ref[...] = m_new

    n_full = start // tk
    blk_any = jnp.max(neg, axis=1, keepdims=True)
    cnt = jnp.int32(0)
    for t in range(ks_ref.shape[0] // tk):
        live = (jnp.max(blk_any[t * bpt:(t + 1) * bpt]) > NEG_INF) & (t < n_full)
        act_ref[cnt] = t
        cnt = cnt + live.astype(jnp.int32)
    act_ref[cnt] = n_full
    pairs = cnt // 2
    scores(act_ref[0], sa_ref)

    def body(u, carry):
        scores(act_ref[2 * u + 1], sb_ref)
        tile(act_ref[2 * u], sa_ref, False)
        scores(act_ref[2 * u + 2], sa_ref)
        tile(act_ref[2 * u + 1], sb_ref, False)
        return carry

    lax.fori_loop(0, pairs, body, 0)

    @pl.when(cnt % 2 == 1)
    def _():
        scores(n_full, sb_ref)
        tile(act_ref[cnt - 1], sa_ref, False)
        tile(n_full, sb_ref, True)

    @pl.when(cnt % 2 == 0)
    def _():
        tile(n_full, sa_ref, True)
    o_s = acc_ref[:NSA_KV, :] * (1.0 / jnp.maximum(acc_ref[NSA_KV:NSA_KV + 1, :], 1e-30))

    wl = tq + WINDOW
    ws = pl.multiple_of(jnp.maximum(start - WINDOW, 0), tq)
    dist = qpos - (ws + _iota((wl, 1), 0))
    sw = jnp.where((dist >= 0) & (dist <= WINDOW), _mm(kw_ref[pl.ds(ws, wl), :], qta), NEG_INF).astype(BF16)
    ew = jnp.exp(sw - jnp.max(sw, axis=0, keepdims=True))
    ow = _mm(vwt_ref[:, pl.ds(ws, wl)], ew)
    o_w = ow[:NSA_KV] * (1.0 / jnp.maximum(ow[NSA_KV:NSA_KV + 1], 1e-30))

    gt = jnp.transpose(_sigmoid(sm_ref[...]))
    gate = [jnp.concatenate([gt[SM_G + 3 * h + c:SM_G + 3 * h + c + 1, :] for h in range(NSA_HEADS)], axis=1)
            for c in range(3)]
    o = gate[0] * o_c + gate[1] * o_s + gate[2] * o_w
    yt = jnp.concatenate([o[(h // NSA_GROUP) * NSA_HD:(h // NSA_GROUP + 1) * NSA_HD, h * tq:(h + 1) * tq]
                          for h in range(NSA_HEADS)], axis=0)
    y_ref[...] = jnp.transpose(yt)


def _nsa_prompt(q8, sm, kca, vct, ksa, vst, kwa, vwt, b, l):
    tq = 256
    tk = 512
    assert l % tk == 0 and l >= tq + WINDOW
    n_ch = l // CMP_STRIDE
    ns = -(-l // SEL_BLOCK)
    nsp = -(-ns // LANES) * LANES
    nq = l // tq
    w = 2 * NSA_KV
    ovt = jnp.transpose(_overlap(n_ch, nsp)).astype(BF16)
    row = lambda bi, i: (bi * nq + i, 0)
    seq_rows = lambda bi, i: (bi, 0)
    seq_cols = lambda bi, i: (0, bi)
    return pl.pallas_call(
        functools.partial(_nsa_prompt_kernel, tq=tq, tk=tk),
        grid=(b, nq),
        in_specs=[pl.BlockSpec((tq, NSA_HEADS * LANES), row),
                  pl.BlockSpec((tq, LANES), row),
                  pl.BlockSpec((n_ch, w), seq_rows),
                  pl.BlockSpec((NSA_KV, n_ch), seq_cols),
                  pl.BlockSpec((l, w), seq_rows),
                  pl.BlockSpec((VS_ROWS, l), seq_cols),
                  pl.BlockSpec((l, w), seq_rows),
                  pl.BlockSpec((VS_ROWS, l), seq_cols),
                  pl.BlockSpec((nsp, n_ch), lambda bi, i: (0, 0))],
        out_specs=pl.BlockSpec((tq, NSA_Q), row),
        out_shape=jax.ShapeDtypeStruct((b * l, NSA_Q), F32),
        scratch_shapes=[pltpu.VMEM((nsp, NSA_HEADS * tq), F32),
                        pltpu.VMEM((1, NSA_HEADS * tq), F32),
                        pltpu.VMEM((VS_ROWS, NSA_HEADS * tq), F32),
                        pltpu.VMEM((2 * NSA_KV, NSA_HEADS * tq), BF16),
                        pltpu.VMEM((tk, NSA_HEADS * tq), F32),
                        pltpu.VMEM((tk, NSA_HEADS * tq), F32),
                        pltpu.SMEM((l // tk + 1,), jnp.int32)],
        compiler_params=_cparams(("parallel", "arbitrary")),
        name="nsa_prompt",
    )(q8, sm, kca, vct, ksa, vst, kwa, vwt, ovt)


def _nsa_sample_kernel(*refs, pg, page, tq, past_len):
    pt_ref = refs[0]
    del pt_ref
    page_refs = refs[1:1 + pg]
    (q8_ref, sm_ref, kc_ref, vc_ref, knew_ref, kw_ref, ov_ref, y_ref,
     m_ref, l_ref, acc_ref, sel_ref, oc_ref) = refs[1 + pg:]
    j = pl.program_id(1)
    nsp = ov_ref.shape[1]
    r = NSA_HEADS * tq
    q8 = jnp.concatenate([q8_ref[:, h * LANES:(h + 1) * LANES].astype(F32) for h in range(NSA_HEADS)],
                         axis=0).astype(BF16)
    qpos = past_len + (_iota((r, 1), 0) & (tq - 1))
    slope = _row_slopes(tq)

    @pl.when(j == 0)
    def _():
        n_ch = kc_ref.shape[1]
        dist = qpos - (_iota((1, n_ch), 1) * CMP_STRIDE + (CMP_LEN - 1))
        o_c, p = _rows_attend(q8, kc_ref[...], vc_ref[...], slope, dist, dist >= 0)
        oc_ref[...] = o_c
        psums = []
        for g in range(NSA_KV_HEADS):
            psum = p[g * NSA_GROUP * tq:(g * NSA_GROUP + 1) * tq]
            for hh in range(1, NSA_GROUP):
                psum = psum + p[(g * NSA_GROUP + hh) * tq:(g * NSA_GROUP + hh + 1) * tq]
            psums.append(psum)
        sel = _top_blocks(_mm(jnp.concatenate(psums, axis=0), ov_ref[...], HI), qpos[:NSA_KV_HEADS * tq], nsp)
        for h in range(NSA_HEADS):
            g = h // NSA_GROUP
            sel_ref[h * tq:(h + 1) * tq, :] = sel[g * tq:(g + 1) * tq]
        m_ref[...] = jnp.full(m_ref.shape, NEG_INF, F32)
        l_ref[...] = jnp.zeros(l_ref.shape, F32)
        acc_ref[...] = jnp.zeros(acc_ref.shape, F32)

    sel_bf = sel_ref[...].astype(BF16)

    def sel_step(kv, kpos):
        n = kv.shape[1]
        expand = jnp.where(_iota((nsp, n), 0) == (kpos >> 6), 1.0, 0.0).astype(BF16)
        dist = qpos - kpos
        mask = (_mm(sel_bf, expand) > 0.5) & (dist >= 0)
        _rows_online(q8, kv[:NSA_KV].astype(BF16), kv[NSA_KV:].astype(BF16), slope, dist, mask,
                     m_ref, l_ref, acc_ref)

    sel_step(jnp.concatenate([page_refs[k][...] for k in range(pg)], axis=1),
             j * (pg * page) + _iota((1, pg * page), 1))

    @pl.when(j == pl.num_programs(1) - 1)
    def _():
        sel_step(knew_ref[...], past_len + _iota((1, knew_ref.shape[1]), 1))
        o_s = acc_ref[...] / jnp.maximum(l_ref[...], 1e-30)
        kw = kw_ref[...]
        kwpos = past_len - WINDOW + _iota((1, kw.shape[1]), 1)
        dist = qpos - kwpos
        mask = (dist >= 0) & (dist <= WINDOW) & (kwpos >= 0) & (kwpos < past_len + tq)
        o_w, _ = _rows_attend(q8, kw[:NSA_KV].astype(BF16), kw[NSA_KV:].astype(BF16), slope, dist, mask)
        y_ref[...] = _merge_heads(oc_ref[...], o_s, o_w, _sigmoid(sm_ref[...]), tq)


def _nsa_sample(q8, sm, kct, vct, knew_t, win_t, cache_t, page_table, b, l):
    n_pages = page_table.shape[1]
    page = cache_t.shape[2]
    past_len = n_pages * page
    assert l <= SEL_BLOCK and past_len % SEL_BLOCK == 0 and past_len >= WINDOW
    pg = 16
    steps = n_pages // pg
    n_ch = past_len // CMP_STRIDE
    ns = -(-(past_len + l) // SEL_BLOCK)
    nsp = -(-ns // LANES) * LANES
    w = 2 * NSA_KV
    ov = _overlap(n_ch, nsp)
    wcols = win_t.shape[2]

    def page_map(k):
        return lambda bi, j, pt: (pt[bi, j * pg + k], 1, 0)

    per_b = lambda bi, j, pt: (bi, 0)
    per_b3 = lambda bi, j, pt: (bi, 0, 0)
    grid_spec = pltpu.PrefetchScalarGridSpec(
        num_scalar_prefetch=1,
        grid=(b, steps),
        in_specs=[pl.BlockSpec((None, w, page), page_map(k)) for k in range(pg)]
        + [pl.BlockSpec((None, l, NSA_HEADS * LANES), per_b3),
           pl.BlockSpec((l, LANES), per_b),
           pl.BlockSpec((None, NSA_KV, n_ch), per_b3),
           pl.BlockSpec((None, NSA_KV, n_ch), per_b3),
           pl.BlockSpec((None, w, LANES), per_b3),
           pl.BlockSpec((None, w, wcols), per_b3),
           pl.BlockSpec((n_ch, nsp), lambda bi, j, pt: (0, 0))],
        out_specs=pl.BlockSpec((l, NSA_Q), per_b),
        scratch_shapes=[pltpu.VMEM((NSA_HEADS * l, 1), F32),
                        pltpu.VMEM((NSA_HEADS * l, 1), F32),
                        pltpu.VMEM((NSA_HEADS * l, NSA_KV), F32),
                        pltpu.VMEM((NSA_HEADS * l, nsp), F32),
                        pltpu.VMEM((NSA_HEADS * l, NSA_KV), F32)],
    )
    return pl.pallas_call(
        functools.partial(_nsa_sample_kernel, pg=pg, page=page, tq=l, past_len=past_len),
        grid_spec=grid_spec,
        out_shape=jax.ShapeDtypeStruct((b * l, NSA_Q), F32),
        compiler_params=_cparams(("parallel", "arbitrary")),
        name="nsa_sample",
    )(page_table, *([cache_t] * pg), q8.reshape(b, l, NSA_HEADS * LANES), sm, kct, vct, knew_t, win_t, ov)


def _tail_kernel(x_ref, ya_ref, yb_ref, ga_ref, gb_ref, wa_ref, wb_ref, wo_ref, n2_ref, wr_ref, br_ref,
                 h_ref, hn_ref, cmb_ref):
    ma = _mm(ya_ref[...].astype(BF16), wa_ref[...])
    mb = _mm(yb_ref[...].astype(BF16), wb_ref[...])
    m = _sigmoid(ga_ref[...]) * ma + _sigmoid(gb_ref[...]) * mb
    h = x_ref[...] + _mm(m.astype(BF16), wo_ref[...])
    h_ref[...] = h
    hn = h * lax.rsqrt(jnp.mean(h * h, axis=-1, keepdims=True) + RMS_EPS) * n2_ref[...]
    hn_hi = hn.astype(BF16)
    hn_ref[...] = hn_hi

    hn_lo = (hn - hn_hi.astype(F32)).astype(BF16)
    logit = _mm(hn_hi, wr_ref[0]) + _mm(hn_lo, wr_ref[0]) + _mm(hn_hi, wr_ref[1]) + br_ref[...]
    lane = _iota(logit.shape, 1)
    is_grp = (lane >= N_EXPERTS) & (lane < N_EXPERTS + N_GROUPS)
    gl = jnp.where(is_grp, logit, NEG_INF)
    gmax = jnp.max(gl, axis=-1, keepdims=True)
    gidx = jnp.min(jnp.where(gl == gmax, lane, LANES), axis=-1, keepdims=True) - N_EXPERTS
    p_grp = 1.0 / jnp.sum(jnp.exp(gl - gmax), axis=-1, keepdims=True)
    el = jnp.where((lane >> 3) == gidx, logit, NEG_INF)
    t1 = jnp.max(el, axis=-1, keepdims=True)
    i1 = jnp.min(jnp.where(el == t1, lane, LANES), axis=-1, keepdims=True)
    el2 = jnp.where(lane == i1, NEG_INF, el)
    t2 = jnp.max(el2, axis=-1, keepdims=True)
    i2 = jnp.min(jnp.where(el2 == t2, lane, LANES), axis=-1, keepdims=True)
    e2 = jnp.exp(t2 - t1)
    w1 = p_grp / (1.0 + e2)
    w2 = p_grp * e2 / (1.0 + e2)
    cmb_ref[...] = jnp.where(lane == i1, w1, jnp.where(lane == i2, w2,
                                                        jnp.where(lane == gidx + N_EXPERTS, 1.0, 0.0)))


def _tail(x2d, ya, yb, ga, gb, wa, wb, wo, norm2_g, w_route, b_route):
    n = x2d.shape[0]
    tm = min(ROW_TILE, n)
    row = lambda i: (i, 0)
    fixed = lambda i: (0, 0)
    return pl.pallas_call(
        _tail_kernel,
        grid=(n // tm,),
        in_specs=[pl.BlockSpec((tm, D_MODEL), row),
                  pl.BlockSpec((tm, GDN_V), row),
                  pl.BlockSpec((tm, NSA_Q), row),
                  pl.BlockSpec((tm, D_MODEL), row),
                  pl.BlockSpec((tm, D_MODEL), row),
                  pl.BlockSpec((GDN_V, D_MODEL), fixed),
                  pl.BlockSpec((NSA_Q, D_MODEL), fixed),
                  pl.BlockSpec((D_MODEL, D_MODEL), fixed),
                  pl.BlockSpec((1, D_MODEL), fixed),
                  pl.BlockSpec((2, D_MODEL, LANES), lambda i: (0, 0, 0)),
                  pl.BlockSpec((1, LANES), fixed)],
        out_specs=[pl.BlockSpec((tm, D_MODEL), row),
                   pl.BlockSpec((tm, D_MODEL), row),
                   pl.BlockSpec((tm, LANES), row)],
        out_shape=[jax.ShapeDtypeStruct((n, D_MODEL), F32),
                   jax.ShapeDtypeStruct((n, D_MODEL), BF16),
                   jax.ShapeDtypeStruct((n, LANES), F32)],
        compiler_params=_cparams(("parallel",)),
        name="tail",
    )(x2d, ya, yb, ga, gb, wa, wb, wo, norm2_g.reshape(1, D_MODEL), w_route, b_route)


MOE_SLABS = 2 * N_GROUPS
MOE_SLAB_EXPERTS = N_EXPERTS // MOE_SLABS


def _moe_kernel(offs_ref, cnts_ref, h_ref, hn_ref, cmb_ref, wg_ref, wu_ref, wd_ref, nf_ref, y_ref,
                xs_ref, cs_ref, pt_ref, acc_ref, *, rb):
    i = pl.program_id(0)
    sl = pl.program_id(1)
    t = hn_ref.shape[0]
    lane = _iota((t, LANES), 1)

    @pl.when(sl == 0)
    def _():
        cmb = cmb_ref[...]
        oh = jnp.where((lane >= N_EXPERTS) & (lane < N_EXPERTS + N_GROUPS), cmb, 0.0)
        oh_bf = oh.astype(BF16)
        r_i = _iota((t, t), 0)
        c_i = _iota((t, t), 1)
        lt = _iota((LANES, LANES), 0) < _iota((LANES, LANES), 1)
        before = _mm(_ones_where(c_i < r_i), oh_bf)
        totals = jnp.broadcast_to(jnp.sum(oh, axis=0, keepdims=True), (SUBLANES, LANES))
        smaller = _mm(totals, lt.astype(F32), HI)[0:1]
        rank_col = jnp.sum((before + smaller) * oh, axis=1, keepdims=True)
        eye = (_iota((LANES, LANES), 0) == _iota((LANES, LANES), 1)).astype(BF16)
        oht = _nt(eye, oh_bf)
        before_t = _mm(oht.astype(BF16), _ones_where(r_i < c_i))
        totals_t = jnp.broadcast_to(jnp.sum(oht, axis=1, keepdims=True), (LANES, LANES))
        gt = _iota((LANES, LANES), 0) > _iota((LANES, LANES), 1)
        smaller_t = _mm(gt.astype(F32), totals_t, HI)[:, 0:1]
        rank_row = jnp.sum((before_t + smaller_t) * oht, axis=0, keepdims=True)
        perm = _ones_where(r_i == rank_row.astype(jnp.int32))
        pt_ref[...] = _ones_where(c_i == rank_col.astype(jnp.int32))
        xs_ref[...] = _mm(perm, hn_ref[...]).astype(BF16)
        c_hi = cmb.astype(BF16)
        c_lo = (cmb - c_hi.astype(F32)).astype(BF16)
        cs_ref[...] = _mm(perm, c_hi) + _mm(perm, c_lo)
        acc_ref[...] = jnp.zeros(acc_ref.shape, F32)

    g = sl // (MOE_SLABS // N_GROUPS)
    off = offs_ref[i, g]
    row0 = (off // (2 * SUBLANES)) * (2 * SUBLANES)
    rows = off + cnts_ref[i, g] - row0
    nfull = rows // rb
    rem = rows - nfull * rb

    def block(lo, bs):
        r0 = pl.multiple_of(jnp.minimum(lo, t - bs), 2 * SUBLANES)
        x = xs_ref[pl.ds(r0, bs), :]
        cw = jnp.where(r0 + _iota((bs, 1), 0) >= lo, cs_ref[pl.ds(r0, bs), :], 0.0)
        lane_b = _iota((bs, LANES), 1)
        parts = []
        for e in range(MOE_SLAB_EXPERTS):
            wgt = jnp.sum(jnp.where(lane_b == sl * MOE_SLAB_EXPERTS + e, cw, 0.0), axis=1, keepdims=True)
            parts.append((_silu(_mm(x, wg_ref[e])) * _mm(x, wu_ref[e]) * wgt).astype(BF16))
        acc_ref[pl.ds(r0, bs), :] += _mm(jnp.concatenate(parts, axis=1), wd_ref[0])

    def full_block(j, carry):
        block(row0 + j * rb, rb)
        return carry

    lax.fori_loop(0, nfull, full_block, 0)
    sizes = sorted({s for s in (rb // 4, rb // 2, rb) if s % (2 * SUBLANES) == 0})
    prev = 0
    for bs in sizes:
        @pl.when((rem > prev) & (rem <= bs) & (cnts_ref[i, g] > 0))
        def _(bs=bs):
            block(row0 + nfull * rb, bs)
        prev = bs

    @pl.when(sl == pl.num_programs(1) - 1)
    def _():
        acc = acc_ref[...]
        a_hi = acc.astype(BF16)
        a_lo = (acc - a_hi.astype(F32)).astype(BF16)
        pt = pt_ref[...]
        v = h_ref[...] + _mm(pt, a_hi) + _mm(pt, a_lo)
        y_ref[...] = v * lax.rsqrt(jnp.mean(v * v, axis=-1, keepdims=True) + RMS_EPS) * nf_ref[...]


def _moe_weights(w_gate, w_up, w_down):
    return (w_gate.astype(BF16), w_up.astype(BF16),
            w_down.astype(BF16).reshape(MOE_SLABS, MOE_SLAB_EXPERTS * D_EXPERT, D_MODEL))


def _moe(h, hn, cmb, wg, wu, wd, norm_f_g):
    n = h.shape[0]
    t = min(1024, n)
    rb = min(256, t)
    tiles = n // t
    cnts = jnp.sum(cmb[:, N_EXPERTS:N_EXPERTS + N_GROUPS].reshape(tiles, t, N_GROUPS), axis=1).astype(jnp.int32)
    offs = jnp.cumsum(cnts, axis=1) - cnts
    ws = MOE_SLAB_EXPERTS * D_EXPERT
    row = lambda i, s, o, c: (i, 0)
    slab = lambda i, s, o, c: (s, 0, 0)
    grid_spec = pltpu.PrefetchScalarGridSpec(
        num_scalar_prefetch=2,
        grid=(tiles, MOE_SLABS),
        in_specs=[pl.BlockSpec((t, D_MODEL), row),
                  pl.BlockSpec((t, D_MODEL), row),
                  pl.BlockSpec((t, LANES), row),
                  pl.BlockSpec((MOE_SLAB_EXPERTS, D_MODEL, D_EXPERT), slab),
                  pl.BlockSpec((MOE_SLAB_EXPERTS, D_MODEL, D_EXPERT), slab),
                  pl.BlockSpec((1, ws, D_MODEL), slab),
                  pl.BlockSpec((1, D_MODEL), lambda i, s, o, c: (0, 0))],
        out_specs=pl.BlockSpec((t, D_MODEL), row),
        scratch_shapes=[pltpu.VMEM((t, D_MODEL), BF16),
                        pltpu.VMEM((t, LANES), F32),
                        pltpu.VMEM((t, t), BF16),
                        pltpu.VMEM((t, D_MODEL), F32)],
    )
    return pl.pallas_call(
        functools.partial(_moe_kernel, rb=rb),
        grid_spec=grid_spec,
        out_shape=jax.ShapeDtypeStruct((n, D_MODEL), F32),
        compiler_params=_cparams(("parallel", "arbitrary")),
        name="moe",
    )(offs, cnts, h, hn, cmb, wg, wu, wd, norm_f_g.reshape(1, D_MODEL))


def _route_weights(w_grp, b_grp, w_rt, b_rt):
    pad = LANES - N_EXPERTS - N_GROUPS
    w = jnp.concatenate([w_rt, w_grp, jnp.zeros((D_MODEL, pad), F32)], axis=1)
    bias = jnp.concatenate([b_rt, b_grp, jnp.zeros((pad,), F32)]).reshape(1, LANES)
    hi = w.astype(BF16)
    return jnp.stack([hi, (w - hi.astype(F32)).astype(BF16)]), bias


def _finish(x2d, ya, yb, ga, gb, wts):
    h, hn, cmb = _tail(x2d, ya, yb, ga, gb, wts["wa"], wts["wb"], wts["wo"], wts["norm2_g"],
                       wts["w_route"], wts["b_route"])
    return _moe(h, hn, cmb, wts["wg"], wts["wu"], wts["wd"], wts["norm_f_g"])


def _prompt_layer(x, wts):
    b, l, _ = x.shape
    x2d = x.reshape(b * l, D_MODEL)
    qkv, z, q8, ga, gb, sm, kv4t, kvwt, cmp_rows, ksa, kwa, vst, vwt = _project(
        x2d, wts["norm1_g"], wts["w_pk"], b, l)
    conv0 = jnp.zeros((b, GDN_CONV - 1, GDN_CONV_DIM), F32)
    s0 = jnp.zeros((b, GDN_HEADS, GDN_DK, GDN_DV), F32)
    ya, s_new, conv_new = _gdn(qkv, z, sm, conv0, s0, wts["conv_w"], wts["a_log"], wts["dt_bias"],
                               wts["gdn_norm_g"], b, l)
    head, tail = _pool_rows(cmp_rows, wts["pw"])
    kca, vct = _compress(head, tail, wts["cmp_wk"], wts["cmp_wv"], b)
    yb = _nsa_prompt(q8, sm, kca, vct, ksa, vst, kwa, vwt, b, l)
    y = _finish(x2d, ya, yb, ga, gb, wts)
    win_buf = min(WINDOW, l)
    kv_new = jnp.transpose(kv4t.reshape(b, 4, NSA_KV_HEADS, NSA_HD, l), (0, 4, 1, 2, 3))
    win_new = jnp.transpose(kvwt[:, :, l - win_buf:].reshape(b, 2, NSA_KV_HEADS, NSA_HD, win_buf), (0, 4, 1, 2, 3))
    return (y.reshape(b, l, D_MODEL), kv_new, win_new, s_new, conv_new)


def _sample_layer(x, cache_kv_l, page_table, cache_win_l, s0, conv_buf, wts):
    b, l, _ = x.shape
    x2d = x.reshape(b * l, D_MODEL)
    qkv, z, q8, ga, gb, sm, kv4, kvw = _project(x2d, wts["norm1_g"], wts["w_pk"], b, l)
    ya, s_new, conv_new = _gdn(qkv, z, sm, conv_buf, s0, wts["conv_w"], wts["a_log"], wts["dt_bias"],
                               wts["gdn_norm_g"], b, l)
    n_phys, page = cache_kv_l.shape[:2]
    w = 2 * NSA_KV
    cache_t = jnp.transpose(cache_kv_l, (0, 2, 3, 4, 1)).reshape(n_phys, 2 * w, page)
    win_buf = cache_win_l.shape[1]
    win_old_t = jnp.transpose(cache_win_l, (0, 2, 3, 4, 1)).reshape(b, w, win_buf)
    head_t, tail_t = _pool_pages(cache_t, page_table, wts["pos_wk"], wts["pos_wv"])
    kct, vct = _compress_t(head_t, tail_t, wts["cmp_wk"], wts["cmp_wv"])
    new_t = jnp.transpose(kv4.reshape(b, l, 2 * w), (0, 2, 1))
    knew_t = jnp.pad(new_t[:, w:], ((0, 0), (0, 0), (0, LANES - l)))
    win_new_t = jnp.transpose(kvw.reshape(b, l, w), (0, 2, 1))
    wcols = -(-(win_buf + l) // LANES) * LANES
    win_t = jnp.concatenate([win_old_t, win_new_t, jnp.zeros((b, w, wcols - win_buf - l), F32)], axis=2)
    yb = _nsa_sample(q8, sm, kct, vct, knew_t, win_t, cache_t, page_table, b, l)
    y = _finish(x2d, ya, yb, ga, gb, wts)
    win_new = jnp.transpose(win_t[:, :, l:l + win_buf].reshape(b, 2, NSA_KV_HEADS, NSA_HD, win_buf), (0, 4, 1, 2, 3))
    return (y.reshape(b, l, D_MODEL), kv4.reshape(b, l, 4, NSA_KV_HEADS, NSA_HD), win_new, s_new, conv_new)


def kernel(x_prompt, x_sample, cache_kv, page_table, cache_win, state_gdn, state_conv, norm1_g, w_in, gdn_conv_w, gdn_a_log, gdn_dt_bias, gdn_norm_g, cmp_pos_wk, cmp_pos_wv, cmp_wk, cmp_wv, w_branch_a, w_branch_b, w_out, norm2_g, w_grp, b_grp, w_rt, b_rt, w_e_gate, w_e_up, w_e_down, norm_f_g):
    assert w_in.shape[0] == 1, "single layer"
    w_route, b_route = _route_weights(w_grp[0], b_grp[0], w_rt[0], b_rt[0])
    wg, wu, wd = _moe_weights(w_e_gate[0], w_e_up[0], w_e_down[0])
    wts = dict(
        norm1_g=norm1_g[0], w_pk=_pack_w_in(w_in[0]),
        conv_w=gdn_conv_w[0], a_log=gdn_a_log[0], dt_bias=gdn_dt_bias[0], gdn_norm_g=gdn_norm_g[0],
        pw=_pos_weights(cmp_pos_wk[0], cmp_pos_wv[0]), pos_wk=cmp_pos_wk[0], pos_wv=cmp_pos_wv[0],
        cmp_wk=cmp_wk[0], cmp_wv=cmp_wv[0],
        wa=w_branch_a[0].astype(BF16), wb=w_branch_b[0].astype(BF16), wo=w_out[0].astype(BF16),
        norm2_g=norm2_g[0], w_route=w_route, b_route=b_route,
        wg=wg, wu=wu, wd=wd, norm_f_g=norm_f_g,
    )
    yp, kvp, winp, sp, cp = _prompt_layer(x_prompt, wts)
    ys, kvs, wins, ss, cs = _sample_layer(x_sample, cache_kv[0], page_table, cache_win[0], state_gdn[0],
                                          state_conv[0], wts)
    return (yp, ys, kvp[None], kvs[None], winp[None], wins[None], sp[None], ss[None], cp[None], cs[None])
```

```python
import functools
import math

import jax
import jax.numpy as jnp
from jax import lax
from jax.experimental import pallas as pl
from jax.experimental.pallas import tpu as pltpu

F32 = jnp.float32
BF16 = jnp.bfloat16
HI = lax.Precision.HIGHEST

LANES = 128
SUBLANES = 8
VMEM_LIMIT = 56 * 1024 * 1024
ROW_TILE = 512

D_MODEL = 1024
GDN_HEADS = 4
GDN_DK = 128
GDN_DV = 128
GDN_QK = GDN_HEADS * GDN_DK
GDN_V = GDN_HEADS * GDN_DV
GDN_CONV_DIM = 2 * GDN_QK + GDN_V
GDN_CONV = 4
GDN_CHUNK = 64
GDN_SEQS_PER_STEP = 4
NSA_HEADS = 8
NSA_KV_HEADS = 2
NSA_GROUP = NSA_HEADS // NSA_KV_HEADS
NSA_HD = 64
NSA_Q = NSA_HEADS * NSA_HD
NSA_KV = NSA_KV_HEADS * NSA_HD
CMP_STRIDE = 16
CMP_LEN = 2 * CMP_STRIDE
SEL_BLOCK = 64
SEL_TOP = 16
WINDOW = 512
FORCE_SCORE = 1.0e4
N_GROUPS = 4
EXPERTS_PER_GROUP = 8
N_EXPERTS = N_GROUPS * EXPERTS_PER_GROUP
D_EXPERT = 256
RMS_EPS = 1e-6
VS_ROWS = NSA_KV + 2 * SUBLANES
NEG_INF = float("-inf")

_OFF_QKV = 0
_OFF_Z = _OFF_QKV + GDN_CONV_DIM
_OFF_B = _OFF_Z + GDN_V
_OFF_A = _OFF_B + GDN_HEADS
_OFF_Q = _OFF_A + GDN_HEADS
_OFF_KV = _OFF_Q + NSA_Q
_OFF_G = _OFF_KV + 6 * NSA_KV
_OFF_GA = _OFF_G + 3 * NSA_HEADS
_OFF_GB = _OFF_GA + D_MODEL

SM_B = 0
SM_A = 4
SM_G = 8

_PK = {}
_c = 0
for _n, _w in (("qkv", GDN_CONV_DIM), ("z", GDN_V), ("q8", NSA_HEADS * LANES), ("kv4", 4 * NSA_KV),
               ("kvw", 2 * NSA_KV), ("ga", D_MODEL), ("gb", D_MODEL), ("sm", LANES)):
    _PK[_n] = (_c, _w)
    _c += _w
PK_DIM = _c


def _cparams(sem):
    return pltpu.CompilerParams(dimension_semantics=sem, vmem_limit_bytes=VMEM_LIMIT)


def _nt(a, b, precision=None):
    return lax.dot_general(a, b, (((1,), (1,)), ((), ())), preferred_element_type=F32, precision=precision)


def _tn(a, b, precision=None):
    return lax.dot_general(a, b, (((0,), (0,)), ((), ())), preferred_element_type=F32, precision=precision)


def _mm(a, b, precision=None):
    return jnp.dot(a, b, preferred_element_type=F32, precision=precision)


def _mmb(a, b):
    return _mm(a.astype(BF16), b.astype(BF16))


def _sigmoid(x):
    return 1.0 / (1.0 + jnp.exp(-x))


def _silu(x):
    return x * _sigmoid(x)


def _iota(shape, dim):
    return lax.broadcasted_iota(jnp.int32, shape, dim)


def _ones_where(cond):
    return jnp.where(cond, 1.0, 0.0).astype(BF16)


def _pack_w_in(w_in):
    q = w_in[:, _OFF_Q:_OFF_Q + NSA_Q]
    zeros64 = jnp.zeros((D_MODEL, NSA_HD), w_in.dtype)
    q8 = []
    for h in range(NSA_HEADS):
        qh = q[:, h * NSA_HD:(h + 1) * NSA_HD]
        q8.append(jnp.concatenate([qh, zeros64] if h < NSA_GROUP else [zeros64, qh], axis=1))
    sm = jnp.concatenate([w_in[:, _OFF_B:_OFF_B + GDN_HEADS], w_in[:, _OFF_A:_OFF_A + GDN_HEADS],
                          w_in[:, _OFF_G:_OFF_G + 3 * NSA_HEADS],
                          jnp.zeros((D_MODEL, LANES - 2 * GDN_HEADS - 3 * NSA_HEADS), w_in.dtype)], axis=1)
    cols = [w_in[:, _OFF_QKV:_OFF_QKV + GDN_CONV_DIM], w_in[:, _OFF_Z:_OFF_Z + GDN_V]] + q8 + [
        w_in[:, _OFF_KV:_OFF_KV + 4 * NSA_KV], w_in[:, _OFF_KV + 4 * NSA_KV:_OFF_KV + 6 * NSA_KV],
        w_in[:, _OFF_GA:_OFF_GA + D_MODEL], w_in[:, _OFF_GB:_OFF_GB + D_MODEL], sm]
    return jnp.concatenate(cols, axis=1).astype(BF16)


def _pos_features(pos, shape):
    lane = _iota(shape, 1)
    feat = jnp.where(lane == 0, (pos >> 7) * LANES, jnp.where(lane == 1, pos & (LANES - 1),
                                                               jnp.where(lane < 4, 1, 0)))
    return feat.astype(F32).astype(BF16)


def _proj_kernel(x_ref, g_ref, w_ref, qkv_ref, z_ref, q8_ref, ga_ref, gb_ref, sm_ref, *kv_refs, tiles_per_seq):
    tm = x_ref.shape[0]
    x = x_ref[...]
    xn = (x * lax.rsqrt(jnp.mean(x * x, axis=-1, keepdims=True) + RMS_EPS) * g_ref[...]).astype(BF16)

    def seg(name):
        a, w = _PK[name]
        return _mm(xn, w_ref[:, a:a + w])

    qkv_ref[...] = seg("qkv")
    z_ref[...] = seg("z")
    q8_ref[...] = (seg("q8") * (NSA_HD ** -0.5)).astype(BF16)
    ga_ref[...] = seg("ga")
    gb_ref[...] = seg("gb")
    sm_ref[...] = seg("sm")
    kv4 = seg("kv4")
    kvw = seg("kvw")
    if tiles_per_seq == 0:
        kv4_ref, kvw_ref = kv_refs
        kv4_ref[...] = kv4
        kvw_ref[...] = kvw
        return
    kv4t_ref, kvwt_ref, cmp_ref, ksa_ref, kwa_ref, vst_ref, vwt_ref = kv_refs
    kv4t = jnp.transpose(kv4)
    kvwt = jnp.transpose(kvw)
    kv4t_ref[...] = kv4t
    kvwt_ref[...] = kvwt
    cmp_ref[...] = kv4[:, :2 * NSA_KV]
    pos = (pl.program_id(0) % tiles_per_seq) * tm + _iota((tm, 1), 0)
    feat = _pos_features(pos, (tm, LANES))
    ksa_ref[:, :NSA_KV] = kv4[:, 2 * NSA_KV:3 * NSA_KV].astype(BF16)
    ksa_ref[:, NSA_KV:] = feat
    kwa_ref[:, :NSA_KV] = kvw[:, :NSA_KV].astype(BF16)
    kwa_ref[:, NSA_KV:] = feat
    vst_ref[:NSA_KV, :] = kv4t[3 * NSA_KV:].astype(BF16)
    ones_row = _ones_where(_iota((VS_ROWS - NSA_KV, tm), 0) == 0)
    vst_ref[NSA_KV:, :] = ones_row
    vwt_ref[:NSA_KV, :] = kvwt[NSA_KV:].astype(BF16)
    vwt_ref[NSA_KV:, :] = ones_row


def _project(x2d, norm_g, w_pk, b, l):
    n = x2d.shape[0]
    tm = min(ROW_TILE, n)
    long_seq = l % tm == 0
    tps = l // tm if long_seq else 0
    w = 2 * NSA_KV
    row = lambda i: (i, 0)
    out_shape = [jax.ShapeDtypeStruct((n, _PK[k][1]), dt) for k, dt in
                 (("qkv", F32), ("z", F32), ("q8", BF16), ("ga", F32), ("gb", F32), ("sm", F32))]
    out_specs = [pl.BlockSpec((tm, s.shape[1]), row) for s in out_shape]
    if long_seq:
        seq_t = lambda i: (i // tps, 0, i % tps)
        out_shape += [jax.ShapeDtypeStruct((b, 2 * w, l), F32), jax.ShapeDtypeStruct((b, w, l), F32),
                      jax.ShapeDtypeStruct((n, w), F32), jax.ShapeDtypeStruct((n, w), BF16),
                      jax.ShapeDtypeStruct((n, w), BF16), jax.ShapeDtypeStruct((VS_ROWS, n), BF16),
                      jax.ShapeDtypeStruct((VS_ROWS, n), BF16)]
        out_specs += [pl.BlockSpec((None, 2 * w, tm), seq_t), pl.BlockSpec((None, w, tm), seq_t),
                      pl.BlockSpec((tm, w), row), pl.BlockSpec((tm, w), row), pl.BlockSpec((tm, w), row),
                      pl.BlockSpec((VS_ROWS, tm), lambda i: (0, i)), pl.BlockSpec((VS_ROWS, tm), lambda i: (0, i))]
    else:
        out_shape += [jax.ShapeDtypeStruct((n, 2 * w), F32), jax.ShapeDtypeStruct((n, w), F32)]
        out_specs += [pl.BlockSpec((tm, 2 * w), row), pl.BlockSpec((tm, w), row)]
    return pl.pallas_call(
        functools.partial(_proj_kernel, tiles_per_seq=tps),
        grid=(n // tm,),
        in_specs=[pl.BlockSpec((tm, D_MODEL), row),
                  pl.BlockSpec((1, D_MODEL), lambda i: (0, 0)),
                  pl.BlockSpec((D_MODEL, PK_DIM), lambda i: (0, 0), pipeline_mode=pl.Buffered(1))],
        out_specs=out_specs,
        out_shape=out_shape,
        compiler_params=_cparams(("parallel",)),
        name="proj",
    )(x2d, norm_g.reshape(1, D_MODEL), w_pk)


def _unit_lower_inverses(a_list, c):
    r = _iota((c, c), 0)
    col = _iota((c, c), 1)
    eye = (r == col).astype(F32)
    n1 = [jnp.where((r >> 3) == (col >> 3), -a, 0.0) for a in a_list]
    n2 = [_mmb(x, x) for x in n1]
    n4 = [_mmb(x, x) for x in n2]
    t = [_mmb(eye + x, eye + y) for x, y in zip(n1, n2)]
    t = [_mmb(x, eye + y) for x, y in zip(t, n4)]
    s = SUBLANES
    while s < c:
        sh = s.bit_length() - 1
        off = ((r >> (sh + 1)) == (col >> (sh + 1))) & ((r >> sh) != (col >> sh))
        ta = [_mmb(x, jnp.where(off, a, 0.0)) for x, a in zip(t, a_list)]
        t = [x - _mmb(y, x) for x, y in zip(t, ta)]
        s *= 2
    return t


def _gdn_kernel(qkv_ref, z_ref, sm_ref, cbuf_ref, s0_ref, cw_ref, alog_ref, dtb_ref, ng_ref,
                y_ref, snew_ref, cnew_ref, ext_ref, st_ref, *, chunk, nb):
    c = chunk
    ci = pl.program_id(1)

    @pl.when(ci == 0)
    def _():
        ext_ref[:, 0:SUBLANES, :] = cbuf_ref[...]
        st_ref[...] = s0_ref[...]

    r = _iota((c, c), 0)
    col = _iota((c, c), 1)
    tri_incl = r >= col
    tri_strict = r > col
    tri_f = tri_incl.astype(F32)
    pick = (_iota((SUBLANES, LANES), 1) == _iota((SUBLANES, LANES), 0) + SM_A).astype(F32)
    base = SUBLANES - (GDN_CONV - 1)
    ng = ng_ref[...]

    units = []
    for bb in range(nb):
        u = qkv_ref[bb]
        ext_ref[bb, SUBLANES:SUBLANES + c, :] = u
        conv = cw_ref[0:1, :] * ext_ref[bb, base:base + c, :]
        for i in range(1, GDN_CONV - 1):
            conv = conv + cw_ref[i:i + 1, :] * ext_ref[bb, base + i:base + i + c, :]
        conv = conv + cw_ref[GDN_CONV - 1:GDN_CONV, :] * u
        halo = ext_ref[bb, c:c + SUBLANES, :]
        ext_ref[bb, 0:SUBLANES, :] = halo
        cnew_ref[bb] = halo
        qkv = _silu(conv)

        sm = sm_ref[bb]
        beta_all = _sigmoid(sm)
        xa = sm + dtb_ref[...]
        softplus = jnp.maximum(xa, 0.0) + jnp.log(1.0 + jnp.exp(-jnp.abs(xa)))
        g_all = -jnp.exp(alog_ref[...]) * softplus
        gcum_all = _mm(tri_f, g_all, HI)
        gcum_rows = _nt(pick, gcum_all, HI)
        for h in range(GDN_HEADS):
            q = qkv[:, h * GDN_DK:(h + 1) * GDN_DK]
            k = qkv[:, GDN_QK + h * GDN_DK:GDN_QK + (h + 1) * GDN_DK]
            v = qkv[:, 2 * GDN_QK + h * GDN_DV:2 * GDN_QK + (h + 1) * GDN_DV]
            q = q * lax.rsqrt(jnp.sum(q * q, axis=-1, keepdims=True) + RMS_EPS) * (GDN_DK ** -0.5)
            k = k * lax.rsqrt(jnp.sum(k * k, axis=-1, keepdims=True) + RMS_EPS)
            beta = beta_all[:, SM_B + h:SM_B + h + 1]
            gc_col = gcum_all[:, SM_A + h:SM_A + h + 1]
            gc_row = gcum_rows[h:h + 1, :]
            gc_last = gcum_all[c - 1:c, SM_A + h:SM_A + h + 1]
            units.append(dict(bb=bb, h=h, q=q, k=k, v=v, beta=beta, gc_col=gc_col, gc_last=gc_last,
                              decay=jnp.exp(jnp.where(tri_incl, gc_col - gc_row, NEG_INF)),
                              eg=jnp.exp(gc_col), kb=k * beta, k_bf=k.astype(BF16)))

    a_list = [jnp.where(tri_strict, _nt(un["kb"].astype(BF16), un["k_bf"]) * un["decay"], 0.0) for un in units]
    t_list = _unit_lower_inverses(a_list, c)
    u_coef = [_mmb(t, un["v"] * un["beta"]) for t, un in zip(t_list, units)]
    w_coef = [_mmb(t, un["kb"] * un["eg"]) for t, un in zip(t_list, units)]
    qk = [(_nt(un["q"].astype(BF16), un["k_bf"]) * un["decay"]).astype(BF16) for un in units]
    s_old = [st_ref[un["bb"], un["h"]] for un in units]
    s_bf = [s.astype(BF16) for s in s_old]
    uu = [uc - _mm(wc.astype(BF16), s) for uc, wc, s in zip(u_coef, w_coef, s_bf)]
    uu_bf = [x.astype(BF16) for x in uu]
    o_list = [_mm((un["q"] * un["eg"]).astype(BF16), s) + _mm(a, x)
              for un, s, a, x in zip(units, s_bf, qk, uu_bf)]
    for un, s, x, o in zip(units, s_old, uu_bf, o_list):
        bb, h = un["bb"], un["h"]
        k_tail = un["k"] * jnp.exp(un["gc_last"] - un["gc_col"])
        st_ref[bb, h] = s * jnp.exp(un["gc_last"]) + _tn(k_tail.astype(BF16), x)
        on = o * lax.rsqrt(jnp.mean(o * o, axis=-1, keepdims=True) + RMS_EPS) * ng
        y_ref[bb, :, h * GDN_DV:(h + 1) * GDN_DV] = on * _silu(z_ref[bb, :, h * GDN_DV:(h + 1) * GDN_DV])

    @pl.when(ci == pl.num_programs(1) - 1)
    def _():
        snew_ref[...] = st_ref[...]


def _gdn(qkv, z, sm, conv_buf, s0, conv_w, a_log, dt_bias, norm_g, b, l):
    c = math.gcd(l, GDN_CHUNK)
    nc = l // c
    nb = math.gcd(b, GDN_SEQS_PER_STEP)
    cbuf8 = jnp.pad(conv_buf, ((0, 0), (SUBLANES - (GDN_CONV - 1), 0), (0, 0)))
    pad_a = (SM_A, LANES - SM_A - GDN_HEADS)
    alog_row = jnp.pad(a_log, pad_a).reshape(1, LANES)
    dtb_row = jnp.pad(dt_bias, pad_a).reshape(1, LANES)
    row = lambda bi, ci: (bi, ci, 0)
    per_seq3 = lambda bi, ci: (bi, 0, 0)
    per_seq4 = lambda bi, ci: (bi, 0, 0, 0)
    fixed = lambda bi, ci: (0, 0)
    y, s_new, c_new = pl.pallas_call(
        functools.partial(_gdn_kernel, chunk=c, nb=nb),
        grid=(b // nb, nc),
        in_specs=[pl.BlockSpec((nb, c, GDN_CONV_DIM), row),
                  pl.BlockSpec((nb, c, GDN_V), row),
                  pl.BlockSpec((nb, c, LANES), row),
                  pl.BlockSpec((nb, SUBLANES, GDN_CONV_DIM), per_seq3),
                  pl.BlockSpec((nb, GDN_HEADS, GDN_DK, GDN_DV), per_seq4),
                  pl.BlockSpec((GDN_CONV, GDN_CONV_DIM), fixed),
                  pl.BlockSpec((1, LANES), fixed),
                  pl.BlockSpec((1, LANES), fixed),
                  pl.BlockSpec((1, GDN_DV), fixed)],
        out_specs=[pl.BlockSpec((nb, c, GDN_V), row),
                   pl.BlockSpec((nb, GDN_HEADS, GDN_DK, GDN_DV), per_seq4),
                   pl.BlockSpec((nb, SUBLANES, GDN_CONV_DIM), per_seq3)],
        out_shape=[jax.ShapeDtypeStruct((b, l, GDN_V), F32),
                   jax.ShapeDtypeStruct((b, GDN_HEADS, GDN_DK, GDN_DV), F32),
                   jax.ShapeDtypeStruct((b, SUBLANES, GDN_CONV_DIM), F32)],
        scratch_shapes=[pltpu.VMEM((nb, c + SUBLANES, GDN_CONV_DIM), F32),
                        pltpu.VMEM((nb, GDN_HEADS, GDN_DK, GDN_DV), F32)],
        compiler_params=_cparams(("parallel", "arbitrary")),
        name="gdn",
    )(qkv.reshape(b, l, GDN_CONV_DIM), z.reshape(b, l, GDN_V), sm.reshape(b, l, LANES), cbuf8, s0, conv_w,
      alog_row, dtb_row, norm_g.reshape(1, GDN_DV))
    return y.reshape(b * l, GDN_V), s_new, c_new[:, SUBLANES - (GDN_CONV - 1):]


def _pool_kernel(*refs, n_in):
    refs = refs[len(refs) - 2 * n_in - 3:]
    x_refs, pw_ref, head_ref, tail_ref = refs[:2 * n_in], refs[2 * n_in], refs[2 * n_in + 1], refs[2 * n_in + 2]
    rows = x_refs[0].shape[0]
    n = rows // CMP_STRIDE
    for j, x_ref in enumerate(x_refs):
        half = slice((j % 2) * NSA_KV, (j % 2 + 1) * NSA_KV)
        head = None
        tail = None
        for p in range(CMP_STRIDE):
            xr = x_ref[pl.ds(p, n, stride=CMP_STRIDE), :]
            hp = xr * pw_ref[p:p + 1, half]
            tp = xr * pw_ref[CMP_STRIDE + p:CMP_STRIDE + p + 1, half]
            head = hp if head is None else head + hp
            tail = tp if tail is None else tail + tp
        head_ref[(j // 2) * n:(j // 2 + 1) * n, half] = head
        tail_ref[(j // 2) * n:(j // 2 + 1) * n, half] = tail


def _pos_weights(pos_wk, pos_wv):
    return jnp.concatenate([jnp.broadcast_to(pos_wk[:, None], (CMP_LEN, NSA_KV)),
                            jnp.broadcast_to(pos_wv[:, None], (CMP_LEN, NSA_KV))], axis=1).astype(F32)


def _pool_rows(kv4, pw):
    n = kv4.shape[0]
    r = min(2048, n)
    w = 2 * NSA_KV
    return pl.pallas_call(
        functools.partial(_pool_kernel, n_in=1),
        grid=(n // r,),
        in_specs=[pl.BlockSpec((r, NSA_KV), lambda i: (i, 0)),
                  pl.BlockSpec((r, NSA_KV), lambda i: (i, 1)),
                  pl.BlockSpec((CMP_LEN, w), lambda i: (0, 0))],
        out_specs=[pl.BlockSpec((r // CMP_STRIDE, w), lambda i: (i, 0))] * 2,
        out_shape=[jax.ShapeDtypeStruct((n // CMP_STRIDE, w), F32)] * 2,
        compiler_params=_cparams(("parallel",)),
        name="pool_rows",
    )(kv4, kv4, pw)


def _pool_pages_kernel(*refs, pg):
    page_refs = refs[1:1 + pg]
    pk, pv, head_ref, tail_ref = refs[1 + pg:]
    x = jnp.concatenate([r[...] for r in page_refs], axis=1)
    n = head_ref.shape[1]
    for half, pool in enumerate((pk, pv)):
        xs = x[half * NSA_KV:(half + 1) * NSA_KV]
        out = _mm(xs.astype(BF16), pool[...])
        head_ref[half * NSA_KV:(half + 1) * NSA_KV, :] = out[:, :n]
        tail_ref[half * NSA_KV:(half + 1) * NSA_KV, :] = out[:, n:]


def _pool_matrix(pos_w, positions):
    chunks = positions // CMP_STRIDE
    p = jnp.arange(positions)[:, None]
    c = jnp.arange(chunks)[None, :]
    inside = (p // CMP_STRIDE) == c
    head = jnp.where(inside, pos_w[:CMP_STRIDE][p % CMP_STRIDE], 0.0)
    tail = jnp.where(inside, pos_w[CMP_STRIDE:][p % CMP_STRIDE], 0.0)
    return jnp.concatenate([head, tail], axis=1).astype(BF16)


def _pool_pages(cache_t, page_table, pos_wk, pos_wv):
    b, n_pages = page_table.shape
    page = cache_t.shape[2]
    pg = LANES * CMP_STRIDE // page
    w = 2 * NSA_KV
    steps = n_pages // pg
    n_ch = n_pages * page // CMP_STRIDE
    mats = (_pool_matrix(pos_wk, pg * page), _pool_matrix(pos_wv, pg * page))

    def page_map(k):
        return lambda bi, j, pt: (pt[bi, j * pg + k], 0, 0)

    fixed = lambda bi, j, pt: (0, 0)
    grid_spec = pltpu.PrefetchScalarGridSpec(
        num_scalar_prefetch=1,
        grid=(b, steps),
        in_specs=[pl.BlockSpec((None, w, page), page_map(k)) for k in range(pg)]
        + [pl.BlockSpec((pg * page, 2 * LANES), fixed)] * 2,
        out_specs=[pl.BlockSpec((None, w, LANES), lambda bi, j, pt: (bi, 0, j))] * 2,
    )
    return pl.pallas_call(
        functools.partial(_pool_pages_kernel, pg=pg),
        grid_spec=grid_spec,
        out_shape=[jax.ShapeDtypeStruct((b, w, n_ch), F32)] * 2,
        compiler_params=_cparams(("parallel", "arbitrary")),
        name="pool_pages",
    )(page_table, *([cache_t] * pg), *mats)


def _cmp_t_kernel(head_ref, tail_ref, wk_ref, wv_ref, kct_ref, vct_ref):
    n = head_ref.shape[1]
    blocks = head_ref[...] + pltpu.roll(tail_ref[...], n - 1, 1)
    kct_ref[...] = _mm(wk_ref[...], blocks[:NSA_KV].astype(BF16)).astype(BF16)
    vct_ref[...] = _mm(wv_ref[...], blocks[NSA_KV:].astype(BF16)).astype(BF16)


def _compress_t(head_t, tail_t, cmp_wk, cmp_wv):
    b, w, n_ch = head_t.shape
    seq = lambda i: (i, 0, 0)
    return pl.pallas_call(
        _cmp_t_kernel,
        grid=(b,),
        in_specs=[pl.BlockSpec((None, w, n_ch), seq)] * 2 + [pl.BlockSpec((NSA_KV, NSA_KV), lambda i: (0, 0))] * 2,
        out_specs=[pl.BlockSpec((None, NSA_KV, n_ch), seq)] * 2,
        out_shape=[jax.ShapeDtypeStruct((b, NSA_KV, n_ch), BF16)] * 2,
        compiler_params=_cparams(("parallel",)),
        name="compress_t",
    )(head_t, tail_t, jnp.transpose(_block_diag2(cmp_wk)), jnp.transpose(_block_diag2(cmp_wv)))


def _cmp_kernel(head_ref, tail_ref, wk_ref, wv_ref, kca_ref, vct_ref):
    n = head_ref.shape[0]
    blocks = head_ref[...] + pltpu.roll(tail_ref[...], n - 1, 0)
    kca_ref[:, :NSA_KV] = _mm(blocks[:, :NSA_KV].astype(BF16), wk_ref[...]).astype(BF16)
    blk_end = _iota((n, 1), 0) * CMP_STRIDE + (CMP_LEN - 1)
    kca_ref[:, NSA_KV:] = _pos_features(blk_end, (n, LANES))
    vct_ref[...] = _nt(wv_ref[...], blocks[:, NSA_KV:].astype(BF16)).astype(BF16)


def _block_diag2(w):
    z = jnp.zeros_like(w)
    return jnp.concatenate([jnp.concatenate([w, z], axis=1), jnp.concatenate([z, w], axis=1)], axis=0).astype(BF16)


def _compress(head, tail, cmp_wk, cmp_wv, b):
    n_ch = head.shape[0] // b
    w = 2 * NSA_KV
    return pl.pallas_call(
        _cmp_kernel,
        grid=(b,),
        in_specs=[pl.BlockSpec((n_ch, w), lambda i: (i, 0))] * 2
        + [pl.BlockSpec((NSA_KV, NSA_KV), lambda i: (0, 0))] * 2,
        out_specs=[pl.BlockSpec((n_ch, w), lambda i: (i, 0)),
                   pl.BlockSpec((NSA_KV, n_ch), lambda i: (0, i))],
        out_shape=[jax.ShapeDtypeStruct((b * n_ch, w), BF16),
                   jax.ShapeDtypeStruct((NSA_KV, b * n_ch), BF16)],
        compiler_params=_cparams(("parallel",)),
        name="compress",
    )(head, tail, _block_diag2(cmp_wk), jnp.transpose(_block_diag2(cmp_wv)))


def _overlap(n_ch, nsp):
    i = jnp.arange(n_ch)[:, None]
    j = jnp.arange(nsp)[None, :]
    lo = jnp.maximum(i * CMP_STRIDE, j * SEL_BLOCK)
    hi = jnp.minimum(i * CMP_STRIDE + CMP_LEN, (j + 1) * SEL_BLOCK)
    ov = jnp.maximum(hi - lo, 0).astype(F32) / CMP_LEN
    return jnp.where(i < n_ch - 1, ov, 0.0)


def _slope(h):
    return 2.0 ** (-(h + 1))


def _softmax_rows(s, mask):
    s = jnp.where(mask, s, NEG_INF)
    m = jnp.max(s, axis=-1, keepdims=True)
    m = jnp.where(m == NEG_INF, 0.0, m)
    e = jnp.where(mask, jnp.exp(s - m), 0.0)
    return e, jnp.maximum(jnp.sum(e, axis=-1, keepdims=True), 1e-30)


def _top_blocks(imp, qpos, ns_lanes):
    blk = _iota(imp.shape, 1)
    cur = qpos >> 6
    forced = (blk == 0) | (blk == cur) | (blk == cur - 1)
    score = jnp.where(forced, FORCE_SCORE, imp)
    work = jnp.where(blk <= cur, score, NEG_INF)
    sel = jnp.zeros(imp.shape, F32)
    for _ in range(SEL_TOP):
        m = jnp.max(work, axis=-1, keepdims=True)
        cand = jnp.where((work == m) & (m > NEG_INF), blk, ns_lanes)
        first = jnp.min(cand, axis=-1, keepdims=True)
        hit = blk == first
        sel = jnp.where(hit, 1.0, sel)
        work = jnp.where(hit, NEG_INF, work)
    return sel


def _row_slopes(tq):
    head = _iota((NSA_HEADS * tq, 1), 0) >> (tq.bit_length() - 1)
    slope = jnp.zeros((NSA_HEADS * tq, 1), F32)
    for h in range(NSA_HEADS):
        slope = jnp.where(head == h, _slope(h), slope)
    return slope


def _rows_attend(q8, kt, vt, slope, dist, mask):
    s = _mm(q8, kt) - slope * dist.astype(F32)
    e, den = _softmax_rows(s, mask)
    p = e / den
    return _nt(p.astype(BF16), vt), p


def _rows_online(q8, kt, vt, slope, dist, mask, m_ref, l_ref, acc_ref):
    s = jnp.where(mask, _mm(q8, kt) - slope * dist.astype(F32), NEG_INF)
    m_old = m_ref[...]
    m_new = jnp.maximum(m_old, jnp.max(s, axis=-1, keepdims=True))
    m_use = jnp.where(m_new == NEG_INF, 0.0, m_new)
    alpha = jnp.exp(m_old - m_use)
    e = jnp.exp(s - m_use)
    l_ref[...] = alpha * l_ref[...] + jnp.sum(e, axis=-1, keepdims=True)
    acc_ref[...] = alpha * acc_ref[...] + _nt(e.astype(BF16), vt)
    m_ref[...] = m_new


def _merge_heads(o_c, o_s, o_w, gates, tq):
    lane = _iota((tq, LANES), 1)
    mixed = []
    for h in range(NSA_HEADS):
        rows = slice(h * tq, (h + 1) * tq)
        g0 = gates[:, SM_G + 3 * h:SM_G + 3 * h + 1]
        g1 = gates[:, SM_G + 3 * h + 1:SM_G + 3 * h + 2]
        g2 = gates[:, SM_G + 3 * h + 2:SM_G + 3 * h + 3]
        mixed.append(g0 * o_c[rows] + g1 * o_s[rows] + g2 * o_w[rows])
    cols = []
    for p in range(NSA_HEADS // 2):
        a, b = mixed[2 * p], mixed[2 * p + 1]
        if 2 * p < NSA_GROUP:
            cols.append(jnp.where(lane < NSA_HD, a, pltpu.roll(b, NSA_HD, 1)))
        else:
            cols.append(jnp.where(lane < NSA_HD, pltpu.roll(a, NSA_HD, 1), b))
    return jnp.concatenate(cols, axis=1)


def _softmax_cols(s, mask):
    s = jnp.where(mask, s, NEG_INF)
    m = jnp.max(s, axis=0, keepdims=True)
    m = jnp.where(m == NEG_INF, 0.0, m)
    e = jnp.exp(s - m)
    return e * (1.0 / jnp.maximum(jnp.sum(e, axis=0, keepdims=True), 1e-30))


def _split3(x):
    hi = x.astype(BF16)
    r1 = x - hi.astype(F32)
    mid = r1.astype(BF16)
    lo = (r1 - mid.astype(F32)).astype(BF16)
    return hi, mid, lo


def _top_blocks_cols(imp, qpos, nsp):
    blk = _iota(imp.shape, 0)
    cur = qpos >> 6
    forced = (blk == 0) | (blk == cur) | (blk == cur - 1)
    work = jnp.where(blk <= cur, jnp.where(forced, FORCE_SCORE, imp), NEG_INF)
    neg = jnp.full(imp.shape, NEG_INF, F32)
    for _ in range(SEL_TOP):
        m = jnp.max(work, axis=0, keepdims=True)
        cand = jnp.where((work == m) & (m > NEG_INF), blk, nsp)
        hit = blk == jnp.min(cand, axis=0, keepdims=True)
        neg = jnp.where(hit, 0.0, neg)
        work = jnp.where(hit, NEG_INF, work)
    return neg


def _nsa_prompt_kernel(q8_ref, sm_ref, kc_ref, vct_ref, ks_ref, vst_ref, kw_ref, vwt_ref, ovt_ref, y_ref,
                       neg_ref, m_ref, acc_ref, qta_ref, sa_ref, sb_ref, act_ref, *, tq, tk):
    i = pl.program_id(1)
    start = i * tq
    r = NSA_HEADS * tq
    nsp, n_ch = ovt_ref.shape
    col = _iota((1, r), 1)
    qpos = start + (col & (tq - 1))

    q8 = jnp.concatenate([q8_ref[:, h * LANES:(h + 1) * LANES] for h in range(NSA_HEADS)], axis=0)
    eye = (_iota((LANES, LANES), 0) == _iota((LANES, LANES), 1)).astype(BF16)
    qt = _nt(eye, q8).astype(BF16)
    head = col >> (tq.bit_length() - 1)
    slope = jnp.zeros((1, r), F32)
    for h in range(NSA_HEADS):
        slope = jnp.where(head == h, _slope(h), slope)
    frow = _iota((2 * SUBLANES, r), 0)
    feat = jnp.where(frow < 2, slope, jnp.where(frow == 2, -slope * ((qpos >> 7) * LANES).astype(F32),
                                                jnp.where(frow == 3, -slope * (qpos & (LANES - 1)).astype(F32), 0.0)))
    qta = jnp.concatenate([qt, feat.astype(BF16), jnp.zeros((LANES - 2 * SUBLANES, r), BF16)], axis=0)

    blk_end = _iota((n_ch, 1), 0) * CMP_STRIDE + (CMP_LEN - 1)
    p = _softmax_cols(_mm(kc_ref[...], qta), blk_end <= qpos)
    o_c = _mm(vct_ref[...], p.astype(BF16))
    psum = []
    for g in range(NSA_KV_HEADS):
        acc = p[:, g * NSA_GROUP * tq:(g * NSA_GROUP + 1) * tq]
        for hh in range(1, NSA_GROUP):
            acc = acc + p[:, (g * NSA_GROUP + hh) * tq:(g * NSA_GROUP + hh + 1) * tq]
        psum.append(acc)
    ovt = ovt_ref[...]
    imp = sum(_mm(ovt, piece) for piece in _split3(jnp.concatenate(psum, axis=1)))
    qpos2 = start + (_iota((1, NSA_KV_HEADS * tq), 1) & (tq - 1))
    neg = _top_blocks_cols(imp, qpos2, nsp)
    for h in range(NSA_HEADS):
        g = h // NSA_GROUP
        neg_ref[:, h * tq:(h + 1) * tq] = neg[:, g * tq:(g + 1) * tq]

    m_ref[...] = jnp.full(m_ref.shape, NEG_INF, F32)
    acc_ref[...] = jnp.zeros(acc_ref.shape, F32)
    bpt = tk // SEL_BLOCK

    qta_ref[...] = qta

    def scores(t, buf):
        off = pl.multiple_of(t * tk, tk)
        buf[...] = _mm(ks_ref[pl.ds(off, tk), :], qta_ref[...])

    def tile(t, buf, causal):
        off = pl.multiple_of(t * tk, tk)
        parts = []
        for j in range(bpt):
            sj = buf[j * SEL_BLOCK:(j + 1) * SEL_BLOCK, :]
            neg = neg_ref[pl.ds(t * bpt + j, 1), :]
            if causal:
                sj = jnp.where(off + j * SEL_BLOCK + _iota((SEL_BLOCK, 1), 0) <= qpos, sj + neg, NEG_INF)
                parts.append(sj.astype(BF16))
            else:
                parts.append(sj.astype(BF16) + neg.astype(BF16))
        s = jnp.concatenate(parts, axis=0)
        m_old = m_ref[...]
        m_new = jnp.maximum(m_old, jnp.max(s, axis=0, keepdims=True).astype(F32))
        m_use = jnp.where(m_new == NEG_INF, 0.0, m_new)
        alpha = jnp.exp(m_old - m_use)
        e = jnp.exp(s - m_use.astype(BF16))
        acc_ref[...] = alpha * acc_ref[...] + _mm(vst_ref[:, pl.ds(off, tk)], e)
        m_ref[...] = m_new

    n_full = start // tk
    blk_any = jnp.max(neg, axis=1, keepdims=True)
    cnt = jnp.int32(0)
    for t in range(ks_ref.shape[0] // tk):
        live = (jnp.max(blk_any[t * bpt:(t + 1) * bpt]) > NEG_INF) & (t < n_full)
        act_ref[cnt] = t
        cnt = cnt + live.astype(jnp.int32)
    act_ref[cnt] = n_full
    pairs = cnt // 2
    scores(act_ref[0], sa_ref)

    def body(u, carry):
        scores(act_ref[2 * u + 1], sb_ref)
        tile(act_ref[2 * u], sa_ref, False)
        scores(act_ref[2 * u + 2], sa_ref)
        tile(act_ref[2 * u + 1], sb_ref, False)
        return carry

    lax.fori_loop(0, pairs, body, 0)

    @pl.when(cnt % 2 == 1)
    def _():
        scores(n_full, sb_ref)
        tile(act_ref[cnt - 1], sa_ref, False)
        tile(n_full, sb_ref, True)

    @pl.when(cnt % 2 == 0)
    def _():
        tile(n_full, sa_ref, True)
    o_s = acc_ref[:NSA_KV, :] * (1.0 / jnp.maximum(acc_ref[NSA_KV:NSA_KV + 1, :], 1e-30))

    wl = tq + WINDOW
    ws = pl.multiple_of(jnp.maximum(start - WINDOW, 0), tq)
    dist = qpos - (ws + _iota((wl, 1), 0))
    sw = jnp.where((dist >= 0) & (dist <= WINDOW), _mm(kw_ref[pl.ds(ws, wl), :], qta), NEG_INF).astype(BF16)
    ew = jnp.exp(sw - jnp.max(sw, axis=0, keepdims=True))
    ow = _mm(vwt_ref[:, pl.ds(ws, wl)], ew)
    o_w = ow[:NSA_KV] * (1.0 / jnp.maximum(ow[NSA_KV:NSA_KV + 1], 1e-30))

    gt = jnp.transpose(_sigmoid(sm_ref[...]))
    gate = [jnp.concatenate([gt[SM_G + 3 * h + c:SM_G + 3 * h + c + 1, :] for h in range(NSA_HEADS)], axis=1)
            for c in range(3)]
    o = gate[0] * o_c + gate[1] * o_s + gate[2] * o_w
    yt = jnp.concatenate([o[(h // NSA_GROUP) * NSA_HD:(h // NSA_GROUP + 1) * NSA_HD, h * tq:(h + 1) * tq]
                          for h in range(NSA_HEADS)], axis=0)
    y_ref[...] = jnp.transpose(yt)


def _nsa_prompt(q8, sm, kca, vct, ksa, vst, kwa, vwt, b, l):
    tq = 256
    tk = 256
    assert l % tk == 0 and l >= tq + WINDOW
    n_ch = l // CMP_STRIDE
    ns = -(-l // SEL_BLOCK)
    nsp = -(-ns // LANES) * LANES
    nq = l // tq
    w = 2 * NSA_KV
    ovt = jnp.transpose(_overlap(n_ch, nsp)).astype(BF16)
    row = lambda bi, i: (bi * nq + i, 0)
    seq_rows = lambda bi, i: (bi, 0)
    seq_cols = lambda bi, i: (0, bi)
    return pl.pallas_call(
        functools.partial(_nsa_prompt_kernel, tq=tq, tk=tk),
        grid=(b, nq),
        in_specs=[pl.BlockSpec((tq, NSA_HEADS * LANES), row),
                  pl.BlockSpec((tq, LANES), row),
                  pl.BlockSpec((n_ch, w), seq_rows),
                  pl.BlockSpec((NSA_KV, n_ch), seq_cols),
                  pl.BlockSpec((l, w), seq_rows),
                  pl.BlockSpec((VS_ROWS, l), seq_cols),
                  pl.BlockSpec((l, w), seq_rows),
                  pl.BlockSpec((VS_ROWS, l), seq_cols),
                  pl.BlockSpec((nsp, n_ch), lambda bi, i: (0, 0))],
        out_specs=pl.BlockSpec((tq, NSA_Q), row),
        out_shape=jax.ShapeDtypeStruct((b * l, NSA_Q), F32),
        scratch_shapes=[pltpu.VMEM((nsp, NSA_HEADS * tq), F32),
                        pltpu.VMEM((1, NSA_HEADS * tq), F32),
                        pltpu.VMEM((VS_ROWS, NSA_HEADS * tq), F32),
                        pltpu.VMEM((2 * NSA_KV, NSA_HEADS * tq), BF16),
                        pltpu.VMEM((tk, NSA_HEADS * tq), F32),
                        pltpu.VMEM((tk, NSA_HEADS * tq), F32),
                        pltpu.SMEM((l // tk + 1,), jnp.int32)],
        compiler_params=_cparams(("parallel", "arbitrary")),
        name="nsa_prompt",
    )(q8, sm, kca, vct, ksa, vst, kwa, vwt, ovt)


def _nsa_sample_kernel(*refs, pg, page, tq, past_len):
    pt_ref = refs[0]
    del pt_ref
    page_refs = refs[1:1 + pg]
    (q8_ref, sm_ref, kc_ref, vc_ref, knew_ref, kw_ref, ov_ref, y_ref,
     m_ref, l_ref, acc_ref, sel_ref, oc_ref) = refs[1 + pg:]
    j = pl.program_id(1)
    nsp = ov_ref.shape[1]
    r = NSA_HEADS * tq
    q8 = jnp.concatenate([q8_ref[:, h * LANES:(h + 1) * LANES].astype(F32) for h in range(NSA_HEADS)],
                         axis=0).astype(BF16)
    qpos = past_len + (_iota((r, 1), 0) & (tq - 1))
    slope = _row_slopes(tq)

    @pl.when(j == 0)
    def _():
        n_ch = kc_ref.shape[1]
        dist = qpos - (_iota((1, n_ch), 1) * CMP_STRIDE + (CMP_LEN - 1))
        o_c, p = _rows_attend(q8, kc_ref[...], vc_ref[...], slope, dist, dist >= 0)
        oc_ref[...] = o_c
        psums = []
        for g in range(NSA_KV_HEADS):
            psum = p[g * NSA_GROUP * tq:(g * NSA_GROUP + 1) * tq]
            for hh in range(1, NSA_GROUP):
                psum = psum + p[(g * NSA_GROUP + hh) * tq:(g * NSA_GROUP + hh + 1) * tq]
            psums.append(psum)
        sel = _top_blocks(_mm(jnp.concatenate(psums, axis=0), ov_ref[...], HI), qpos[:NSA_KV_HEADS * tq], nsp)
        for h in range(NSA_HEADS):
            g = h // NSA_GROUP
            sel_ref[h * tq:(h + 1) * tq, :] = sel[g * tq:(g + 1) * tq]
        m_ref[...] = jnp.full(m_ref.shape, NEG_INF, F32)
        l_ref[...] = jnp.zeros(l_ref.shape, F32)
        acc_ref[...] = jnp.zeros(acc_ref.shape, F32)

    sel_bf = sel_ref[...].astype(BF16)

    def sel_step(kv, kpos):
        n = kv.shape[1]
        expand = jnp.where(_iota((nsp, n), 0) == (kpos >> 6), 1.0, 0.0).astype(BF16)
        dist = qpos - kpos
        mask = (_mm(sel_bf, expand) > 0.5) & (dist >= 0)
        _rows_online(q8, kv[:NSA_KV].astype(BF16), kv[NSA_KV:].astype(BF16), slope, dist, mask,
                     m_ref, l_ref, acc_ref)

    sel_step(jnp.concatenate([page_refs[k][...] for k in range(pg)], axis=1),
             j * (pg * page) + _iota((1, pg * page), 1))

    @pl.when(j == pl.num_programs(1) - 1)
    def _():
        sel_step(knew_ref[...], past_len + _iota((1, knew_ref.shape[1]), 1))
        o_s = acc_ref[...] / jnp.maximum(l_ref[...], 1e-30)
        kw = kw_ref[...]
        kwpos = past_len - WINDOW + _iota((1, kw.shape[1]), 1)
        dist = qpos - kwpos
        mask = (dist >= 0) & (dist <= WINDOW) & (kwpos >= 0) & (kwpos < past_len + tq)
        o_w, _ = _rows_attend(q8, kw[:NSA_KV].astype(BF16), kw[NSA_KV:].astype(BF16), slope, dist, mask)
        y_ref[...] = _merge_heads(oc_ref[...], o_s, o_w, _sigmoid(sm_ref[...]), tq)


def _nsa_sample(q8, sm, kct, vct, knew_t, win_t, cache_t, page_table, b, l):
    n_pages = page_table.shape[1]
    page = cache_t.shape[2]
    past_len = n_pages * page
    assert l <= SEL_BLOCK and past_len % SEL_BLOCK == 0 and past_len >= WINDOW
    pg = 16
    steps = n_pages // pg
    n_ch = past_len // CMP_STRIDE
    ns = -(-(past_len + l) // SEL_BLOCK)
    nsp = -(-ns // LANES) * LANES
    w = 2 * NSA_KV
    ov = _overlap(n_ch, nsp)
    wcols = win_t.shape[2]

    def page_map(k):
        return lambda bi, j, pt: (pt[bi, j * pg + k], 1, 0)

    per_b = lambda bi, j, pt: (bi, 0)
    per_b3 = lambda bi, j, pt: (bi, 0, 0)
    grid_spec = pltpu.PrefetchScalarGridSpec(
        num_scalar_prefetch=1,
        grid=(b, steps),
        in_specs=[pl.BlockSpec((None, w, page), page_map(k)) for k in range(pg)]
        + [pl.BlockSpec((None, l, NSA_HEADS * LANES), per_b3),
           pl.BlockSpec((l, LANES), per_b),
           pl.BlockSpec((None, NSA_KV, n_ch), per_b3),
           pl.BlockSpec((None, NSA_KV, n_ch), per_b3),
           pl.BlockSpec((None, w, LANES), per_b3),
           pl.BlockSpec((None, w, wcols), per_b3),
           pl.BlockSpec((n_ch, nsp), lambda bi, j, pt: (0, 0))],
        out_specs=pl.BlockSpec((l, NSA_Q), per_b),
        scratch_shapes=[pltpu.VMEM((NSA_HEADS * l, 1), F32),
                        pltpu.VMEM((NSA_HEADS * l, 1), F32),
                        pltpu.VMEM((NSA_HEADS * l, NSA_KV), F32),
                        pltpu.VMEM((NSA_HEADS * l, nsp), F32),
                        pltpu.VMEM((NSA_HEADS * l, NSA_KV), F32)],
    )
    return pl.pallas_call(
        functools.partial(_nsa_sample_kernel, pg=pg, page=page, tq=l, past_len=past_len),
        grid_spec=grid_spec,
        out_shape=jax.ShapeDtypeStruct((b * l, NSA_Q), F32),
        compiler_params=_cparams(("parallel", "arbitrary")),
        name="nsa_sample",
    )(page_table, *([cache_t] * pg), q8.reshape(b, l, NSA_HEADS * LANES), sm, kct, vct, knew_t, win_t, ov)


def _tail_kernel(x_ref, ya_ref, yb_ref, ga_ref, gb_ref, wa_ref, wb_ref, wo_ref, n2_ref, wr_ref, br_ref,
                 h_ref, hn_ref, cmb_ref):
    ma = _mm(ya_ref[...].astype(BF16), wa_ref[...])
    mb = _mm(yb_ref[...].astype(BF16), wb_ref[...])
    m = _sigmoid(ga_ref[...]) * ma + _sigmoid(gb_ref[...]) * mb
    h = x_ref[...] + _mm(m.astype(BF16), wo_ref[...])
    h_ref[...] = h
    hn = h * lax.rsqrt(jnp.mean(h * h, axis=-1, keepdims=True) + RMS_EPS) * n2_ref[...]
    hn_hi = hn.astype(BF16)
    hn_ref[...] = hn_hi

    hn_lo = (hn - hn_hi.astype(F32)).astype(BF16)
    logit = _mm(hn_hi, wr_ref[0]) + _mm(hn_lo, wr_ref[0]) + _mm(hn_hi, wr_ref[1]) + br_ref[...]
    lane = _iota(logit.shape, 1)
    is_grp = (lane >= N_EXPERTS) & (lane < N_EXPERTS + N_GROUPS)
    gl = jnp.where(is_grp, logit, NEG_INF)
    gmax = jnp.max(gl, axis=-1, keepdims=True)
    gidx = jnp.min(jnp.where(gl == gmax, lane, LANES), axis=-1, keepdims=True) - N_EXPERTS
    p_grp = 1.0 / jnp.sum(jnp.exp(gl - gmax), axis=-1, keepdims=True)
    el = jnp.where((lane >> 3) == gidx, logit, NEG_INF)
    t1 = jnp.max(el, axis=-1, keepdims=True)
    i1 = jnp.min(jnp.where(el == t1, lane, LANES), axis=-1, keepdims=True)
    el2 = jnp.where(lane == i1, NEG_INF, el)
    t2 = jnp.max(el2, axis=-1, keepdims=True)
    i2 = jnp.min(jnp.where(el2 == t2, lane, LANES), axis=-1, keepdims=True)
    e2 = jnp.exp(t2 - t1)
    w1 = p_grp / (1.0 + e2)
    w2 = p_grp * e2 / (1.0 + e2)
    cmb_ref[...] = jnp.where(lane == i1, w1, jnp.where(lane == i2, w2,
                                                        jnp.where(lane == gidx + N_EXPERTS, 1.0, 0.0)))


def _tail(x2d, ya, yb, ga, gb, wa, wb, wo, norm2_g, w_route, b_route):
    n = x2d.shape[0]
    tm = min(ROW_TILE, n)
    row = lambda i: (i, 0)
    fixed = lambda i: (0, 0)
    return pl.pallas_call(
        _tail_kernel,
        grid=(n // tm,),
        in_specs=[pl.BlockSpec((tm, D_MODEL), row),
                  pl.BlockSpec((tm, GDN_V), row),
                  pl.BlockSpec((tm, NSA_Q), row),
                  pl.BlockSpec((tm, D_MODEL), row),
                  pl.BlockSpec((tm, D_MODEL), row),
                  pl.BlockSpec((GDN_V, D_MODEL), fixed),
                  pl.BlockSpec((NSA_Q, D_MODEL), fixed),
                  pl.BlockSpec((D_MODEL, D_MODEL), fixed),
                  pl.BlockSpec((1, D_MODEL), fixed),
                  pl.BlockSpec((2, D_MODEL, LANES), lambda i: (0, 0, 0)),
                  pl.BlockSpec((1, LANES), fixed)],
        out_specs=[pl.BlockSpec((tm, D_MODEL), row),
                   pl.BlockSpec((tm, D_MODEL), row),
                   pl.BlockSpec((tm, LANES), row)],
        out_shape=[jax.ShapeDtypeStruct((n, D_MODEL), F32),
                   jax.ShapeDtypeStruct((n, D_MODEL), BF16),
                   jax.ShapeDtypeStruct((n, LANES), F32)],
        compiler_params=_cparams(("parallel",)),
        name="tail",
    )(x2d, ya, yb, ga, gb, wa, wb, wo, norm2_g.reshape(1, D_MODEL), w_route, b_route)


MOE_SLABS = 2 * N_GROUPS
MOE_SLAB_EXPERTS = N_EXPERTS // MOE_SLABS


def _moe_kernel(offs_ref, cnts_ref, h_ref, hn_ref, cmb_ref, wg_ref, wu_ref, wd_ref, nf_ref, y_ref,
                xs_ref, cs_ref, pt_ref, acc_ref, *, rb):
    i = pl.program_id(0)
    sl = pl.program_id(1)
    t = hn_ref.shape[0]
    lane = _iota((t, LANES), 1)

    @pl.when(sl == 0)
    def _():
        cmb = cmb_ref[...]
        oh = jnp.where((lane >= N_EXPERTS) & (lane < N_EXPERTS + N_GROUPS), cmb, 0.0)
        oh_bf = oh.astype(BF16)
        r_i = _iota((t, t), 0)
        c_i = _iota((t, t), 1)
        lt = _iota((LANES, LANES), 0) < _iota((LANES, LANES), 1)
        before = _mm(_ones_where(c_i < r_i), oh_bf)
        totals = jnp.broadcast_to(jnp.sum(oh, axis=0, keepdims=True), (SUBLANES, LANES))
        smaller = _mm(totals, lt.astype(F32), HI)[0:1]
        rank_col = jnp.sum((before + smaller) * oh, axis=1, keepdims=True)
        eye = (_iota((LANES, LANES), 0) == _iota((LANES, LANES), 1)).astype(BF16)
        oht = _nt(eye, oh_bf)
        before_t = _mm(oht.astype(BF16), _ones_where(r_i < c_i))
        totals_t = jnp.broadcast_to(jnp.sum(oht, axis=1, keepdims=True), (LANES, LANES))
        gt = _iota((LANES, LANES), 0) > _iota((LANES, LANES), 1)
        smaller_t = _mm(gt.astype(F32), totals_t, HI)[:, 0:1]
        rank_row = jnp.sum((before_t + smaller_t) * oht, axis=0, keepdims=True)
        perm = _ones_where(r_i == rank_row.astype(jnp.int32))
        pt_ref[...] = _ones_where(c_i == rank_col.astype(jnp.int32))
        xs_ref[...] = _mm(perm, hn_ref[...]).astype(BF16)
        c_hi = cmb.astype(BF16)
        c_lo = (cmb - c_hi.astype(F32)).astype(BF16)
        cs_ref[...] = _mm(perm, c_hi) + _mm(perm, c_lo)
        acc_ref[...] = jnp.zeros(acc_ref.shape, F32)

    g = sl // (MOE_SLABS // N_GROUPS)
    off = offs_ref[i, g]
    row0 = (off // (2 * SUBLANES)) * (2 * SUBLANES)
    rows = off + cnts_ref[i, g] - row0
    nfull = rows // rb
    rem = rows - nfull * rb

    def block(lo, bs):
        r0 = pl.multiple_of(jnp.minimum(lo, t - bs), 2 * SUBLANES)
        x = xs_ref[pl.ds(r0, bs), :]
        cw = jnp.where(r0 + _iota((bs, 1), 0) >= lo, cs_ref[pl.ds(r0, bs), :], 0.0)
        lane_b = _iota((bs, LANES), 1)
        parts = []
        for e in range(MOE_SLAB_EXPERTS):
            wgt = jnp.sum(jnp.where(lane_b == sl * MOE_SLAB_EXPERTS + e, cw, 0.0), axis=1, keepdims=True)
            parts.append((_silu(_mm(x, wg_ref[e])) * _mm(x, wu_ref[e]) * wgt).astype(BF16))
        acc_ref[pl.ds(r0, bs), :] += _mm(jnp.concatenate(parts, axis=1), wd_ref[0])

    def full_block(j, carry):
        block(row0 + j * rb, rb)
        return carry

    lax.fori_loop(0, nfull, full_block, 0)
    sizes = sorted({s for s in (rb // 4, rb // 2, rb) if s % (2 * SUBLANES) == 0})
    prev = 0
    for bs in sizes:
        @pl.when((rem > prev) & (rem <= bs) & (cnts_ref[i, g] > 0))
        def _(bs=bs):
            block(row0 + nfull * rb, bs)
        prev = bs

    @pl.when(sl == pl.num_programs(1) - 1)
    def _():
        acc = acc_ref[...]
        a_hi = acc.astype(BF16)
        a_lo = (acc - a_hi.astype(F32)).astype(BF16)
        pt = pt_ref[...]
        v = h_ref[...] + _mm(pt, a_hi) + _mm(pt, a_lo)
        y_ref[...] = v * lax.rsqrt(jnp.mean(v * v, axis=-1, keepdims=True) + RMS_EPS) * nf_ref[...]


def _moe_weights(w_gate, w_up, w_down):
    return (w_gate.astype(BF16), w_up.astype(BF16),
            w_down.astype(BF16).reshape(MOE_SLABS, MOE_SLAB_EXPERTS * D_EXPERT, D_MODEL))


def _moe(h, hn, cmb, wg, wu, wd, norm_f_g):
    n = h.shape[0]
    t = min(1024, n)
    rb = min(256, t)
    tiles = n // t
    cnts = jnp.sum(cmb[:, N_EXPERTS:N_EXPERTS + N_GROUPS].reshape(tiles, t, N_GROUPS), axis=1).astype(jnp.int32)
    offs = jnp.cumsum(cnts, axis=1) - cnts
    ws = MOE_SLAB_EXPERTS * D_EXPERT
    row = lambda i, s, o, c: (i, 0)
    slab = lambda i, s, o, c: (s, 0, 0)
    grid_spec = pltpu.PrefetchScalarGridSpec(
        num_scalar_prefetch=2,
        grid=(tiles, MOE_SLABS),
        in_specs=[pl.BlockSpec((t, D_MODEL), row),
                  pl.BlockSpec((t, D_MODEL), row),
                  pl.BlockSpec((t, LANES), row),
                  pl.BlockSpec((MOE_SLAB_EXPERTS, D_MODEL, D_EXPERT), slab),
                  pl.BlockSpec((MOE_SLAB_EXPERTS, D_MODEL, D_EXPERT), slab),
                  pl.BlockSpec((1, ws, D_MODEL), slab),
                  pl.BlockSpec((1, D_MODEL), lambda i, s, o, c: (0, 0))],
        out_specs=pl.BlockSpec((t, D_MODEL), row),
        scratch_shapes=[pltpu.VMEM((t, D_MODEL), BF16),
                        pltpu.VMEM((t, LANES), F32),
                        pltpu.VMEM((t, t), BF16),
                        pltpu.VMEM((t, D_MODEL), F32)],
    )
    return pl.pallas_call(
        functools.partial(_moe_kernel, rb=rb),
        grid_spec=grid_spec,
        out_shape=jax.ShapeDtypeStruct((n, D_MODEL), F32),
        compiler_params=_cparams(("parallel", "arbitrary")),
        name="moe",
    )(offs, cnts, h, hn, cmb, wg, wu, wd, norm_f_g.reshape(1, D_MODEL))


def _route_weights(w_grp, b_grp, w_rt, b_rt):
    pad = LANES - N_EXPERTS - N_GROUPS
    w = jnp.concatenate([w_rt, w_grp, jnp.zeros((D_MODEL, pad), F32)], axis=1)
    bias = jnp.concatenate([b_rt, b_grp, jnp.zeros((pad,), F32)]).reshape(1, LANES)
    hi = w.astype(BF16)
    return jnp.stack([hi, (w - hi.astype(F32)).astype(BF16)]), bias


def _finish(x2d, ya, yb, ga, gb, wts):
    h, hn, cmb = _tail(x2d, ya, yb, ga, gb, wts["wa"], wts["wb"], wts["wo"], wts["norm2_g"],
                       wts["w_route"], wts["b_route"])
    return _moe(h, hn, cmb, wts["wg"], wts["wu"], wts["wd"], wts["norm_f_g"])


def _prompt_layer(x, wts):
    b, l, _ = x.shape
    x2d = x.reshape(b * l, D_MODEL)
    qkv, z, q8, ga, gb, sm, kv4t, kvwt, cmp_rows, ksa, kwa, vst, vwt = _project(
        x2d, wts["norm1_g"], wts["w_pk"], b, l)
    conv0 = jnp.zeros((b, GDN_CONV - 1, GDN_CONV_DIM), F32)
    s0 = jnp.zeros((b, GDN_HEADS, GDN_DK, GDN_DV), F32)
    ya, s_new, conv_new = _gdn(qkv, z, sm, conv0, s0, wts["conv_w"], wts["a_log"], wts["dt_bias"],
                               wts["gdn_norm_g"], b, l)
    head, tail = _pool_rows(cmp_rows, wts["pw"])
    kca, vct = _compress(head, tail, wts["cmp_wk"], wts["cmp_wv"], b)
    yb = _nsa_prompt(q8, sm, kca, vct, ksa, vst, kwa, vwt, b, l)
    y = _finish(x2d, ya, yb, ga, gb, wts)
    win_buf = min(WINDOW, l)
    kv_new = jnp.transpose(kv4t.reshape(b, 4, NSA_KV_HEADS, NSA_HD, l), (0, 4, 1, 2, 3))
    win_new = jnp.transpose(kvwt[:, :, l - win_buf:].reshape(b, 2, NSA_KV_HEADS, NSA_HD, win_buf), (0, 4, 1, 2, 3))
    return (y.reshape(b, l, D_MODEL), kv_new, win_new, s_new, conv_new)


def _sample_layer(x, cache_kv_l, page_table, cache_win_l, s0, conv_buf, wts):
    b, l, _ = x.shape
    x2d = x.reshape(b * l, D_MODEL)
    qkv, z, q8, ga, gb, sm, kv4, kvw = _project(x2d, wts["norm1_g"], wts["w_pk"], b, l)
    ya, s_new, conv_new = _gdn(qkv, z, sm, conv_buf, s0, wts["conv_w"], wts["a_log"], wts["dt_bias"],
                               wts["gdn_norm_g"], b, l)
    n_phys, page = cache_kv_l.shape[:2]
    w = 2 * NSA_KV
    cache_t = jnp.transpose(cache_kv_l, (0, 2, 3, 4, 1)).reshape(n_phys, 2 * w, page)
    win_buf = cache_win_l.shape[1]
    win_old_t = jnp.transpose(cache_win_l, (0, 2, 3, 4, 1)).reshape(b, w, win_buf)
    head_t, tail_t = _pool_pages(cache_t, page_table, wts["pos_wk"], wts["pos_wv"])
    kct, vct = _compress_t(head_t, tail_t, wts["cmp_wk"], wts["cmp_wv"])
    new_t = jnp.transpose(kv4.reshape(b, l, 2 * w), (0, 2, 1))
    knew_t = jnp.pad(new_t[:, w:], ((0, 0), (0, 0), (0, LANES - l)))
    win_new_t = jnp.transpose(kvw.reshape(b, l, w), (0, 2, 1))
    wcols = -(-(win_buf + l) // LANES) * LANES
    win_t = jnp.concatenate([win_old_t, win_new_t, jnp.zeros((b, w, wcols - win_buf - l), F32)], axis=2)
    yb = _nsa_sample(q8, sm, kct, vct, knew_t, win_t, cache_t, page_table, b, l)
    y = _finish(x2d, ya, yb, ga, gb, wts)
    win_new = jnp.transpose(win_t[:, :, l:l + win_buf].reshape(b, 2, NSA_KV_HEADS, NSA_HD, win_buf), (0, 4, 1, 2, 3))
    return (y.reshape(b, l, D_MODEL), kv4.reshape(b, l, 4, NSA_KV_HEADS, NSA_HD), win_new, s_new, conv_new)


def kernel(x_prompt, x_sample, cache_kv, page_table, cache_win, state_gdn, state_conv, norm1_g, w_in, gdn_conv_w, gdn_a_log, gdn_dt_bias, gdn_norm_g, cmp_pos_wk, cmp_pos_wv, cmp_wk, cmp_wv, w_branch_a, w_branch_b, w_out, norm2_g, w_grp, b_grp, w_rt, b_rt, w_e_gate, w_e_up, w_e_down, norm_f_g):
    assert w_in.shape[0] == 1, "single layer"
    w_route, b_route = _route_weights(w_grp[0], b_grp[0], w_rt[0], b_rt[0])
    wg, wu, wd = _moe_weights(w_e_gate[0], w_e_up[0], w_e_down[0])
    wts = dict(
        norm1_g=norm1_g[0], w_pk=_pack_w_in(w_in[0]),
        conv_w=gdn_conv_w[0], a_log=gdn_a_log[0], dt_bias=gdn_dt_bias[0], gdn_norm_g=gdn_norm_g[0],
        pw=_pos_weights(cmp_pos_wk[0], cmp_pos_wv[0]), pos_wk=cmp_pos_wk[0], pos_wv=cmp_pos_wv[0],
        cmp_wk=cmp_wk[0], cmp_wv=cmp_wv[0],
        wa=w_branch_a[0].astype(BF16), wb=w_branch_b[0].astype(BF16), wo=w_out[0].astype(BF16),
        norm2_g=norm2_g[0], w_route=w_route, b_route=b_route,
        wg=wg, wu=wu, wd=wd, norm_f_g=norm_f_g,
    )
    yp, kvp, winp, sp, cp = _prompt_layer(x_prompt, wts)
    ys, kvs, wins, ss, cs = _sample_layer(x_sample, cache_kv[0], page_table, cache_win[0], state_gdn[0],
                                          state_conv[0], wts)
    return (yp, ys, kvp[None], kvs[None], winp[None], wins[None], sp[None], ss[None], cp[None], cs[None])
```

```python
import functools
import math

import jax
import jax.numpy as jnp
from jax import lax
from jax.experimental import pallas as pl
from jax.experimental.pallas import tpu as pltpu

F32 = jnp.float32
BF16 = jnp.bfloat16
HI = lax.Precision.HIGHEST

LANES = 128
SUBLANES = 8
VMEM_LIMIT = 56 * 1024 * 1024
ROW_TILE = 512

D_MODEL = 1024
GDN_HEADS = 4
GDN_DK = 128
GDN_DV = 128
GDN_QK = GDN_HEADS * GDN_DK
GDN_V = GDN_HEADS * GDN_DV
GDN_CONV_DIM = 2 * GDN_QK + GDN_V
GDN_CONV = 4
GDN_CHUNK = 64
GDN_SEQS_PER_STEP = 4
NSA_HEADS = 8
NSA_KV_HEADS = 2
NSA_GROUP = NSA_HEADS // NSA_KV_HEADS
NSA_HD = 64
NSA_Q = NSA_HEADS * NSA_HD
NSA_KV = NSA_KV_HEADS * NSA_HD
CMP_STRIDE = 16
CMP_LEN = 2 * CMP_STRIDE
SEL_BLOCK = 64
SEL_TOP = 16
WINDOW = 512
FORCE_SCORE = 1.0e4
N_GROUPS = 4
EXPERTS_PER_GROUP = 8
N_EXPERTS = N_GROUPS * EXPERTS_PER_GROUP
D_EXPERT = 256
RMS_EPS = 1e-6
VS_ROWS = NSA_KV + 2 * SUBLANES
NEG_INF = float("-inf")

_OFF_QKV = 0
_OFF_Z = _OFF_QKV + GDN_CONV_DIM
_OFF_B = _OFF_Z + GDN_V
_OFF_A = _OFF_B + GDN_HEADS
_OFF_Q = _OFF_A + GDN_HEADS
_OFF_KV = _OFF_Q + NSA_Q
_OFF_G = _OFF_KV + 6 * NSA_KV
_OFF_GA = _OFF_G + 3 * NSA_HEADS
_OFF_GB = _OFF_GA + D_MODEL

SM_B = 0
SM_A = 4
SM_G = 8

_PK = {}
_c = 0
for _n, _w in (("qkv", GDN_CONV_DIM), ("z", GDN_V), ("q8", NSA_HEADS * LANES), ("kv4", 4 * NSA_KV),
               ("kvw", 2 * NSA_KV), ("ga", D_MODEL), ("gb", D_MODEL), ("sm", LANES)):
    _PK[_n] = (_c, _w)
    _c += _w
PK_DIM = _c


def _cparams(sem):
    return pltpu.CompilerParams(dimension_semantics=sem, vmem_limit_bytes=VMEM_LIMIT)


def _nt(a, b, precision=None):
    return lax.dot_general(a, b, (((1,), (1,)), ((), ())), preferred_element_type=F32, precision=precision)


def _tn(a, b, precision=None):
    return lax.dot_general(a, b, (((0,), (0,)), ((), ())), preferred_element_type=F32, precision=precision)


def _mm(a, b, precision=None):
    return jnp.dot(a, b, preferred_element_type=F32, precision=precision)


def _mmb(a, b):
    return _mm(a.astype(BF16), b.astype(BF16))


def _sigmoid(x):
    return 1.0 / (1.0 + jnp.exp(-x))


def _silu(x):
    return x * _sigmoid(x)


def _iota(shape, dim):
    return lax.broadcasted_iota(jnp.int32, shape, dim)


def _ones_where(cond):
    return jnp.where(cond, 1.0, 0.0).astype(BF16)


def _pack_w_in(w_in):
    q = w_in[:, _OFF_Q:_OFF_Q + NSA_Q]
    zeros64 = jnp.zeros((D_MODEL, NSA_HD), w_in.dtype)
    q8 = []
    for h in range(NSA_HEADS):
        qh = q[:, h * NSA_HD:(h + 1) * NSA_HD]
        q8.append(jnp.concatenate([qh, zeros64] if h < NSA_GROUP else [zeros64, qh], axis=1))
    sm = jnp.concatenate([w_in[:, _OFF_B:_OFF_B + GDN_HEADS], w_in[:, _OFF_A:_OFF_A + GDN_HEADS],
                          w_in[:, _OFF_G:_OFF_G + 3 * NSA_HEADS],
                          jnp.zeros((D_MODEL, LANES - 2 * GDN_HEADS - 3 * NSA_HEADS), w_in.dtype)], axis=1)
    cols = [w_in[:, _OFF_QKV:_OFF_QKV + GDN_CONV_DIM], w_in[:, _OFF_Z:_OFF_Z + GDN_V]] + q8 + [
        w_in[:, _OFF_KV:_OFF_KV + 4 * NSA_KV], w_in[:, _OFF_KV + 4 * NSA_KV:_OFF_KV + 6 * NSA_KV],
        w_in[:, _OFF_GA:_OFF_GA + D_MODEL], w_in[:, _OFF_GB:_OFF_GB + D_MODEL], sm]
    return jnp.concatenate(cols, axis=1).astype(BF16)


def _pos_features(pos, shape):
    lane = _iota(shape, 1)
    feat = jnp.where(lane == 0, (pos >> 7) * LANES, jnp.where(lane == 1, pos & (LANES - 1),
                                                               jnp.where(lane < 4, 1, 0)))
    return feat.astype(F32).astype(BF16)


def _proj_kernel(x_ref, g_ref, w_ref, qkv_ref, z_ref, q8_ref, ga_ref, gb_ref, sm_ref, *kv_refs, tiles_per_seq):
    tm = x_ref.shape[0]
    x = x_ref[...]
    xn = (x * lax.rsqrt(jnp.mean(x * x, axis=-1, keepdims=True) + RMS_EPS) * g_ref[...]).astype(BF16)

    def seg(name):
        a, w = _PK[name]
        return _mm(xn, w_ref[:, a:a + w])

    qkv_ref[...] = seg("qkv")
    z_ref[...] = seg("z")
    q8_ref[...] = (seg("q8") * (NSA_HD ** -0.5)).astype(BF16)
    ga_ref[...] = seg("ga")
    gb_ref[...] = seg("gb")
    sm_ref[...] = seg("sm")
    kv4 = seg("kv4")
    kvw = seg("kvw")
    if tiles_per_seq == 0:
        kv4_ref, kvw_ref = kv_refs
        kv4_ref[...] = kv4
        kvw_ref[...] = kvw
        return
    kv4t_ref, kvwt_ref, cmp_ref, ksa_ref, kwa_ref, vst_ref, vwt_ref = kv_refs
    kv4t = jnp.transpose(kv4)
    kvwt = jnp.transpose(kvw)
    kv4t_ref[...] = kv4t
    kvwt_ref[...] = kvwt
    cmp_ref[...] = kv4[:, :2 * NSA_KV]
    pos = (pl.program_id(0) % tiles_per_seq) * tm + _iota((tm, 1), 0)
    feat = _pos_features(pos, (tm, LANES))
    ksa_ref[:, :NSA_KV] = kv4[:, 2 * NSA_KV:3 * NSA_KV].astype(BF16)
    ksa_ref[:, NSA_KV:] = feat
    kwa_ref[:, :NSA_KV] = kvw[:, :NSA_KV].astype(BF16)
    kwa_ref[:, NSA_KV:] = feat
    vst_ref[:NSA_KV, :] = kv4t[3 * NSA_KV:].astype(BF16)
    ones_row = _ones_where(_iota((VS_ROWS - NSA_KV, tm), 0) == 0)
    vst_ref[NSA_KV:, :] = ones_row
    vwt_ref[:NSA_KV, :] = kvwt[NSA_KV:].astype(BF16)
    vwt_ref[NSA_KV:, :] = ones_row


def _project(x2d, norm_g, w_pk, b, l):
    n = x2d.shape[0]
    tm = min(ROW_TILE, n)
    long_seq = l % tm == 0
    tps = l // tm if long_seq else 0
    w = 2 * NSA_KV
    row = lambda i: (i, 0)
    out_shape = [jax.ShapeDtypeStruct((n, _PK[k][1]), dt) for k, dt in
                 (("qkv", F32), ("z", F32), ("q8", BF16), ("ga", F32), ("gb", F32), ("sm", F32))]
    out_specs = [pl.BlockSpec((tm, s.shape[1]), row) for s in out_shape]
    if long_seq:
        seq_t = lambda i: (i // tps, 0, i % tps)
        out_shape += [jax.ShapeDtypeStruct((b, 2 * w, l), F32), jax.ShapeDtypeStruct((b, w, l), F32),
                      jax.ShapeDtypeStruct((n, w), F32), jax.ShapeDtypeStruct((n, w), BF16),
                      jax.ShapeDtypeStruct((n, w), BF16), jax.ShapeDtypeStruct((VS_ROWS, n), BF16),
                      jax.ShapeDtypeStruct((VS_ROWS, n), BF16)]
        out_specs += [pl.BlockSpec((None, 2 * w, tm), seq_t), pl.BlockSpec((None, w, tm), seq_t),
                      pl.BlockSpec((tm, w), row), pl.BlockSpec((tm, w), row), pl.BlockSpec((tm, w), row),
                      pl.BlockSpec((VS_ROWS, tm), lambda i: (0, i)), pl.BlockSpec((VS_ROWS, tm), lambda i: (0, i))]
    else:
        out_shape += [jax.ShapeDtypeStruct((n, 2 * w), F32), jax.ShapeDtypeStruct((n, w), F32)]
        out_specs += [pl.BlockSpec((tm, 2 * w), row), pl.BlockSpec((tm, w), row)]
    return pl.pallas_call(
        functools.partial(_proj_kernel, tiles_per_seq=tps),
        grid=(n // tm,),
        in_specs=[pl.BlockSpec((tm, D_MODEL), row),
                  pl.BlockSpec((1, D_MODEL), lambda i: (0, 0)),
                  pl.BlockSpec((D_MODEL, PK_DIM), lambda i: (0, 0), pipeline_mode=pl.Buffered(1))],
        out_specs=out_specs,
        out_shape=out_shape,
        compiler_params=_cparams(("parallel",)),
        name="proj",
    )(x2d, norm_g.reshape(1, D_MODEL), w_pk)


def _unit_lower_inverses(a_list, c):
    r = _iota((c, c), 0)
    col = _iota((c, c), 1)
    eye = (r == col).astype(F32)
    n1 = [jnp.where((r >> 3) == (col >> 3), -a, 0.0) for a in a_list]
    n2 = [_mmb(x, x) for x in n1]
    n4 = [_mmb(x, x) for x in n2]
    t = [_mmb(eye + x, eye + y) for x, y in zip(n1, n2)]
    t = [_mmb(x, eye + y) for x, y in zip(t, n4)]
    s = SUBLANES
    while s < c:
        sh = s.bit_length() - 1
        off = ((r >> (sh + 1)) == (col >> (sh + 1))) & ((r >> sh) != (col >> sh))
        ta = [_mmb(x, jnp.where(off, a, 0.0)) for x, a in zip(t, a_list)]
        t = [x - _mmb(y, x) for x, y in zip(t, ta)]
        s *= 2
    return t


def _gdn_kernel(qkv_ref, z_ref, sm_ref, cbuf_ref, s0_ref, cw_ref, alog_ref, dtb_ref, ng_ref,
                y_ref, snew_ref, cnew_ref, ext_ref, st_ref, *, chunk, nb):
    c = chunk
    ci = pl.program_id(1)

    @pl.when(ci == 0)
    def _():
        ext_ref[:, 0:SUBLANES, :] = cbuf_ref[...]
        st_ref[...] = s0_ref[...]

    r = _iota((c, c), 0)
    col = _iota((c, c), 1)
    tri_incl = r >= col
    tri_strict = r > col
    tri_f = tri_incl.astype(F32)
    pick = (_iota((SUBLANES, LANES), 1) == _iota((SUBLANES, LANES), 0) + SM_A).astype(F32)
    base = SUBLANES - (GDN_CONV - 1)
    ng = ng_ref[...]

    units = []
    for bb in range(nb):
        u = qkv_ref[bb]
        ext_ref[bb, SUBLANES:SUBLANES + c, :] = u
        conv = cw_ref[0:1, :] * ext_ref[bb, base:base + c, :]
        for i in range(1, GDN_CONV - 1):
            conv = conv + cw_ref[i:i + 1, :] * ext_ref[bb, base + i:base + i + c, :]
        conv = conv + cw_ref[GDN_CONV - 1:GDN_CONV, :] * u
        halo = ext_ref[bb, c:c + SUBLANES, :]
        ext_ref[bb, 0:SUBLANES, :] = halo
        cnew_ref[bb] = halo
        qkv = _silu(conv)

        sm = sm_ref[bb]
        beta_all = _sigmoid(sm)
        xa = sm + dtb_ref[...]
        softplus = jnp.maximum(xa, 0.0) + jnp.log(1.0 + jnp.exp(-jnp.abs(xa)))
        g_all = -jnp.exp(alog_ref[...]) * softplus
        gcum_all = _mm(tri_f, g_all, HI)
        gcum_rows = _nt(pick, gcum_all, HI)
        for h in range(GDN_HEADS):
            q = qkv[:, h * GDN_DK:(h + 1) * GDN_DK]
            k = qkv[:, GDN_QK + h * GDN_DK:GDN_QK + (h + 1) * GDN_DK]
            v = qkv[:, 2 * GDN_QK + h * GDN_DV:2 * GDN_QK + (h + 1) * GDN_DV]
            q = q * lax.rsqrt(jnp.sum(q * q, axis=-1, keepdims=True) + RMS_EPS) * (GDN_DK ** -0.5)
            k = k * lax.rsqrt(jnp.sum(k * k, axis=-1, keepdims=True) + RMS_EPS)
            beta = beta_all[:, SM_B + h:SM_B + h + 1]
            gc_col = gcum_all[:, SM_A + h:SM_A + h + 1]
            gc_row = gcum_rows[h:h + 1, :]
            gc_last = gcum_all[c - 1:c, SM_A + h:SM_A + h + 1]
            units.append(dict(bb=bb, h=h, q=q, k=k, v=v, beta=beta, gc_col=gc_col, gc_last=gc_last,
                              decay=jnp.exp(jnp.where(tri_incl, gc_col - gc_row, NEG_INF)),
                              eg=jnp.exp(gc_col), kb=k * beta, k_bf=k.astype(BF16)))

    a_list = [jnp.where(tri_strict, _nt(un["kb"].astype(BF16), un["k_bf"]) * un["decay"], 0.0) for un in units]
    t_list = _unit_lower_inverses(a_list, c)
    u_coef = [_mmb(t, un["v"] * un["beta"]) for t, un in zip(t_list, units)]
    w_coef = [_mmb(t, un["kb"] * un["eg"]) for t, un in zip(t_list, units)]
    qk = [(_nt(un["q"].astype(BF16), un["k_bf"]) * un["decay"]).astype(BF16) for un in units]
    s_old = [st_ref[un["bb"], un["h"]] for un in units]
    s_bf = [s.astype(BF16) for s in s_old]
    uu = [uc - _mm(wc.astype(BF16), s) for uc, wc, s in zip(u_coef, w_coef, s_bf)]
    uu_bf = [x.astype(BF16) for x in uu]
    o_list = [_mm((un["q"] * un["eg"]).astype(BF16), s) + _mm(a, x)
              for un, s, a, x in zip(units, s_bf, qk, uu_bf)]
    for un, s, x, o in zip(units, s_old, uu_bf, o_list):
        bb, h = un["bb"], un["h"]
        k_tail = un["k"] * jnp.exp(un["gc_last"] - un["gc_col"])
        st_ref[bb, h] = s * jnp.exp(un["gc_last"]) + _tn(k_tail.astype(BF16), x)
        on = o * lax.rsqrt(jnp.mean(o * o, axis=-1, keepdims=True) + RMS_EPS) * ng
        y_ref[bb, :, h * GDN_DV:(h + 1) * GDN_DV] = on * _silu(z_ref[bb, :, h * GDN_DV:(h + 1) * GDN_DV])

    @pl.when(ci == pl.num_programs(1) - 1)
    def _():
        snew_ref[...] = st_ref[...]


def _gdn(qkv, z, sm, conv_buf, s0, conv_w, a_log, dt_bias, norm_g, b, l):
    c = math.gcd(l, GDN_CHUNK)
    nc = l // c
    nb = math.gcd(b, GDN_SEQS_PER_STEP)
    cbuf8 = jnp.pad(conv_buf, ((0, 0), (SUBLANES - (GDN_CONV - 1), 0), (0, 0)))
    pad_a = (SM_A, LANES - SM_A - GDN_HEADS)
    alog_row = jnp.pad(a_log, pad_a).reshape(1, LANES)
    dtb_row = jnp.pad(dt_bias, pad_a).reshape(1, LANES)
    row = lambda bi, ci: (bi, ci, 0)
    per_seq3 = lambda bi, ci: (bi, 0, 0)
    per_seq4 = lambda bi, ci: (bi, 0, 0, 0)
    fixed = lambda bi, ci: (0, 0)
    y, s_new, c_new = pl.pallas_call(
        functools.partial(_gdn_kernel, chunk=c, nb=nb),
        grid=(b // nb, nc),
        in_specs=[pl.BlockSpec((nb, c, GDN_CONV_DIM), row),
                  pl.BlockSpec((nb, c, GDN_V), row),
                  pl.BlockSpec((nb, c, LANES), row),
                  pl.BlockSpec((nb, SUBLANES, GDN_CONV_DIM), per_seq3),
                  pl.BlockSpec((nb, GDN_HEADS, GDN_DK, GDN_DV), per_seq4),
                  pl.BlockSpec((GDN_CONV, GDN_CONV_DIM), fixed),
                  pl.BlockSpec((1, LANES), fixed),
                  pl.BlockSpec((1, LANES), fixed),
                  pl.BlockSpec((1, GDN_DV), fixed)],
        out_specs=[pl.BlockSpec((nb, c, GDN_V), row),
                   pl.BlockSpec((nb, GDN_HEADS, GDN_DK, GDN_DV), per_seq4),
                   pl.BlockSpec((nb, SUBLANES, GDN_CONV_DIM), per_seq3)],
        out_shape=[jax.ShapeDtypeStruct((b, l, GDN_V), F32),
                   jax.ShapeDtypeStruct((b, GDN_HEADS, GDN_DK, GDN_DV), F32),
                   jax.ShapeDtypeStruct((b, SUBLANES, GDN_CONV_DIM), F32)],
        scratch_shapes=[pltpu.VMEM((nb, c + SUBLANES, GDN_CONV_DIM), F32),
                        pltpu.VMEM((nb, GDN_HEADS, GDN_DK, GDN_DV), F32)],
        compiler_params=_cparams(("parallel", "arbitrary")),
        name="gdn",
    )(qkv.reshape(b, l, GDN_CONV_DIM), z.reshape(b, l, GDN_V), sm.reshape(b, l, LANES), cbuf8, s0, conv_w,
      alog_row, dtb_row, norm_g.reshape(1, GDN_DV))
    return y.reshape(b * l, GDN_V), s_new, c_new[:, SUBLANES - (GDN_CONV - 1):]


def _pool_kernel(*refs, n_in):
    refs = refs[len(refs) - 2 * n_in - 3:]
    x_refs, pw_ref, head_ref, tail_ref = refs[:2 * n_in], refs[2 * n_in], refs[2 * n_in + 1], refs[2 * n_in + 2]
    rows = x_refs[0].shape[0]
    n = rows // CMP_STRIDE
    for j, x_ref in enumerate(x_refs):
        half = slice((j % 2) * NSA_KV, (j % 2 + 1) * NSA_KV)
        head = None
        tail = None
        for p in range(CMP_STRIDE):
            xr = x_ref[pl.ds(p, n, stride=CMP_STRIDE), :]
            hp = xr * pw_ref[p:p + 1, half]
            tp = xr * pw_ref[CMP_STRIDE + p:CMP_STRIDE + p + 1, half]
            head = hp if head is None else head + hp
            tail = tp if tail is None else tail + tp
        head_ref[(j // 2) * n:(j // 2 + 1) * n, half] = head
        tail_ref[(j // 2) * n:(j // 2 + 1) * n, half] = tail


def _pos_weights(pos_wk, pos_wv):
    return jnp.concatenate([jnp.broadcast_to(pos_wk[:, None], (CMP_LEN, NSA_KV)),
                            jnp.broadcast_to(pos_wv[:, None], (CMP_LEN, NSA_KV))], axis=1).astype(F32)


def _pool_rows(kv4, pw):
    n = kv4.shape[0]
    r = min(2048, n)
    w = 2 * NSA_KV
    return pl.pallas_call(
        functools.partial(_pool_kernel, n_in=1),
        grid=(n // r,),
        in_specs=[pl.BlockSpec((r, NSA_KV), lambda i: (i, 0)),
                  pl.BlockSpec((r, NSA_KV), lambda i: (i, 1)),
                  pl.BlockSpec((CMP_LEN, w), lambda i: (0, 0))],
        out_specs=[pl.BlockSpec((r // CMP_STRIDE, w), lambda i: (i, 0))] * 2,
        out_shape=[jax.ShapeDtypeStruct((n // CMP_STRIDE, w), F32)] * 2,
        compiler_params=_cparams(("parallel",)),
        name="pool_rows",
    )(kv4, kv4, pw)


def _pool_pages_kernel(*refs, pg):
    page_refs = refs[1:1 + pg]
    pk, pv, head_ref, tail_ref = refs[1 + pg:]
    x = jnp.concatenate([r[...] for r in page_refs], axis=1)
    n = head_ref.shape[1]
    for half, pool in enumerate((pk, pv)):
        xs = x[half * NSA_KV:(half + 1) * NSA_KV]
        out = _mm(xs.astype(BF16), pool[...])
        head_ref[half * NSA_KV:(half + 1) * NSA_KV, :] = out[:, :n]
        tail_ref[half * NSA_KV:(half + 1) * NSA_KV, :] = out[:, n:]


def _pool_matrix(pos_w, positions):
    chunks = positions // CMP_STRIDE
    p = jnp.arange(positions)[:, None]
    c = jnp.arange(chunks)[None, :]
    inside = (p // CMP_STRIDE) == c
    head = jnp.where(inside, pos_w[:CMP_STRIDE][p % CMP_STRIDE], 0.0)
    tail = jnp.where(inside, pos_w[CMP_STRIDE:][p % CMP_STRIDE], 0.0)
    return jnp.concatenate([head, tail], axis=1).astype(BF16)


def _pool_pages(cache_t, page_table, pos_wk, pos_wv):
    b, n_pages = page_table.shape
    page = cache_t.shape[2]
    pg = LANES * CMP_STRIDE // page
    w = 2 * NSA_KV
    steps = n_pages // pg
    n_ch = n_pages * page // CMP_STRIDE
    mats = (_pool_matrix(pos_wk, pg * page), _pool_matrix(pos_wv, pg * page))

    def page_map(k):
        return lambda bi, j, pt: (pt[bi, j * pg + k], 0, 0)

    fixed = lambda bi, j, pt: (0, 0)
    grid_spec = pltpu.PrefetchScalarGridSpec(
        num_scalar_prefetch=1,
        grid=(b, steps),
        in_specs=[pl.BlockSpec((None, w, page), page_map(k)) for k in range(pg)]
        + [pl.BlockSpec((pg * page, 2 * LANES), fixed)] * 2,
        out_specs=[pl.BlockSpec((None, w, LANES), lambda bi, j, pt: (bi, 0, j))] * 2,
    )
    return pl.pallas_call(
        functools.partial(_pool_pages_kernel, pg=pg),
        grid_spec=grid_spec,
        out_shape=[jax.ShapeDtypeStruct((b, w, n_ch), F32)] * 2,
        compiler_params=_cparams(("parallel", "arbitrary")),
        name="pool_pages",
    )(page_table, *([cache_t] * pg), *mats)


def _cmp_t_kernel(head_ref, tail_ref, wk_ref, wv_ref, kct_ref, vct_ref):
    n = head_ref.shape[1]
    blocks = head_ref[...] + pltpu.roll(tail_ref[...], n - 1, 1)
    kct_ref[...] = _mm(wk_ref[...], blocks[:NSA_KV].astype(BF16)).astype(BF16)
    vct_ref[...] = _mm(wv_ref[...], blocks[NSA_KV:].astype(BF16)).astype(BF16)


def _compress_t(head_t, tail_t, cmp_wk, cmp_wv):
    b, w, n_ch = head_t.shape
    seq = lambda i: (i, 0, 0)
    return pl.pallas_call(
        _cmp_t_kernel,
        grid=(b,),
        in_specs=[pl.BlockSpec((None, w, n_ch), seq)] * 2 + [pl.BlockSpec((NSA_KV, NSA_KV), lambda i: (0, 0))] * 2,
        out_specs=[pl.BlockSpec((None, NSA_KV, n_ch), seq)] * 2,
        out_shape=[jax.ShapeDtypeStruct((b, NSA_KV, n_ch), BF16)] * 2,
        compiler_params=_cparams(("parallel",)),
        name="compress_t",
    )(head_t, tail_t, jnp.transpose(_block_diag2(cmp_wk)), jnp.transpose(_block_diag2(cmp_wv)))


def _cmp_kernel(head_ref, tail_ref, wk_ref, wv_ref, kca_ref, vct_ref):
    n = head_ref.shape[0]
    blocks = head_ref[...] + pltpu.roll(tail_ref[...], n - 1, 0)
    kca_ref[:, :NSA_KV] = _mm(blocks[:, :NSA_KV].astype(BF16), wk_ref[...]).astype(BF16)
    blk_end = _iota((n, 1), 0) * CMP_STRIDE + (CMP_LEN - 1)
    kca_ref[:, NSA_KV:] = _pos_features(blk_end, (n, LANES))
    vct_ref[...] = _nt(wv_ref[...], blocks[:, NSA_KV:].astype(BF16)).astype(BF16)


def _block_diag2(w):
    z = jnp.zeros_like(w)
    return jnp.concatenate([jnp.concatenate([w, z], axis=1), jnp.concatenate([z, w], axis=1)], axis=0).astype(BF16)


def _compress(head, tail, cmp_wk, cmp_wv, b):
    n_ch = head.shape[0] // b
    w = 2 * NSA_KV
    return pl.pallas_call(
        _cmp_kernel,
        grid=(b,),
        in_specs=[pl.BlockSpec((n_ch, w), lambda i: (i, 0))] * 2
        + [pl.BlockSpec((NSA_KV, NSA_KV), lambda i: (0, 0))] * 2,
        out_specs=[pl.BlockSpec((n_ch, w), lambda i: (i, 0)),
                   pl.BlockSpec((NSA_KV, n_ch), lambda i: (0, i))],
        out_shape=[jax.ShapeDtypeStruct((b * n_ch, w), BF16),
                   jax.ShapeDtypeStruct((NSA_KV, b * n_ch), BF16)],
        compiler_params=_cparams(("parallel",)),
        name="compress",
    )(head, tail, _block_diag2(cmp_wk), jnp.transpose(_block_diag2(cmp_wv)))


def _overlap(n_ch, nsp):
    i = jnp.arange(n_ch)[:, None]
    j = jnp.arange(nsp)[None, :]
    lo = jnp.maximum(i * CMP_STRIDE, j * SEL_BLOCK)
    hi = jnp.minimum(i * CMP_STRIDE + CMP_LEN, (j + 1) * SEL_BLOCK)
    ov = jnp.maximum(hi - lo, 0).astype(F32) / CMP_LEN
    return jnp.where(i < n_ch - 1, ov, 0.0)


def _slope(h):
    return 2.0 ** (-(h + 1))


def _softmax_rows(s, mask):
    s = jnp.where(mask, s, NEG_INF)
    m = jnp.max(s, axis=-1, keepdims=True)
    m = jnp.where(m == NEG_INF, 0.0, m)
    e = jnp.where(mask, jnp.exp(s - m), 0.0)
    return e, jnp.maximum(jnp.sum(e, axis=-1, keepdims=True), 1e-30)


def _top_blocks(imp, qpos, ns_lanes):
    blk = _iota(imp.shape, 1)
    cur = qpos >> 6
    forced = (blk == 0) | (blk == cur) | (blk == cur - 1)
    score = jnp.where(forced, FORCE_SCORE, imp)
    work = jnp.where(blk <= cur, score, NEG_INF)
    sel = jnp.zeros(imp.shape, F32)
    for _ in range(SEL_TOP):
        m = jnp.max(work, axis=-1, keepdims=True)
        cand = jnp.where((work == m) & (m > NEG_INF), blk, ns_lanes)
        first = jnp.min(cand, axis=-1, keepdims=True)
        hit = blk == first
        sel = jnp.where(hit, 1.0, sel)
        work = jnp.where(hit, NEG_INF, work)
    return sel


def _row_slopes(tq):
    head = _iota((NSA_HEADS * tq, 1), 0) >> (tq.bit_length() - 1)
    slope = jnp.zeros((NSA_HEADS * tq, 1), F32)
    for h in range(NSA_HEADS):
        slope = jnp.where(head == h, _slope(h), slope)
    return slope


def _rows_attend(q8, kt, vt, slope, dist, mask):
    s = _mm(q8, kt) - slope * dist.astype(F32)
    e, den = _softmax_rows(s, mask)
    p = e / den
    return _nt(p.astype(BF16), vt), p


def _rows_online(q8, kt, vt, slope, dist, mask, m_ref, l_ref, acc_ref):
    s = jnp.where(mask, _mm(q8, kt) - slope * dist.astype(F32), NEG_INF)
    m_old = m_ref[...]
    m_new = jnp.maximum(m_old, jnp.max(s, axis=-1, keepdims=True))
    m_use = jnp.where(m_new == NEG_INF, 0.0, m_new)
    alpha = jnp.exp(m_old - m_use)
    e = jnp.exp(s - m_use)
    l_ref[...] = alpha * l_ref[...] + jnp.sum(e, axis=-1, keepdims=True)
    acc_ref[...] = alpha * acc_ref[...] + _nt(e.astype(BF16), vt)
    m_ref[...] = m_new


def _merge_heads(o_c, o_s, o_w, gates, tq):
    lane = _iota((tq, LANES), 1)
    mixed = []
    for h in range(NSA_HEADS):
        rows = slice(h * tq, (h + 1) * tq)
        g0 = gates[:, SM_G + 3 * h:SM_G + 3 * h + 1]
        g1 = gates[:, SM_G + 3 * h + 1:SM_G + 3 * h + 2]
        g2 = gates[:, SM_G + 3 * h + 2:SM_G + 3 * h + 3]
        mixed.append(g0 * o_c[rows] + g1 * o_s[rows] + g2 * o_w[rows])
    cols = []
    for p in range(NSA_HEADS // 2):
        a, b = mixed[2 * p], mixed[2 * p + 1]
        if 2 * p < NSA_GROUP:
            cols.append(jnp.where(lane < NSA_HD, a, pltpu.roll(b, NSA_HD, 1)))
        else:
            cols.append(jnp.where(lane < NSA_HD, pltpu.roll(a, NSA_HD, 1), b))
    return jnp.concatenate(cols, axis=1)


def _softmax_cols(s, mask):
    s = jnp.where(mask, s, NEG_INF)
    m = jnp.max(s, axis=0, keepdims=True)
    m = jnp.where(m == NEG_INF, 0.0, m)
    e = jnp.exp(s - m)
    return e * (1.0 / jnp.maximum(jnp.sum(e, axis=0, keepdims=True), 1e-30))


def _split3(x):
    hi = x.astype(BF16)
    r1 = x - hi.astype(F32)
    mid = r1.astype(BF16)
    lo = (r1 - mid.astype(F32)).astype(BF16)
    return hi, mid, lo


def _top_blocks_cols(imp, qpos, nsp):
    blk = _iota(imp.shape, 0)
    cur = qpos >> 6
    forced = (blk == 0) | (blk == cur) | (blk == cur - 1)
    score = jnp.where(blk <= cur, jnp.where(forced, FORCE_SCORE, imp), NEG_INF)
    work = score
    for _ in range(SEL_TOP):
        m = jnp.max(work, axis=0, keepdims=True)
        hit = blk == jnp.min(jnp.where(work == m, blk, nsp), axis=0, keepdims=True)
        work = jnp.where(hit, NEG_INF, work)
    return jnp.where((work == NEG_INF) & (score > NEG_INF), 0.0, NEG_INF)


def _nsa_prompt_kernel(q8_ref, sm_ref, kc_ref, vct_ref, ks_ref, vst_ref, kw_ref, vwt_ref, ovt_ref, y_ref,
                       neg_ref, m_ref, acc_ref, qta_ref, sa_ref, sb_ref, act_ref, part_ref, *, tq, tk):
    i = pl.program_id(1)
    start = i * tq
    r = NSA_HEADS * tq
    nsp, n_ch = ovt_ref.shape
    col = _iota((1, r), 1)
    qpos = start + (col & (tq - 1))

    q8 = jnp.concatenate([q8_ref[:, h * LANES:(h + 1) * LANES] for h in range(NSA_HEADS)], axis=0)
    eye = (_iota((LANES, LANES), 0) == _iota((LANES, LANES), 1)).astype(BF16)
    qt = _nt(eye, q8).astype(BF16)
    head = col >> (tq.bit_length() - 1)
    slope = jnp.zeros((1, r), F32)
    for h in range(NSA_HEADS):
        slope = jnp.where(head == h, _slope(h), slope)
    frow = _iota((2 * SUBLANES, r), 0)
    feat = jnp.where(frow < 2, slope, jnp.where(frow == 2, -slope * ((qpos >> 7) * LANES).astype(F32),
                                                jnp.where(frow == 3, -slope * (qpos & (LANES - 1)).astype(F32), 0.0)))
    qta = jnp.concatenate([qt, feat.astype(BF16), jnp.zeros((LANES - 2 * SUBLANES, r), BF16)], axis=0)

    qta_ref[...] = qta
    qpos_h = start + _iota((1, tq), 1)
    cmp_valid = _iota((n_ch, 1), 0) * CMP_STRIDE + (CMP_LEN - 1) <= qpos_h
    kc = kc_ref[...]
    vct = vct_ref[...]
    o_c_parts = []
    psum = [None] * NSA_KV_HEADS
    s_list = [_mm(kc, qta_ref[:, h * tq:(h + 1) * tq]) for h in range(NSA_HEADS)]
    p_list = [_softmax_cols(s, cmp_valid) for s in s_list]
    o_c_parts = [_mm(vct, p.astype(BF16)) for p in p_list]
    for h, p in enumerate(p_list):
        g = h // NSA_GROUP
        psum[g] = p if psum[g] is None else psum[g] + p
    o_c = jnp.concatenate(o_c_parts, axis=1)
    ovt = ovt_ref[...]
    imp = sum(_mm(ovt, piece) for piece in _split3(jnp.concatenate(psum, axis=1)))
    wl = tq + WINDOW
    ws = pl.multiple_of(jnp.maximum(start - WINDOW, 0), tq)
    dist = qpos_h - (ws + _iota((wl, 1), 0))
    win_valid = (dist >= 0) & (dist <= WINDOW)
    kw = kw_ref[pl.ds(ws, wl), :]
    vw = vwt_ref[:, pl.ds(ws, wl)]
    sw_list = [_mm(kw, qta_ref[:, h * tq:(h + 1) * tq]) for h in range(NSA_HEADS)]
    sw_list = [jnp.where(win_valid, s, NEG_INF).astype(BF16) for s in sw_list]
    ew_list = [jnp.exp(s - jnp.max(s, axis=0, keepdims=True)) for s in sw_list]
    ow_list = [_mm(vw, e) for e in ew_list]
    o_w = jnp.concatenate([ow[:NSA_KV] * (1.0 / jnp.maximum(ow[NSA_KV:NSA_KV + 1], 1e-30)) for ow in ow_list],
                          axis=1)
    gt = jnp.transpose(_sigmoid(sm_ref[...]))
    gate = [jnp.concatenate([gt[SM_G + 3 * h + c:SM_G + 3 * h + c + 1, :] for h in range(NSA_HEADS)], axis=1)
            for c in range(3)]
    part_ref[...] = gate[0] * o_c + gate[2] * o_w

    qpos2 = start + (_iota((1, NSA_KV_HEADS * tq), 1) & (tq - 1))
    neg = _top_blocks_cols(imp, qpos2, nsp)
    for h in range(NSA_HEADS):
        g = h // NSA_GROUP
        neg_ref[:, h * tq:(h + 1) * tq] = neg[:, g * tq:(g + 1) * tq]

    m_ref[...] = jnp.full(m_ref.shape, NEG_INF, F32)
    acc_ref[...] = jnp.zeros(acc_ref.shape, F32)
    bpt = tk // SEL_BLOCK

    def scores(t, buf):
        off = pl.multiple_of(t * tk, tk)
        buf[...] = _mm(ks_ref[pl.ds(off, tk), :], qta_ref[...])

    def tile(t, buf, causal):
        off = pl.multiple_of(t * tk, tk)
        parts = []
        for j in range(bpt):
            sj = buf[j * SEL_BLOCK:(j + 1) * SEL_BLOCK, :]
            neg = neg_ref[pl.ds(t * bpt + j, 1), :]
            if causal:
                sj = jnp.where(off + j * SEL_BLOCK + _iota((SEL_BLOCK, 1), 0) <= qpos, sj + neg, NEG_INF)
                parts.append(sj.astype(BF16))
            else:
                parts.append(sj.astype(BF16) + neg.astype(BF16))
        s = jnp.concatenate(parts, axis=0)
        m_old = m_ref[...]
        m_new = jnp.maximum(m_old, jnp.max(s, axis=0, keepdims=True).astype(F32))
        m_use = jnp.where(m_new == NEG_INF, 0.0, m_new)
        alpha = jnp.exp(m_old - m_use)
        e = jnp.exp(s - m_use.astype(BF16))
        acc_ref[...] = alpha * acc_ref[...] + _mm(vst_ref[:, pl.ds(off, tk)], e)
        m_ref[...] = m_new

    n_full = start // tk
    blk_any = jnp.max(neg, axis=1, keepdims=True)
    cnt = jnp.int32(0)
    for t in range(ks_ref.shape[0] // tk):
        live = (jnp.max(blk_any[t * bpt:(t + 1) * bpt]) > NEG_INF) & (t < n_full)
        act_ref[cnt] = t
        cnt = cnt + live.astype(jnp.int32)
    act_ref[cnt] = n_full
    pairs = cnt // 2
    scores(act_ref[0], sa_ref)

    def body(u, carry):
        scores(act_ref[2 * u + 1], sb_ref)
        tile(act_ref[2 * u], sa_ref, False)
        scores(act_ref[2 * u + 2], sa_ref)
        tile(act_ref[2 * u + 1], sb_ref, False)
        return carry

    lax.fori_loop(0, pairs, body, 0)

    @pl.when(cnt % 2 == 1)
    def _():
        scores(n_full, sb_ref)
        tile(act_ref[cnt - 1], sa_ref, False)
        tile(n_full, sb_ref, True)

    @pl.when(cnt % 2 == 0)
    def _():
        tile(n_full, sa_ref, True)
    o_s = acc_ref[:NSA_KV, :] * (1.0 / jnp.maximum(acc_ref[NSA_KV:NSA_KV + 1, :], 1e-30))

    o = part_ref[...] + gate[1] * o_s
    yt = jnp.concatenate([o[(h // NSA_GROUP) * NSA_HD:(h // NSA_GROUP + 1) * NSA_HD, h * tq:(h + 1) * tq]
                          for h in range(NSA_HEADS)], axis=0)
    y_ref[...] = jnp.transpose(yt)


def _nsa_prompt(q8, sm, kca, vct, ksa, vst, kwa, vwt, b, l):
    tq = 256
    tk = 256
    assert l % tk == 0 and l >= tq + WINDOW
    n_ch = l // CMP_STRIDE
    ns = -(-l // SEL_BLOCK)
    nsp = -(-ns // LANES) * LANES
    nq = l // tq
    w = 2 * NSA_KV
    ovt = jnp.transpose(_overlap(n_ch, nsp)).astype(BF16)
    row = lambda bi, i: (bi * nq + i, 0)
    seq_rows = lambda bi, i: (bi, 0)
    seq_cols = lambda bi, i: (0, bi)
    return pl.pallas_call(
        functools.partial(_nsa_prompt_kernel, tq=tq, tk=tk),
        grid=(b, nq),
        in_specs=[pl.BlockSpec((tq, NSA_HEADS * LANES), row),
                  pl.BlockSpec((tq, LANES), row),
                  pl.BlockSpec((n_ch, w), seq_rows),
                  pl.BlockSpec((NSA_KV, n_ch), seq_cols),
                  pl.BlockSpec((l, w), seq_rows),
                  pl.BlockSpec((VS_ROWS, l), seq_cols),
                  pl.BlockSpec((l, w), seq_rows),
                  pl.BlockSpec((VS_ROWS, l), seq_cols),
                  pl.BlockSpec((nsp, n_ch), lambda bi, i: (0, 0))],
        out_specs=pl.BlockSpec((tq, NSA_Q), row),
        out_shape=jax.ShapeDtypeStruct((b * l, NSA_Q), F32),
        scratch_shapes=[pltpu.VMEM((nsp, NSA_HEADS * tq), F32),
                        pltpu.VMEM((1, NSA_HEADS * tq), F32),
                        pltpu.VMEM((VS_ROWS, NSA_HEADS * tq), F32),
                        pltpu.VMEM((2 * NSA_KV, NSA_HEADS * tq), BF16),
                        pltpu.VMEM((tk, NSA_HEADS * tq), F32),
                        pltpu.VMEM((tk, NSA_HEADS * tq), F32),
                        pltpu.SMEM((l // tk + 1,), jnp.int32),
                        pltpu.VMEM((NSA_KV, NSA_HEADS * tq), F32)],
        compiler_params=_cparams(("parallel", "arbitrary")),
        name="nsa_prompt",
    )(q8, sm, kca, vct, ksa, vst, kwa, vwt, ovt)


def _nsa_sample_kernel(*refs, pg, page, tq, past_len):
    pt_ref = refs[0]
    del pt_ref
    page_refs = refs[1:1 + pg]
    (q8_ref, sm_ref, kc_ref, vc_ref, knew_ref, kw_ref, ov_ref, y_ref,
     m_ref, l_ref, acc_ref, sel_ref, oc_ref) = refs[1 + pg:]
    j = pl.program_id(1)
    nsp = ov_ref.shape[1]
    r = NSA_HEADS * tq
    q8 = jnp.concatenate([q8_ref[:, h * LANES:(h + 1) * LANES].astype(F32) for h in range(NSA_HEADS)],
                         axis=0).astype(BF16)
    qpos = past_len + (_iota((r, 1), 0) & (tq - 1))
    slope = _row_slopes(tq)

    @pl.when(j == 0)
    def _():
        n_ch = kc_ref.shape[1]
        dist = qpos - (_iota((1, n_ch), 1) * CMP_STRIDE + (CMP_LEN - 1))
        o_c, p = _rows_attend(q8, kc_ref[...], vc_ref[...], slope, dist, dist >= 0)
        oc_ref[...] = o_c
        psums = []
        for g in range(NSA_KV_HEADS):
            psum = p[g * NSA_GROUP * tq:(g * NSA_GROUP + 1) * tq]
            for hh in range(1, NSA_GROUP):
                psum = psum + p[(g * NSA_GROUP + hh) * tq:(g * NSA_GROUP + hh + 1) * tq]
            psums.append(psum)
        sel = _top_blocks(_mm(jnp.concatenate(psums, axis=0), ov_ref[...], HI), qpos[:NSA_KV_HEADS * tq], nsp)
        for h in range(NSA_HEADS):
            g = h // NSA_GROUP
            sel_ref[h * tq:(h + 1) * tq, :] = sel[g * tq:(g + 1) * tq]
        m_ref[...] = jnp.full(m_ref.shape, NEG_INF, F32)
        l_ref[...] = jnp.zeros(l_ref.shape, F32)
        acc_ref[...] = jnp.zeros(acc_ref.shape, F32)

    sel_bf = sel_ref[...].astype(BF16)

    def sel_step(kv, kpos):
        n = kv.shape[1]
        expand = jnp.where(_iota((nsp, n), 0) == (kpos >> 6), 1.0, 0.0).astype(BF16)
        dist = qpos - kpos
        mask = (_mm(sel_bf, expand) > 0.5) & (dist >= 0)
        _rows_online(q8, kv[:NSA_KV].astype(BF16), kv[NSA_KV:].astype(BF16), slope, dist, mask,
                     m_ref, l_ref, acc_ref)

    sel_step(jnp.concatenate([page_refs[k][...] for k in range(pg)], axis=1),
             j * (pg * page) + _iota((1, pg * page), 1))

    @pl.when(j == pl.num_programs(1) - 1)
    def _():
        sel_step(knew_ref[...], past_len + _iota((1, knew_ref.shape[1]), 1))
        o_s = acc_ref[...] / jnp.maximum(l_ref[...], 1e-30)
        kw = kw_ref[...]
        kwpos = past_len - WINDOW + _iota((1, kw.shape[1]), 1)
        dist = qpos - kwpos
        mask = (dist >= 0) & (dist <= WINDOW) & (kwpos >= 0) & (kwpos < past_len + tq)
        o_w, _ = _rows_attend(q8, kw[:NSA_KV].astype(BF16), kw[NSA_KV:].astype(BF16), slope, dist, mask)
        y_ref[...] = _merge_heads(oc_ref[...], o_s, o_w, _sigmoid(sm_ref[...]), tq)


def _nsa_sample(q8, sm, kct, vct, knew_t, win_t, cache_t, page_table, b, l):
    n_pages = page_table.shape[1]
    page = cache_t.shape[2]
    past_len = n_pages * page
    assert l <= SEL_BLOCK and past_len % SEL_BLOCK == 0 and past_len >= WINDOW
    pg = 16
    steps = n_pages // pg
    n_ch = past_len // CMP_STRIDE
    ns = -(-(past_len + l) // SEL_BLOCK)
    nsp = -(-ns // LANES) * LANES
    w = 2 * NSA_KV
    ov = _overlap(n_ch, nsp)
    wcols = win_t.shape[2]

    def page_map(k):
        return lambda bi, j, pt: (pt[bi, j * pg + k], 1, 0)

    per_b = lambda bi, j, pt: (bi, 0)
    per_b3 = lambda bi, j, pt: (bi, 0, 0)
    grid_spec = pltpu.PrefetchScalarGridSpec(
        num_scalar_prefetch=1,
        grid=(b, steps),
        in_specs=[pl.BlockSpec((None, w, page), page_map(k)) for k in range(pg)]
        + [pl.BlockSpec((None, l, NSA_HEADS * LANES), per_b3),
           pl.BlockSpec((l, LANES), per_b),
           pl.BlockSpec((None, NSA_KV, n_ch), per_b3),
           pl.BlockSpec((None, NSA_KV, n_ch), per_b3),
           pl.BlockSpec((None, w, LANES), per_b3),
           pl.BlockSpec((None, w, wcols), per_b3),
           pl.BlockSpec((n_ch, nsp), lambda bi, j, pt: (0, 0))],
        out_specs=pl.BlockSpec((l, NSA_Q), per_b),
        scratch_shapes=[pltpu.VMEM((NSA_HEADS * l, 1), F32),
                        pltpu.VMEM((NSA_HEADS * l, 1), F32),
                        pltpu.VMEM((NSA_HEADS * l, NSA_KV), F32),
                        pltpu.VMEM((NSA_HEADS * l, nsp), F32),
                        pltpu.VMEM((NSA_HEADS * l, NSA_KV), F32)],
    )
    return pl.pallas_call(
        functools.partial(_nsa_sample_kernel, pg=pg, page=page, tq=l, past_len=past_len),
        grid_spec=grid_spec,
        out_shape=jax.ShapeDtypeStruct((b * l, NSA_Q), F32),
        compiler_params=_cparams(("parallel", "arbitrary")),
        name="nsa_sample",
    )(page_table, *([cache_t] * pg), q8.reshape(b, l, NSA_HEADS * LANES), sm, kct, vct, knew_t, win_t, ov)


def _tail_kernel(x_ref, ya_ref, yb_ref, ga_ref, gb_ref, wa_ref, wb_ref, wo_ref, n2_ref, wr_ref, br_ref,
                 h_ref, hn_ref, cmb_ref):
    ma = _mm(ya_ref[...].astype(BF16), wa_ref[...])
    mb = _mm(yb_ref[...].astype(BF16), wb_ref[...])
    m = _sigmoid(ga_ref[...]) * ma + _sigmoid(gb_ref[...]) * mb
    h = x_ref[...] + _mm(m.astype(BF16), wo_ref[...])
    h_ref[...] = h
    hn = h * lax.rsqrt(jnp.mean(h * h, axis=-1, keepdims=True) + RMS_EPS) * n2_ref[...]
    hn_hi = hn.astype(BF16)
    hn_ref[...] = hn_hi

    hn_lo = (hn - hn_hi.astype(F32)).astype(BF16)
    logit = _mm(hn_hi, wr_ref[0]) + _mm(hn_lo, wr_ref[0]) + _mm(hn_hi, wr_ref[1]) + br_ref[...]
    lane = _iota(logit.shape, 1)
    is_grp = (lane >= N_EXPERTS) & (lane < N_EXPERTS + N_GROUPS)
    gl = jnp.where(is_grp, logit, NEG_INF)
    gmax = jnp.max(gl, axis=-1, keepdims=True)
    gidx = jnp.min(jnp.where(gl == gmax, lane, LANES), axis=-1, keepdims=True) - N_EXPERTS
    p_grp = 1.0 / jnp.sum(jnp.exp(gl - gmax), axis=-1, keepdims=True)
    el = jnp.where((lane >> 3) == gidx, logit, NEG_INF)
    t1 = jnp.max(el, axis=-1, keepdims=True)
    i1 = jnp.min(jnp.where(el == t1, lane, LANES), axis=-1, keepdims=True)
    el2 = jnp.where(lane == i1, NEG_INF, el)
    t2 = jnp.max(el2, axis=-1, keepdims=True)
    i2 = jnp.min(jnp.where(el2 == t2, lane, LANES), axis=-1, keepdims=True)
    e2 = jnp.exp(t2 - t1)
    w1 = p_grp / (1.0 + e2)
    w2 = p_grp * e2 / (1.0 + e2)
    cmb_ref[...] = jnp.where(lane == i1, w1, jnp.where(lane == i2, w2,
                                                        jnp.where(lane == gidx + N_EXPERTS, 1.0, 0.0)))


def _tail(x2d, ya, yb, ga, gb, wa, wb, wo, norm2_g, w_route, b_route):
    n = x2d.shape[0]
    tm = min(ROW_TILE, n)
    row = lambda i: (i, 0)
    fixed = lambda i: (0, 0)
    return pl.pallas_call(
        _tail_kernel,
        grid=(n // tm,),
        in_specs=[pl.BlockSpec((tm, D_MODEL), row),
                  pl.BlockSpec((tm, GDN_V), row),
                  pl.BlockSpec((tm, NSA_Q), row),
                  pl.BlockSpec((tm, D_MODEL), row),
                  pl.BlockSpec((tm, D_MODEL), row),
                  pl.BlockSpec((GDN_V, D_MODEL), fixed),
                  pl.BlockSpec((NSA_Q, D_MODEL), fixed),
                  pl.BlockSpec((D_MODEL, D_MODEL), fixed),
                  pl.BlockSpec((1, D_MODEL), fixed),
                  pl.BlockSpec((2, D_MODEL, LANES), lambda i: (0, 0, 0)),
                  pl.BlockSpec((1, LANES), fixed)],
        out_specs=[pl.BlockSpec((tm, D_MODEL), row),
                   pl.BlockSpec((tm, D_MODEL), row),
                   pl.BlockSpec((tm, LANES), row)],
        out_shape=[jax.ShapeDtypeStruct((n, D_MODEL), F32),
                   jax.ShapeDtypeStruct((n, D_MODEL), BF16),
                   jax.ShapeDtypeStruct((n, LANES), F32)],
        compiler_params=_cparams(("parallel",)),
        name="tail",
    )(x2d, ya, yb, ga, gb, wa, wb, wo, norm2_g.reshape(1, D_MODEL), w_route, b_route)


MOE_SLABS = 2 * N_GROUPS
MOE_SLAB_EXPERTS = N_EXPERTS // MOE_SLABS


def _moe_kernel(offs_ref, cnts_ref, h_ref, hn_ref, cmb_ref, wg_ref, wu_ref, wd_ref, nf_ref, y_ref,
                xs_ref, cs_ref, pt_ref, acc_ref, *, rb):
    i = pl.program_id(0)
    sl = pl.program_id(1)
    t = hn_ref.shape[0]
    lane = _iota((t, LANES), 1)

    @pl.when(sl == 0)
    def _():
        cmb = cmb_ref[...]
        oh = jnp.where((lane >= N_EXPERTS) & (lane < N_EXPERTS + N_GROUPS), cmb, 0.0)
        oh_bf = oh.astype(BF16)
        r_i = _iota((t, t), 0)
        c_i = _iota((t, t), 1)
        lt = _iota((LANES, LANES), 0) < _iota((LANES, LANES), 1)
        before = _mm(_ones_where(c_i < r_i), oh_bf)
        totals = jnp.broadcast_to(jnp.sum(oh, axis=0, keepdims=True), (SUBLANES, LANES))
        smaller = _mm(totals, lt.astype(F32), HI)[0:1]
        rank_col = jnp.sum((before + smaller) * oh, axis=1, keepdims=True)
        eye = (_iota((LANES, LANES), 0) == _iota((LANES, LANES), 1)).astype(BF16)
        oht = _nt(eye, oh_bf)
        before_t = _mm(oht.astype(BF16), _ones_where(r_i < c_i))
        totals_t = jnp.broadcast_to(jnp.sum(oht, axis=1, keepdims=True), (LANES, LANES))
        gt = _iota((LANES, LANES), 0) > _iota((LANES, LANES), 1)
        smaller_t = _mm(gt.astype(F32), totals_t, HI)[:, 0:1]
        rank_row = jnp.sum((before_t + smaller_t) * oht, axis=0, keepdims=True)
        perm = _ones_where(r_i == rank_row.astype(jnp.int32))
        pt_ref[...] = _ones_where(c_i == rank_col.astype(jnp.int32))
        xs_ref[...] = _mm(perm, hn_ref[...]).astype(BF16)
        c_hi = cmb.astype(BF16)
        c_lo = (cmb - c_hi.astype(F32)).astype(BF16)
        cs_ref[...] = _mm(perm, c_hi) + _mm(perm, c_lo)
        acc_ref[...] = jnp.zeros(acc_ref.shape, F32)

    g = sl // (MOE_SLABS // N_GROUPS)
    off = offs_ref[i, g]
    row0 = (off // (2 * SUBLANES)) * (2 * SUBLANES)
    rows = off + cnts_ref[i, g] - row0
    nfull = rows // rb
    rem = rows - nfull * rb

    def block(lo, bs):
        r0 = pl.multiple_of(jnp.minimum(lo, t - bs), 2 * SUBLANES)
        x = xs_ref[pl.ds(r0, bs), :]
        cw = jnp.where(r0 + _iota((bs, 1), 0) >= lo, cs_ref[pl.ds(r0, bs), :], 0.0)
        lane_b = _iota((bs, LANES), 1)
        parts = []
        for e in range(MOE_SLAB_EXPERTS):
            wgt = jnp.sum(jnp.where(lane_b == sl * MOE_SLAB_EXPERTS + e, cw, 0.0), axis=1, keepdims=True)
            parts.append((_silu(_mm(x, wg_ref[e])) * _mm(x, wu_ref[e]) * wgt).astype(BF16))
        acc_ref[pl.ds(r0, bs), :] += _mm(jnp.concatenate(parts, axis=1), wd_ref[0])

    def full_block(j, carry):
        block(row0 + j * rb, rb)
        return carry

    lax.fori_loop(0, nfull, full_block, 0)
    sizes = sorted({s for s in (rb // 4, rb // 2, rb) if s % (2 * SUBLANES) == 0})
    prev = 0
    for bs in sizes:
        @pl.when((rem > prev) & (rem <= bs) & (cnts_ref[i, g] > 0))
        def _(bs=bs):
            block(row0 + nfull * rb, bs)
        prev = bs

    @pl.when(sl == pl.num_programs(1) - 1)
    def _():
        acc = acc_ref[...]
        a_hi = acc.astype(BF16)
        a_lo = (acc - a_hi.astype(F32)).astype(BF16)
        pt = pt_ref[...]
        v = h_ref[...] + _mm(pt, a_hi) + _mm(pt, a_lo)
        y_ref[...] = v * lax.rsqrt(jnp.mean(v * v, axis=-1, keepdims=True) + RMS_EPS) * nf_ref[...]


def _moe_weights(w_gate, w_up, w_down):
    return (w_gate.astype(BF16), w_up.astype(BF16),
            w_down.astype(BF16).reshape(MOE_SLABS, MOE_SLAB_EXPERTS * D_EXPERT, D_MODEL))


def _moe(h, hn, cmb, wg, wu, wd, norm_f_g):
    n = h.shape[0]
    t = min(1024, n)
    rb = min(256, t)
    tiles = n // t
    cnts = jnp.sum(cmb[:, N_EXPERTS:N_EXPERTS + N_GROUPS].reshape(tiles, t, N_GROUPS), axis=1).astype(jnp.int32)
    offs = jnp.cumsum(cnts, axis=1) - cnts
    ws = MOE_SLAB_EXPERTS * D_EXPERT
    row = lambda i, s, o, c: (i, 0)
    slab = lambda i, s, o, c: (s, 0, 0)
    grid_spec = pltpu.PrefetchScalarGridSpec(
        num_scalar_prefetch=2,
        grid=(tiles, MOE_SLABS),
        in_specs=[pl.BlockSpec((t, D_MODEL), row),
                  pl.BlockSpec((t, D_MODEL), row),
                  pl.BlockSpec((t, LANES), row),
                  pl.BlockSpec((MOE_SLAB_EXPERTS, D_MODEL, D_EXPERT), slab),
                  pl.BlockSpec((MOE_SLAB_EXPERTS, D_MODEL, D_EXPERT), slab),
                  pl.BlockSpec((1, ws, D_MODEL), slab),
                  pl.BlockSpec((1, D_MODEL), lambda i, s, o, c: (0, 0))],
        out_specs=pl.BlockSpec((t, D_MODEL), row),
        scratch_shapes=[pltpu.VMEM((t, D_MODEL), BF16),
                        pltpu.VMEM((t, LANES), F32),
                        pltpu.VMEM((t, t), BF16),
                        pltpu.VMEM((t, D_MODEL), F32)],
    )
    return pl.pallas_call(
        functools.partial(_moe_kernel, rb=rb),
        grid_spec=grid_spec,
        out_shape=jax.ShapeDtypeStruct((n, D_MODEL), F32),
        compiler_params=_cparams(("parallel", "arbitrary")),
        name="moe",
    )(offs, cnts, h, hn, cmb, wg, wu, wd, norm_f_g.reshape(1, D_MODEL))


def _route_weights(w_grp, b_grp, w_rt, b_rt):
    pad = LANES - N_EXPERTS - N_GROUPS
    w = jnp.concatenate([w_rt, w_grp, jnp.zeros((D_MODEL, pad), F32)], axis=1)
    bias = jnp.concatenate([b_rt, b_grp, jnp.zeros((pad,), F32)]).reshape(1, LANES)
    hi = w.astype(BF16)
    return jnp.stack([hi, (w - hi.astype(F32)).astype(BF16)]), bias


def _finish(x2d, ya, yb, ga, gb, wts):
    h, hn, cmb = _tail(x2d, ya, yb, ga, gb, wts["wa"], wts["wb"], wts["wo"], wts["norm2_g"],
                       wts["w_route"], wts["b_route"])
    return _moe(h, hn, cmb, wts["wg"], wts["wu"], wts["wd"], wts["norm_f_g"])


def _prompt_layer(x, wts):
    b, l, _ = x.shape
    x2d = x.reshape(b * l, D_MODEL)
    qkv, z, q8, ga, gb, sm, kv4t, kvwt, cmp_rows, ksa, kwa, vst, vwt = _project(
        x2d, wts["norm1_g"], wts["w_pk"], b, l)
    conv0 = jnp.zeros((b, GDN_CONV - 1, GDN_CONV_DIM), F32)
    s0 = jnp.zeros((b, GDN_HEADS, GDN_DK, GDN_DV), F32)
    ya, s_new, conv_new = _gdn(qkv, z, sm, conv0, s0, wts["conv_w"], wts["a_log"], wts["dt_bias"],
                               wts["gdn_norm_g"], b, l)
    head, tail = _pool_rows(cmp_rows, wts["pw"])
    kca, vct = _compress(head, tail, wts["cmp_wk"], wts["cmp_wv"], b)
    yb = _nsa_prompt(q8, sm, kca, vct, ksa, vst, kwa, vwt, b, l)
    y = _finish(x2d, ya, yb, ga, gb, wts)
    win_buf = min(WINDOW, l)
    kv_new = jnp.transpose(kv4t.reshape(b, 4, NSA_KV_HEADS, NSA_HD, l), (0, 4, 1, 2, 3))
    win_new = jnp.transpose(kvwt[:, :, l - win_buf:].reshape(b, 2, NSA_KV_HEADS, NSA_HD, win_buf), (0, 4, 1, 2, 3))
    return (y.reshape(b, l, D_MODEL), kv_new, win_new, s_new, conv_new)


def _sample_layer(x, cache_kv_l, page_table, cache_win_l, s0, conv_buf, wts):
    b, l, _ = x.shape
    x2d = x.reshape(b * l, D_MODEL)
    qkv, z, q8, ga, gb, sm, kv4, kvw = _project(x2d, wts["norm1_g"], wts["w_pk"], b, l)
    ya, s_new, conv_new = _gdn(qkv, z, sm, conv_buf, s0, wts["conv_w"], wts["a_log"], wts["dt_bias"],
                               wts["gdn_norm_g"], b, l)
    n_phys, page = cache_kv_l.shape[:2]
    w = 2 * NSA_KV
    cache_t = jnp.transpose(cache_kv_l, (0, 2, 3, 4, 1)).reshape(n_phys, 2 * w, page)
    win_buf = cache_win_l.shape[1]
    win_old_t = jnp.transpose(cache_win_l, (0, 2, 3, 4, 1)).reshape(b, w, win_buf)
    head_t, tail_t = _pool_pages(cache_t, page_table, wts["pos_wk"], wts["pos_wv"])
    kct, vct = _compress_t(head_t, tail_t, wts["cmp_wk"], wts["cmp_wv"])
    new_t = jnp.transpose(kv4.reshape(b, l, 2 * w), (0, 2, 1))
    knew_t = jnp.pad(new_t[:, w:], ((0, 0), (0, 0), (0, LANES - l)))
    win_new_t = jnp.transpose(kvw.reshape(b, l, w), (0, 2, 1))
    wcols = -(-(win_buf + l) // LANES) * LANES
    win_t = jnp.concatenate([win_old_t, win_new_t, jnp.zeros((b, w, wcols - win_buf - l), F32)], axis=2)
    yb = _nsa_sample(q8, sm, kct, vct, knew_t, win_t, cache_t, page_table, b, l)
    y = _finish(x2d, ya, yb, ga, gb, wts)
    win_new = jnp.transpose(win_t[:, :, l:l + win_buf].reshape(b, 2, NSA_KV_HEADS, NSA_HD, win_buf), (0, 4, 1, 2, 3))
    return (y.reshape(b, l, D_MODEL), kv4.reshape(b, l, 4, NSA_KV_HEADS, NSA_HD), win_new, s_new, conv_new)


def kernel(x_prompt, x_sample, cache_kv, page_table, cache_win, state_gdn, state_conv, norm1_g, w_in, gdn_conv_w, gdn_a_log, gdn_dt_bias, gdn_norm_g, cmp_pos_wk, cmp_pos_wv, cmp_wk, cmp_wv, w_branch_a, w_branch_b, w_out, norm2_g, w_grp, b_grp, w_rt, b_rt, w_e_gate, w_e_up, w_e_down, norm_f_g):
    assert w_in.shape[0] == 1, "single layer"
    w_route, b_route = _route_weights(w_grp[0], b_grp[0], w_rt[0], b_rt[0])
    wg, wu, wd = _moe_weights(w_e_gate[0], w_e_up[0], w_e_down[0])
    wts = dict(
        norm1_g=norm1_g[0], w_pk=_pack_w_in(w_in[0]),
        conv_w=gdn_conv_w[0], a_log=gdn_a_log[0], dt_bias=gdn_dt_bias[0], gdn_norm_g=gdn_norm_g[0],
        pw=_pos_weights(cmp_pos_wk[0], cmp_pos_wv[0]), pos_wk=cmp_pos_wk[0], pos_wv=cmp_pos_wv[0],
        cmp_wk=cmp_wk[0], cmp_wv=cmp_wv[0],
        wa=w_branch_a[0].astype(BF16), wb=w_branch_b[0].astype(BF16), wo=w_out[0].astype(BF16),
        norm2_g=norm2_g[0], w_route=w_route, b_route=b_route,
        wg=wg, wu=wu, wd=wd, norm_f_g=norm_f_g,
    )
    yp, kvp, winp, sp, cp = _prompt_layer(x_prompt, wts)
    ys, kvs, wins, ss, cs = _sample_layer(x_sample, cache_kv[0], page_table, cache_win[0], state_gdn[0],
                                          state_conv[0], wts)
    return (yp, ys, kvp[None], kvs[None], winp[None], wins[None], sp[None], ss[None], cp[None], cs[None])
```
